```python
import jax, jax.numpy as jnp
from jax import lax
import numpy as np

D_MODEL = 2048
BATCH = 8
SEQ = 4096
DEPTH = 4

HEAD_DIM = 64
EPS = 1e-6
Q_BLOCK = 128
FOX_HEADS = 8
FOX_WIDTH = FOX_HEADS * HEAD_DIM
CONV_WIDTH = 512
CONV_TAPS = 3
SGU_GROUPS = 4
SGU_GROUP_DIM = 128
SGU_WIDTH = SGU_GROUPS * SGU_GROUP_DIM
SGU_CHUNK = 128
DIL_PATTERNS = ((128, 1), (512, 4), (2048, 16))
DIL_HEADS_PER_GROUP = 4
DIL_HEADS = DIL_HEADS_PER_GROUP * len(DIL_PATTERNS)
DIL_WIDTH = DIL_HEADS * HEAD_DIM
DIL_OUT = DIL_HEADS_PER_GROUP * HEAD_DIM
ROPE_THETA = 500000.0
ROPE_DIM = HEAD_DIM // 4
N_BRANCH = 4
IN_SIZES = (3 * FOX_WIDTH, FOX_HEADS, 3 * CONV_WIDTH, 2 * SGU_WIDTH, 3 * DIL_WIDTH, N_BRANCH * D_MODEL)
D_IN = 3 * FOX_WIDTH + FOX_HEADS + 3 * CONV_WIDTH + 2 * SGU_WIDTH + 3 * DIL_WIDTH + N_BRANCH * D_MODEL
D_FF = 5632
FFN_TAPS = 3
PLE_DIM = 256

kernel_name = "hybrid_parallel_gated_fox_conv_sgu_dilated"


def rms_norm(x, g):
    xf = x.astype(jnp.float32)
    var = jnp.mean(xf * xf, axis=-1, keepdims=True)
    return (xf * lax.rsqrt(var + EPS)).astype(x.dtype) * g


def causal_dwconv(z, w):
    K = w.shape[0]
    S = z.shape[1]
    zp = jnp.pad(z, ((0, 0), (K - 1, 0), (0, 0)))
    return sum(w[k] * zp[:, k:k + S] for k in range(K))


def partial_rope(x, positions):
    half = ROPE_DIM // 2
    inv = ROPE_THETA ** (-jnp.arange(half, dtype=jnp.float32) * (2.0 / ROPE_DIM))
    ang = positions.astype(jnp.float32)[..., None] * inv
    cos = jnp.cos(ang)[:, :, None, :]
    sin = jnp.sin(ang)[:, :, None, :]
    x1 = x[..., :half].astype(jnp.float32)
    x2 = x[..., half:ROPE_DIM].astype(jnp.float32)
    rot = jnp.concatenate([(x1 * cos - x2 * sin).astype(x.dtype),
                           (x1 * sin + x2 * cos).astype(x.dtype),
                           x[..., ROPE_DIM:]], axis=-1)
    return rot


def forgetting_attention(q, k, v, log_f):
    B, S, H, Dh = q.shape
    nb = S // Q_BLOCK
    F = jnp.cumsum(log_f, axis=1).transpose(0, 2, 1)
    qb = q.reshape(B, nb, Q_BLOCK, H, Dh).transpose(1, 0, 2, 3, 4)
    Fb = F.reshape(B, H, nb, Q_BLOCK).transpose(2, 0, 1, 3)
    kpos = jnp.arange(S)
    scale = Dh ** -0.5

    def one_block(args):
        blk, q_blk, f_blk = args
        s = jnp.einsum('bqhd,bkhd->bhqk', q_blk, k, preferred_element_type=jnp.float32) * scale
        s = s + (f_blk[..., :, None] - F[..., None, :])
        qpos = blk * Q_BLOCK + jnp.arange(Q_BLOCK)
        s = jnp.where(kpos[None, :] <= qpos[:, None], s, -jnp.inf)
        p = jax.nn.softmax(s, axis=-1).astype(v.dtype)
        return jnp.einsum('bhqk,bkhd->bqhd', p, v)

    out = lax.map(one_block, (jnp.arange(nb), qb, Fb))
    return out.transpose(1, 0, 2, 3, 4).reshape(B, S, H * Dh)


def dilated_window_attention(q, k, v, window, dilation):
    B, S, H, Dh = q.shape
    L = S // dilation
    span = window // dilation
    nb = -(-L // span)
    Lp = nb * span

    def strided_blocks(t):
        t = t.reshape(B, L, dilation, H, Dh).transpose(0, 2, 1, 3, 4)
        t = jnp.pad(t, ((0, 0), (0, 0), (0, Lp - L), (0, 0), (0, 0)))
        return t.reshape(B, dilation, nb, span, H, Dh)

    def with_prev(t):
        prev = jnp.pad(t[:, :, :-1], ((0, 0), (0, 0), (1, 0), (0, 0), (0, 0), (0, 0)))
        return jnp.concatenate([prev, t], axis=3)

    qs = strided_blocks(q)
    kc = with_prev(strided_blocks(k))
    vc = with_prev(strided_blocks(v))
    s = jnp.einsum('brnqhd,brnkhd->brnhqk', qs, kc, preferred_element_type=jnp.float32) * (Dh ** -0.5)
    blk = jnp.arange(nb)[:, None, None]
    qi = jnp.arange(span)[None, :, None]
    ki = jnp.arange(2 * span)[None, None, :]
    dist = span + qi - ki
    valid = (dist >= 0) & (dist <= span) & ((blk > 0) | (ki >= span))
    s = jnp.where(valid[:, None], s, -jnp.inf)
    lse = jax.nn.logsumexp(s, axis=-1, keepdims=True)
    p = jnp.exp(s - lse).astype(v.dtype)
    o = jnp.einsum('brnhqk,brnkhd->brnqhd', p, vc)
    o = o.reshape(B, dilation, Lp, H, Dh)[:, :, :L].transpose(0, 2, 1, 3, 4).reshape(B, S, H, Dh)
    lse = lse[..., 0].transpose(0, 1, 2, 4, 3).reshape(B, dilation, Lp, H)[:, :, :L]
    lse = lse.transpose(0, 2, 1, 3).reshape(B, S, H)
    return o, lse


def chunked_spatial_gating(z, norm_g, w_s, b_s):
    u, v = jnp.split(z, 2, axis=-1)
    v = rms_norm(v, norm_g)
    B, S, _ = v.shape
    nc = S // SGU_CHUNK
    vc = v.reshape(B, nc, SGU_CHUNK, SGU_GROUPS, SGU_GROUP_DIM)
    mask = jnp.tril(jnp.ones((SGU_CHUNK, SGU_CHUNK), dtype=bool))
    ws = jnp.where(mask[None], w_s, jnp.zeros_like(w_s))
    mixed = jnp.einsum('gts,bcsgd->bctgd', ws, vc) + b_s.T[None, None, :, :, None]
    return u * mixed.reshape(B, S, SGU_WIDTH)


def _fwd_setup_inputs(seed: int = 0) -> dict:
    key = jax.random.key(seed)
    ks = jax.random.split(key, 24)

    def nrm(k, shape, scale):
        return jax.random.normal(k, shape, jnp.float32) * scale

    def gain(k, shape):
        return 1.0 + 0.1 * jax.random.normal(k, shape, jnp.float32)

    offset = jax.random.randint(ks[2], (BATCH, 1), 0, 1024, dtype=jnp.int32)
    positions = offset + jnp.arange(SEQ, dtype=jnp.int32)[None, :]
    return {
        "x": nrm(ks[0], (BATCH, SEQ, D_MODEL), 1.0),
        "p": nrm(ks[1], (DEPTH, BATCH, SEQ, PLE_DIM), 1.0),
        "positions": positions,
        "norm_mix_g": gain(ks[3], (DEPTH, D_MODEL)),
        "w_in": nrm(ks[4], (DEPTH, D_MODEL, D_IN), D_MODEL ** -0.5),
        "fox_forget_b": 2.0 + 3.0 * jax.random.uniform(ks[5], (DEPTH, FOX_HEADS), jnp.float32),
        "shortconv_w": nrm(ks[6], (DEPTH, CONV_TAPS, CONV_WIDTH), CONV_TAPS ** -0.5),
        "sgu_norm_g": gain(ks[7], (DEPTH, SGU_WIDTH)),
        "sgu_w": nrm(ks[8], (DEPTH, SGU_GROUPS, SGU_CHUNK, SGU_CHUNK), SGU_CHUNK ** -0.5),
        "sgu_b": gain(ks[9], (DEPTH, SGU_GROUPS, SGU_CHUNK)),
        "w_br_fox": nrm(ks[10], (DEPTH, FOX_WIDTH, D_MODEL), FOX_WIDTH ** -0.5),
        "w_br_conv": nrm(ks[11], (DEPTH, CONV_WIDTH, D_MODEL), CONV_WIDTH ** -0.5),
        "w_br_sgu": nrm(ks[12], (DEPTH, SGU_WIDTH, D_MODEL), SGU_WIDTH ** -0.5),
        "w_br_dil": nrm(ks[13], (DEPTH, DIL_OUT, D_MODEL), DIL_OUT ** -0.5),
        "w_out": nrm(ks[14], (DEPTH, D_MODEL, D_MODEL), D_MODEL ** -0.5),
        "norm_ffn_g": gain(ks[15], (DEPTH, D_MODEL)),
        "w_up": nrm(ks[16], (DEPTH, D_MODEL, 2 * D_FF), D_MODEL ** -0.5),
        "ffn_conv_w": nrm(ks[17], (DEPTH, FFN_TAPS, 2 * D_FF), FFN_TAPS ** -0.5),
        "w_down": nrm(ks[18], (DEPTH, D_FF, D_MODEL), D_FF ** -0.5),
        "norm_ple_g": gain(ks[19], (DEPTH, D_MODEL)),
        "w_ple_gate": nrm(ks[20], (DEPTH, D_MODEL, D_MODEL), D_MODEL ** -0.5),
        "w_ple_proj": nrm(ks[21], (DEPTH, PLE_DIM, D_MODEL), PLE_DIM ** -0.5),
        "final_norm_g": gain(ks[22], (D_MODEL,)),
    }


def _fwd_reference(x, p, positions, norm_mix_g, w_in, fox_forget_b, shortconv_w, sgu_norm_g, sgu_w, sgu_b,
              w_br_fox, w_br_conv, w_br_sgu, w_br_dil, w_out, norm_ffn_g, w_up, ffn_conv_w, w_down,
              norm_ple_g, w_ple_gate, w_ple_proj, final_norm_g):
    B, S, _ = x.shape
    split_points = [int(c) for c in np.cumsum(IN_SIZES)[:-1]]
    for i in range(DEPTH):
        h = rms_norm(x, norm_mix_g[i])
        proj = h @ w_in[i]
        a_qkv, a_f, b_in, c_in, d_qkv, gate_logits = jnp.split(proj, split_points, axis=-1)

        a_qkv = a_qkv.reshape(B, S, 3, FOX_HEADS, HEAD_DIM)
        log_f = jax.nn.log_sigmoid(a_f.astype(jnp.float32) + fox_forget_b[i].astype(jnp.float32))
        o_a = forgetting_attention(a_qkv[:, :, 0], a_qkv[:, :, 1], a_qkv[:, :, 2], log_f)

        xb, gate_b, gate_c = jnp.split(b_in, 3, axis=-1)
        o_b = gate_b * causal_dwconv(gate_c * xb, shortconv_w[i])

        o_c = chunked_spatial_gating(jax.nn.gelu(c_in), sgu_norm_g[i], sgu_w[i], sgu_b[i])

        d_qkv = d_qkv.reshape(B, S, 3, DIL_HEADS, HEAD_DIM)
        qd = partial_rope(d_qkv[:, :, 0], positions)
        kd = partial_rope(d_qkv[:, :, 1], positions)
        vd = d_qkv[:, :, 2]
        outs, lses = [], []
        for g, (window, dil) in enumerate(DIL_PATTERNS):
            hs = slice(g * DIL_HEADS_PER_GROUP, (g + 1) * DIL_HEADS_PER_GROUP)
            o_g, l_g = dilated_window_attention(qd[:, :, hs], kd[:, :, hs], vd[:, :, hs], window, dil)
            outs.append(o_g)
            lses.append(l_g)
        wts = jax.nn.softmax(jnp.stack(lses, axis=0), axis=0)
        o_d = jnp.sum(wts[..., None] * jnp.stack(outs, axis=0), axis=0).astype(x.dtype).reshape(B, S, DIL_OUT)

        gates = jax.nn.sigmoid(gate_logits).reshape(B, S, N_BRANCH, D_MODEL)
        merged = (gates[:, :, 0] * (o_a @ w_br_fox[i]) + gates[:, :, 1] * (o_b @ w_br_conv[i])
                  + gates[:, :, 2] * (o_c @ w_br_sgu[i]) + gates[:, :, 3] * (o_d @ w_br_dil[i]))
        x = x + merged @ w_out[i]

        h = rms_norm(x, norm_ffn_g[i])
        up = causal_dwconv(h @ w_up[i], ffn_conv_w[i])
        up_gate, up_val = jnp.split(up, 2, axis=-1)
        x = x + (jax.nn.silu(up_gate) * up_val) @ w_down[i]

        ple_gate = jax.nn.sigmoid(rms_norm(x, norm_ple_g[i]) @ w_ple_gate[i])
        x = x + ple_gate * (p[i] @ w_ple_proj[i])
    return rms_norm(x, final_norm_g)


import jax as _jax
import jax.numpy as _jnp

TWIN_FORMAT = 'train_step'
FWD_PARAMS = ['x', 'p', 'positions', 'norm_mix_g', 'w_in', 'fox_forget_b', 'shortconv_w', 'sgu_norm_g', 'sgu_w', 'sgu_b', 'w_br_fox', 'w_br_conv', 'w_br_sgu', 'w_br_dil', 'w_out', 'norm_ffn_g', 'w_up', 'ffn_conv_w', 'w_down', 'norm_ple_g', 'w_ple_gate', 'w_ple_proj', 'final_norm_g']
TWIN_WEIGHTS = ['norm_mix_g', 'w_in', 'fox_forget_b', 'shortconv_w', 'sgu_norm_g', 'sgu_w', 'sgu_b', 'w_br_fox', 'w_br_conv', 'w_br_sgu', 'w_br_dil', 'w_out', 'norm_ffn_g', 'w_up', 'ffn_conv_w', 'w_down', 'norm_ple_g', 'w_ple_gate', 'w_ple_proj', 'final_norm_g']
TWIN_DIFF_INPUT = 'x'
TWIN_INPUTS = ['x', 'p', 'positions', 'norm_mix_g', 'w_in', 'fox_forget_b', 'shortconv_w', 'sgu_norm_g', 'sgu_w', 'sgu_b', 'w_br_fox', 'w_br_conv', 'w_br_sgu', 'w_br_dil', 'w_out', 'norm_ffn_g', 'w_up', 'ffn_conv_w', 'w_down', 'norm_ple_g', 'w_ple_gate', 'w_ple_proj', 'final_norm_g', 'loss_target', 'm_norm_mix_g', 'm_w_in', 'm_fox_forget_b', 'm_shortconv_w', 'm_sgu_norm_g', 'm_sgu_w', 'm_sgu_b', 'm_w_br_fox', 'm_w_br_conv', 'm_w_br_sgu', 'm_w_br_dil', 'm_w_out', 'm_norm_ffn_g', 'm_w_up', 'm_ffn_conv_w', 'm_w_down', 'm_norm_ple_g', 'm_w_ple_gate', 'm_w_ple_proj', 'm_final_norm_g', 'v_norm_mix_g', 'v_w_in', 'v_fox_forget_b', 'v_shortconv_w', 'v_sgu_norm_g', 'v_sgu_w', 'v_sgu_b', 'v_w_br_fox', 'v_w_br_conv', 'v_w_br_sgu', 'v_w_br_dil', 'v_w_out', 'v_norm_ffn_g', 'v_w_up', 'v_ffn_conv_w', 'v_w_down', 'v_norm_ple_g', 'v_w_ple_gate', 'v_w_ple_proj', 'v_final_norm_g']
TWIN_OUTPUTS = ['loss', 'grad_x', 'grad_norm_mix_g', 'grad_w_in', 'grad_fox_forget_b', 'grad_shortconv_w', 'grad_sgu_norm_g', 'grad_sgu_w', 'grad_sgu_b', 'grad_w_br_fox', 'grad_w_br_conv', 'grad_w_br_sgu', 'grad_w_br_dil', 'grad_w_out', 'grad_norm_ffn_g', 'grad_w_up', 'grad_ffn_conv_w', 'grad_w_down', 'grad_norm_ple_g', 'grad_w_ple_gate', 'grad_w_ple_proj', 'grad_final_norm_g', 'delta_norm_mix_g', 'delta_w_in', 'delta_fox_forget_b', 'delta_shortconv_w', 'delta_sgu_norm_g', 'delta_sgu_w', 'delta_sgu_b', 'delta_w_br_fox', 'delta_w_br_conv', 'delta_w_br_sgu', 'delta_w_br_dil', 'delta_w_out', 'delta_norm_ffn_g', 'delta_w_up', 'delta_ffn_conv_w', 'delta_w_down', 'delta_norm_ple_g', 'delta_w_ple_gate', 'delta_w_ple_proj', 'delta_final_norm_g', 'new_m_norm_mix_g', 'new_m_w_in', 'new_m_fox_forget_b', 'new_m_shortconv_w', 'new_m_sgu_norm_g', 'new_m_sgu_w', 'new_m_sgu_b', 'new_m_w_br_fox', 'new_m_w_br_conv', 'new_m_w_br_sgu', 'new_m_w_br_dil', 'new_m_w_out', 'new_m_norm_ffn_g', 'new_m_w_up', 'new_m_ffn_conv_w', 'new_m_w_down', 'new_m_norm_ple_g', 'new_m_w_ple_gate', 'new_m_w_ple_proj', 'new_m_final_norm_g', 'new_v_norm_mix_g', 'new_v_w_in', 'new_v_fox_forget_b', 'new_v_shortconv_w', 'new_v_sgu_norm_g', 'new_v_sgu_w', 'new_v_sgu_b', 'new_v_w_br_fox', 'new_v_w_br_conv', 'new_v_w_br_sgu', 'new_v_w_br_dil', 'new_v_w_out', 'new_v_norm_ffn_g', 'new_v_w_up', 'new_v_ffn_conv_w', 'new_v_w_down', 'new_v_norm_ple_g', 'new_v_w_ple_gate', 'new_v_w_ple_proj', 'new_v_final_norm_g']
TWIN_LEAF_KINDS = {'loss': 'loss', 'grad_x': 'grad_x', 'grad_norm_mix_g': 'grad_w', 'grad_w_in': 'grad_w', 'grad_fox_forget_b': 'grad_w', 'grad_shortconv_w': 'grad_w', 'grad_sgu_norm_g': 'grad_w', 'grad_sgu_w': 'grad_w', 'grad_sgu_b': 'grad_w', 'grad_w_br_fox': 'grad_w', 'grad_w_br_conv': 'grad_w', 'grad_w_br_sgu': 'grad_w', 'grad_w_br_dil': 'grad_w', 'grad_w_out': 'grad_w', 'grad_norm_ffn_g': 'grad_w', 'grad_w_up': 'grad_w', 'grad_ffn_conv_w': 'grad_w', 'grad_w_down': 'grad_w', 'grad_norm_ple_g': 'grad_w', 'grad_w_ple_gate': 'grad_w', 'grad_w_ple_proj': 'grad_w', 'grad_final_norm_g': 'grad_w', 'delta_norm_mix_g': 'delta_w', 'delta_w_in': 'delta_w', 'delta_fox_forget_b': 'delta_w', 'delta_shortconv_w': 'delta_w', 'delta_sgu_norm_g': 'delta_w', 'delta_sgu_w': 'delta_w', 'delta_sgu_b': 'delta_w', 'delta_w_br_fox': 'delta_w', 'delta_w_br_conv': 'delta_w', 'delta_w_br_sgu': 'delta_w', 'delta_w_br_dil': 'delta_w', 'delta_w_out': 'delta_w', 'delta_norm_ffn_g': 'delta_w', 'delta_w_up': 'delta_w', 'delta_ffn_conv_w': 'delta_w', 'delta_w_down': 'delta_w', 'delta_norm_ple_g': 'delta_w', 'delta_w_ple_gate': 'delta_w', 'delta_w_ple_proj': 'delta_w', 'delta_final_norm_g': 'delta_w', 'new_m_norm_mix_g': 'new_m', 'new_m_w_in': 'new_m', 'new_m_fox_forget_b': 'new_m', 'new_m_shortconv_w': 'new_m', 'new_m_sgu_norm_g': 'new_m', 'new_m_sgu_w': 'new_m', 'new_m_sgu_b': 'new_m', 'new_m_w_br_fox': 'new_m', 'new_m_w_br_conv': 'new_m', 'new_m_w_br_sgu': 'new_m', 'new_m_w_br_dil': 'new_m', 'new_m_w_out': 'new_m', 'new_m_norm_ffn_g': 'new_m', 'new_m_w_up': 'new_m', 'new_m_ffn_conv_w': 'new_m', 'new_m_w_down': 'new_m', 'new_m_norm_ple_g': 'new_m', 'new_m_w_ple_gate': 'new_m', 'new_m_w_ple_proj': 'new_m', 'new_m_final_norm_g': 'new_m', 'new_v_norm_mix_g': 'new_v', 'new_v_w_in': 'new_v', 'new_v_fox_forget_b': 'new_v', 'new_v_shortconv_w': 'new_v', 'new_v_sgu_norm_g': 'new_v', 'new_v_sgu_w': 'new_v', 'new_v_sgu_b': 'new_v', 'new_v_w_br_fox': 'new_v', 'new_v_w_br_conv': 'new_v', 'new_v_w_br_sgu': 'new_v', 'new_v_w_br_dil': 'new_v', 'new_v_w_out': 'new_v', 'new_v_norm_ffn_g': 'new_v', 'new_v_w_up': 'new_v', 'new_v_ffn_conv_w': 'new_v', 'new_v_w_down': 'new_v', 'new_v_norm_ple_g': 'new_v', 'new_v_w_ple_gate': 'new_v', 'new_v_w_ple_proj': 'new_v', 'new_v_final_norm_g': 'new_v'}


def _forward(args):
    return _fwd_reference(*[args[k] for k in FWD_PARAMS])


def _output_shape():
    def fwd():
        inp = _fwd_setup_inputs(0)
        return _fwd_reference(*[inp[k] for k in FWD_PARAMS])
    out = _jax.eval_shape(fwd)
    return out.shape, out.dtype

N_MICROBATCH = 1
ADAM_LR = 0.001
ADAM_B1 = 0.9
ADAM_B2 = 0.999
ADAM_EPS = 1e-08
ADAM_WD = 0.01
ADAM_STEP = 10
PER_EXAMPLE_BATCH_AXIS = {'x': 0, 'p': 1, 'positions': 0, 'loss_target': 0}
SHARED_INPUTS = []
_WEIGHT_DTYPES = {'norm_mix_g': _jnp.float32, 'w_in': _jnp.float32, 'fox_forget_b': _jnp.float32, 'shortconv_w': _jnp.float32, 'sgu_norm_g': _jnp.float32, 'sgu_w': _jnp.float32, 'sgu_b': _jnp.float32, 'w_br_fox': _jnp.float32, 'w_br_conv': _jnp.float32, 'w_br_sgu': _jnp.float32, 'w_br_dil': _jnp.float32, 'w_out': _jnp.float32, 'norm_ffn_g': _jnp.float32, 'w_up': _jnp.float32, 'ffn_conv_w': _jnp.float32, 'w_down': _jnp.float32, 'norm_ple_g': _jnp.float32, 'w_ple_gate': _jnp.float32, 'w_ple_proj': _jnp.float32, 'final_norm_g': _jnp.float32}
MOMENT_SCALE = {'norm_mix_g': 8.469724e-02, 'w_in': 3.099341e-02, 'fox_forget_b': 7.321554e-02, 'shortconv_w': 7.969087e-02, 'sgu_norm_g': 3.757247e-02, 'sgu_w': 3.567258e-02, 'sgu_b': 5.166420e-02, 'w_br_fox': 1.167879e-02, 'w_br_conv': 3.990757e-02, 'w_br_sgu': 3.833886e-02, 'w_br_dil': 8.241599e-03, 'w_out': 5.668478e-02, 'norm_ffn_g': 5.418025e-02, 'w_up': 2.339748e-02, 'ffn_conv_w': 2.337341e-02, 'w_down': 3.849386e-02, 'norm_ple_g': 1.392170e-02, 'w_ple_gate': 1.388540e-02, 'w_ple_proj': 3.385788e-02, 'final_norm_g': 1.605218e+01}


def _to_microbatches(a, axis):
    t = _jnp.moveaxis(a, axis, 0)
    t = t.reshape((N_MICROBATCH, t.shape[0] // N_MICROBATCH) + t.shape[1:])
    return _jnp.moveaxis(t, 1, axis + 1)


def setup_inputs(seed: int = 0) -> dict:
    inp = _fwd_setup_inputs(seed)
    key = _jax.random.fold_in(_jax.random.key(seed), 7919)
    shape, _ = _output_shape()
    out = dict(inp)
    out["loss_target"] = _jax.random.normal(_jax.random.fold_in(key, 0), shape, _jnp.float32)
    for i, name in enumerate(TWIN_WEIGHTS):
        w = inp[name].astype(_jnp.float32)
        if MOMENT_SCALE is None:
            s = _jnp.sqrt(_jnp.mean(_jnp.square(w)) + 1e-30)
        else:
            s = MOMENT_SCALE[name]
        km, kv = _jax.random.split(_jax.random.fold_in(key, i + 1))
        out[name] = w
        out["m_" + name] = s * _jax.random.normal(km, w.shape, _jnp.float32)
        out["v_" + name] = (s * s) * _jax.random.uniform(kv, w.shape, _jnp.float32, 0.5, 1.5)
    if N_MICROBATCH > 1:
        for name, axis in PER_EXAMPLE_BATCH_AXIS.items():
            out[name] = _to_microbatches(out[name], axis)
    return {'x': out['x'], 'p': out['p'], 'positions': out['positions'], 'norm_mix_g': out['norm_mix_g'], 'w_in': out['w_in'], 'fox_forget_b': out['fox_forget_b'], 'shortconv_w': out['shortconv_w'], 'sgu_norm_g': out['sgu_norm_g'], 'sgu_w': out['sgu_w'], 'sgu_b': out['sgu_b'], 'w_br_fox': out['w_br_fox'], 'w_br_conv': out['w_br_conv'], 'w_br_sgu': out['w_br_sgu'], 'w_br_dil': out['w_br_dil'], 'w_out': out['w_out'], 'norm_ffn_g': out['norm_ffn_g'], 'w_up': out['w_up'], 'ffn_conv_w': out['ffn_conv_w'], 'w_down': out['w_down'], 'norm_ple_g': out['norm_ple_g'], 'w_ple_gate': out['w_ple_gate'], 'w_ple_proj': out['w_ple_proj'], 'final_norm_g': out['final_norm_g'], 'loss_target': out['loss_target'], 'm_norm_mix_g': out['m_norm_mix_g'], 'm_w_in': out['m_w_in'], 'm_fox_forget_b': out['m_fox_forget_b'], 'm_shortconv_w': out['m_shortconv_w'], 'm_sgu_norm_g': out['m_sgu_norm_g'], 'm_sgu_w': out['m_sgu_w'], 'm_sgu_b': out['m_sgu_b'], 'm_w_br_fox': out['m_w_br_fox'], 'm_w_br_conv': out['m_w_br_conv'], 'm_w_br_sgu': out['m_w_br_sgu'], 'm_w_br_dil': out['m_w_br_dil'], 'm_w_out': out['m_w_out'], 'm_norm_ffn_g': out['m_norm_ffn_g'], 'm_w_up': out['m_w_up'], 'm_ffn_conv_w': out['m_ffn_conv_w'], 'm_w_down': out['m_w_down'], 'm_norm_ple_g': out['m_norm_ple_g'], 'm_w_ple_gate': out['m_w_ple_gate'], 'm_w_ple_proj': out['m_w_ple_proj'], 'm_final_norm_g': out['m_final_norm_g'], 'v_norm_mix_g': out['v_norm_mix_g'], 'v_w_in': out['v_w_in'], 'v_fox_forget_b': out['v_fox_forget_b'], 'v_shortconv_w': out['v_shortconv_w'], 'v_sgu_norm_g': out['v_sgu_norm_g'], 'v_sgu_w': out['v_sgu_w'], 'v_sgu_b': out['v_sgu_b'], 'v_w_br_fox': out['v_w_br_fox'], 'v_w_br_conv': out['v_w_br_conv'], 'v_w_br_sgu': out['v_w_br_sgu'], 'v_w_br_dil': out['v_w_br_dil'], 'v_w_out': out['v_w_out'], 'v_norm_ffn_g': out['v_norm_ffn_g'], 'v_w_up': out['v_w_up'], 'v_ffn_conv_w': out['v_ffn_conv_w'], 'v_w_down': out['v_w_down'], 'v_norm_ple_g': out['v_norm_ple_g'], 'v_w_ple_gate': out['v_w_ple_gate'], 'v_w_ple_proj': out['v_w_ple_proj'], 'v_final_norm_g': out['v_final_norm_g']}


def _loss(weights, diff, rest, loss_target):
    with _jax.named_scope("forward"):
        args = {**rest, TWIN_DIFF_INPUT: diff, **{k: w.astype(_WEIGHT_DTYPES[k]) for k, w in weights.items()}}
        y = _forward(args)
    with _jax.named_scope("loss_head"):
        err = _jnp.square(y.astype(_jnp.float32) - loss_target)
        return 0.5 * _jnp.sum(_jnp.mean(err, axis=-1)) if err.ndim else 0.5 * err


def _adamw(w, g, m, v):
    m = ADAM_B1 * m + (1.0 - ADAM_B1) * g
    v = ADAM_B2 * v + (1.0 - ADAM_B2) * _jnp.square(g)
    m_hat = m / (1.0 - ADAM_B1 ** ADAM_STEP)
    v_hat = v / (1.0 - ADAM_B2 ** ADAM_STEP)
    delta = -ADAM_LR * (m_hat / (_jnp.sqrt(v_hat) + ADAM_EPS) + ADAM_WD * w)
    return delta, m, v


def reference(x, p, positions, norm_mix_g, w_in, fox_forget_b, shortconv_w, sgu_norm_g, sgu_w, sgu_b, w_br_fox, w_br_conv, w_br_sgu, w_br_dil, w_out, norm_ffn_g, w_up, ffn_conv_w, w_down, norm_ple_g, w_ple_gate, w_ple_proj, final_norm_g, loss_target, m_norm_mix_g, m_w_in, m_fox_forget_b, m_shortconv_w, m_sgu_norm_g, m_sgu_w, m_sgu_b, m_w_br_fox, m_w_br_conv, m_w_br_sgu, m_w_br_dil, m_w_out, m_norm_ffn_g, m_w_up, m_ffn_conv_w, m_w_down, m_norm_ple_g, m_w_ple_gate, m_w_ple_proj, m_final_norm_g, v_norm_mix_g, v_w_in, v_fox_forget_b, v_shortconv_w, v_sgu_norm_g, v_sgu_w, v_sgu_b, v_w_br_fox, v_w_br_conv, v_w_br_sgu, v_w_br_dil, v_w_out, v_norm_ffn_g, v_w_up, v_ffn_conv_w, v_w_down, v_norm_ple_g, v_w_ple_gate, v_w_ple_proj, v_final_norm_g):
    given = dict(x=x, p=p, positions=positions, norm_mix_g=norm_mix_g, w_in=w_in, fox_forget_b=fox_forget_b, shortconv_w=shortconv_w, sgu_norm_g=sgu_norm_g, sgu_w=sgu_w, sgu_b=sgu_b, w_br_fox=w_br_fox, w_br_conv=w_br_conv, w_br_sgu=w_br_sgu, w_br_dil=w_br_dil, w_out=w_out, norm_ffn_g=norm_ffn_g, w_up=w_up, ffn_conv_w=ffn_conv_w, w_down=w_down, norm_ple_g=norm_ple_g, w_ple_gate=w_ple_gate, w_ple_proj=w_ple_proj, final_norm_g=final_norm_g, loss_target=loss_target, m_norm_mix_g=m_norm_mix_g, m_w_in=m_w_in, m_fox_forget_b=m_fox_forget_b, m_shortconv_w=m_shortconv_w, m_sgu_norm_g=m_sgu_norm_g, m_sgu_w=m_sgu_w, m_sgu_b=m_sgu_b, m_w_br_fox=m_w_br_fox, m_w_br_conv=m_w_br_conv, m_w_br_sgu=m_w_br_sgu, m_w_br_dil=m_w_br_dil, m_w_out=m_w_out, m_norm_ffn_g=m_norm_ffn_g, m_w_up=m_w_up, m_ffn_conv_w=m_ffn_conv_w, m_w_down=m_w_down, m_norm_ple_g=m_norm_ple_g, m_w_ple_gate=m_w_ple_gate, m_w_ple_proj=m_w_ple_proj, m_final_norm_g=m_final_norm_g, v_norm_mix_g=v_norm_mix_g, v_w_in=v_w_in, v_fox_forget_b=v_fox_forget_b, v_shortconv_w=v_shortconv_w, v_sgu_norm_g=v_sgu_norm_g, v_sgu_w=v_sgu_w, v_sgu_b=v_sgu_b, v_w_br_fox=v_w_br_fox, v_w_br_conv=v_w_br_conv, v_w_br_sgu=v_w_br_sgu, v_w_br_dil=v_w_br_dil, v_w_out=v_w_out, v_norm_ffn_g=v_norm_ffn_g, v_w_up=v_w_up, v_ffn_conv_w=v_ffn_conv_w, v_w_down=v_w_down, v_norm_ple_g=v_norm_ple_g, v_w_ple_gate=v_w_ple_gate, v_w_ple_proj=v_w_ple_proj, v_final_norm_g=v_final_norm_g)
    weights = {n: given[n] for n in TWIN_WEIGHTS}
    shared = {n: given[n] for n in SHARED_INPUTS}
    per_example = {n: given[n] for n in ['x', 'p', 'positions']}
    grad_fn = _jax.value_and_grad(_loss, argnums=(0, 1))

    def one_microbatch(ex, loss_target):
        ex = dict(ex)
        diff = ex.pop(TWIN_DIFF_INPUT)
        return grad_fn(weights, diff, {**shared, **ex}, loss_target)

    if N_MICROBATCH == 1:
        loss, (grad_w, grad_x) = one_microbatch(per_example, given["loss_target"])
    else:
        def body(carry, xs):
            loss_sum, grad_sum = carry
            l_k, (gw_k, gx_k) = one_microbatch(xs[0], xs[1])
            with _jax.named_scope("update"):
                return (loss_sum + l_k, _jax.tree.map(_jnp.add, grad_sum, gw_k)), gx_k

        init = (_jnp.zeros((), _jnp.float32), _jax.tree.map(_jnp.zeros_like, weights))
        (loss, grad_w), grad_x = _jax.lax.scan(body, init, (per_example, given["loss_target"]))
    with _jax.named_scope("update"):
        delta_w, new_m, new_v = {}, {}, {}
        for n in TWIN_WEIGHTS:
            delta_w[n], new_m[n], new_v[n] = _adamw(weights[n], grad_w[n], given["m_" + n], given["v_" + n])
    return (loss, grad_x, *[grad_w[n] for n in TWIN_WEIGHTS], *[delta_w[n] for n in TWIN_WEIGHTS],
            *[new_m[n] for n in TWIN_WEIGHTS], *[new_v[n] for n in TWIN_WEIGHTS])
```

```python
import functools

import jax
import jax.numpy as jnp
from jax import lax
from jax.experimental import pallas as pl
from jax.experimental.pallas import tpu as pltpu

F32 = jnp.float32
BF16 = jnp.bfloat16
MESH = pl.DeviceIdType.MESH
BS = pl.BlockSpec
SDS = jax.ShapeDtypeStruct
ANY = pl.BlockSpec(memory_space=pl.ANY)

VMEM_LIMIT_BYTES = 52 * 1024 * 1024
LANES = 128

D_MODEL = 2048
HEAD_DIM = 64
EPS = 1e-6
NEG = -1e30
FOX_HEADS = 8
FOX_WIDTH = 512
CONV_WIDTH = 512
SGU_WIDTH = 512
SGU_CHUNK = 128
DIL_PATTERNS = ((128, 1), (512, 4), (2048, 16))
DIL_SPAN = 128
DIL_HEADS = 12
DIL_WIDTH = 768
DIL_OUT = 256
ROPE_THETA = 500000.0
ROPE_DIM = 16
N_BRANCH = 4
D_FF = 5632
PLE_DIM = 256
D_IN = 14600

OFF_G, OFF_A, OFF_B, OFF_C, OFF_D, OFF_F, W_AL = 0, 8192, 9728, 11264, 12288, 14592, 14848
SRC_A, SRC_F, SRC_B, SRC_C, SRC_DQ, SRC_G = 0, 1536, 1544, 3080, 4104, 6408

ADAM_LR, ADAM_B1, ADAM_B2, ADAM_EPS, ADAM_WD, ADAM_STEP = 0.001, 0.9, 0.999, 1e-08, 0.01, 10

PACK_COLS = 1024
N_CHIPS = 4
N_DEV = 8

SHARDED = ("w_in", "shortconv_w", "w_br_fox", "w_br_conv", "w_br_sgu", "w_br_dil", "w_out", "w_up",
           "ffn_conv_w", "w_down", "w_ple_gate", "w_ple_proj")
ROW_SHARDED = ("w_out", "w_down", "w_ple_gate")
REPLICATED = ("norm_mix_g", "fox_forget_b", "sgu_norm_g", "sgu_w", "sgu_b", "norm_ffn_g", "norm_ple_g")
WEIGHTS = ("norm_mix_g", "w_in", "fox_forget_b", "shortconv_w", "sgu_norm_g", "sgu_w", "sgu_b", "w_br_fox",
           "w_br_conv", "w_br_sgu", "w_br_dil", "w_out", "norm_ffn_g", "w_up", "ffn_conv_w", "w_down",
           "norm_ple_g", "w_ple_gate", "w_ple_proj", "final_norm_g")


def _params(n_grid):
    return pltpu.CompilerParams(dimension_semantics=("arbitrary",) * n_grid, vmem_limit_bytes=VMEM_LIMIT_BYTES)


def _tile(n, pref, unit=LANES):
    best = None
    t = unit
    while t <= min(n, pref):
        if n % t == 0:
            best = t
        t += unit
    return best if best is not None else n


def _sigmoid(z):
    return 1.0 / (1.0 + jnp.exp(-z))


def matmul(a, b, mode, out_dtype, name, res=None, tm=1024, tn=1024, tk=512):
    if mode == "nn":
        (M, K), (K2, N) = a.shape, b.shape
    elif mode == "nt":
        (M, K), (N, K2) = a.shape, b.shape
    else:
        (K, M), (K2, N) = a.shape, b.shape
    assert K == K2, (name, a.shape, b.shape)
    tm, tn, tk = _tile(M, tm), _tile(N, tn), _tile(K, tk)
    nk = K // tk
    if mode == "nn":
        a_spec, b_spec = BS((tm, tk), lambda i, j, k: (i, k)), BS((tk, tn), lambda i, j, k: (k, j))
        dims = (((1,), (0,)), ((), ()))
    elif mode == "nt":
        a_spec, b_spec = BS((tm, tk), lambda i, j, k: (i, k)), BS((tn, tk), lambda i, j, k: (j, k))
        dims = (((1,), (1,)), ((), ()))
    else:
        a_spec, b_spec = BS((tk, tm), lambda i, j, k: (k, i)), BS((tk, tn), lambda i, j, k: (k, j))
        dims = (((0,), (0,)), ((), ()))
    has_res = res is not None

    def body(*refs):
        if has_res:
            a_ref, b_ref, r_ref, o_ref, acc = refs
        else:
            a_ref, b_ref, o_ref, acc = refs
        k = pl.program_id(2)

        @pl.when(k == 0)
        def _():
            acc[...] = jnp.zeros_like(acc)

        acc[...] += lax.dot_general(a_ref[...].astype(BF16), b_ref[...].astype(BF16), dims,
                                    preferred_element_type=F32)

        @pl.when(k == nk - 1)
        def _():
            r = acc[...]
            if has_res:
                r = r + r_ref[...]
            o_ref[...] = r.astype(o_ref.dtype)

    in_specs = [a_spec, b_spec]
    args = [a, b]
    if has_res:
        in_specs.append(BS((tm, tn), lambda i, j, k: (i, j)))
        args.append(res)
    return pl.pallas_call(
        body, name=name, grid=(M // tm, N // tn, nk), in_specs=in_specs,
        out_specs=BS((tm, tn), lambda i, j, k: (i, j)), out_shape=SDS((M, N), out_dtype),
        scratch_shapes=[pltpu.VMEM((tm, tn), F32)], compiler_params=_params(3))(*args)


def rmsnorm_fwd(x, g, name):
    S, Dm = x.shape
    tm = _tile(S, 256, 8)

    def body(x_ref, g_ref, y_ref):
        xf = x_ref[...]
        r = lax.rsqrt(jnp.mean(xf * xf, axis=-1, keepdims=True) + EPS)
        y_ref[...] = ((xf * r) * g_ref[...]).astype(y_ref.dtype)

    return pl.pallas_call(
        body, name=name, grid=(S // tm,),
        in_specs=[BS((tm, Dm), lambda i: (i, 0)), BS((1, Dm), lambda i: (0, 0))],
        out_specs=BS((tm, Dm), lambda i: (i, 0)), out_shape=SDS((S, Dm), BF16),
        compiler_params=_params(1))(x, g.reshape(1, Dm))


def rmsnorm_bwd(x, g, dy, dres, name):
    S, Dm = x.shape
    tm = _tile(S, 256, 8)

    def body(x_ref, g_ref, dy_ref, dres_ref, dx_ref, dg_ref):
        xf = x_ref[...]
        r = lax.rsqrt(jnp.mean(xf * xf, axis=-1, keepdims=True) + EPS)
        xh = xf * r
        dy = dy_ref[...].astype(F32)
        dxh = dy * g_ref[...]
        dx_ref[...] = r * (dxh - xh * jnp.mean(dxh * xh, axis=-1, keepdims=True)) + dres_ref[...]

        @pl.when(pl.program_id(0) == 0)
        def _():
            dg_ref[...] = jnp.zeros_like(dg_ref)

        dg_ref[...] += jnp.sum(dy * xh, axis=0, keepdims=True)

    row = BS((tm, Dm), lambda i: (i, 0))
    vec = BS((1, Dm), lambda i: (0, 0))
    return pl.pallas_call(
        body, name=name, grid=(S // tm,), in_specs=[row, vec, row, row], out_specs=[row, vec],
        out_shape=[SDS((S, Dm), F32), SDS((1, Dm), F32)], compiler_params=_params(1))(x, g.reshape(1, Dm), dy, dres)


def final_loss(x, g, target, name):
    S, Dm = x.shape
    tm = _tile(S, 256, 8)

    def body(x_ref, g_ref, t_ref, loss_ref, dx_ref, dg_ref):
        xf = x_ref[...]
        r = lax.rsqrt(jnp.mean(xf * xf, axis=-1, keepdims=True) + EPS)
        xh = xf * r
        gv = g_ref[...]
        err = xh * gv - t_ref[...]
        dy = err * (1.0 / Dm)
        dxh = dy * gv
        dx_ref[...] = r * (dxh - xh * jnp.mean(dxh * xh, axis=-1, keepdims=True))

        @pl.when(pl.program_id(0) == 0)
        def _():
            dg_ref[...] = jnp.zeros_like(dg_ref)
            loss_ref[...] = jnp.zeros_like(loss_ref)

        dg_ref[...] += jnp.sum(dy * xh, axis=0, keepdims=True)
        part = 0.5 * jnp.sum(jnp.mean(err * err, axis=-1, keepdims=True), axis=0, keepdims=True)
        loss_ref[...] += jnp.broadcast_to(part, loss_ref.shape)

    row = BS((tm, Dm), lambda i: (i, 0))
    vec = BS((1, Dm), lambda i: (0, 0))
    return pl.pallas_call(
        body, name=name, grid=(S // tm,), in_specs=[row, vec, row],
        out_specs=[BS((1, LANES), lambda i: (0, 0)), row, vec],
        out_shape=[SDS((1, LANES), F32), SDS((S, Dm), F32), SDS((1, Dm), F32)],
        compiler_params=_params(1))(x, g.reshape(1, Dm), target)


def _dot_f32(a, b):
    return jnp.dot(a, b, preferred_element_type=F32, precision=lax.Precision.HIGHEST)


def _dot(a, b):
    return jnp.dot(a, b, preferred_element_type=F32)


def _dot_nt(a, b):
    return lax.dot_general(a, b, (((1,), (1,)), ((), ())), preferred_element_type=F32)


def fox_prep_fwd(af_t, bias, name):
    H, S = af_t.shape
    nc = S // LANES

    def body(a_ref, b_ref, f_ref):
        z = a_ref[...] + b_ref[...]
        logf = jnp.minimum(z, 0.0) - jnp.log(1.0 + jnp.exp(-jnp.abs(z)))
        row = lax.broadcasted_iota(jnp.int32, (LANES, LANES), 0)
        col = lax.broadcasted_iota(jnp.int32, (LANES, LANES), 1)
        upper = (row <= col).astype(F32)
        carry = jnp.zeros((H, 1), F32)
        for c in range(nc):
            chunk = logf[:, c * LANES:(c + 1) * LANES]
            f_ref[:, c * LANES:(c + 1) * LANES] = _dot_f32(chunk, upper) + carry
            carry = carry + jnp.sum(chunk, axis=1, keepdims=True)

    full = BS((H, S), lambda i: (0, 0))
    return pl.pallas_call(
        body, name=name, grid=(1,), in_specs=[full, BS((H, 1), lambda i: (0, 0))], out_specs=full,
        out_shape=SDS((H, S), F32), compiler_params=_params(1))(af_t, bias)


def fox_prep_bwd(af_t, bias, d_fq, d_fk, name):
    H, S = af_t.shape
    nc = S // LANES

    def body(a_ref, b_ref, dfq_ref, dfk_ref, da_ref, db_ref):
        row = lax.broadcasted_iota(jnp.int32, (LANES, LANES), 0)
        col = lax.broadcasted_iota(jnp.int32, (LANES, LANES), 1)
        lower = (row >= col).astype(F32)
        carry = jnp.zeros((H, 1), F32)
        dbias = jnp.zeros((H, 1), F32)
        for c in range(nc - 1, -1, -1):
            sl = slice(c * LANES, (c + 1) * LANES)
            chunk = dfq_ref[:, sl] + dfk_ref[:, sl]
            dlogf = _dot_f32(chunk, lower) + carry
            carry = carry + jnp.sum(chunk, axis=1, keepdims=True)
            z = a_ref[:, sl] + b_ref[...]
            da = dlogf * _sigmoid(-z)
            da_ref[:, sl] = da
            dbias = dbias + jnp.sum(da, axis=1, keepdims=True)
        db_ref[...] = dbias

    full = BS((H, S), lambda i: (0, 0))
    vec = BS((H, 1), lambda i: (0, 0))
    return pl.pallas_call(
        body, name=name, grid=(1,), in_specs=[full, vec, full, full], out_specs=[full, vec],
        out_shape=[SDS((H, S), F32), SDS((H, 1), F32)], compiler_params=_params(1))(af_t, bias, d_fq, d_fk)


FOX_T = 256


def fox_attn_fwd(q, k, v, f_col, f_row, name):
    H, S, Dh = q.shape
    T = min(FOX_T, S)
    nq = S // T
    scale = Dh ** -0.5

    def body(q_ref, k_ref, v_ref, fc_ref, fr_ref, o_ref, lse_ref):
        qi = pl.program_id(1)
        qv = q_ref[...]
        fq = fc_ref[...]
        rows = qi * T + lax.broadcasted_iota(jnp.int32, (T, T), 0)
        cols0 = lax.broadcasted_iota(jnp.int32, (T, T), 1)

        def step(j, carry):
            m, l, acc = carry
            off = pl.multiple_of(j * T, T)
            kv = k_ref[pl.ds(off, T), :]
            vv = v_ref[pl.ds(off, T), :]
            s = _dot_nt(qv, kv) * scale + (fq - fr_ref[j])
            s = jnp.where(cols0 + j * T <= rows, s, NEG)
            m_new = jnp.maximum(m, jnp.max(s, axis=-1, keepdims=True))
            p = jnp.exp(s - m_new)
            alpha = jnp.exp(m - m_new)
            l = alpha * l + jnp.sum(p, axis=-1, keepdims=True)
            acc = alpha * acc + _dot(p.astype(BF16), vv)
            return m_new, l, acc

        init = (jnp.full((T, 1), NEG, F32), jnp.zeros((T, 1), F32), jnp.zeros((T, Dh), F32))
        m, l, acc = lax.fori_loop(0, qi + 1, step, init)
        o_ref[...] = (acc / l).astype(o_ref.dtype)
        lse_ref[...] = m + jnp.log(l)

    blk = BS((None, T, Dh), lambda h, i: (h, i, 0))
    full = BS((None, S, Dh), lambda h, i: (h, 0, 0))
    colb = BS((None, T, 1), lambda h, i: (h, i, 0))
    rowf = BS((None, nq, 1, T), lambda h, i: (h, 0, 0, 0))
    return pl.pallas_call(
        body, name=name, grid=(H, nq), in_specs=[blk, full, full, colb, rowf], out_specs=[blk, colb],
        out_shape=[SDS((H, S, Dh), BF16), SDS((H, S, 1), F32)], compiler_params=_params(2))(q, k, v, f_col, f_row)


def fox_attn_bwd_dq(q, k, v, f_col, f_row, o, lse, do, name):
    H, S, Dh = q.shape
    T = min(FOX_T, S)
    nq = S // T
    scale = Dh ** -0.5

    def body(q_ref, k_ref, v_ref, fc_ref, fr_ref, o_ref, lse_ref, do_ref, dq_ref, dl_ref, df_ref):
        qi = pl.program_id(1)
        qv = q_ref[...]
        fq = fc_ref[...]
        lse_v = lse_ref[...]
        dov = do_ref[...]
        delta = jnp.sum(dov.astype(F32) * o_ref[...].astype(F32), axis=-1, keepdims=True)
        dl_ref[...] = delta
        rows = qi * T + lax.broadcasted_iota(jnp.int32, (T, T), 0)
        cols0 = lax.broadcasted_iota(jnp.int32, (T, T), 1)

        def step(j, carry):
            dq, dfq = carry
            off = pl.multiple_of(j * T, T)
            kv = k_ref[pl.ds(off, T), :]
            vv = v_ref[pl.ds(off, T), :]
            s = _dot_nt(qv, kv) * scale + (fq - fr_ref[j])
            s = jnp.where(cols0 + j * T <= rows, s, NEG)
            p = jnp.exp(s - lse_v)
            ds = p * (_dot_nt(dov, vv) - delta)
            return dq + _dot(ds.astype(BF16), kv), dfq + jnp.sum(ds, axis=-1, keepdims=True)

        dq, dfq = lax.fori_loop(0, qi + 1, step, (jnp.zeros((T, Dh), F32), jnp.zeros((T, 1), F32)))
        dq_ref[...] = dq * scale
        df_ref[...] = dfq

    blk = BS((None, T, Dh), lambda h, i: (h, i, 0))
    full = BS((None, S, Dh), lambda h, i: (h, 0, 0))
    colb = BS((None, T, 1), lambda h, i: (h, i, 0))
    rowf = BS((None, nq, 1, T), lambda h, i: (h, 0, 0, 0))
    return pl.pallas_call(
        body, name=name, grid=(H, nq), in_specs=[blk, full, full, colb, rowf, blk, colb, blk],
        out_specs=[blk, colb, colb], out_shape=[SDS((H, S, Dh), F32), SDS((H, S, 1), F32), SDS((H, S, 1), F32)],
        compiler_params=_params(2))(q, k, v, f_col, f_row, o, lse, do)


def fox_attn_bwd_dkv(q, k, v, f_col, f_row, lse_row, delta_row, do, name):
    H, S, Dh = q.shape
    T = min(FOX_T, S)
    nq = S // T
    scale = Dh ** -0.5

    def body(q_ref, k_ref, v_ref, fc_ref, fr_ref, lse_ref, dl_ref, do_ref, dk_ref, dv_ref, df_ref):
        kj = pl.program_id(1)
        kv = k_ref[...]
        vv = v_ref[...]
        fk = fc_ref[...]
        srow = kj * T + lax.broadcasted_iota(jnp.int32, (T, T), 0)
        tcol0 = lax.broadcasted_iota(jnp.int32, (T, T), 1)

        def step(i, carry):
            dk, dv, dfk = carry
            off = pl.multiple_of(i * T, T)
            qv = q_ref[pl.ds(off, T), :]
            dov = do_ref[pl.ds(off, T), :]
            st = _dot_nt(kv, qv) * scale + (fr_ref[i] - fk)
            st = jnp.where(srow <= tcol0 + i * T, st, NEG)
            pt = jnp.exp(st - lse_ref[i])
            dv = dv + _dot(pt.astype(BF16), dov)
            dst = pt * (_dot_nt(vv, dov) - dl_ref[i])
            dk = dk + _dot(dst.astype(BF16), qv)
            dfk = dfk + jnp.sum(dst, axis=-1, keepdims=True)
            return dk, dv, dfk

        init = (jnp.zeros((T, Dh), F32), jnp.zeros((T, Dh), F32), jnp.zeros((T, 1), F32))
        dk, dv, dfk = lax.fori_loop(kj, nq, step, init)
        dk_ref[...] = dk * scale
        dv_ref[...] = dv
        df_ref[...] = -dfk

    blk = BS((None, T, Dh), lambda h, i: (h, i, 0))
    full = BS((None, S, Dh), lambda h, i: (h, 0, 0))
    colb = BS((None, T, 1), lambda h, i: (h, i, 0))
    rowf = BS((None, nq, 1, T), lambda h, i: (h, 0, 0, 0))
    return pl.pallas_call(
        body, name=name, grid=(H, nq), in_specs=[full, blk, blk, colb, rowf, rowf, rowf, full],
        out_specs=[blk, blk, colb],
        out_shape=[SDS((H, S, Dh), F32), SDS((H, S, Dh), F32), SDS((H, S, 1), F32)],
        compiler_params=_params(2))(q, k, v, f_col, f_row, lse_row, delta_row, do)


HALO = 8
CONV_TM = 512


def _conv_ext(e, w_ref):
    return w_ref[0:1, :] * pltpu.roll(e, 2, 0) + w_ref[1:2, :] * pltpu.roll(e, 1, 0) + w_ref[2:3, :] * e


def _conv_t_ext(d, w_ref):
    n = d.shape[0]
    return w_ref[2:3, :] * d + w_ref[1:2, :] * pltpu.roll(d, n - 1, 0) + w_ref[0:1, :] * pltpu.roll(d, n - 2, 0)


def _time_specs(S, tm, width, col_block):
    per = tm // HALO
    last = S // HALO - 1
    cur = BS((tm, width), lambda j, i: (i, col_block(j)))
    prev = BS((HALO, width), lambda j, i: (jnp.maximum(i * per - 1, 0), col_block(j)))
    nxt = BS((HALO, width), lambda j, i: (jnp.minimum((i + 1) * per, last), col_block(j)))
    return cur, prev, nxt


def shortconv_fwd(proj, w, name):
    S = proj.shape[0]
    tm = _tile(S, CONV_TM, 8)
    nb = CONV_WIDTH // LANES
    b0 = OFF_B // LANES

    def body(xb_ref, xbp_ref, gb_ref, gc_ref, gcp_ref, w_ref, o_ref):
        keep = (pl.program_id(1) > 0).astype(F32)
        e = jnp.concatenate([gcp_ref[...] * xbp_ref[...] * keep, gc_ref[...] * xb_ref[...]], axis=0)
        o_ref[...] = (gb_ref[...] * _conv_ext(e, w_ref)[HALO:, :]).astype(o_ref.dtype)

    xb, xbp, _ = _time_specs(S, tm, LANES, lambda j: b0 + j)
    gb, _, _ = _time_specs(S, tm, LANES, lambda j: b0 + nb + j)
    gc, gcp, _ = _time_specs(S, tm, LANES, lambda j: b0 + 2 * nb + j)
    return pl.pallas_call(
        body, name=name, grid=(nb, S // tm),
        in_specs=[xb, xbp, gb, gc, gcp, BS((3, LANES), lambda j, i: (0, j))],
        out_specs=BS((tm, LANES), lambda j, i: (i, j)), out_shape=SDS((S, CONV_WIDTH), BF16),
        compiler_params=_params(2))(proj, proj, proj, proj, proj, w)


def shortconv_bwd(proj, w, do_b, name):
    S = proj.shape[0]
    tm = _tile(S, CONV_TM, 8)
    nt = S // tm
    nb = CONV_WIDTH // LANES
    b0 = OFF_B // LANES

    def body(xb_ref, xbp_ref, gb_ref, gbn_ref, gc_ref, gcp_ref, do_ref, don_ref, w_ref,
             dxb_ref, dgb_ref, dgc_ref, dw_ref):
        i = pl.program_id(1)
        keep_prev = (i > 0).astype(F32)
        keep_next = (i < nt - 1).astype(F32)
        xb, gb, gc = xb_ref[...], gb_ref[...], gc_ref[...]
        do = do_ref[...].astype(F32)
        u = gc * xb
        e = jnp.concatenate([gcp_ref[...] * xbp_ref[...] * keep_prev, u], axis=0)
        dgb_ref[...] = (do * _conv_ext(e, w_ref)[HALO:, :]).astype(dgb_ref.dtype)
        dcv = do * gb
        d_ext = jnp.concatenate([dcv, don_ref[...].astype(F32) * gbn_ref[...] * keep_next], axis=0)
        du = _conv_t_ext(d_ext, w_ref)[:tm, :]
        dgc_ref[...] = (du * xb).astype(dgc_ref.dtype)
        dxb_ref[...] = (du * gc).astype(dxb_ref.dtype)

        @pl.when(i == 0)
        def _():
            dw_ref[...] = jnp.zeros_like(dw_ref)

        dw_ref[0:1, :] += jnp.sum(dcv * pltpu.roll(e, 2, 0)[HALO:, :], axis=0, keepdims=True)
        dw_ref[1:2, :] += jnp.sum(dcv * pltpu.roll(e, 1, 0)[HALO:, :], axis=0, keepdims=True)
        dw_ref[2:3, :] += jnp.sum(dcv * u, axis=0, keepdims=True)

    xb, xbp, _ = _time_specs(S, tm, LANES, lambda j: b0 + j)
    gb, _, gbn = _time_specs(S, tm, LANES, lambda j: b0 + nb + j)
    gc, gcp, _ = _time_specs(S, tm, LANES, lambda j: b0 + 2 * nb + j)
    do, _, don = _time_specs(S, tm, LANES, lambda j: j)
    out = BS((tm, LANES), lambda j, i: (i, j))
    wspec = BS((3, LANES), lambda j, i: (0, j))
    return pl.pallas_call(
        body, name=name, grid=(nb, nt), in_specs=[xb, xbp, gb, gbn, gc, gcp, do, don, wspec],
        out_specs=[out, out, out, wspec],
        out_shape=[SDS((S, CONV_WIDTH), BF16)] * 3 + [SDS((3, CONV_WIDTH), F32)],
        compiler_params=_params(2))(proj, proj, proj, proj, proj, proj, do_b, do_b, w)


_GELU_C = 0.7978845608028654


def _gelu(x):
    return 0.5 * x * (1.0 + jnp.tanh(_GELU_C * (x + 0.044715 * x * x * x)))


def _gelu_grad(x):
    t = jnp.tanh(_GELU_C * (x + 0.044715 * x * x * x))
    return 0.5 * (1.0 + t) + 0.5 * x * (1.0 - t * t) * _GELU_C * (1.0 + 3.0 * 0.044715 * x * x)


def _tril_masks():
    row = lax.broadcasted_iota(jnp.int32, (SGU_CHUNK, SGU_CHUNK), 0)
    col = lax.broadcasted_iota(jnp.int32, (SGU_CHUNK, SGU_CHUNK), 1)
    return row >= col, row <= col


def _lane_column(mat, g):
    lane = lax.broadcasted_iota(jnp.int32, mat.shape, 1)
    return jnp.sum(jnp.where(lane == g, mat, 0.0), axis=-1, keepdims=True)


def sgu_fwd(proj, norm_g, w_s, b_t, name):
    S = proj.shape[0]
    T = SGU_CHUNK
    G = SGU_WIDTH // T
    c0 = OFF_C // (2 * SGU_WIDTH)

    def body(c_ref, g_ref, w_ref, b_ref, o_ref):
        u = _gelu(c_ref[:, 0:SGU_WIDTH])
        v = _gelu(c_ref[:, SGU_WIDTH:2 * SGU_WIDTH])
        r = lax.rsqrt(jnp.mean(v * v, axis=-1, keepdims=True) + EPS)
        vn = ((v * r) * g_ref[...]).astype(BF16)
        mask, _ = _tril_masks()
        bias = b_ref[...]
        for g in range(G):
            sl = slice(g * T, (g + 1) * T)
            wt = jnp.where(mask, w_ref[g], 0.0).astype(BF16)
            mixed = _dot(wt, vn[:, sl]) + _lane_column(bias, g)
            o_ref[:, sl] = (u[:, sl] * mixed).astype(o_ref.dtype)

    return pl.pallas_call(
        body, name=name, grid=(S // T,),
        in_specs=[BS((T, 2 * SGU_WIDTH), lambda i: (i, c0)), BS((1, SGU_WIDTH), lambda i: (0, 0)),
                  BS((G, T, T), lambda i: (0, 0, 0)), BS((T, T), lambda i: (0, 0))],
        out_specs=BS((T, SGU_WIDTH), lambda i: (i, 0)), out_shape=SDS((S, SGU_WIDTH), BF16),
        compiler_params=_params(1))(proj, norm_g, w_s, b_t)


def sgu_bwd(proj, norm_g, w_s, w_st, b_t, do_c, name):
    S = proj.shape[0]
    T = SGU_CHUNK
    G = SGU_WIDTH // T
    c0 = OFF_C // (2 * SGU_WIDTH)

    def body(c_ref, g_ref, w_ref, wt_ref, b_ref, do_ref, dc_ref, dg_ref, dw_ref, db_ref):
        cu = c_ref[:, 0:SGU_WIDTH]
        cv = c_ref[:, SGU_WIDTH:2 * SGU_WIDTH]
        u = _gelu(cu)
        v = _gelu(cv)
        r = lax.rsqrt(jnp.mean(v * v, axis=-1, keepdims=True) + EPS)
        xh = v * r
        gv = g_ref[...]
        vn = (xh * gv).astype(BF16)
        do = do_ref[...].astype(F32)
        mask, mask_t = _tril_masks()
        bias = b_ref[...]
        lane = lax.broadcasted_iota(jnp.int32, (T, T), 1)

        @pl.when(pl.program_id(0) == 0)
        def _():
            dg_ref[...] = jnp.zeros_like(dg_ref)
            dw_ref[...] = jnp.zeros_like(dw_ref)
            db_ref[...] = jnp.zeros_like(db_ref)

        dvn_parts = []
        du_parts = []
        dbias = jnp.zeros((T, T), F32)
        for g in range(G):
            sl = slice(g * T, (g + 1) * T)
            wt = jnp.where(mask, w_ref[g], 0.0).astype(BF16)
            mixed = _dot(wt, vn[:, sl]) + _lane_column(bias, g)
            du_parts.append(do[:, sl] * mixed)
            dmix = do[:, sl] * u[:, sl]
            dmix_b = dmix.astype(BF16)
            dw_ref[g] += jnp.where(mask, _dot_nt(dmix_b, vn[:, sl]), 0.0)
            dbias = dbias + jnp.where(lane == g, jnp.sum(dmix, axis=-1, keepdims=True), 0.0)
            wtt = jnp.where(mask_t, wt_ref[g], 0.0).astype(BF16)
            dvn_parts.append(_dot(wtt, dmix_b))
        db_ref[...] += dbias
        dvn = jnp.concatenate(dvn_parts, axis=-1)
        du = jnp.concatenate(du_parts, axis=-1)
        dg_ref[...] += jnp.sum(dvn * xh, axis=0, keepdims=True)
        dxh = dvn * gv
        dv = r * (dxh - xh * jnp.mean(dxh * xh, axis=-1, keepdims=True))
        dc_ref[:, 0:SGU_WIDTH] = (du * _gelu_grad(cu)).astype(dc_ref.dtype)
        dc_ref[:, SGU_WIDTH:2 * SGU_WIDTH] = (dv * _gelu_grad(cv)).astype(dc_ref.dtype)

    wspec = BS((G, T, T), lambda i: (0, 0, 0))
    gspec = BS((1, SGU_WIDTH), lambda i: (0, 0))
    return pl.pallas_call(
        body, name=name, grid=(S // T,),
        in_specs=[BS((T, 2 * SGU_WIDTH), lambda i: (i, c0)), gspec, wspec, wspec, BS((T, T), lambda i: (0, 0)),
                  BS((T, SGU_WIDTH), lambda i: (i, 0))],
        out_specs=[BS((T, 2 * SGU_WIDTH), lambda i: (i, 0)), gspec, wspec, BS((T, T), lambda i: (0, 0))],
        out_shape=[SDS((S, 2 * SGU_WIDTH), BF16), SDS((1, SGU_WIDTH), F32), SDS((G, T, T), F32), SDS((T, T), F32)],
        compiler_params=_params(1))(proj, norm_g, w_s, w_st, b_t, do_c)


def rope_tables(positions):
    half = ROPE_DIM // 2
    inv = ROPE_THETA ** (-jnp.arange(half, dtype=F32) * (2.0 / ROPE_DIM))
    ang = positions.astype(F32)[:, None] * inv
    cos, sin = jnp.cos(ang), jnp.sin(ang)
    S = positions.shape[0]
    ones = jnp.ones((S, HEAD_DIM - ROPE_DIM), F32)
    zeros = jnp.zeros((S, HEAD_DIM - ROPE_DIM), F32)
    zh = jnp.zeros((S, half), F32)
    c = jnp.concatenate([cos, cos, ones], axis=-1)
    sa = jnp.concatenate([zh, sin, zeros], axis=-1)
    sb = jnp.concatenate([-sin, zh, zeros], axis=-1)
    tile2 = lambda t: jnp.concatenate([t, t], axis=-1)
    return tile2(c), tile2(sa), tile2(sb)


def rope_apply(x, col0, ncols, cos, sa, sb, out_dtype, name):
    S = x.shape[0]
    tm = _tile(S, 512, 8)
    half = ROPE_DIM // 2
    b0 = col0 // LANES

    def body(x_ref, c_ref, sa_ref, sb_ref, o_ref):
        xv = x_ref[...].astype(F32)
        o_ref[...] = (xv * c_ref[...] + pltpu.roll(xv, half, 1) * sa_ref[...]
                      + pltpu.roll(xv, LANES - half, 1) * sb_ref[...]).astype(o_ref.dtype)

    tab = BS((tm, LANES), lambda i, j: (i, 0))
    return pl.pallas_call(
        body, name=name, grid=(S // tm, ncols // LANES),
        in_specs=[BS((tm, LANES), lambda i, j: (i, b0 + j)), tab, tab, tab],
        out_specs=BS((tm, LANES), lambda i, j: (i, j)), out_shape=SDS((S, ncols), out_dtype),
        compiler_params=_params(2))(x, cos, sa, sb)


def _dil_masks(first):
    qi = lax.broadcasted_iota(jnp.int32, (DIL_SPAN, DIL_SPAN), 0)
    ki = lax.broadcasted_iota(jnp.int32, (DIL_SPAN, DIL_SPAN), 1)
    return ki >= qi + first.astype(jnp.int32) * DIL_SPAN, ki <= qi


def dil_attn_fwd(q, k, v, nb, name):
    H, S, Dh = q.shape
    T = DIL_SPAN
    nblk = S // T
    scale = Dh ** -0.5

    def body(q_ref, kp_ref, kc_ref, vp_ref, vc_ref, o_ref, lse_ref):
        b = pl.program_id(1)
        mp, mc = _dil_masks(b % nb == 0)
        qv = q_ref[...]
        sp = jnp.where(mp, _dot_nt(qv, kp_ref[...]) * scale, NEG)
        sc = jnp.where(mc, _dot_nt(qv, kc_ref[...]) * scale, NEG)
        m = jnp.maximum(jnp.max(sp, axis=-1, keepdims=True), jnp.max(sc, axis=-1, keepdims=True))
        ep = jnp.exp(sp - m)
        ec = jnp.exp(sc - m)
        l = jnp.sum(ep, axis=-1, keepdims=True) + jnp.sum(ec, axis=-1, keepdims=True)
        o_ref[...] = (_dot(ep.astype(BF16), vp_ref[...]) + _dot(ec.astype(BF16), vc_ref[...])) / l
        lse_ref[...] = m + jnp.log(l)

    cur = BS((None, T, Dh), lambda h, b: (h, b, 0))
    prev = BS((None, T, Dh), lambda h, b: (h, jnp.maximum(b - 1, 0), 0))
    colb = BS((None, T, 1), lambda h, b: (h, b, 0))
    return pl.pallas_call(
        body, name=name, grid=(H, nblk), in_specs=[cur, prev, cur, prev, cur], out_specs=[cur, colb],
        out_shape=[SDS((H, S, Dh), F32), SDS((H, S, 1), F32)], compiler_params=_params(2))(q, k, k, v, v)


def dil_attn_bwd_dq(q, k, v, o, lse, do, dlse, nb, name):
    H, S, Dh = q.shape
    T = DIL_SPAN
    nblk = S // T
    scale = Dh ** -0.5

    def body(q_ref, kp_ref, kc_ref, vp_ref, vc_ref, o_ref, lse_ref, do_ref, dlse_ref, dq_ref, dl_ref):
        b = pl.program_id(1)
        mp, mc = _dil_masks(b % nb == 0)
        qv = q_ref[...]
        kp, kc = kp_ref[...], kc_ref[...]
        dov = do_ref[...]
        delta = jnp.sum(dov * o_ref[...], axis=-1, keepdims=True) - dlse_ref[...]
        dl_ref[...] = delta
        dob = dov.astype(BF16)
        lse_v = lse_ref[...]
        pp = jnp.exp(jnp.where(mp, _dot_nt(qv, kp) * scale, NEG) - lse_v)
        pc = jnp.exp(jnp.where(mc, _dot_nt(qv, kc) * scale, NEG) - lse_v)
        dsp = pp * (_dot_nt(dob, vp_ref[...]) - delta)
        dsc = pc * (_dot_nt(dob, vc_ref[...]) - delta)
        dq_ref[...] = (_dot(dsp.astype(BF16), kp) + _dot(dsc.astype(BF16), kc)) * scale

    cur = BS((None, T, Dh), lambda h, b: (h, b, 0))
    prev = BS((None, T, Dh), lambda h, b: (h, jnp.maximum(b - 1, 0), 0))
    colb = BS((None, T, 1), lambda h, b: (h, b, 0))
    return pl.pallas_call(
        body, name=name, grid=(H, nblk), in_specs=[cur, prev, cur, prev, cur, cur, colb, cur, colb],
        out_specs=[cur, colb], out_shape=[SDS((H, S, Dh), F32), SDS((H, S, 1), F32)],
        compiler_params=_params(2))(q, k, k, v, v, o, lse, do, dlse)


def dil_attn_bwd_dkv(q, k, v, lse_row, delta_row, do, nb, name):
    H, S, Dh = q.shape
    T = DIL_SPAN
    nblk = S // T
    scale = Dh ** -0.5

    def body(k_ref, v_ref, qc_ref, qn_ref, doc_ref, don_ref, lc_ref, ln_ref, dc_ref, dn_ref, dk_ref, dv_ref):
        c = pl.program_id(1)
        no_next = ((c + 1) % nb == 0).astype(jnp.int32)
        si = lax.broadcasted_iota(jnp.int32, (T, T), 0)
        ti = lax.broadcasted_iota(jnp.int32, (T, T), 1)
        m_cur = si <= ti
        m_next = si >= ti + no_next * T
        kv, vv = k_ref[...], v_ref[...]
        qc, qn = qc_ref[...], qn_ref[...]
        doc, don = doc_ref[...].astype(BF16), don_ref[...].astype(BF16)
        pt = jnp.exp(jnp.where(m_cur, _dot_nt(kv, qc) * scale, NEG) - lc_ref[...])
        ptn = jnp.exp(jnp.where(m_next, _dot_nt(kv, qn) * scale, NEG) - ln_ref[...])
        dv_ref[...] = _dot(pt.astype(BF16), doc) + _dot(ptn.astype(BF16), don)
        dst = pt * (_dot_nt(vv, doc) - dc_ref[...])
        dstn = ptn * (_dot_nt(vv, don) - dn_ref[...])
        dk_ref[...] = (_dot(dst.astype(BF16), qc) + _dot(dstn.astype(BF16), qn)) * scale

    cur = BS((None, T, Dh), lambda h, b: (h, b, 0))
    nxt = BS((None, T, Dh), lambda h, b: (h, jnp.minimum(b + 1, nblk - 1), 0))
    rcur = BS((None, None, 1, T), lambda h, b: (h, b, 0, 0))
    rnxt = BS((None, None, 1, T), lambda h, b: (h, jnp.minimum(b + 1, nblk - 1), 0, 0))
    return pl.pallas_call(
        body, name=name, grid=(H, nblk), in_specs=[cur, cur, cur, nxt, cur, nxt, rcur, rnxt, rcur, rnxt],
        out_specs=[cur, cur], out_shape=[SDS((H, S, Dh), F32), SDS((H, S, Dh), F32)],
        compiler_params=_params(2))(k, v, q, q, do, do, lse_row, lse_row, delta_row, delta_row)


def dil_merge_fwd(outs, lses, name):
    H, S, Dh = outs[0].shape
    tm = _tile(S, 512, 8)

    def body(o0, o1, o2, l0, l1, l2, out_ref):
        ls = [l0[...], l1[...], l2[...]]
        m = jnp.maximum(jnp.maximum(ls[0], ls[1]), ls[2])
        es = [jnp.exp(l - m) for l in ls]
        den = es[0] + es[1] + es[2]
        out_ref[...] = (es[0] * o0[...] + es[1] * o1[...] + es[2] * o2[...]) / den

    blk = BS((None, tm, Dh), lambda h, i: (h, i, 0))
    colb = BS((None, tm, 1), lambda h, i: (h, i, 0))
    return pl.pallas_call(
        body, name=name, grid=(H, S // tm), in_specs=[blk] * 3 + [colb] * 3, out_specs=blk,
        out_shape=SDS((H, S, Dh), F32), compiler_params=_params(2))(*outs, *lses)


def dil_merge_bwd(outs, lses, d_out, name):
    H, S, Dh = outs[0].shape
    tm = _tile(S, 512, 8)

    def body(o0, o1, o2, l0, l1, l2, d_ref, do0, do1, do2, dl0, dl1, dl2):
        ls = [l0[...], l1[...], l2[...]]
        m = jnp.maximum(jnp.maximum(ls[0], ls[1]), ls[2])
        es = [jnp.exp(l - m) for l in ls]
        den = es[0] + es[1] + es[2]
        ws = [e / den for e in es]
        dv = d_ref[...]
        dws = [jnp.sum(dv * o[...], axis=-1, keepdims=True) for o in (o0, o1, o2)]
        mean = ws[0] * dws[0] + ws[1] * dws[1] + ws[2] * dws[2]
        for w, dw, do_ref, dl_ref in zip(ws, dws, (do0, do1, do2), (dl0, dl1, dl2)):
            do_ref[...] = w * dv
            dl_ref[...] = w * (dw - mean)

    blk = BS((None, tm, Dh), lambda h, i: (h, i, 0))
    colb = BS((None, tm, 1), lambda h, i: (h, i, 0))
    return pl.pallas_call(
        body, name=name, grid=(H, S // tm), in_specs=[blk] * 3 + [colb] * 3 + [blk],
        out_specs=[blk] * 3 + [colb] * 3,
        out_shape=[SDS((H, S, Dh), F32)] * 3 + [SDS((H, S, 1), F32)] * 3,
        compiler_params=_params(2))(*outs, *lses, d_out)


MERGE_TN = 512


def merge_fwd(o_stack, w_br, proj, name):
    _, S, K = o_stack.shape
    tm = _tile(S, 512, 8)
    tn = MERGE_TN
    nj = D_MODEL // tn

    def body(o_ref, w_ref, gl_ref, m_ref, acc):
        br = pl.program_id(2)

        @pl.when(br == 0)
        def _():
            acc[...] = jnp.zeros_like(acc)

        acc[...] += _sigmoid(gl_ref[...]) * _dot(o_ref[...], w_ref[...])

        @pl.when(br == N_BRANCH - 1)
        def _():
            m_ref[...] = acc[...].astype(m_ref.dtype)

    return pl.pallas_call(
        body, name=name, grid=(S // tm, nj, N_BRANCH),
        in_specs=[BS((None, tm, K), lambda i, j, b: (b, i, 0)), BS((None, K, tn), lambda i, j, b: (b, 0, j)),
                  BS((tm, tn), lambda i, j, b: (i, b * nj + j))],
        out_specs=BS((tm, tn), lambda i, j, b: (i, j)), out_shape=SDS((S, D_MODEL), BF16),
        scratch_shapes=[pltpu.VMEM((tm, tn), F32)], compiler_params=_params(3))(o_stack, w_br, proj)


def merge_bwd(o_stack, w_br, proj, d_merged, name):
    _, S, K = o_stack.shape
    tm = _tile(S, 512, 8)
    tn = MERGE_TN
    nj = D_MODEL // tn

    def body(o_ref, w_ref, gl_ref, dm_ref, dgl_ref, dy_ref):
        gate = _sigmoid(gl_ref[...])
        y = _dot(o_ref[...], w_ref[...])
        dm = dm_ref[...].astype(F32)
        dgl_ref[...] = (dm * y * gate * (1.0 - gate)).astype(dgl_ref.dtype)
        dy_ref[...] = (dm * gate).astype(dy_ref.dtype)

    return pl.pallas_call(
        body, name=name, grid=(S // tm, nj, N_BRANCH),
        in_specs=[BS((None, tm, K), lambda i, j, b: (b, i, 0)), BS((None, K, tn), lambda i, j, b: (b, 0, j)),
                  BS((tm, tn), lambda i, j, b: (i, b * nj + j)), BS((tm, tn), lambda i, j, b: (i, j))],
        out_specs=[BS((tm, tn), lambda i, j, b: (i, b * nj + j)), BS((None, tm, tn), lambda i, j, b: (b, i, j))],
        out_shape=[SDS((S, N_BRANCH * D_MODEL), BF16), SDS((N_BRANCH, S, D_MODEL), BF16)],
        compiler_params=_params(3))(o_stack, w_br, proj, d_merged)


def ffn_mid_fwd(pre, w, name):
    S = pre.shape[0]
    tm = _tile(S, CONV_TM, 8)

    def body(p_ref, pp_ref, w_ref, a_ref):
        keep = (pl.program_id(1) > 0).astype(F32)
        e = jnp.concatenate([pp_ref[...] * keep, p_ref[...]], axis=0)
        up = _conv_ext(e, w_ref)[HALO:, :]
        ug, uv = up[:, 0:LANES], up[:, LANES:2 * LANES]
        a_ref[...] = (ug * _sigmoid(ug) * uv).astype(a_ref.dtype)

    cur, prev, _ = _time_specs(S, tm, 2 * LANES, lambda j: j)
    return pl.pallas_call(
        body, name=name, grid=(D_FF // LANES, S // tm),
        in_specs=[cur, prev, BS((3, 2 * LANES), lambda j, i: (0, j))],
        out_specs=BS((tm, LANES), lambda j, i: (i, j)), out_shape=SDS((S, D_FF), BF16),
        compiler_params=_params(2))(pre, pre, w)


def ffn_mid_bwd(pre, w, d_a, name):
    S = pre.shape[0]
    tm = _tile(S, CONV_TM, 8)
    nt = S // tm

    def body(p_ref, pp_ref, pn_ref, da_ref, dan_ref, w_ref, dp_ref, dw_ref):
        i = pl.program_id(1)
        keep_prev = (i > 0).astype(F32)
        keep_next = (i < nt - 1).astype(F32)
        pv = p_ref[...]
        e = jnp.concatenate([pp_ref[...] * keep_prev, pv, pn_ref[...]], axis=0)
        up = _conv_ext(e, w_ref)
        ug, uv = up[:, 0:LANES], up[:, LANES:2 * LANES]
        da = jnp.concatenate([jnp.zeros((HALO, LANES), F32), da_ref[...], dan_ref[...] * keep_next], axis=0)
        sg = _sigmoid(ug)
        dug = da * uv * (sg * (1.0 + ug * (1.0 - sg)))
        duv = da * (ug * sg)
        dup_e = jnp.concatenate([dug, duv], axis=-1)
        dp_ref[...] = _conv_t_ext(dup_e, w_ref)[HALO:HALO + tm, :].astype(dp_ref.dtype)
        dup = dup_e[HALO:HALO + tm, :]

        @pl.when(i == 0)
        def _():
            dw_ref[...] = jnp.zeros_like(dw_ref)

        dw_ref[0:1, :] += jnp.sum(dup * pltpu.roll(e, 2, 0)[HALO:HALO + tm, :], axis=0, keepdims=True)
        dw_ref[1:2, :] += jnp.sum(dup * pltpu.roll(e, 1, 0)[HALO:HALO + tm, :], axis=0, keepdims=True)
        dw_ref[2:3, :] += jnp.sum(dup * pv, axis=0, keepdims=True)

    cur, prev, nxt = _time_specs(S, tm, 2 * LANES, lambda j: j)
    dcur, _, dnxt = _time_specs(S, tm, LANES, lambda j: j)
    wspec = BS((3, 2 * LANES), lambda j, i: (0, j))
    return pl.pallas_call(
        body, name=name, grid=(D_FF // LANES, nt), in_specs=[cur, prev, nxt, dcur, dnxt, wspec],
        out_specs=[cur, wspec], out_shape=[SDS((S, 2 * D_FF), BF16), SDS((3, 2 * D_FF), F32)],
        compiler_params=_params(2))(pre, pre, pre, d_a, d_a, w)


def ple_fwd(x, a, e, name):
    S, Dm = x.shape
    tm = _tile(S, 256, 8)

    def body(x_ref, a_ref, e_ref, o_ref):
        o_ref[...] = x_ref[...] + _sigmoid(a_ref[...]) * e_ref[...]

    row = BS((tm, Dm), lambda i: (i, 0))
    return pl.pallas_call(body, name=name, grid=(S // tm,), in_specs=[row] * 3, out_specs=row,
                          out_shape=SDS((S, Dm), F32), compiler_params=_params(1))(x, a, e)


def ple_bwd(a, e, dx, name):
    S, Dm = a.shape
    tm = _tile(S, 256, 8)

    def body(a_ref, e_ref, dx_ref, da_ref, de_ref):
        s = _sigmoid(a_ref[...])
        d = dx_ref[...]
        da_ref[...] = (d * e_ref[...] * s * (1.0 - s)).astype(da_ref.dtype)
        de_ref[...] = (d * s).astype(de_ref.dtype)

    row = BS((tm, Dm), lambda i: (i, 0))
    return pl.pallas_call(body, name=name, grid=(S // tm,), in_specs=[row] * 3, out_specs=[row, row],
                          out_shape=[SDS((S, Dm), BF16)] * 2, compiler_params=_params(1))(a, e, dx)


def to_heads(x, n_heads, dil):
    S = x.shape[0]
    x = x.reshape(S // dil, dil, n_heads, HEAD_DIM).transpose(2, 1, 0, 3)
    return x.reshape(n_heads, S, HEAD_DIM)


def from_heads(y, dil):
    H, S, C = y.shape
    y = y.reshape(H, dil, S // dil, C).transpose(2, 1, 0, 3)
    return y.reshape(S, H * C)


def rows_of(col, t):
    H, S, _ = col.shape
    return col.reshape(H, S // t, 1, t)


def w_in_to_aligned(w):
    pad = jnp.zeros(w.shape[:-1] + (W_AL - D_IN,), w.dtype)
    return jnp.concatenate([w[..., SRC_G:D_IN], w[..., SRC_A:SRC_F], w[..., SRC_B:SRC_C], w[..., SRC_C:SRC_DQ],
                            w[..., SRC_DQ:SRC_G], w[..., SRC_F:SRC_B], pad], axis=-1)


def w_in_from_aligned(w):
    return jnp.concatenate([w[..., OFF_A:OFF_B], w[..., OFF_F:OFF_F + FOX_HEADS], w[..., OFF_B:OFF_C],
                            w[..., OFF_C:OFF_D], w[..., OFF_D:OFF_F], w[..., OFF_G:OFF_A]], axis=-1)


def ffn_interleave(w):
    lead = w.shape[:-1]
    g = w[..., :D_FF].reshape(lead + (D_FF // LANES, 1, LANES))
    v = w[..., D_FF:].reshape(lead + (D_FF // LANES, 1, LANES))
    return jnp.concatenate([g, v], axis=-2).reshape(lead + (2 * D_FF,))


def ffn_deinterleave(w):
    lead = w.shape[:-1]
    w = w.reshape(lead + (D_FF // LANES, 2, LANES))
    return jnp.concatenate([w[..., 0, :].reshape(lead + (D_FF,)), w[..., 1, :].reshape(lead + (D_FF,))], axis=-1)


def layer_fwd(x, p_l, rope, w, tag):
    S = x.shape[0]
    sv = {"x0": x}
    h = rmsnorm_fwd(x, w["norm_mix_g"], f"{tag}_norm_mix")
    proj = matmul(h, w["w_in_al"], "nn", F32, f"{tag}_proj")
    sv["h"], sv["proj"] = h, proj

    af_t = proj[:, OFF_F:OFF_F + FOX_HEADS].T
    f_cum = fox_prep_fwd(af_t, w["fox_forget_b"].reshape(FOX_HEADS, 1), f"{tag}_fox_prep")
    T = min(FOX_T, S)
    f_col = f_cum.reshape(FOX_HEADS, S, 1)
    f_row = f_cum.reshape(FOX_HEADS, S // T, 1, T)
    qkv = to_heads(proj[:, OFF_A:OFF_B].astype(BF16), 3 * FOX_HEADS, 1)
    qa, ka, va = qkv[:FOX_HEADS], qkv[FOX_HEADS:2 * FOX_HEADS], qkv[2 * FOX_HEADS:]
    oa_h, lse_a = fox_attn_fwd(qa, ka, va, f_col, f_row, f"{tag}_fox_fwd")
    o_a = from_heads(oa_h, 1)
    sv.update(af_t=af_t, f_col=f_col, f_row=f_row, qa=qa, ka=ka, va=va, oa_h=oa_h, lse_a=lse_a)

    o_b = shortconv_fwd(proj, w["shortconv_w"], f"{tag}_sconv_fwd")

    o_c = sgu_fwd(proj, w["sgu_norm_g"].reshape(1, SGU_WIDTH), w["sgu_w"], _sgu_bias(w["sgu_b"]), f"{tag}_sgu_fwd")

    cos, sa, sb = rope
    qk = rope_apply(proj, OFF_D, 2 * DIL_WIDTH, cos, sa, sb, BF16, f"{tag}_rope_fwd")
    vd = proj[:, OFF_D + 2 * DIL_WIDTH:OFF_D + 3 * DIL_WIDTH].astype(BF16)
    outs, lses, dil_sv = [], [], []
    for g, (window, dil) in enumerate(DIL_PATTERNS):
        sl = slice(g * DIL_OUT, (g + 1) * DIL_OUT)
        qg = to_heads(qk[:, sl], 4, dil)
        kg = to_heads(qk[:, DIL_WIDTH:][:, sl], 4, dil)
        vg = to_heads(vd[:, sl], 4, dil)
        nb = (S // dil) // DIL_SPAN
        og, lg = dil_attn_fwd(qg, kg, vg, nb, f"{tag}_dil{g}_fwd")
        dil_sv.append((qg, kg, vg, og, lg, nb))
        outs.append(_heads_unperm(og, dil))
        lses.append(_col_unperm(lg, dil))
    od_h = dil_merge_fwd(outs, lses, f"{tag}_dil_merge_fwd")
    o_d = from_heads(od_h, 1)
    sv.update(dil=dil_sv, outs=outs, lses=lses)

    o_d_pad = jnp.concatenate([o_d.astype(BF16), jnp.zeros((S, FOX_WIDTH - DIL_OUT), BF16)], axis=-1)
    o_stack = jnp.stack([o_a, o_b, o_c, o_d_pad], axis=0)
    merged = merge_fwd(o_stack, w["w_br"], proj, f"{tag}_merge_fwd")
    x1 = matmul(merged, w["w_out"], "nn", F32, f"{tag}_out_proj", res=x)
    sv.update(o_stack=o_stack, merged=merged, x1=x1)

    h2 = rmsnorm_fwd(x1, w["norm_ffn_g"], f"{tag}_norm_ffn")
    pre = matmul(h2, w["w_up_il"], "nn", F32, f"{tag}_up")
    a = ffn_mid_fwd(pre, w["ffn_conv_il"], f"{tag}_ffn_mid_fwd")
    x2 = matmul(a, w["w_down"], "nn", F32, f"{tag}_down", res=x1)
    sv.update(h2=h2, pre=pre, a=a, x2=x2)

    n3 = rmsnorm_fwd(x2, w["norm_ple_g"], f"{tag}_norm_ple")
    pg = matmul(n3, w["w_ple_gate"], "nn", F32, f"{tag}_ple_gate")
    pe = matmul(p_l, w["w_ple_proj"], "nn", F32, f"{tag}_ple_proj")
    x3 = ple_fwd(x2, pg, pe, f"{tag}_ple_fwd")
    sv.update(n3=n3, pg=pg, pe=pe, p_l=p_l)
    return x3, sv


def _sgu_bias(b):
    return jnp.pad(b.T, ((0, 0), (0, SGU_CHUNK - b.shape[0])))


def _col_unperm(col, dil):
    H, S, _ = col.shape
    return col.reshape(H, dil, S // dil).transpose(0, 2, 1).reshape(H, S, 1)


def _col_perm(col, dil):
    H, S, _ = col.shape
    return col.reshape(H, S // dil, dil).transpose(0, 2, 1).reshape(H, S, 1)


def _heads_perm(y, dil):
    H, S, C = y.shape
    return y.reshape(H, S // dil, dil, C).transpose(0, 2, 1, 3).reshape(H, S, C)


def _heads_unperm(y, dil):
    H, S, C = y.shape
    return y.reshape(H, dil, S // dil, C).transpose(0, 2, 1, 3).reshape(H, S, C)


def layer_bwd(dx3, sv, rope, w, tag):
    S = dx3.shape[0]
    gr = {}
    da, de = ple_bwd(sv["pg"], sv["pe"], dx3, f"{tag}_ple_bwd")
    gr["w_ple_proj"] = matmul(sv["p_l"], de, "tn", F32, f"{tag}_dw_ple_proj")
    gr["w_ple_gate"] = matmul(sv["n3"], da, "tn", F32, f"{tag}_dw_ple_gate")
    dn3 = matmul(da, w["w_ple_gate"], "nt", BF16, f"{tag}_dn3")
    dx2, gr["norm_ple_g"] = rmsnorm_bwd(sv["x2"], w["norm_ple_g"], dn3, dx3, f"{tag}_norm_ple_bwd")

    d_a = matmul(dx2, w["w_down"], "nt", F32, f"{tag}_da")
    gr["w_down"] = matmul(sv["a"], dx2, "tn", F32, f"{tag}_dw_down")
    dpre, gr["ffn_conv_il"] = ffn_mid_bwd(sv["pre"], w["ffn_conv_il"], d_a, f"{tag}_ffn_mid_bwd")
    gr["w_up_il"] = matmul(sv["h2"], dpre, "tn", F32, f"{tag}_dw_up")
    dh2 = matmul(dpre, w["w_up_il"], "nt", BF16, f"{tag}_dh2")
    dx1, gr["norm_ffn_g"] = rmsnorm_bwd(sv["x1"], w["norm_ffn_g"], dh2, dx2, f"{tag}_norm_ffn_bwd")

    d_merged = matmul(dx1, w["w_out"], "nt", BF16, f"{tag}_dmerged")
    gr["w_out"] = matmul(sv["merged"], dx1, "tn", F32, f"{tag}_dw_out")
    proj = sv["proj"]
    dgl, dy = merge_bwd(sv["o_stack"], w["w_br"], proj, d_merged, f"{tag}_merge_bwd")
    d_o, d_wbr = [], []
    for b in range(N_BRANCH):
        d_o.append(matmul(dy[b], w["w_br"][b], "nt", F32, f"{tag}_do{b}"))
        d_wbr.append(matmul(sv["o_stack"][b], dy[b], "tn", F32, f"{tag}_dw_br{b}"))
    gr["w_br"] = d_wbr

    do_a = to_heads(d_o[0].astype(BF16), FOX_HEADS, 1)
    T = min(FOX_T, S)
    dqa, delta_a, d_fq = fox_attn_bwd_dq(sv["qa"], sv["ka"], sv["va"], sv["f_col"], sv["f_row"], sv["oa_h"],
                                         sv["lse_a"], do_a, f"{tag}_fox_dq")
    dka, dva, d_fk = fox_attn_bwd_dkv(sv["qa"], sv["ka"], sv["va"], sv["f_col"], sv["f_row"], rows_of(sv["lse_a"], T),
                                      rows_of(delta_a, T), do_a, f"{tag}_fox_dkv")
    daf_t, dfb = fox_prep_bwd(sv["af_t"], w["fox_forget_b"].reshape(FOX_HEADS, 1), d_fq.reshape(FOX_HEADS, S),
                              d_fk.reshape(FOX_HEADS, S), f"{tag}_fox_prep_bwd")
    gr["fox_forget_b"] = dfb.reshape(FOX_HEADS)
    d_proj_a = from_heads(jnp.concatenate([dqa, dka, dva], axis=0), 1).astype(BF16)

    dxb, dgb, dgc, gr["shortconv_w"] = shortconv_bwd(proj, w["shortconv_w"], d_o[1], f"{tag}_sconv_bwd")

    d_c, dsg, dsw, dsb = sgu_bwd(proj, w["sgu_norm_g"].reshape(1, SGU_WIDTH), w["sgu_w"],
                                 jnp.swapaxes(w["sgu_w"], 1, 2), _sgu_bias(w["sgu_b"]), d_o[2], f"{tag}_sgu_bwd")
    gr["sgu_norm_g"] = dsg.reshape(SGU_WIDTH)
    gr["sgu_w"] = dsw
    gr["sgu_b"] = dsb[:, :SGU_WIDTH // SGU_CHUNK].T

    d_od = to_heads(d_o[3][:, :DIL_OUT], 4, 1)
    d_outs_lses = dil_merge_bwd(sv["outs"], sv["lses"], d_od, f"{tag}_dil_merge_bwd")
    d_outs, d_lses = d_outs_lses[:3], d_outs_lses[3:]
    dq_parts, dk_parts, dv_parts = [], [], []
    for g, (window, dil) in enumerate(DIL_PATTERNS):
        qg, kg, vg, og, lg, nb = sv["dil"][g]
        do_g = _heads_perm(d_outs[g], dil)
        dl_g = _col_perm(d_lses[g], dil)
        dqg, delta_g = dil_attn_bwd_dq(qg, kg, vg, og, lg, do_g, dl_g, nb, f"{tag}_dil{g}_dq")
        dkg, dvg = dil_attn_bwd_dkv(qg, kg, vg, rows_of(lg, DIL_SPAN), rows_of(delta_g, DIL_SPAN), do_g, nb,
                                    f"{tag}_dil{g}_dkv")
        dq_parts.append(from_heads(dqg, dil))
        dk_parts.append(from_heads(dkg, dil))
        dv_parts.append(from_heads(dvg, dil))
    cos, sa, sb = rope
    d_qk_rot = jnp.concatenate(dq_parts + dk_parts, axis=-1)
    d_qk = rope_apply(d_qk_rot, 0, 2 * DIL_WIDTH, cos, -sa, -sb, BF16, f"{tag}_rope_bwd")
    d_vd = jnp.concatenate(dv_parts, axis=-1).astype(BF16)

    d_f_cols = jnp.concatenate([daf_t.T.astype(BF16), jnp.zeros((S, W_AL - OFF_F - FOX_HEADS), BF16)], axis=-1)
    d_proj = jnp.concatenate([dgl, d_proj_a, dxb, dgb, dgc, d_c, d_qk, d_vd, d_f_cols], axis=-1)
    gr["w_in_al"] = matmul(sv["h"], d_proj, "tn", F32, f"{tag}_dw_in", tm=2048, tn=512)
    dh = matmul(d_proj, w["w_in_al"], "nt", BF16, f"{tag}_dh")
    dx0, gr["norm_mix_g"] = rmsnorm_bwd(sv["x0"], w["norm_mix_g"], dh, dx1, f"{tag}_norm_mix_bwd")
    return dx0, gr


def local_weights(full, layer):
    w = {n: full[n][layer] for n in REPLICATED}
    w["w_in_al"] = w_in_to_aligned(full["w_in"][layer]).astype(BF16)
    w["shortconv_w"] = full["shortconv_w"][layer].astype(F32)
    pad = jnp.zeros((FOX_WIDTH - DIL_OUT, D_MODEL), BF16)
    w["w_br"] = jnp.stack([full["w_br_fox"][layer].astype(BF16), full["w_br_conv"][layer].astype(BF16),
                           full["w_br_sgu"][layer].astype(BF16),
                           jnp.concatenate([full["w_br_dil"][layer].astype(BF16), pad], axis=0)], axis=0)
    w["w_out"] = full["w_out"][layer].astype(BF16)
    w["w_up_il"] = ffn_interleave(full["w_up"][layer]).astype(BF16)
    w["ffn_conv_il"] = ffn_interleave(full["ffn_conv_w"][layer]).astype(F32)
    w["w_down"] = full["w_down"][layer].astype(BF16)
    w["w_ple_gate"] = full["w_ple_gate"][layer].astype(BF16)
    w["w_ple_proj"] = full["w_ple_proj"][layer].astype(BF16)
    return w


def grads_to_reference_layout(gr):
    out = {n: gr[n] for n in ("fox_forget_b", "shortconv_w", "sgu_norm_g", "sgu_w", "sgu_b", "w_out", "w_down",
                              "w_ple_gate", "w_ple_proj")}
    out["norm_mix_g"] = gr["norm_mix_g"].reshape(D_MODEL)
    out["norm_ffn_g"] = gr["norm_ffn_g"].reshape(D_MODEL)
    out["norm_ple_g"] = gr["norm_ple_g"].reshape(D_MODEL)
    out["w_in"] = w_in_from_aligned(gr["w_in_al"])
    out["w_br_fox"], out["w_br_conv"], out["w_br_sgu"] = gr["w_br"][0], gr["w_br"][1], gr["w_br"][2]
    out["w_br_dil"] = gr["w_br"][3][:DIL_OUT]
    out["w_up"] = ffn_deinterleave(gr["w_up_il"])
    out["ffn_conv_w"] = ffn_deinterleave(gr["ffn_conv_il"])
    return out


def local_step(x, p, positions, full, final_norm_g, loss_target):
    depth = p.shape[0]
    rope = rope_tables(positions)
    saved, ws = [], []
    for layer in range(depth):
        w = local_weights(full, layer)
        x, sv = layer_fwd(x, p[layer].astype(BF16), rope, w, f"l{layer}")
        saved.append(sv)
        ws.append(w)
    loss_part, dx, dgf = final_loss(x, final_norm_g, loss_target, "final_loss")
    grads = [None] * depth
    for layer in range(depth - 1, -1, -1):
        dx, gr = layer_bwd(dx, saved[layer], rope, ws[layer], f"l{layer}")
        grads[layer] = grads_to_reference_layout(gr)
    return loss_part[0, 0], dx, grads, dgf.reshape(-1)


def _position():
    return lax.axis_index("x"), lax.axis_index("y"), lax.axis_index("c")


def _other_chips(x, y):
    return [(1 - x, y), (x, 1 - y), (1 - x, 1 - y)]


def _remote(src, dst, send_sem, recv_sem, device):
    return pltpu.make_async_remote_copy(src_ref=src, dst_ref=dst, send_sem=send_sem, recv_sem=recv_sem,
                                        device_id=device, device_id_type=MESH)


def gather_chip_shards(pack, name):
    R, C = pack.shape
    H = R // 2

    def body(src, out, send_sems, recv_sems, local_sem):
        x, y, c = _position()
        me = 2 * x + y
        sibling = (x, y, 1 - c)
        chips = _other_chips(x, y)

        def half(chip, core):
            return out.at[chip, pl.ds(core * H, H), :]

        local = pltpu.make_async_copy(src, out.at[me], local_sem)
        local.start()
        first = [_remote(src.at[pl.ds(c * H, H), :], half(me, c), send_sems.at[j], recv_sems.at[j], (px, py, c))
                 for j, (px, py) in enumerate(chips)]
        for cp in first:
            cp.start()
        passed = []
        for j, (px, py) in enumerate(chips):
            k = 2 * px + py
            _remote(half(k, c), half(k, c), send_sems.at[j], recv_sems.at[j], (px, py, c)).wait_recv()
            fwd = _remote(half(k, c), half(k, c), send_sems.at[3 + j], recv_sems.at[3 + j], sibling)
            fwd.start()
            passed.append(fwd)
        for j, (px, py) in enumerate(chips):
            k = 2 * px + py
            _remote(half(k, 1 - c), half(k, 1 - c), send_sems.at[3 + j], recv_sems.at[3 + j], sibling).wait_recv()
        for cp in first + passed:
            cp.wait_send()
        local.wait()

    return pl.pallas_call(
        body, name=name, in_specs=[ANY], out_specs=ANY, out_shape=SDS((N_CHIPS, R, C), pack.dtype),
        scratch_shapes=[pltpu.SemaphoreType.DMA((6,)), pltpu.SemaphoreType.DMA((6,)), pltpu.SemaphoreType.DMA(())],
    )(pack)


def swap_halves_with_sibling(g, name):
    n, R, C = g.shape
    H = R // 2

    def body(src, land, send_sem, recv_sem):
        x, y, c = _position()
        cp = _remote(src.at[:, pl.ds((1 - c) * H, H), :], land, send_sem, recv_sem, (x, y, 1 - c))
        cp.start()
        cp.wait()

    return pl.pallas_call(
        body, name=name, in_specs=[ANY], out_specs=ANY, out_shape=SDS((n, H, C), g.dtype),
        scratch_shapes=[pltpu.SemaphoreType.DMA(()), pltpu.SemaphoreType.DMA(())])(g)


def add_my_half(g, other, name):
    n, R, C = g.shape
    H = R // 2
    tr = _tile(H, 512, 16)
    nb = H // tr
    core = lax.axis_index("c").astype(jnp.int32).reshape(1)

    def body(c_ref, g_ref, o_ref, out_ref):
        out_ref[...] = (g_ref[...] + o_ref[...]).astype(out_ref.dtype)

    grid_spec = pltpu.PrefetchScalarGridSpec(
        num_scalar_prefetch=1, grid=(n, nb),
        in_specs=[BS((None, tr, C), lambda s, i, c_ref: (s, c_ref[0] * nb + i, 0)),
                  BS((None, tr, C), lambda s, i, c_ref: (s, i, 0))],
        out_specs=BS((None, tr, C), lambda s, i, c_ref: (s, i, 0)))
    return pl.pallas_call(body, name=name, grid_spec=grid_spec, out_shape=SDS((n, H, C), BF16),
                          compiler_params=_params(2))(core, g, other)


def exchange_slots_between_chips(part, name):
    n, H, C = part.shape

    def body(src, land, send_sems, recv_sems, local_sem):
        x, y, c = _position()
        me = 2 * x + y
        chips = _other_chips(x, y)
        local = pltpu.make_async_copy(src.at[me], land.at[me], local_sem)
        local.start()
        sends = [_remote(src.at[2 * px + py], land.at[me], send_sems.at[j], recv_sems.at[j], (px, py, c))
                 for j, (px, py) in enumerate(chips)]
        for cp in sends:
            cp.start()
        for j, (px, py) in enumerate(chips):
            k = 2 * px + py
            _remote(src.at[k], land.at[k], send_sems.at[j], recv_sems.at[j], (px, py, c)).wait_recv()
        for cp in sends:
            cp.wait_send()
        local.wait()

    return pl.pallas_call(
        body, name=name, in_specs=[ANY], out_specs=ANY, out_shape=SDS((n, H, C), part.dtype),
        scratch_shapes=[pltpu.SemaphoreType.DMA((3,)), pltpu.SemaphoreType.DMA((3,)), pltpu.SemaphoreType.DMA(())],
    )(part)


def sum_slots(parts, name):
    n, H, C = parts.shape
    tr = _tile(H, 256, 16)

    def body(p_ref, o_ref):
        acc = p_ref[0].astype(F32)
        for k in range(1, n):
            acc = acc + p_ref[k].astype(F32)
        o_ref[...] = acc

    return pl.pallas_call(
        body, name=name, grid=(H // tr,), in_specs=[BS((n, tr, C), lambda i: (0, i, 0))],
        out_specs=BS((tr, C), lambda i: (i, 0)), out_shape=SDS((H, C), F32), compiler_params=_params(1))(parts)


def join_halves_with_sibling(mine, name):
    H, C = mine.shape

    def body(src, out, send_sem, recv_sem, local_sem):
        x, y, c = _position()
        local = pltpu.make_async_copy(src, out.at[pl.ds(c * H, H), :], local_sem)
        local.start()
        cp = _remote(src, out.at[pl.ds(c * H, H), :], send_sem, recv_sem, (x, y, 1 - c))
        cp.start()
        _remote(src, out.at[pl.ds((1 - c) * H, H), :], send_sem, recv_sem, (x, y, 1 - c)).wait_recv()
        cp.wait_send()
        local.wait()

    return pl.pallas_call(
        body, name=name, in_specs=[ANY], out_specs=ANY, out_shape=SDS((2 * H, C), mine.dtype),
        scratch_shapes=[pltpu.SemaphoreType.DMA(()), pltpu.SemaphoreType.DMA(()), pltpu.SemaphoreType.DMA(())],
    )(mine)


def reduce_scatter_grads(g, tag):
    other = swap_halves_with_sibling(g, f"{tag}_swap")
    part = add_my_half(g, other, f"{tag}_pair_sum")
    landed = exchange_slots_between_chips(part, f"{tag}_ici")
    mine = sum_slots(landed, f"{tag}_chip_sum")
    return join_halves_with_sibling(mine, f"{tag}_join")


def gather_all_devices(pack, name):
    R, C = pack.shape

    def body(src, out, send_sems, recv_sems, local_sem):
        x, y, c = _position()
        me = 4 * x + 2 * y + c
        local = pltpu.make_async_copy(src, out.at[me], local_sem)
        local.start()
        peers = []
        for m in range(1, N_DEV):
            px = 1 - x if m & 4 else x
            py = 1 - y if m & 2 else y
            pc = 1 - c if m & 1 else c
            peers.append((px, py, pc))
        sends = [_remote(src, out.at[me], send_sems.at[j], recv_sems.at[j], peer) for j, peer in enumerate(peers)]
        for cp in sends:
            cp.start()
        for j, (px, py, pc) in enumerate(peers):
            k = 4 * px + 2 * py + pc
            _remote(src, out.at[k], send_sems.at[j], recv_sems.at[j], (px, py, pc)).wait_recv()
        for cp in sends:
            cp.wait_send()
        local.wait()

    return pl.pallas_call(
        body, name=name, in_specs=[ANY], out_specs=ANY, out_shape=SDS((N_DEV, R, C), pack.dtype),
        scratch_shapes=[pltpu.SemaphoreType.DMA((N_DEV - 1,)), pltpu.SemaphoreType.DMA((N_DEV - 1,)),
                        pltpu.SemaphoreType.DMA(())])(pack)


def adamw(w, g, m, v, name):
    shape = w.shape
    cols = shape[-1] if len(shape) > 1 else shape[0]
    rows = w.size // cols
    two = lambda t: t.reshape(rows, cols)
    tr = rows
    if rows * cols * 4 > (1 << 21):
        tr = _tile(rows, max(8, ((1 << 21) // (cols * 4)) // 8 * 8), 8)
    c1 = 1.0 / (1.0 - ADAM_B1 ** ADAM_STEP)
    c2 = 1.0 / (1.0 - ADAM_B2 ** ADAM_STEP)

    def body(w_ref, g_ref, m_ref, v_ref, d_ref, mo_ref, vo_ref):
        gv = g_ref[...]
        mn = ADAM_B1 * m_ref[...] + (1.0 - ADAM_B1) * gv
        vn = ADAM_B2 * v_ref[...] + (1.0 - ADAM_B2) * (gv * gv)
        d_ref[...] = -ADAM_LR * ((mn * c1) / (jnp.sqrt(vn * c2) + ADAM_EPS) + ADAM_WD * w_ref[...])
        mo_ref[...] = mn
        vo_ref[...] = vn

    blk = BS((tr, cols), lambda i: (i, 0))
    d, mo, vo = pl.pallas_call(
        body, name=name, grid=(rows // tr,), in_specs=[blk] * 4, out_specs=[blk] * 3,
        out_shape=[SDS((rows, cols), F32)] * 3, compiler_params=_params(1))(two(w), two(g), two(m), two(v))
    return d.reshape(shape), mo.reshape(shape), vo.reshape(shape)


def _rows_for(n, unit):
    rows = -(-n // PACK_COLS)
    return -(-rows // unit) * unit


def pack_shards(shards, dtype):
    flat = jnp.concatenate([s.astype(dtype).reshape(-1) for s in shards])
    R = _rows_for(flat.shape[0], 32)
    flat = jnp.concatenate([flat, jnp.zeros((R * PACK_COLS - flat.shape[0],), dtype)])
    return flat.reshape(R, PACK_COLS)


def unpack_full(gathered, names, shard_shapes):
    flat = gathered.reshape(N_CHIPS, -1)
    out, off = {}, 0
    for n in names:
        shp = shard_shapes[n]
        size = shp[0] * shp[1]
        t = flat[:, off:off + size].reshape(N_CHIPS, shp[0], shp[1])
        if n in ROW_SHARDED:
            out[n] = t.reshape(N_CHIPS * shp[0], shp[1])
        else:
            out[n] = t.transpose(1, 0, 2).reshape(shp[0], N_CHIPS * shp[1])
        off += size
    return out


def pack_grad_slots(grads, names, shard_shapes):
    parts = []
    for n in names:
        shp = shard_shapes[n]
        g = grads[n]
        if n in ROW_SHARDED:
            parts.append(g.reshape(N_CHIPS, shp[0] * shp[1]))
        else:
            parts.append(g.reshape(shp[0], N_CHIPS, shp[1]).transpose(1, 0, 2).reshape(N_CHIPS, shp[0] * shp[1]))
    flat = jnp.concatenate(parts, axis=1)
    R = _rows_for(flat.shape[1], 32)
    flat = jnp.concatenate([flat, jnp.zeros((N_CHIPS, R * PACK_COLS - flat.shape[1]), F32)], axis=1)
    return flat.reshape(N_CHIPS, R, PACK_COLS)


def unpack_shards(pack, names, shard_shapes):
    flat = pack.reshape(-1)
    out, off = {}, 0
    for n in names:
        shp = shard_shapes[n]
        size = shp[0] * shp[1]
        out[n] = flat[off:off + size].reshape(shp)
        off += size
    return out


REPL_SHAPES = {"norm_mix_g": (D_MODEL,), "fox_forget_b": (FOX_HEADS,), "sgu_norm_g": (SGU_WIDTH,),
               "sgu_w": (4, SGU_CHUNK, SGU_CHUNK), "sgu_b": (4, SGU_CHUNK), "norm_ffn_g": (D_MODEL,),
               "norm_ple_g": (D_MODEL,)}


def kernel(x, p, positions, norm_mix_g, w_in, fox_forget_b, shortconv_w, sgu_norm_g, sgu_w, sgu_b, w_br_fox, w_br_conv, w_br_sgu, w_br_dil, w_out, norm_ffn_g, w_up, ffn_conv_w, w_down, norm_ple_g, w_ple_gate, w_ple_proj, final_norm_g, loss_target, m_norm_mix_g, m_w_in, m_fox_forget_b, m_shortconv_w, m_sgu_norm_g, m_sgu_w, m_sgu_b, m_w_br_fox, m_w_br_conv, m_w_br_sgu, m_w_br_dil, m_w_out, m_norm_ffn_g, m_w_up, m_ffn_conv_w, m_w_down, m_norm_ple_g, m_w_ple_gate, m_w_ple_proj, m_final_norm_g, v_norm_mix_g, v_w_in, v_fox_forget_b, v_shortconv_w, v_sgu_norm_g, v_sgu_w, v_sgu_b, v_w_br_fox, v_w_br_conv, v_w_br_sgu, v_w_br_dil, v_w_out, v_norm_ffn_g, v_w_up, v_ffn_conv_w, v_w_down, v_norm_ple_g, v_w_ple_gate, v_w_ple_proj, v_final_norm_g):
    weights = dict(norm_mix_g=norm_mix_g, w_in=w_in, fox_forget_b=fox_forget_b, shortconv_w=shortconv_w,
                   sgu_norm_g=sgu_norm_g, sgu_w=sgu_w, sgu_b=sgu_b, w_br_fox=w_br_fox, w_br_conv=w_br_conv,
                   w_br_sgu=w_br_sgu, w_br_dil=w_br_dil, w_out=w_out, norm_ffn_g=norm_ffn_g, w_up=w_up,
                   ffn_conv_w=ffn_conv_w, w_down=w_down, norm_ple_g=norm_ple_g, w_ple_gate=w_ple_gate,
                   w_ple_proj=w_ple_proj, final_norm_g=final_norm_g)
    mom1 = dict(norm_mix_g=m_norm_mix_g, w_in=m_w_in, fox_forget_b=m_fox_forget_b, shortconv_w=m_shortconv_w,
                sgu_norm_g=m_sgu_norm_g, sgu_w=m_sgu_w, sgu_b=m_sgu_b, w_br_fox=m_w_br_fox, w_br_conv=m_w_br_conv,
                w_br_sgu=m_w_br_sgu, w_br_dil=m_w_br_dil, w_out=m_w_out, norm_ffn_g=m_norm_ffn_g, w_up=m_w_up,
                ffn_conv_w=m_ffn_conv_w, w_down=m_w_down, norm_ple_g=m_norm_ple_g, w_ple_gate=m_w_ple_gate,
                w_ple_proj=m_w_ple_proj, final_norm_g=m_final_norm_g)
    mom2 = dict(norm_mix_g=v_norm_mix_g, w_in=v_w_in, fox_forget_b=v_fox_forget_b, shortconv_w=v_shortconv_w,
                sgu_norm_g=v_sgu_norm_g, sgu_w=v_sgu_w, sgu_b=v_sgu_b, w_br_fox=v_w_br_fox, w_br_conv=v_w_br_conv,
                w_br_sgu=v_w_br_sgu, w_br_dil=v_w_br_dil, w_out=v_w_out, norm_ffn_g=v_norm_ffn_g, w_up=v_w_up,
                ffn_conv_w=v_ffn_conv_w, w_down=v_w_down, norm_ple_g=v_norm_ple_g, w_ple_gate=v_w_ple_gate,
                w_ple_proj=v_w_ple_proj, final_norm_g=v_final_norm_g)
    depth = w_in.shape[0]
    shard_shapes = {n: tuple(weights[n].shape[1:]) for n in SHARDED}

    per_layer = []
    for layer in range(depth):
        pack = pack_shards([weights[n][layer] for n in SHARDED], BF16)
        gathered = gather_chip_shards(pack, f"gather_w{layer}")
        per_layer.append(unpack_full(gathered, SHARDED, shard_shapes))
    full = {n: [per_layer[layer][n] for layer in range(depth)] for n in SHARDED}
    for n in REPLICATED:
        full[n] = weights[n]

    loss_part, grad_x, grads, d_final = local_step(x[0], p[:, 0], positions[0], full, final_norm_g, loss_target[0])
    loss = lax.psum(loss_part, ("x", "y", "c"))

    g_sh = {n: [] for n in SHARDED}
    for layer in range(depth):
        slots = pack_grad_slots(grads[layer], SHARDED, shard_shapes)
        mine = reduce_scatter_grads(slots, f"rs{layer}")
        got = unpack_shards(mine, SHARDED, shard_shapes)
        for n in SHARDED:
            g_sh[n].append(got[n])
    grad_w = {n: jnp.stack(g_sh[n], axis=0) for n in SHARDED}

    flat = jnp.concatenate([grads[layer][n].astype(F32).reshape(-1) for layer in range(depth) for n in REPLICATED]
                           + [d_final])
    Rr = _rows_for(flat.shape[0], 16)
    packed = jnp.concatenate([flat, jnp.zeros((Rr * PACK_COLS - flat.shape[0],), F32)]).reshape(Rr, PACK_COLS)
    total = sum_slots(gather_all_devices(packed, "gather_repl"), "sum_repl").reshape(-1)
    off = 0
    g_rep = {n: [] for n in REPLICATED}
    for layer in range(depth):
        for n in REPLICATED:
            size = 1
            for s in REPL_SHAPES[n]:
                size *= s
            g_rep[n].append(total[off:off + size].reshape(REPL_SHAPES[n]))
            off += size
    for n in REPLICATED:
        grad_w[n] = jnp.stack(g_rep[n], axis=0)
    grad_w["final_norm_g"] = total[off:off + D_MODEL]

    deltas, new_m, new_v = {}, {}, {}
    for n in WEIGHTS:
        deltas[n], new_m[n], new_v[n] = adamw(weights[n], grad_w[n], mom1[n], mom2[n], f"adamw_{n}")
    return (loss, grad_x[None], *[grad_w[n] for n in WEIGHTS], *[deltas[n] for n in WEIGHTS],
            *[new_m[n] for n in WEIGHTS], *[new_v[n] for n in WEIGHTS])
```

```python
import functools

import jax
import jax.numpy as jnp
from jax import lax
from jax.experimental import pallas as pl
from jax.experimental.pallas import tpu as pltpu

F32 = jnp.float32
BF16 = jnp.bfloat16
MESH = pl.DeviceIdType.MESH
BS = pl.BlockSpec
SDS = jax.ShapeDtypeStruct
ANY = pl.BlockSpec(memory_space=pl.ANY)

VMEM_LIMIT_BYTES = 52 * 1024 * 1024
LANES = 128

D_MODEL = 2048
HEAD_DIM = 64
EPS = 1e-6
NEG = -1e30
FOX_HEADS = 8
FOX_WIDTH = 512
CONV_WIDTH = 512
SGU_WIDTH = 512
SGU_CHUNK = 128
DIL_PATTERNS = ((128, 1), (512, 4), (2048, 16))
DIL_SPAN = 128
DIL_HEADS = 12
DIL_WIDTH = 768
DIL_OUT = 256
ROPE_THETA = 500000.0
ROPE_DIM = 16
N_BRANCH = 4
D_FF = 5632
PLE_DIM = 256
D_IN = 14600

OFF_G, OFF_A, OFF_B, OFF_C, OFF_D, OFF_F, W_AL = 0, 8192, 9728, 11264, 12288, 14592, 14848
SRC_A, SRC_F, SRC_B, SRC_C, SRC_DQ, SRC_G = 0, 1536, 1544, 3080, 4104, 6408

ADAM_LR, ADAM_B1, ADAM_B2, ADAM_EPS, ADAM_WD, ADAM_STEP = 0.001, 0.9, 0.999, 1e-08, 0.01, 10

PACK_COLS = 1024
N_CHIPS = 4
N_DEV = 8

SHARDED = ("w_in", "shortconv_w", "w_br_fox", "w_br_conv", "w_br_sgu", "w_br_dil", "w_out", "w_up",
           "ffn_conv_w", "w_down", "w_ple_gate", "w_ple_proj")
ROW_SHARDED = ("w_out", "w_down", "w_ple_gate")
REPLICATED = ("norm_mix_g", "fox_forget_b", "sgu_norm_g", "sgu_w", "sgu_b", "norm_ffn_g", "norm_ple_g")
WEIGHTS = ("norm_mix_g", "w_in", "fox_forget_b", "shortconv_w", "sgu_norm_g", "sgu_w", "sgu_b", "w_br_fox",
           "w_br_conv", "w_br_sgu", "w_br_dil", "w_out", "norm_ffn_g", "w_up", "ffn_conv_w", "w_down",
           "norm_ple_g", "w_ple_gate", "w_ple_proj", "final_norm_g")


def _params(n_grid):
    return pltpu.CompilerParams(dimension_semantics=("arbitrary",) * n_grid, vmem_limit_bytes=VMEM_LIMIT_BYTES)


def _tile(n, pref, unit=LANES):
    best = None
    t = unit
    while t <= min(n, pref):
        if n % t == 0:
            best = t
        t += unit
    return best if best is not None else n


def _sigmoid(z):
    return 1.0 / (1.0 + jnp.exp(-z))


MAX_RESIDENT_K = 2048


def matmul(a, b, mode, out_dtype, name, res=None, tm=1024, tn=1024, tk=512):
    if mode == "nn":
        (M, K), (K2, N) = a.shape, b.shape
    elif mode == "nt":
        (M, K), (N, K2) = a.shape, b.shape
    else:
        (K, M), (K2, N) = a.shape, b.shape
    assert K == K2, (name, a.shape, b.shape)
    if mode != "tn" and K <= MAX_RESIDENT_K:
        tk = K
    tm, tn, tk = _tile(M, tm), _tile(N, tn), _tile(K, tk)
    nk = K // tk
    if mode == "nn":
        a_spec, b_spec = BS((tm, tk), lambda i, j, k: (i, k)), BS((tk, tn), lambda i, j, k: (k, j))
        dims = (((1,), (0,)), ((), ()))
    elif mode == "nt":
        a_spec, b_spec = BS((tm, tk), lambda i, j, k: (i, k)), BS((tn, tk), lambda i, j, k: (j, k))
        dims = (((1,), (1,)), ((), ()))
    else:
        a_spec, b_spec = BS((tk, tm), lambda i, j, k: (k, i)), BS((tk, tn), lambda i, j, k: (k, j))
        dims = (((0,), (0,)), ((), ()))
    has_res = res is not None

    def body(*refs):
        if has_res:
            a_ref, b_ref, r_ref, o_ref, acc = refs
        else:
            a_ref, b_ref, o_ref, acc = refs
        k = pl.program_id(2)

        @pl.when(k == 0)
        def _():
            acc[...] = jnp.zeros_like(acc)

        acc[...] += lax.dot_general(a_ref[...].astype(BF16), b_ref[...].astype(BF16), dims,
                                    preferred_element_type=F32)

        @pl.when(k == nk - 1)
        def _():
            r = acc[...]
            if has_res:
                r = r + r_ref[...]
            o_ref[...] = r.astype(o_ref.dtype)

    in_specs = [a_spec, b_spec]
    args = [a, b]
    if has_res:
        in_specs.append(BS((tm, tn), lambda i, j, k: (i, j)))
        args.append(res)
    return pl.pallas_call(
        body, name=name, grid=(M // tm, N // tn, nk), in_specs=in_specs,
        out_specs=BS((tm, tn), lambda i, j, k: (i, j)), out_shape=SDS((M, N), out_dtype),
        scratch_shapes=[pltpu.VMEM((tm, tn), F32)], compiler_params=_params(3))(*args)


def rmsnorm_fwd(x, g, name):
    S, Dm = x.shape
    tm = _tile(S, 256, 8)

    def body(x_ref, g_ref, y_ref):
        xf = x_ref[...]
        r = lax.rsqrt(jnp.mean(xf * xf, axis=-1, keepdims=True) + EPS)
        y_ref[...] = ((xf * r) * g_ref[...]).astype(y_ref.dtype)

    return pl.pallas_call(
        body, name=name, grid=(S // tm,),
        in_specs=[BS((tm, Dm), lambda i: (i, 0)), BS((1, Dm), lambda i: (0, 0))],
        out_specs=BS((tm, Dm), lambda i: (i, 0)), out_shape=SDS((S, Dm), BF16),
        compiler_params=_params(1))(x, g.reshape(1, Dm))


def rmsnorm_bwd(x, g, dy, dres, name):
    S, Dm = x.shape
    tm = _tile(S, 256, 8)

    def body(x_ref, g_ref, dy_ref, dres_ref, dx_ref, dg_ref):
        xf = x_ref[...]
        r = lax.rsqrt(jnp.mean(xf * xf, axis=-1, keepdims=True) + EPS)
        xh = xf * r
        dy = dy_ref[...].astype(F32)
        dxh = dy * g_ref[...]
        dx_ref[...] = r * (dxh - xh * jnp.mean(dxh * xh, axis=-1, keepdims=True)) + dres_ref[...]

        @pl.when(pl.program_id(0) == 0)
        def _():
            dg_ref[...] = jnp.zeros_like(dg_ref)

        dg_ref[...] += jnp.sum(dy * xh, axis=0, keepdims=True)

    row = BS((tm, Dm), lambda i: (i, 0))
    vec = BS((1, Dm), lambda i: (0, 0))
    return pl.pallas_call(
        body, name=name, grid=(S // tm,), in_specs=[row, vec, row, row], out_specs=[row, vec],
        out_shape=[SDS((S, Dm), F32), SDS((1, Dm), F32)], compiler_params=_params(1))(x, g.reshape(1, Dm), dy, dres)


def final_loss(x, g, target, name):
    S, Dm = x.shape
    tm = _tile(S, 256, 8)

    def body(x_ref, g_ref, t_ref, loss_ref, dx_ref, dg_ref):
        xf = x_ref[...]
        r = lax.rsqrt(jnp.mean(xf * xf, axis=-1, keepdims=True) + EPS)
        xh = xf * r
        gv = g_ref[...]
        err = xh * gv - t_ref[...]
        dy = err * (1.0 / Dm)
        dxh = dy * gv
        dx_ref[...] = r * (dxh - xh * jnp.mean(dxh * xh, axis=-1, keepdims=True))

        @pl.when(pl.program_id(0) == 0)
        def _():
            dg_ref[...] = jnp.zeros_like(dg_ref)
            loss_ref[...] = jnp.zeros_like(loss_ref)

        dg_ref[...] += jnp.sum(dy * xh, axis=0, keepdims=True)
        part = 0.5 * jnp.sum(jnp.mean(err * err, axis=-1, keepdims=True), axis=0, keepdims=True)
        loss_ref[...] += jnp.broadcast_to(part, loss_ref.shape)

    row = BS((tm, Dm), lambda i: (i, 0))
    vec = BS((1, Dm), lambda i: (0, 0))
    return pl.pallas_call(
        body, name=name, grid=(S // tm,), in_specs=[row, vec, row],
        out_specs=[BS((1, LANES), lambda i: (0, 0)), row, vec],
        out_shape=[SDS((1, LANES), F32), SDS((S, Dm), F32), SDS((1, Dm), F32)],
        compiler_params=_params(1))(x, g.reshape(1, Dm), target)


def _dot_f32(a, b):
    return jnp.dot(a, b, preferred_element_type=F32, precision=lax.Precision.HIGHEST)


def _dot(a, b):
    return jnp.dot(a, b, preferred_element_type=F32)


def _dot_nt(a, b):
    return lax.dot_general(a, b, (((1,), (1,)), ((), ())), preferred_element_type=F32)


def fox_prep_fwd(af_t, bias, name):
    H, S = af_t.shape
    nc = S // LANES

    def body(a_ref, b_ref, f_ref):
        z = a_ref[...] + b_ref[...]
        logf = jnp.minimum(z, 0.0) - jnp.log(1.0 + jnp.exp(-jnp.abs(z)))
        row = lax.broadcasted_iota(jnp.int32, (LANES, LANES), 0)
        col = lax.broadcasted_iota(jnp.int32, (LANES, LANES), 1)
        upper = (row <= col).astype(F32)
        carry = jnp.zeros((H, 1), F32)
        for c in range(nc):
            chunk = logf[:, c * LANES:(c + 1) * LANES]
            f_ref[:, c * LANES:(c + 1) * LANES] = _dot_f32(chunk, upper) + carry
            carry = carry + jnp.sum(chunk, axis=1, keepdims=True)

    full = BS((H, S), lambda i: (0, 0))
    return pl.pallas_call(
        body, name=name, grid=(1,), in_specs=[full, BS((H, 1), lambda i: (0, 0))], out_specs=full,
        out_shape=SDS((H, S), F32), compiler_params=_params(1))(af_t, bias)


def fox_prep_bwd(af_t, bias, d_fq, d_fk, name):
    H, S = af_t.shape
    nc = S // LANES

    def body(a_ref, b_ref, dfq_ref, dfk_ref, da_ref, db_ref):
        row = lax.broadcasted_iota(jnp.int32, (LANES, LANES), 0)
        col = lax.broadcasted_iota(jnp.int32, (LANES, LANES), 1)
        lower = (row >= col).astype(F32)
        carry = jnp.zeros((H, 1), F32)
        dbias = jnp.zeros((H, 1), F32)
        for c in range(nc - 1, -1, -1):
            sl = slice(c * LANES, (c + 1) * LANES)
            chunk = dfq_ref[:, sl] + dfk_ref[:, sl]
            dlogf = _dot_f32(chunk, lower) + carry
            carry = carry + jnp.sum(chunk, axis=1, keepdims=True)
            z = a_ref[:, sl] + b_ref[...]
            da = dlogf * _sigmoid(-z)
            da_ref[:, sl] = da
            dbias = dbias + jnp.sum(da, axis=1, keepdims=True)
        db_ref[...] = dbias

    full = BS((H, S), lambda i: (0, 0))
    vec = BS((H, 1), lambda i: (0, 0))
    return pl.pallas_call(
        body, name=name, grid=(1,), in_specs=[full, vec, full, full], out_specs=[full, vec],
        out_shape=[SDS((H, S), F32), SDS((H, 1), F32)], compiler_params=_params(1))(af_t, bias, d_fq, d_fk)


FOX_T = 128


def _causal_tile(T):
    return lax.broadcasted_iota(jnp.int32, (T, T), 1) <= lax.broadcasted_iota(jnp.int32, (T, T), 0)


def fox_attn_fwd(q, k, v, f_col, f_row, name):
    H, S, Dh = q.shape
    T = min(FOX_T, S)
    nq = S // T
    scale = Dh ** -0.5

    def body(q_ref, k_ref, v_ref, fc_ref, fr_ref, o_ref, lse_ref):
        qi = pl.program_id(1)
        qv = q_ref[...]
        fq = fc_ref[...]

        def step(j, carry, diagonal):
            m, l, acc = carry
            off = pl.multiple_of(j * T, T)
            kv = k_ref[pl.ds(off, T), :]
            vv = v_ref[pl.ds(off, T), :]
            s = _dot_nt(qv, kv) * scale + (fq - fr_ref[j])
            if diagonal:
                s = jnp.where(_causal_tile(T), s, NEG)
            m_new = jnp.maximum(m, jnp.max(s, axis=-1, keepdims=True))
            p = jnp.exp(s - m_new)
            alpha = jnp.exp(m - m_new)
            l = alpha * l + jnp.sum(p, axis=-1, keepdims=True)
            acc = alpha * acc + _dot(p.astype(BF16), vv)
            return m_new, l, acc

        init = (jnp.full((T, 1), NEG, F32), jnp.zeros((T, 1), F32), jnp.zeros((T, Dh), F32))
        carry = lax.fori_loop(0, qi, functools.partial(step, diagonal=False), init)
        m, l, acc = step(qi, carry, True)
        o_ref[...] = (acc / l).astype(o_ref.dtype)
        lse_ref[...] = m + jnp.log(l)

    blk = BS((None, T, Dh), lambda h, i: (h, i, 0))
    full = BS((None, S, Dh), lambda h, i: (h, 0, 0))
    colb = BS((None, T, 1), lambda h, i: (h, i, 0))
    rowf = BS((None, nq, 1, T), lambda h, i: (h, 0, 0, 0))
    return pl.pallas_call(
        body, name=name, grid=(H, nq), in_specs=[blk, full, full, colb, rowf], out_specs=[blk, colb],
        out_shape=[SDS((H, S, Dh), BF16), SDS((H, S, 1), F32)], compiler_params=_params(2))(q, k, v, f_col, f_row)


def fox_attn_bwd_dq(q, k, v, f_col, f_row, o, lse, do, name):
    H, S, Dh = q.shape
    T = min(FOX_T, S)
    nq = S // T
    scale = Dh ** -0.5

    def body(q_ref, k_ref, v_ref, fc_ref, fr_ref, o_ref, lse_ref, do_ref, dq_ref, dl_ref, df_ref):
        qi = pl.program_id(1)
        qv = q_ref[...]
        fq = fc_ref[...]
        lse_v = lse_ref[...]
        dov = do_ref[...]
        delta = jnp.sum(dov.astype(F32) * o_ref[...].astype(F32), axis=-1, keepdims=True)
        dl_ref[...] = delta

        def step(j, carry, diagonal):
            dq, dfq = carry
            off = pl.multiple_of(j * T, T)
            kv = k_ref[pl.ds(off, T), :]
            vv = v_ref[pl.ds(off, T), :]
            s = _dot_nt(qv, kv) * scale + (fq - fr_ref[j])
            if diagonal:
                s = jnp.where(_causal_tile(T), s, NEG)
            p = jnp.exp(s - lse_v)
            ds = p * (_dot_nt(dov, vv) - delta)
            return dq + _dot(ds.astype(BF16), kv), dfq + jnp.sum(ds, axis=-1, keepdims=True)

        carry = lax.fori_loop(0, qi, functools.partial(step, diagonal=False),
                              (jnp.zeros((T, Dh), F32), jnp.zeros((T, 1), F32)))
        dq, dfq = step(qi, carry, True)
        dq_ref[...] = dq * scale
        df_ref[...] = dfq

    blk = BS((None, T, Dh), lambda h, i: (h, i, 0))
    full = BS((None, S, Dh), lambda h, i: (h, 0, 0))
    colb = BS((None, T, 1), lambda h, i: (h, i, 0))
    rowf = BS((None, nq, 1, T), lambda h, i: (h, 0, 0, 0))
    return pl.pallas_call(
        body, name=name, grid=(H, nq), in_specs=[blk, full, full, colb, rowf, blk, colb, blk],
        out_specs=[blk, colb, colb], out_shape=[SDS((H, S, Dh), F32), SDS((H, S, 1), F32), SDS((H, S, 1), F32)],
        compiler_params=_params(2))(q, k, v, f_col, f_row, o, lse, do)


def fox_attn_bwd_dkv(q, k, v, f_col, f_row, lse_row, delta_row, do, name):
    H, S, Dh = q.shape
    T = min(FOX_T, S)
    nq = S // T
    scale = Dh ** -0.5

    def body(q_ref, k_ref, v_ref, fc_ref, fr_ref, lse_ref, dl_ref, do_ref, dk_ref, dv_ref, df_ref):
        kj = pl.program_id(1)
        kv = k_ref[...]
        vv = v_ref[...]
        fk = fc_ref[...]
        def step(i, carry, diagonal):
            dk, dv, dfk = carry
            off = pl.multiple_of(i * T, T)
            qv = q_ref[pl.ds(off, T), :]
            dov = do_ref[pl.ds(off, T), :]
            st = _dot_nt(kv, qv) * scale + (fr_ref[i] - fk)
            if diagonal:
                st = jnp.where(lax.broadcasted_iota(jnp.int32, (T, T), 0)
                               <= lax.broadcasted_iota(jnp.int32, (T, T), 1), st, NEG)
            pt = jnp.exp(st - lse_ref[i])
            dv = dv + _dot(pt.astype(BF16), dov)
            dst = pt * (_dot_nt(vv, dov) - dl_ref[i])
            dk = dk + _dot(dst.astype(BF16), qv)
            dfk = dfk + jnp.sum(dst, axis=-1, keepdims=True)
            return dk, dv, dfk

        init = (jnp.zeros((T, Dh), F32), jnp.zeros((T, Dh), F32), jnp.zeros((T, 1), F32))
        carry = step(kj, init, True)
        dk, dv, dfk = lax.fori_loop(kj + 1, nq, functools.partial(step, diagonal=False), carry)
        dk_ref[...] = dk * scale
        dv_ref[...] = dv
        df_ref[...] = -dfk

    blk = BS((None, T, Dh), lambda h, i: (h, i, 0))
    full = BS((None, S, Dh), lambda h, i: (h, 0, 0))
    colb = BS((None, T, 1), lambda h, i: (h, i, 0))
    rowf = BS((None, nq, 1, T), lambda h, i: (h, 0, 0, 0))
    return pl.pallas_call(
        body, name=name, grid=(H, nq), in_specs=[full, blk, blk, colb, rowf, rowf, rowf, full],
        out_specs=[blk, blk, colb],
        out_shape=[SDS((H, S, Dh), F32), SDS((H, S, Dh), F32), SDS((H, S, 1), F32)],
        compiler_params=_params(2))(q, k, v, f_col, f_row, lse_row, delta_row, do)


HALO = 8
CONV_TM = 512


def _conv_ext(e, w_ref):
    return w_ref[0:1, :] * pltpu.roll(e, 2, 0) + w_ref[1:2, :] * pltpu.roll(e, 1, 0) + w_ref[2:3, :] * e


def _conv_t_ext(d, w_ref):
    n = d.shape[0]
    return w_ref[2:3, :] * d + w_ref[1:2, :] * pltpu.roll(d, n - 1, 0) + w_ref[0:1, :] * pltpu.roll(d, n - 2, 0)


def _time_specs(S, tm, width, col_block):
    per = tm // HALO
    last = S // HALO - 1
    cur = BS((tm, width), lambda j, i: (i, col_block(j)))
    prev = BS((HALO, width), lambda j, i: (jnp.maximum(i * per - 1, 0), col_block(j)))
    nxt = BS((HALO, width), lambda j, i: (jnp.minimum((i + 1) * per, last), col_block(j)))
    return cur, prev, nxt


def shortconv_fwd(proj, w, name):
    S = proj.shape[0]
    tm = _tile(S, CONV_TM, 8)
    nb = CONV_WIDTH // LANES
    b0 = OFF_B // LANES

    def body(xb_ref, xbp_ref, gb_ref, gc_ref, gcp_ref, w_ref, o_ref):
        keep = (pl.program_id(1) > 0).astype(F32)
        e = jnp.concatenate([gcp_ref[...] * xbp_ref[...] * keep, gc_ref[...] * xb_ref[...]], axis=0)
        o_ref[...] = (gb_ref[...] * _conv_ext(e, w_ref)[HALO:, :]).astype(o_ref.dtype)

    xb, xbp, _ = _time_specs(S, tm, LANES, lambda j: b0 + j)
    gb, _, _ = _time_specs(S, tm, LANES, lambda j: b0 + nb + j)
    gc, gcp, _ = _time_specs(S, tm, LANES, lambda j: b0 + 2 * nb + j)
    return pl.pallas_call(
        body, name=name, grid=(nb, S // tm),
        in_specs=[xb, xbp, gb, gc, gcp, BS((3, LANES), lambda j, i: (0, j))],
        out_specs=BS((tm, LANES), lambda j, i: (i, j)), out_shape=SDS((S, CONV_WIDTH), BF16),
        compiler_params=_params(2))(proj, proj, proj, proj, proj, w)


def shortconv_bwd(proj, w, do_b, name):
    S = proj.shape[0]
    tm = _tile(S, CONV_TM, 8)
    nt = S // tm
    nb = CONV_WIDTH // LANES
    b0 = OFF_B // LANES

    def body(xb_ref, xbp_ref, gb_ref, gbn_ref, gc_ref, gcp_ref, do_ref, don_ref, w_ref,
             dxb_ref, dgb_ref, dgc_ref, dw_ref):
        i = pl.program_id(1)
        keep_prev = (i > 0).astype(F32)
        keep_next = (i < nt - 1).astype(F32)
        xb, gb, gc = xb_ref[...], gb_ref[...], gc_ref[...]
        do = do_ref[...].astype(F32)
        u = gc * xb
        e = jnp.concatenate([gcp_ref[...] * xbp_ref[...] * keep_prev, u], axis=0)
        dgb_ref[...] = (do * _conv_ext(e, w_ref)[HALO:, :]).astype(dgb_ref.dtype)
        dcv = do * gb
        d_ext = jnp.concatenate([dcv, don_ref[...].astype(F32) * gbn_ref[...] * keep_next], axis=0)
        du = _conv_t_ext(d_ext, w_ref)[:tm, :]
        dgc_ref[...] = (du * xb).astype(dgc_ref.dtype)
        dxb_ref[...] = (du * gc).astype(dxb_ref.dtype)

        @pl.when(i == 0)
        def _():
            dw_ref[...] = jnp.zeros_like(dw_ref)

        dw_ref[0:1, :] += jnp.sum(dcv * pltpu.roll(e, 2, 0)[HALO:, :], axis=0, keepdims=True)
        dw_ref[1:2, :] += jnp.sum(dcv * pltpu.roll(e, 1, 0)[HALO:, :], axis=0, keepdims=True)
        dw_ref[2:3, :] += jnp.sum(dcv * u, axis=0, keepdims=True)

    xb, xbp, _ = _time_specs(S, tm, LANES, lambda j: b0 + j)
    gb, _, gbn = _time_specs(S, tm, LANES, lambda j: b0 + nb + j)
    gc, gcp, _ = _time_specs(S, tm, LANES, lambda j: b0 + 2 * nb + j)
    do, _, don = _time_specs(S, tm, LANES, lambda j: j)
    out = BS((tm, LANES), lambda j, i: (i, j))
    wspec = BS((3, LANES), lambda j, i: (0, j))
    return pl.pallas_call(
        body, name=name, grid=(nb, nt), in_specs=[xb, xbp, gb, gbn, gc, gcp, do, don, wspec],
        out_specs=[out, out, out, wspec],
        out_shape=[SDS((S, CONV_WIDTH), BF16)] * 3 + [SDS((3, CONV_WIDTH), F32)],
        compiler_params=_params(2))(proj, proj, proj, proj, proj, proj, do_b, do_b, w)


_GELU_C = 0.7978845608028654


def _gelu(x):
    return 0.5 * x * (1.0 + jnp.tanh(_GELU_C * (x + 0.044715 * x * x * x)))


def _gelu_grad(x):
    t = jnp.tanh(_GELU_C * (x + 0.044715 * x * x * x))
    return 0.5 * (1.0 + t) + 0.5 * x * (1.0 - t * t) * _GELU_C * (1.0 + 3.0 * 0.044715 * x * x)


def _tril_masks():
    row = lax.broadcasted_iota(jnp.int32, (SGU_CHUNK, SGU_CHUNK), 0)
    col = lax.broadcasted_iota(jnp.int32, (SGU_CHUNK, SGU_CHUNK), 1)
    return row >= col, row <= col


def _lane_column(mat, g):
    lane = lax.broadcasted_iota(jnp.int32, mat.shape, 1)
    return jnp.sum(jnp.where(lane == g, mat, 0.0), axis=-1, keepdims=True)


def sgu_fwd(proj, norm_g, w_s, b_t, name):
    S = proj.shape[0]
    T = SGU_CHUNK
    G = SGU_WIDTH // T
    c0 = OFF_C // (2 * SGU_WIDTH)

    def body(c_ref, g_ref, w_ref, b_ref, o_ref):
        u = _gelu(c_ref[:, 0:SGU_WIDTH])
        v = _gelu(c_ref[:, SGU_WIDTH:2 * SGU_WIDTH])
        r = lax.rsqrt(jnp.mean(v * v, axis=-1, keepdims=True) + EPS)
        vn = ((v * r) * g_ref[...]).astype(BF16)
        mask, _ = _tril_masks()
        bias = b_ref[...]
        for g in range(G):
            sl = slice(g * T, (g + 1) * T)
            wt = jnp.where(mask, w_ref[g], 0.0).astype(BF16)
            mixed = _dot(wt, vn[:, sl]) + _lane_column(bias, g)
            o_ref[:, sl] = (u[:, sl] * mixed).astype(o_ref.dtype)

    return pl.pallas_call(
        body, name=name, grid=(S // T,),
        in_specs=[BS((T, 2 * SGU_WIDTH), lambda i: (i, c0)), BS((1, SGU_WIDTH), lambda i: (0, 0)),
                  BS((G, T, T), lambda i: (0, 0, 0)), BS((T, T), lambda i: (0, 0))],
        out_specs=BS((T, SGU_WIDTH), lambda i: (i, 0)), out_shape=SDS((S, SGU_WIDTH), BF16),
        compiler_params=_params(1))(proj, norm_g, w_s, b_t)


def sgu_bwd(proj, norm_g, w_s, w_st, b_t, do_c, name):
    S = proj.shape[0]
    T = SGU_CHUNK
    G = SGU_WIDTH // T
    c0 = OFF_C // (2 * SGU_WIDTH)

    def body(c_ref, g_ref, w_ref, wt_ref, b_ref, do_ref, dc_ref, dg_ref, dw_ref, db_ref):
        cu = c_ref[:, 0:SGU_WIDTH]
        cv = c_ref[:, SGU_WIDTH:2 * SGU_WIDTH]
        u = _gelu(cu)
        v = _gelu(cv)
        r = lax.rsqrt(jnp.mean(v * v, axis=-1, keepdims=True) + EPS)
        xh = v * r
        gv = g_ref[...]
        vn = (xh * gv).astype(BF16)
        do = do_ref[...].astype(F32)
        mask, mask_t = _tril_masks()
        bias = b_ref[...]
        lane = lax.broadcasted_iota(jnp.int32, (T, T), 1)

        @pl.when(pl.program_id(0) == 0)
        def _():
            dg_ref[...] = jnp.zeros_like(dg_ref)
            dw_ref[...] = jnp.zeros_like(dw_ref)
            db_ref[...] = jnp.zeros_like(db_ref)

        dvn_parts = []
        du_parts = []
        dbias = jnp.zeros((T, T), F32)
        for g in range(G):
            sl = slice(g * T, (g + 1) * T)
            wt = jnp.where(mask, w_ref[g], 0.0).astype(BF16)
            mixed = _dot(wt, vn[:, sl]) + _lane_column(bias, g)
            du_parts.append(do[:, sl] * mixed)
            dmix = do[:, sl] * u[:, sl]
            dmix_b = dmix.astype(BF16)
            dw_ref[g] += jnp.where(mask, _dot_nt(dmix_b, vn[:, sl]), 0.0)
            dbias = dbias + jnp.where(lane == g, jnp.sum(dmix, axis=-1, keepdims=True), 0.0)
            wtt = jnp.where(mask_t, wt_ref[g], 0.0).astype(BF16)
            dvn_parts.append(_dot(wtt, dmix_b))
        db_ref[...] += dbias
        dvn = jnp.concatenate(dvn_parts, axis=-1)
        du = jnp.concatenate(du_parts, axis=-1)
        dg_ref[...] += jnp.sum(dvn * xh, axis=0, keepdims=True)
        dxh = dvn * gv
        dv = r * (dxh - xh * jnp.mean(dxh * xh, axis=-1, keepdims=True))
        dc_ref[:, 0:SGU_WIDTH] = (du * _gelu_grad(cu)).astype(dc_ref.dtype)
        dc_ref[:, SGU_WIDTH:2 * SGU_WIDTH] = (dv * _gelu_grad(cv)).astype(dc_ref.dtype)

    wspec = BS((G, T, T), lambda i: (0, 0, 0))
    gspec = BS((1, SGU_WIDTH), lambda i: (0, 0))
    return pl.pallas_call(
        body, name=name, grid=(S // T,),
        in_specs=[BS((T, 2 * SGU_WIDTH), lambda i: (i, c0)), gspec, wspec, wspec, BS((T, T), lambda i: (0, 0)),
                  BS((T, SGU_WIDTH), lambda i: (i, 0))],
        out_specs=[BS((T, 2 * SGU_WIDTH), lambda i: (i, 0)), gspec, wspec, BS((T, T), lambda i: (0, 0))],
        out_shape=[SDS((S, 2 * SGU_WIDTH), BF16), SDS((1, SGU_WIDTH), F32), SDS((G, T, T), F32), SDS((T, T), F32)],
        compiler_params=_params(1))(proj, norm_g, w_s, w_st, b_t, do_c)


def rope_tables(positions):
    half = ROPE_DIM // 2
    inv = ROPE_THETA ** (-jnp.arange(half, dtype=F32) * (2.0 / ROPE_DIM))
    ang = positions.astype(F32)[:, None] * inv
    cos, sin = jnp.cos(ang), jnp.sin(ang)
    S = positions.shape[0]
    ones = jnp.ones((S, HEAD_DIM - ROPE_DIM), F32)
    zeros = jnp.zeros((S, HEAD_DIM - ROPE_DIM), F32)
    zh = jnp.zeros((S, half), F32)
    c = jnp.concatenate([cos, cos, ones], axis=-1)
    sa = jnp.concatenate([zh, sin, zeros], axis=-1)
    sb = jnp.concatenate([-sin, zh, zeros], axis=-1)
    tile2 = lambda t: jnp.concatenate([t, t], axis=-1)
    return tile2(c), tile2(sa), tile2(sb)


def rope_apply(x, col0, ncols, cos, sa, sb, out_dtype, name):
    S = x.shape[0]
    tm = _tile(S, 512, 8)
    half = ROPE_DIM // 2
    b0 = col0 // LANES

    def body(x_ref, c_ref, sa_ref, sb_ref, o_ref):
        xv = x_ref[...].astype(F32)
        o_ref[...] = (xv * c_ref[...] + pltpu.roll(xv, half, 1) * sa_ref[...]
                      + pltpu.roll(xv, LANES - half, 1) * sb_ref[...]).astype(o_ref.dtype)

    tab = BS((tm, LANES), lambda i, j: (i, 0))
    return pl.pallas_call(
        body, name=name, grid=(S // tm, ncols // LANES),
        in_specs=[BS((tm, LANES), lambda i, j: (i, b0 + j)), tab, tab, tab],
        out_specs=BS((tm, LANES), lambda i, j: (i, j)), out_shape=SDS((S, ncols), out_dtype),
        compiler_params=_params(2))(x, cos, sa, sb)


def _dil_masks(first):
    qi = lax.broadcasted_iota(jnp.int32, (DIL_SPAN, DIL_SPAN), 0)
    ki = lax.broadcasted_iota(jnp.int32, (DIL_SPAN, DIL_SPAN), 1)
    return ki >= qi + first.astype(jnp.int32) * DIL_SPAN, ki <= qi


def dil_attn_fwd(q, k, v, nb, name):
    H, S, Dh = q.shape
    T = DIL_SPAN
    nblk = S // T
    scale = Dh ** -0.5

    def body(q_ref, kp_ref, kc_ref, vp_ref, vc_ref, o_ref, lse_ref):
        b = pl.program_id(1)
        mp, mc = _dil_masks(b % nb == 0)
        qv = q_ref[...]
        sp = jnp.where(mp, _dot_nt(qv, kp_ref[...]) * scale, NEG)
        sc = jnp.where(mc, _dot_nt(qv, kc_ref[...]) * scale, NEG)
        m = jnp.maximum(jnp.max(sp, axis=-1, keepdims=True), jnp.max(sc, axis=-1, keepdims=True))
        ep = jnp.exp(sp - m)
        ec = jnp.exp(sc - m)
        l = jnp.sum(ep, axis=-1, keepdims=True) + jnp.sum(ec, axis=-1, keepdims=True)
        o_ref[...] = (_dot(ep.astype(BF16), vp_ref[...]) + _dot(ec.astype(BF16), vc_ref[...])) / l
        lse_ref[...] = m + jnp.log(l)

    cur = BS((None, T, Dh), lambda h, b: (h, b, 0))
    prev = BS((None, T, Dh), lambda h, b: (h, jnp.maximum(b - 1, 0), 0))
    colb = BS((None, T, 1), lambda h, b: (h, b, 0))
    return pl.pallas_call(
        body, name=name, grid=(H, nblk), in_specs=[cur, prev, cur, prev, cur], out_specs=[cur, colb],
        out_shape=[SDS((H, S, Dh), F32), SDS((H, S, 1), F32)], compiler_params=_params(2))(q, k, k, v, v)


def dil_attn_bwd_dq(q, k, v, o, lse, do, dlse, nb, name):
    H, S, Dh = q.shape
    T = DIL_SPAN
    nblk = S // T
    scale = Dh ** -0.5

    def body(q_ref, kp_ref, kc_ref, vp_ref, vc_ref, o_ref, lse_ref, do_ref, dlse_ref, dq_ref, dl_ref):
        b = pl.program_id(1)
        mp, mc = _dil_masks(b % nb == 0)
        qv = q_ref[...]
        kp, kc = kp_ref[...], kc_ref[...]
        dov = do_ref[...]
        delta = jnp.sum(dov * o_ref[...], axis=-1, keepdims=True) - dlse_ref[...]
        dl_ref[...] = delta
        dob = dov.astype(BF16)
        lse_v = lse_ref[...]
        pp = jnp.exp(jnp.where(mp, _dot_nt(qv, kp) * scale, NEG) - lse_v)
        pc = jnp.exp(jnp.where(mc, _dot_nt(qv, kc) * scale, NEG) - lse_v)
        dsp = pp * (_dot_nt(dob, vp_ref[...]) - delta)
        dsc = pc * (_dot_nt(dob, vc_ref[...]) - delta)
        dq_ref[...] = (_dot(dsp.astype(BF16), kp) + _dot(dsc.astype(BF16), kc)) * scale

    cur = BS((None, T, Dh), lambda h, b: (h, b, 0))
    prev = BS((None, T, Dh), lambda h, b: (h, jnp.maximum(b - 1, 0), 0))
    colb = BS((None, T, 1), lambda h, b: (h, b, 0))
    return pl.pallas_call(
        body, name=name, grid=(H, nblk), in_specs=[cur, prev, cur, prev, cur, cur, colb, cur, colb],
        out_specs=[cur, colb], out_shape=[SDS((H, S, Dh), F32), SDS((H, S, 1), F32)],
        compiler_params=_params(2))(q, k, k, v, v, o, lse, do, dlse)


def dil_attn_bwd_dkv(q, k, v, lse_row, delta_row, do, nb, name):
    H, S, Dh = q.shape
    T = DIL_SPAN
    nblk = S // T
    scale = Dh ** -0.5

    def body(k_ref, v_ref, qc_ref, qn_ref, doc_ref, don_ref, lc_ref, ln_ref, dc_ref, dn_ref, dk_ref, dv_ref):
        c = pl.program_id(1)
        no_next = ((c + 1) % nb == 0).astype(jnp.int32)
        si = lax.broadcasted_iota(jnp.int32, (T, T), 0)
        ti = lax.broadcasted_iota(jnp.int32, (T, T), 1)
        m_cur = si <= ti
        m_next = si >= ti + no_next * T
        kv, vv = k_ref[...], v_ref[...]
        qc, qn = qc_ref[...], qn_ref[...]
        doc, don = doc_ref[...].astype(BF16), don_ref[...].astype(BF16)
        pt = jnp.exp(jnp.where(m_cur, _dot_nt(kv, qc) * scale, NEG) - lc_ref[...])
        ptn = jnp.exp(jnp.where(m_next, _dot_nt(kv, qn) * scale, NEG) - ln_ref[...])
        dv_ref[...] = _dot(pt.astype(BF16), doc) + _dot(ptn.astype(BF16), don)
        dst = pt * (_dot_nt(vv, doc) - dc_ref[...])
        dstn = ptn * (_dot_nt(vv, don) - dn_ref[...])
        dk_ref[...] = (_dot(dst.astype(BF16), qc) + _dot(dstn.astype(BF16), qn)) * scale

    cur = BS((None, T, Dh), lambda h, b: (h, b, 0))
    nxt = BS((None, T, Dh), lambda h, b: (h, jnp.minimum(b + 1, nblk - 1), 0))
    rcur = BS((None, None, 1, T), lambda h, b: (h, b, 0, 0))
    rnxt = BS((None, None, 1, T), lambda h, b: (h, jnp.minimum(b + 1, nblk - 1), 0, 0))
    return pl.pallas_call(
        body, name=name, grid=(H, nblk), in_specs=[cur, cur, cur, nxt, cur, nxt, rcur, rnxt, rcur, rnxt],
        out_specs=[cur, cur], out_shape=[SDS((H, S, Dh), F32), SDS((H, S, Dh), F32)],
        compiler_params=_params(2))(k, v, q, q, do, do, lse_row, lse_row, delta_row, delta_row)


def dil_merge_fwd(outs, lses, name):
    H, S, Dh = outs[0].shape
    tm = _tile(S, 512, 8)

    def body(o0, o1, o2, l0, l1, l2, out_ref):
        ls = [l0[...], l1[...], l2[...]]
        m = jnp.maximum(jnp.maximum(ls[0], ls[1]), ls[2])
        es = [jnp.exp(l - m) for l in ls]
        den = es[0] + es[1] + es[2]
        out_ref[...] = (es[0] * o0[...] + es[1] * o1[...] + es[2] * o2[...]) / den

    blk = BS((None, tm, Dh), lambda h, i: (h, i, 0))
    colb = BS((None, tm, 1), lambda h, i: (h, i, 0))
    return pl.pallas_call(
        body, name=name, grid=(H, S // tm), in_specs=[blk] * 3 + [colb] * 3, out_specs=blk,
        out_shape=SDS((H, S, Dh), F32), compiler_params=_params(2))(*outs, *lses)


def dil_merge_bwd(outs, lses, d_out, name):
    H, S, Dh = outs[0].shape
    tm = _tile(S, 512, 8)

    def body(o0, o1, o2, l0, l1, l2, d_ref, do0, do1, do2, dl0, dl1, dl2):
        ls = [l0[...], l1[...], l2[...]]
        m = jnp.maximum(jnp.maximum(ls[0], ls[1]), ls[2])
        es = [jnp.exp(l - m) for l in ls]
        den = es[0] + es[1] + es[2]
        ws = [e / den for e in es]
        dv = d_ref[...]
        dws = [jnp.sum(dv * o[...], axis=-1, keepdims=True) for o in (o0, o1, o2)]
        mean = ws[0] * dws[0] + ws[1] * dws[1] + ws[2] * dws[2]
        for w, dw, do_ref, dl_ref in zip(ws, dws, (do0, do1, do2), (dl0, dl1, dl2)):
            do_ref[...] = w * dv
            dl_ref[...] = w * (dw - mean)

    blk = BS((None, tm, Dh), lambda h, i: (h, i, 0))
    colb = BS((None, tm, 1), lambda h, i: (h, i, 0))
    return pl.pallas_call(
        body, name=name, grid=(H, S // tm), in_specs=[blk] * 3 + [colb] * 3 + [blk],
        out_specs=[blk] * 3 + [colb] * 3,
        out_shape=[SDS((H, S, Dh), F32)] * 3 + [SDS((H, S, 1), F32)] * 3,
        compiler_params=_params(2))(*outs, *lses, d_out)


MERGE_TN = 512


def merge_fwd(o_stack, w_br, proj, name):
    _, S, K = o_stack.shape
    tm = _tile(S, 512, 8)
    tn = MERGE_TN
    nj = D_MODEL // tn

    def body(o_ref, w_ref, gl_ref, m_ref, acc):
        br = pl.program_id(2)

        @pl.when(br == 0)
        def _():
            acc[...] = jnp.zeros_like(acc)

        acc[...] += _sigmoid(gl_ref[...]) * _dot(o_ref[...], w_ref[...])

        @pl.when(br == N_BRANCH - 1)
        def _():
            m_ref[...] = acc[...].astype(m_ref.dtype)

    return pl.pallas_call(
        body, name=name, grid=(S // tm, nj, N_BRANCH),
        in_specs=[BS((None, tm, K), lambda i, j, b: (b, i, 0)), BS((None, K, tn), lambda i, j, b: (b, 0, j)),
                  BS((tm, tn), lambda i, j, b: (i, b * nj + j))],
        out_specs=BS((tm, tn), lambda i, j, b: (i, j)), out_shape=SDS((S, D_MODEL), BF16),
        scratch_shapes=[pltpu.VMEM((tm, tn), F32)], compiler_params=_params(3))(o_stack, w_br, proj)


def merge_bwd(o_stack, w_br, proj, d_merged, name):
    _, S, K = o_stack.shape
    tm = _tile(S, 512, 8)
    tn = MERGE_TN
    nj = D_MODEL // tn

    def body(o_ref, w_ref, gl_ref, dm_ref, dgl_ref, dy_ref):
        gate = _sigmoid(gl_ref[...])
        y = _dot(o_ref[...], w_ref[...])
        dm = dm_ref[...].astype(F32)
        dgl_ref[...] = (dm * y * gate * (1.0 - gate)).astype(dgl_ref.dtype)
        dy_ref[...] = (dm * gate).astype(dy_ref.dtype)

    return pl.pallas_call(
        body, name=name, grid=(S // tm, nj, N_BRANCH),
        in_specs=[BS((None, tm, K), lambda i, j, b: (b, i, 0)), BS((None, K, tn), lambda i, j, b: (b, 0, j)),
                  BS((tm, tn), lambda i, j, b: (i, b * nj + j)), BS((tm, tn), lambda i, j, b: (i, j))],
        out_specs=[BS((tm, tn), lambda i, j, b: (i, b * nj + j)), BS((None, tm, tn), lambda i, j, b: (b, i, j))],
        out_shape=[SDS((S, N_BRANCH * D_MODEL), BF16), SDS((N_BRANCH, S, D_MODEL), BF16)],
        compiler_params=_params(3))(o_stack, w_br, proj, d_merged)


FFN_CW = 256


def ffn_mid_fwd(pre_g, pre_v, w_g, w_v, name):
    S = pre_g.shape[0]
    tm = _tile(S, CONV_TM, 8)

    def body(g_ref, gp_ref, v_ref, vp_ref, wg_ref, wv_ref, a_ref):
        keep = (pl.program_id(1) > 0).astype(F32)
        ug = _conv_ext(jnp.concatenate([gp_ref[...] * keep, g_ref[...]], axis=0), wg_ref)[HALO:, :]
        uv = _conv_ext(jnp.concatenate([vp_ref[...] * keep, v_ref[...]], axis=0), wv_ref)[HALO:, :]
        a_ref[...] = (ug * _sigmoid(ug) * uv).astype(a_ref.dtype)

    cur, prev, _ = _time_specs(S, tm, FFN_CW, lambda j: j)
    wspec = BS((3, FFN_CW), lambda j, i: (0, j))
    return pl.pallas_call(
        body, name=name, grid=(D_FF // FFN_CW, S // tm), in_specs=[cur, prev, cur, prev, wspec, wspec],
        out_specs=cur, out_shape=SDS((S, D_FF), BF16), compiler_params=_params(2))(
            pre_g, pre_g, pre_v, pre_v, w_g, w_v)


def ffn_mid_bwd(pre_g, pre_v, w_g, w_v, d_a, name):
    S = pre_g.shape[0]
    tm = _tile(S, CONV_TM, 8)
    nt = S // tm

    def body(g_ref, gp_ref, gn_ref, v_ref, vp_ref, vn_ref, wg_ref, wv_ref, da_ref, dan_ref,
             dg_ref, dv_ref, dwg_ref, dwv_ref):
        i = pl.program_id(1)
        keep_prev = (i > 0).astype(F32)
        keep_next = (i < nt - 1).astype(F32)
        eg = jnp.concatenate([gp_ref[...] * keep_prev, g_ref[...], gn_ref[...]], axis=0)
        ev = jnp.concatenate([vp_ref[...] * keep_prev, v_ref[...], vn_ref[...]], axis=0)
        ug = _conv_ext(eg, wg_ref)
        uv = _conv_ext(ev, wv_ref)
        da = jnp.concatenate([jnp.zeros((HALO, FFN_CW), F32), da_ref[...], dan_ref[...] * keep_next], axis=0)
        sg = _sigmoid(ug)
        dug = da * uv * (sg * (1.0 + ug * (1.0 - sg)))
        duv = da * (ug * sg)
        dg_ref[...] = _conv_t_ext(dug, wg_ref)[HALO:HALO + tm, :].astype(dg_ref.dtype)
        dv_ref[...] = _conv_t_ext(duv, wv_ref)[HALO:HALO + tm, :].astype(dv_ref.dtype)

        @pl.when(i == 0)
        def _():
            dwg_ref[...] = jnp.zeros_like(dwg_ref)
            dwv_ref[...] = jnp.zeros_like(dwv_ref)

        for dup_e, e, dw_ref in ((dug, eg, dwg_ref), (duv, ev, dwv_ref)):
            dup = dup_e[HALO:HALO + tm, :]
            dw_ref[0:1, :] += jnp.sum(dup * pltpu.roll(e, 2, 0)[HALO:HALO + tm, :], axis=0, keepdims=True)
            dw_ref[1:2, :] += jnp.sum(dup * pltpu.roll(e, 1, 0)[HALO:HALO + tm, :], axis=0, keepdims=True)
            dw_ref[2:3, :] += jnp.sum(dup * e[HALO:HALO + tm, :], axis=0, keepdims=True)

    cur, prev, nxt = _time_specs(S, tm, FFN_CW, lambda j: j)
    wspec = BS((3, FFN_CW), lambda j, i: (0, j))
    return pl.pallas_call(
        body, name=name, grid=(D_FF // FFN_CW, nt),
        in_specs=[cur, prev, nxt, cur, prev, nxt, wspec, wspec, cur, nxt],
        out_specs=[cur, cur, wspec, wspec],
        out_shape=[SDS((S, D_FF), BF16)] * 2 + [SDS((3, D_FF), F32)] * 2,
        compiler_params=_params(2))(pre_g, pre_g, pre_g, pre_v, pre_v, pre_v, w_g, w_v, d_a, d_a)


def ple_fwd(x, a, e, name):
    S, Dm = x.shape
    tm = _tile(S, 256, 8)

    def body(x_ref, a_ref, e_ref, o_ref):
        o_ref[...] = x_ref[...] + _sigmoid(a_ref[...]) * e_ref[...]

    row = BS((tm, Dm), lambda i: (i, 0))
    return pl.pallas_call(body, name=name, grid=(S // tm,), in_specs=[row] * 3, out_specs=row,
                          out_shape=SDS((S, Dm), F32), compiler_params=_params(1))(x, a, e)


def ple_bwd(a, e, dx, name):
    S, Dm = a.shape
    tm = _tile(S, 256, 8)

    def body(a_ref, e_ref, dx_ref, da_ref, de_ref):
        s = _sigmoid(a_ref[...])
        d = dx_ref[...]
        da_ref[...] = (d * e_ref[...] * s * (1.0 - s)).astype(da_ref.dtype)
        de_ref[...] = (d * s).astype(de_ref.dtype)

    row = BS((tm, Dm), lambda i: (i, 0))
    return pl.pallas_call(body, name=name, grid=(S // tm,), in_specs=[row] * 3, out_specs=[row, row],
                          out_shape=[SDS((S, Dm), BF16)] * 2, compiler_params=_params(1))(a, e, dx)


def to_heads(x, n_heads, dil):
    S = x.shape[0]
    x = x.reshape(S // dil, dil, n_heads, HEAD_DIM).transpose(2, 1, 0, 3)
    return x.reshape(n_heads, S, HEAD_DIM)


def from_heads(y, dil):
    H, S, C = y.shape
    y = y.reshape(H, dil, S // dil, C).transpose(2, 1, 0, 3)
    return y.reshape(S, H * C)


def rows_of(col, t):
    H, S, _ = col.shape
    return col.reshape(H, S // t, 1, t)


IN_SEGMENTS = ((SRC_A, SRC_F, OFF_A), (SRC_F, SRC_B, OFF_F), (SRC_B, SRC_C, OFF_B), (SRC_C, SRC_DQ, OFF_C),
               (SRC_DQ, SRC_G, OFF_D), (SRC_G, D_IN, OFF_G))
IN_SHARD = D_IN // N_CHIPS


def w_in_aligned_from_chips(t):
    pieces = []
    for ref_lo, ref_hi, _ in sorted(IN_SEGMENTS, key=lambda seg: seg[2]):
        for k in range(N_CHIPS):
            lo, hi = max(ref_lo, k * IN_SHARD), min(ref_hi, (k + 1) * IN_SHARD)
            if lo < hi:
                pieces.append(t[k][:, lo - k * IN_SHARD:hi - k * IN_SHARD])
    pieces.append(jnp.zeros((t.shape[1], W_AL - D_IN), t.dtype))
    return jnp.concatenate(pieces, axis=1)


def w_in_chips_from_aligned(g):
    slots = []
    for k in range(N_CHIPS):
        pieces = []
        for ref_lo, ref_hi, al in IN_SEGMENTS:
            lo, hi = max(ref_lo, k * IN_SHARD), min(ref_hi, (k + 1) * IN_SHARD)
            if lo < hi:
                pieces.append(g[:, al + lo - ref_lo:al + hi - ref_lo])
        slots.append(jnp.concatenate(pieces, axis=1))
    return jnp.stack(slots, axis=0)


def chips_to_full(t, name):
    _, r, c = t.shape
    if name in ROW_SHARDED:
        return t.reshape(N_CHIPS * r, c)
    return t.transpose(1, 0, 2).reshape(r, N_CHIPS * c)


def full_to_chips(g, name):
    if name in ROW_SHARDED:
        return g.reshape(N_CHIPS, g.shape[0] // N_CHIPS, g.shape[1])
    return g.reshape(g.shape[0], N_CHIPS, g.shape[1] // N_CHIPS).transpose(1, 0, 2)


def halves_from_chips(t):
    return jnp.concatenate([t[0], t[1]], axis=1), jnp.concatenate([t[2], t[3]], axis=1)


def chips_from_halves(g, v):
    c = g.shape[1] // 2
    return jnp.stack([g[:, :c], g[:, c:], v[:, :c], v[:, c:]], axis=0)


def layer_fwd(x, p_l, rope, w, tag):
    S = x.shape[0]
    sv = {"x0": x}
    h = rmsnorm_fwd(x, w["norm_mix_g"], f"{tag}_norm_mix")
    proj = matmul(h, w["w_in_al"], "nn", F32, f"{tag}_proj")
    sv["h"], sv["proj"] = h, proj

    af_t = proj[:, OFF_F:OFF_F + FOX_HEADS].T
    f_cum = fox_prep_fwd(af_t, w["fox_forget_b"].reshape(FOX_HEADS, 1), f"{tag}_fox_prep")
    T = min(FOX_T, S)
    f_col = f_cum.reshape(FOX_HEADS, S, 1)
    f_row = f_cum.reshape(FOX_HEADS, S // T, 1, T)
    qkv = to_heads(proj[:, OFF_A:OFF_B].astype(BF16), 3 * FOX_HEADS, 1)
    qa, ka, va = qkv[:FOX_HEADS], qkv[FOX_HEADS:2 * FOX_HEADS], qkv[2 * FOX_HEADS:]
    oa_h, lse_a = fox_attn_fwd(qa, ka, va, f_col, f_row, f"{tag}_fox_fwd")
    o_a = from_heads(oa_h, 1)
    sv.update(af_t=af_t, f_col=f_col, f_row=f_row, qa=qa, ka=ka, va=va, oa_h=oa_h, lse_a=lse_a)

    o_b = shortconv_fwd(proj, w["shortconv_w"], f"{tag}_sconv_fwd")

    o_c = sgu_fwd(proj, w["sgu_norm_g"].reshape(1, SGU_WIDTH), w["sgu_w"], _sgu_bias(w["sgu_b"]), f"{tag}_sgu_fwd")

    cos, sa, sb = rope
    qk = rope_apply(proj, OFF_D, 2 * DIL_WIDTH, cos, sa, sb, BF16, f"{tag}_rope_fwd")
    vd = proj[:, OFF_D + 2 * DIL_WIDTH:OFF_D + 3 * DIL_WIDTH].astype(BF16)
    outs, lses, dil_sv = [], [], []
    for g, (window, dil) in enumerate(DIL_PATTERNS):
        sl = slice(g * DIL_OUT, (g + 1) * DIL_OUT)
        qg = to_heads(qk[:, sl], 4, dil)
        kg = to_heads(qk[:, DIL_WIDTH:][:, sl], 4, dil)
        vg = to_heads(vd[:, sl], 4, dil)
        nb = (S // dil) // DIL_SPAN
        og, lg = dil_attn_fwd(qg, kg, vg, nb, f"{tag}_dil{g}_fwd")
        dil_sv.append((qg, kg, vg, og, lg, nb))
        outs.append(_heads_unperm(og, dil))
        lses.append(_col_unperm(lg, dil))
    od_h = dil_merge_fwd(outs, lses, f"{tag}_dil_merge_fwd")
    o_d = from_heads(od_h, 1)
    sv.update(dil=dil_sv, outs=outs, lses=lses)

    o_d_pad = jnp.concatenate([o_d.astype(BF16), jnp.zeros((S, FOX_WIDTH - DIL_OUT), BF16)], axis=-1)
    o_stack = jnp.stack([o_a, o_b, o_c, o_d_pad], axis=0)
    merged = merge_fwd(o_stack, w["w_br"], proj, f"{tag}_merge_fwd")
    x1 = matmul(merged, w["w_out"], "nn", F32, f"{tag}_out_proj", res=x)
    sv.update(o_stack=o_stack, merged=merged, x1=x1)

    h2 = rmsnorm_fwd(x1, w["norm_ffn_g"], f"{tag}_norm_ffn")
    pre = (matmul(h2, w["w_up_g"], "nn", F32, f"{tag}_up_g"), matmul(h2, w["w_up_v"], "nn", F32, f"{tag}_up_v"))
    a = ffn_mid_fwd(pre[0], pre[1], w["ffn_conv_g"], w["ffn_conv_v"], f"{tag}_ffn_mid_fwd")
    x2 = matmul(a, w["w_down"], "nn", F32, f"{tag}_down", res=x1)
    sv.update(h2=h2, pre=pre, a=a, x2=x2)

    n3 = rmsnorm_fwd(x2, w["norm_ple_g"], f"{tag}_norm_ple")
    pg = matmul(n3, w["w_ple_gate"], "nn", F32, f"{tag}_ple_gate")
    pe = matmul(p_l, w["w_ple_proj"], "nn", F32, f"{tag}_ple_proj")
    x3 = ple_fwd(x2, pg, pe, f"{tag}_ple_fwd")
    sv.update(n3=n3, pg=pg, pe=pe, p_l=p_l)
    return x3, sv


def _sgu_bias(b):
    return jnp.pad(b.T, ((0, 0), (0, SGU_CHUNK - b.shape[0])))


def _col_unperm(col, dil):
    H, S, _ = col.shape
    return col.reshape(H, dil, S // dil).transpose(0, 2, 1).reshape(H, S, 1)


def _col_perm(col, dil):
    H, S, _ = col.shape
    return col.reshape(H, S // dil, dil).transpose(0, 2, 1).reshape(H, S, 1)


def _heads_perm(y, dil):
    H, S, C = y.shape
    return y.reshape(H, S // dil, dil, C).transpose(0, 2, 1, 3).reshape(H, S, C)


def _heads_unperm(y, dil):
    H, S, C = y.shape
    return y.reshape(H, dil, S // dil, C).transpose(0, 2, 1, 3).reshape(H, S, C)


def layer_bwd(dx3, sv, rope, w, tag):
    S = dx3.shape[0]
    gr = {}
    da, de = ple_bwd(sv["pg"], sv["pe"], dx3, f"{tag}_ple_bwd")
    gr["w_ple_proj"] = matmul(sv["p_l"], de, "tn", F32, f"{tag}_dw_ple_proj")
    gr["w_ple_gate"] = matmul(sv["n3"], da, "tn", F32, f"{tag}_dw_ple_gate")
    dn3 = matmul(da, w["w_ple_gate"], "nt", BF16, f"{tag}_dn3")
    dx2, gr["norm_ple_g"] = rmsnorm_bwd(sv["x2"], w["norm_ple_g"], dn3, dx3, f"{tag}_norm_ple_bwd")

    d_a = matmul(dx2, w["w_down"], "nt", F32, f"{tag}_da")
    gr["w_down"] = matmul(sv["a"], dx2, "tn", F32, f"{tag}_dw_down")
    dpre_g, dpre_v, dwc_g, dwc_v = ffn_mid_bwd(sv["pre"][0], sv["pre"][1], w["ffn_conv_g"], w["ffn_conv_v"], d_a,
                                               f"{tag}_ffn_mid_bwd")
    gr["ffn_conv_w"] = (dwc_g, dwc_v)
    gr["w_up"] = (matmul(sv["h2"], dpre_g, "tn", F32, f"{tag}_dw_up_g"),
                  matmul(sv["h2"], dpre_v, "tn", F32, f"{tag}_dw_up_v"))
    dh2_g = matmul(dpre_g, w["w_up_g"], "nt", F32, f"{tag}_dh2_g")
    dh2 = matmul(dpre_v, w["w_up_v"], "nt", BF16, f"{tag}_dh2_v", res=dh2_g)
    dx1, gr["norm_ffn_g"] = rmsnorm_bwd(sv["x1"], w["norm_ffn_g"], dh2, dx2, f"{tag}_norm_ffn_bwd")

    d_merged = matmul(dx1, w["w_out"], "nt", BF16, f"{tag}_dmerged")
    gr["w_out"] = matmul(sv["merged"], dx1, "tn", F32, f"{tag}_dw_out")
    proj = sv["proj"]
    dgl, dy = merge_bwd(sv["o_stack"], w["w_br"], proj, d_merged, f"{tag}_merge_bwd")
    d_o, d_wbr = [], []
    for b in range(N_BRANCH):
        d_o.append(matmul(dy[b], w["w_br"][b], "nt", F32, f"{tag}_do{b}"))
        d_wbr.append(matmul(sv["o_stack"][b], dy[b], "tn", F32, f"{tag}_dw_br{b}"))
    gr["w_br"] = d_wbr

    do_a = to_heads(d_o[0].astype(BF16), FOX_HEADS, 1)
    T = min(FOX_T, S)
    dqa, delta_a, d_fq = fox_attn_bwd_dq(sv["qa"], sv["ka"], sv["va"], sv["f_col"], sv["f_row"], sv["oa_h"],
                                         sv["lse_a"], do_a, f"{tag}_fox_dq")
    dka, dva, d_fk = fox_attn_bwd_dkv(sv["qa"], sv["ka"], sv["va"], sv["f_col"], sv["f_row"], rows_of(sv["lse_a"], T),
                                      rows_of(delta_a, T), do_a, f"{tag}_fox_dkv")
    daf_t, dfb = fox_prep_bwd(sv["af_t"], w["fox_forget_b"].reshape(FOX_HEADS, 1), d_fq.reshape(FOX_HEADS, S),
                              d_fk.reshape(FOX_HEADS, S), f"{tag}_fox_prep_bwd")
    gr["fox_forget_b"] = dfb.reshape(FOX_HEADS)
    d_proj_a = from_heads(jnp.concatenate([dqa, dka, dva], axis=0), 1).astype(BF16)

    dxb, dgb, dgc, gr["shortconv_w"] = shortconv_bwd(proj, w["shortconv_w"], d_o[1], f"{tag}_sconv_bwd")

    d_c, dsg, dsw, dsb = sgu_bwd(proj, w["sgu_norm_g"].reshape(1, SGU_WIDTH), w["sgu_w"],
                                 jnp.swapaxes(w["sgu_w"], 1, 2), _sgu_bias(w["sgu_b"]), d_o[2], f"{tag}_sgu_bwd")
    gr["sgu_norm_g"] = dsg.reshape(SGU_WIDTH)
    gr["sgu_w"] = dsw
    gr["sgu_b"] = dsb[:, :SGU_WIDTH // SGU_CHUNK].T

    d_od = to_heads(d_o[3][:, :DIL_OUT], 4, 1)
    d_outs_lses = dil_merge_bwd(sv["outs"], sv["lses"], d_od, f"{tag}_dil_merge_bwd")
    d_outs, d_lses = d_outs_lses[:3], d_outs_lses[3:]
    dq_parts, dk_parts, dv_parts = [], [], []
    for g, (window, dil) in enumerate(DIL_PATTERNS):
        qg, kg, vg, og, lg, nb = sv["dil"][g]
        do_g = _heads_perm(d_outs[g], dil)
        dl_g = _col_perm(d_lses[g], dil)
        dqg, delta_g = dil_attn_bwd_dq(qg, kg, vg, og, lg, do_g, dl_g, nb, f"{tag}_dil{g}_dq")
        dkg, dvg = dil_attn_bwd_dkv(qg, kg, vg, rows_of(lg, DIL_SPAN), rows_of(delta_g, DIL_SPAN), do_g, nb,
                                    f"{tag}_dil{g}_dkv")
        dq_parts.append(from_heads(dqg, dil))
        dk_parts.append(from_heads(dkg, dil))
        dv_parts.append(from_heads(dvg, dil))
    cos, sa, sb = rope
    d_qk_rot = jnp.concatenate(dq_parts + dk_parts, axis=-1)
    d_qk = rope_apply(d_qk_rot, 0, 2 * DIL_WIDTH, cos, -sa, -sb, BF16, f"{tag}_rope_bwd")
    d_vd = jnp.concatenate(dv_parts, axis=-1).astype(BF16)

    d_f_cols = jnp.concatenate([daf_t.T.astype(BF16), jnp.zeros((S, W_AL - OFF_F - FOX_HEADS), BF16)], axis=-1)
    d_proj = jnp.concatenate([dgl, d_proj_a, dxb, dgb, dgc, d_c, d_qk, d_vd, d_f_cols], axis=-1)
    gr["w_in_al"] = matmul(sv["h"], d_proj, "tn", F32, f"{tag}_dw_in", tm=2048, tn=512)
    dh = matmul(d_proj, w["w_in_al"], "nt", BF16, f"{tag}_dh")
    dx0, gr["norm_mix_g"] = rmsnorm_bwd(sv["x0"], w["norm_mix_g"], dh, dx1, f"{tag}_norm_mix_bwd")
    return dx0, gr


def local_weights(chips, repl, layer):
    w = {n: repl[n][layer] for n in REPLICATED}
    full = {n: chips_to_full(chips[n], n).astype(BF16)
            for n in ("w_br_fox", "w_br_conv", "w_br_sgu", "w_br_dil", "w_out", "w_down", "w_ple_gate", "w_ple_proj")}
    w["w_in_al"] = w_in_aligned_from_chips(chips["w_in"]).astype(BF16)
    w["shortconv_w"] = chips_to_full(chips["shortconv_w"], "shortconv_w").astype(F32)
    pad = jnp.zeros((FOX_WIDTH - DIL_OUT, D_MODEL), BF16)
    w["w_br"] = jnp.stack([full["w_br_fox"], full["w_br_conv"], full["w_br_sgu"],
                           jnp.concatenate([full["w_br_dil"], pad], axis=0)], axis=0)
    w["w_up_g"], w["w_up_v"] = halves_from_chips(chips["w_up"].astype(BF16))
    w["ffn_conv_g"], w["ffn_conv_v"] = halves_from_chips(chips["ffn_conv_w"].astype(F32))
    for n in ("w_out", "w_down", "w_ple_gate", "w_ple_proj"):
        w[n] = full[n]
    return w


def grads_to_chips(gr):
    out = {n: gr[n] for n in ("fox_forget_b", "sgu_norm_g", "sgu_w", "sgu_b")}
    out["norm_mix_g"] = gr["norm_mix_g"].reshape(D_MODEL)
    out["norm_ffn_g"] = gr["norm_ffn_g"].reshape(D_MODEL)
    out["norm_ple_g"] = gr["norm_ple_g"].reshape(D_MODEL)
    out["w_in"] = w_in_chips_from_aligned(gr["w_in_al"])
    out["w_up"] = chips_from_halves(*gr["w_up"])
    out["ffn_conv_w"] = chips_from_halves(*gr["ffn_conv_w"])
    for b, n in enumerate(("w_br_fox", "w_br_conv", "w_br_sgu")):
        out[n] = full_to_chips(gr["w_br"][b], n)
    out["w_br_dil"] = full_to_chips(gr["w_br"][3][:DIL_OUT], "w_br_dil")
    for n in ("shortconv_w", "w_out", "w_down", "w_ple_gate", "w_ple_proj"):
        out[n] = full_to_chips(gr[n], n)
    return out


def local_step(x, p, positions, chips, repl, final_norm_g, loss_target):
    depth = p.shape[0]
    rope = rope_tables(positions)
    saved, ws = [], []
    for layer in range(depth):
        w = local_weights(chips[layer], repl, layer)
        x, sv = layer_fwd(x, p[layer].astype(BF16), rope, w, f"l{layer}")
        saved.append(sv)
        ws.append(w)
    loss_part, dx, dgf = final_loss(x, final_norm_g, loss_target, "final_loss")
    grads = [None] * depth
    for layer in range(depth - 1, -1, -1):
        dx, gr = layer_bwd(dx, saved[layer], rope, ws[layer], f"l{layer}")
        grads[layer] = grads_to_chips(gr)
    return loss_part[0, 0], dx, grads, dgf.reshape(-1)


def _position():
    return lax.axis_index("x"), lax.axis_index("y"), lax.axis_index("c")


def _other_chips(x, y):
    return [(1 - x, y), (x, 1 - y), (1 - x, 1 - y)]


def _remote(src, dst, send_sem, recv_sem, device):
    return pltpu.make_async_remote_copy(src_ref=src, dst_ref=dst, send_sem=send_sem, recv_sem=recv_sem,
                                        device_id=device, device_id_type=MESH)


def _chip_index():
    return 2 * lax.axis_index("x") + lax.axis_index("y")


def _block_rows(rows, cols, unit):
    return _tile(rows, max(unit, (1 << 19) // cols // unit * unit), unit)


def gather_chip_shards(packs, name):
    n = len(packs)
    halves = [p.shape[0] // 2 for p in packs]

    def body(*refs):
        srcs, outs, (send_sems, recv_sems) = refs[:n], refs[n:2 * n], refs[2 * n:]
        x, y, c = _position()
        me = 2 * x + y
        sibling = (x, y, 1 - c)
        chips = _other_chips(x, y)

        def half(t, chip, core):
            return outs[t].at[chip, pl.ds(core * halves[t], halves[t]), :]

        started = []
        for t in range(n):
            for j, (px, py) in enumerate(chips):
                cp = _remote(srcs[t].at[pl.ds(c * halves[t], halves[t]), :], half(t, me, c),
                             send_sems.at[6 * t + j], recv_sems.at[6 * t + j], (px, py, c))
                cp.start()
                started.append(cp)
        for t in range(n):
            for j, (px, py) in enumerate(chips):
                k = 2 * px + py
                s = 6 * t + j
                _remote(half(t, k, c), half(t, k, c), send_sems.at[s], recv_sems.at[s], (px, py, c)).wait_recv()
                fwd = _remote(half(t, k, c), half(t, k, c), send_sems.at[s + 3], recv_sems.at[s + 3], sibling)
                fwd.start()
                started.append(fwd)
        for t in range(n):
            for j, (px, py) in enumerate(chips):
                k = 2 * px + py
                s = 6 * t + j + 3
                _remote(half(t, k, 1 - c), half(t, k, 1 - c), send_sems.at[s], recv_sems.at[s], sibling).wait_recv()
        for cp in started:
            cp.wait_send()

    return pl.pallas_call(
        body, name=name, in_specs=[ANY] * n, out_specs=[ANY] * n,
        out_shape=[SDS((N_CHIPS,) + p.shape, p.dtype) for p in packs],
        scratch_shapes=[pltpu.SemaphoreType.DMA((6 * n,)), pltpu.SemaphoreType.DMA((6 * n,))])(*packs)


def swap_halves_with_sibling(gs, name):
    n = len(gs)
    halves = [g.shape[1] // 2 for g in gs]

    def body(*refs):
        srcs, lands, (send_sems, recv_sems) = refs[:n], refs[n:2 * n], refs[2 * n:]
        x, y, c = _position()
        copies = [_remote(srcs[t].at[:, pl.ds((1 - c) * halves[t], halves[t]), :], lands[t], send_sems.at[t],
                          recv_sems.at[t], (x, y, 1 - c)) for t in range(n)]
        for cp in copies:
            cp.start()
        for cp in copies:
            cp.wait()

    return pl.pallas_call(
        body, name=name, in_specs=[ANY] * n, out_specs=[ANY] * n,
        out_shape=[SDS((g.shape[0], h, g.shape[2]), g.dtype) for g, h in zip(gs, halves)],
        scratch_shapes=[pltpu.SemaphoreType.DMA((n,)), pltpu.SemaphoreType.DMA((n,))])(*gs)


def add_my_half(g, other, out_dtype, name):
    n, R, C = g.shape
    H = R // 2
    tr = _block_rows(H, C, 16) if H % 16 == 0 else H
    nb = H // tr
    core = lax.axis_index("c").astype(jnp.int32).reshape(1)

    def body(c_ref, g_ref, o_ref, out_ref):
        out_ref[...] = (g_ref[...] + o_ref[...]).astype(out_ref.dtype)

    grid_spec = pltpu.PrefetchScalarGridSpec(
        num_scalar_prefetch=1, grid=(n, nb),
        in_specs=[BS((None, tr, C), lambda s, i, c_ref: (s, c_ref[0] * nb + i, 0)),
                  BS((None, tr, C), lambda s, i, c_ref: (s, i, 0))],
        out_specs=BS((None, tr, C), lambda s, i, c_ref: (s, i, 0)))
    return pl.pallas_call(body, name=name, grid_spec=grid_spec, out_shape=SDS((n, H, C), out_dtype),
                          compiler_params=_params(2))(core, g, other)


def exchange_slots_between_chips(parts, name):
    n = len(parts)

    def body(*refs):
        srcs, lands, (send_sems, recv_sems) = refs[:n], refs[n:2 * n], refs[2 * n:]
        x, y, c = _position()
        me = 2 * x + y
        chips = _other_chips(x, y)
        sends = []
        for t in range(n):
            for j, (px, py) in enumerate(chips):
                cp = _remote(srcs[t].at[2 * px + py], lands[t].at[me], send_sems.at[3 * t + j],
                             recv_sems.at[3 * t + j], (px, py, c))
                cp.start()
                sends.append(cp)
        for t in range(n):
            for j, (px, py) in enumerate(chips):
                k = 2 * px + py
                _remote(srcs[t].at[k], lands[t].at[k], send_sems.at[3 * t + j], recv_sems.at[3 * t + j],
                        (px, py, c)).wait_recv()
        for cp in sends:
            cp.wait_send()

    return pl.pallas_call(
        body, name=name, in_specs=[ANY] * n, out_specs=[ANY] * n,
        out_shape=[SDS(p.shape, p.dtype) for p in parts],
        scratch_shapes=[pltpu.SemaphoreType.DMA((3 * n,)), pltpu.SemaphoreType.DMA((3 * n,))])(*parts)


def sum_slots_into_my_half(parts, name):
    n, H, C = parts.shape
    tr = _block_rows(H, C, 16) if H % 16 == 0 else H
    nb = H // tr
    core = lax.axis_index("c").astype(jnp.int32).reshape(1)

    def body(c_ref, p_ref, o_ref):
        acc = p_ref[0].astype(F32)
        for k in range(1, n):
            acc = acc + p_ref[k].astype(F32)
        o_ref[...] = acc

    grid_spec = pltpu.PrefetchScalarGridSpec(
        num_scalar_prefetch=1, grid=(nb,), in_specs=[BS((n, tr, C), lambda i, c_ref: (0, i, 0))],
        out_specs=BS((tr, C), lambda i, c_ref: (c_ref[0] * nb + i, 0)))
    return pl.pallas_call(body, name=name, grid_spec=grid_spec, out_shape=SDS((2 * H, C), F32),
                          compiler_params=_params(1))(core, parts)


def sum_slots(parts, name):
    n, H, C = parts.shape
    tr = _tile(H, 256, 16)

    def body(p_ref, o_ref):
        acc = p_ref[0].astype(F32)
        for k in range(1, n):
            acc = acc + p_ref[k].astype(F32)
        o_ref[...] = acc

    return pl.pallas_call(
        body, name=name, grid=(H // tr,), in_specs=[BS((n, tr, C), lambda i: (0, i, 0))],
        out_specs=BS((tr, C), lambda i: (i, 0)), out_shape=SDS((H, C), F32), compiler_params=_params(1))(parts)


def join_halves_with_sibling(arrs, name):
    n = len(arrs)
    halves = [a.shape[0] // 2 for a in arrs]

    def body(*refs):
        outs, (send_sems, recv_sems) = refs[n:2 * n], refs[2 * n:]
        x, y, c = _position()

        def half(t, core):
            return outs[t].at[pl.ds(core * halves[t], halves[t]), :]

        sends = [_remote(half(t, c), half(t, c), send_sems.at[t], recv_sems.at[t], (x, y, 1 - c)) for t in range(n)]
        for cp in sends:
            cp.start()
        for t in range(n):
            _remote(half(t, 1 - c), half(t, 1 - c), send_sems.at[t], recv_sems.at[t], (x, y, 1 - c)).wait_recv()
        for cp in sends:
            cp.wait_send()

    return pl.pallas_call(
        body, name=name, in_specs=[ANY] * n, out_specs=[ANY] * n, out_shape=[SDS(a.shape, a.dtype) for a in arrs],
        input_output_aliases={t: t for t in range(n)},
        scratch_shapes=[pltpu.SemaphoreType.DMA((n,)), pltpu.SemaphoreType.DMA((n,))])(*arrs)


def reduce_scatter_grads(gs, tag):
    n = len(gs)
    me = _chip_index()
    others = swap_halves_with_sibling(gs, f"{tag}_swap")
    parts = [add_my_half(g, o, BF16 if t < n - 1 else F32, f"{tag}_pair_sum{t}")
             for t, (g, o) in enumerate(zip(gs, others))]
    landed = exchange_slots_between_chips(parts, f"{tag}_ici")
    landed = [lax.dynamic_update_index_in_dim(l, lax.dynamic_index_in_dim(p, me, 0, keepdims=False), me, 0)
              for l, p in zip(landed, parts)]
    sums = [sum_slots_into_my_half(l, f"{tag}_chip_sum{t}") for t, l in enumerate(landed)]
    return join_halves_with_sibling(sums, f"{tag}_join")


def gather_all_devices(pack, name):
    R, C = pack.shape

    def body(src, out, send_sems, recv_sems, local_sem):
        x, y, c = _position()
        me = 4 * x + 2 * y + c
        local = pltpu.make_async_copy(src, out.at[me], local_sem)
        local.start()
        peers = []
        for m in range(1, N_DEV):
            px = 1 - x if m & 4 else x
            py = 1 - y if m & 2 else y
            pc = 1 - c if m & 1 else c
            peers.append((px, py, pc))
        sends = [_remote(src, out.at[me], send_sems.at[j], recv_sems.at[j], peer) for j, peer in enumerate(peers)]
        for cp in sends:
            cp.start()
        for j, (px, py, pc) in enumerate(peers):
            k = 4 * px + 2 * py + pc
            _remote(src, out.at[k], send_sems.at[j], recv_sems.at[j], (px, py, pc)).wait_recv()
        for cp in sends:
            cp.wait_send()
        local.wait()

    return pl.pallas_call(
        body, name=name, in_specs=[ANY], out_specs=ANY, out_shape=SDS((N_DEV, R, C), pack.dtype),
        scratch_shapes=[pltpu.SemaphoreType.DMA((N_DEV - 1,)), pltpu.SemaphoreType.DMA((N_DEV - 1,)),
                        pltpu.SemaphoreType.DMA(())])(pack)


def adamw(w, g, m, v, name):
    shape = w.shape
    cols = shape[-1] if len(shape) > 1 else shape[0]
    rows = w.size // cols
    two = lambda t: t.reshape(rows, cols)
    tr = rows
    if rows * cols * 4 > (1 << 21):
        tr = _tile(rows, max(8, ((1 << 21) // (cols * 4)) // 8 * 8), 8)
    c1 = 1.0 / (1.0 - ADAM_B1 ** ADAM_STEP)
    c2 = 1.0 / (1.0 - ADAM_B2 ** ADAM_STEP)

    def body(w_ref, g_ref, m_ref, v_ref, d_ref, mo_ref, vo_ref):
        gv = g_ref[...]
        mn = ADAM_B1 * m_ref[...] + (1.0 - ADAM_B1) * gv
        vn = ADAM_B2 * v_ref[...] + (1.0 - ADAM_B2) * (gv * gv)
        d_ref[...] = -ADAM_LR * ((mn * c1) / (jnp.sqrt(vn * c2) + ADAM_EPS) + ADAM_WD * w_ref[...])
        mo_ref[...] = mn
        vo_ref[...] = vn

    blk = BS((tr, cols), lambda i: (i, 0))
    d, mo, vo = pl.pallas_call(
        body, name=name, grid=(rows // tr,), in_specs=[blk] * 4, out_specs=[blk] * 3,
        out_shape=[SDS((rows, cols), F32)] * 3, compiler_params=_params(1))(two(w), two(g), two(m), two(v))
    return d.reshape(shape), mo.reshape(shape), vo.reshape(shape)


def _rows_for(n, unit):
    rows = -(-n // PACK_COLS)
    return -(-rows // unit) * unit


ROWS_GROUP = ("w_out", "w_ple_gate", "w_down")
COLS_GROUP = ("w_br_fox", "w_br_conv", "w_br_sgu", "w_br_dil", "w_ple_proj")
SMALL_GROUP = ("shortconv_w", "ffn_conv_w")
SMALL_ROWS = 16


def group_shards(t, dtype):
    lead = t["w_in"].shape[:-2]
    small = jnp.concatenate([t[n].astype(F32).reshape(lead + (-1,)) for n in SMALL_GROUP], axis=-1)
    pad = jnp.zeros(lead + (SMALL_ROWS * PACK_COLS - small.shape[-1],), F32)
    small = jnp.concatenate([small, pad], axis=-1).reshape(lead + (SMALL_ROWS, PACK_COLS))
    return [t["w_in"].astype(dtype), t["w_up"].astype(dtype),
            jnp.concatenate([t[n].astype(dtype) for n in ROWS_GROUP], axis=-2),
            jnp.concatenate([t[n].astype(dtype) for n in COLS_GROUP], axis=-2), small]


def ungroup_shards(arrs, shard_shapes):
    w_in_s, w_up_s, rows, cols, small = arrs
    lead = w_in_s.shape[:-2]
    out = {"w_in": w_in_s, "w_up": w_up_s}
    for group, arr in ((ROWS_GROUP, rows), (COLS_GROUP, cols)):
        off = 0
        for n in group:
            r = shard_shapes[n][0]
            out[n] = arr[..., off:off + r, :]
            off += r
    flat = small.reshape(lead + (-1,))
    off = 0
    for n in SMALL_GROUP:
        size = shard_shapes[n][0] * shard_shapes[n][1]
        out[n] = flat[..., off:off + size].reshape(lead + shard_shapes[n])
        off += size
    return out


REPL_SHAPES = {"norm_mix_g": (D_MODEL,), "fox_forget_b": (FOX_HEADS,), "sgu_norm_g": (SGU_WIDTH,),
               "sgu_w": (4, SGU_CHUNK, SGU_CHUNK), "sgu_b": (4, SGU_CHUNK), "norm_ffn_g": (D_MODEL,),
               "norm_ple_g": (D_MODEL,)}


def kernel(x, p, positions, norm_mix_g, w_in, fox_forget_b, shortconv_w, sgu_norm_g, sgu_w, sgu_b, w_br_fox, w_br_conv, w_br_sgu, w_br_dil, w_out, norm_ffn_g, w_up, ffn_conv_w, w_down, norm_ple_g, w_ple_gate, w_ple_proj, final_norm_g, loss_target, m_norm_mix_g, m_w_in, m_fox_forget_b, m_shortconv_w, m_sgu_norm_g, m_sgu_w, m_sgu_b, m_w_br_fox, m_w_br_conv, m_w_br_sgu, m_w_br_dil, m_w_out, m_norm_ffn_g, m_w_up, m_ffn_conv_w, m_w_down, m_norm_ple_g, m_w_ple_gate, m_w_ple_proj, m_final_norm_g, v_norm_mix_g, v_w_in, v_fox_forget_b, v_shortconv_w, v_sgu_norm_g, v_sgu_w, v_sgu_b, v_w_br_fox, v_w_br_conv, v_w_br_sgu, v_w_br_dil, v_w_out, v_norm_ffn_g, v_w_up, v_ffn_conv_w, v_w_down, v_norm_ple_g, v_w_ple_gate, v_w_ple_proj, v_final_norm_g):
    weights = dict(norm_mix_g=norm_mix_g, w_in=w_in, fox_forget_b=fox_forget_b, shortconv_w=shortconv_w,
                   sgu_norm_g=sgu_norm_g, sgu_w=sgu_w, sgu_b=sgu_b, w_br_fox=w_br_fox, w_br_conv=w_br_conv,
                   w_br_sgu=w_br_sgu, w_br_dil=w_br_dil, w_out=w_out, norm_ffn_g=norm_ffn_g, w_up=w_up,
                   ffn_conv_w=ffn_conv_w, w_down=w_down, norm_ple_g=norm_ple_g, w_ple_gate=w_ple_gate,
                   w_ple_proj=w_ple_proj, final_norm_g=final_norm_g)
    mom1 = dict(norm_mix_g=m_norm_mix_g, w_in=m_w_in, fox_forget_b=m_fox_forget_b, shortconv_w=m_shortconv_w,
                sgu_norm_g=m_sgu_norm_g, sgu_w=m_sgu_w, sgu_b=m_sgu_b, w_br_fox=m_w_br_fox, w_br_conv=m_w_br_conv,
                w_br_sgu=m_w_br_sgu, w_br_dil=m_w_br_dil, w_out=m_w_out, norm_ffn_g=m_norm_ffn_g, w_up=m_w_up,
                ffn_conv_w=m_ffn_conv_w, w_down=m_w_down, norm_ple_g=m_norm_ple_g, w_ple_gate=m_w_ple_gate,
                w_ple_proj=m_w_ple_proj, final_norm_g=m_final_norm_g)
    mom2 = dict(norm_mix_g=v_norm_mix_g, w_in=v_w_in, fox_forget_b=v_fox_forget_b, shortconv_w=v_shortconv_w,
                sgu_norm_g=v_sgu_norm_g, sgu_w=v_sgu_w, sgu_b=v_sgu_b, w_br_fox=v_w_br_fox, w_br_conv=v_w_br_conv,
                w_br_sgu=v_w_br_sgu, w_br_dil=v_w_br_dil, w_out=v_w_out, norm_ffn_g=v_norm_ffn_g, w_up=v_w_up,
                ffn_conv_w=v_ffn_conv_w, w_down=v_w_down, norm_ple_g=v_norm_ple_g, w_ple_gate=v_w_ple_gate,
                w_ple_proj=v_w_ple_proj, final_norm_g=v_final_norm_g)
    depth = w_in.shape[0]
    shard_shapes = {n: tuple(weights[n].shape[1:]) for n in SHARDED}

    me = _chip_index()
    per_layer = []
    for layer in range(depth):
        packs = group_shards({n: weights[n][layer] for n in SHARDED}, BF16)
        gathered = gather_chip_shards(packs, f"gather_w{layer}")
        gathered = [lax.dynamic_update_index_in_dim(g, pk, me, 0) for g, pk in zip(gathered, packs)]
        per_layer.append(ungroup_shards(gathered, shard_shapes))
    repl = {n: weights[n] for n in REPLICATED}

    loss_part, grad_x, grads, d_final = local_step(x[0], p[:, 0], positions[0], per_layer, repl, final_norm_g,
                                                   loss_target[0])
    loss = lax.psum(loss_part, ("x", "y", "c"))

    g_sh = {n: [] for n in SHARDED}
    for layer in range(depth):
        slots = group_shards({n: grads[layer][n] for n in SHARDED}, F32)
        got = ungroup_shards(reduce_scatter_grads(slots, f"rs{layer}"), shard_shapes)
        for n in SHARDED:
            g_sh[n].append(got[n])
    grad_w = {n: jnp.stack(g_sh[n], axis=0) for n in SHARDED}

    flat = jnp.concatenate([grads[layer][n].astype(F32).reshape(-1) for layer in range(depth) for n in REPLICATED]
                           + [d_final])
    Rr = _rows_for(flat.shape[0], 16)
    packed = jnp.concatenate([flat, jnp.zeros((Rr * PACK_COLS - flat.shape[0],), F32)]).reshape(Rr, PACK_COLS)
    total = sum_slots(gather_all_devices(packed, "gather_repl"), "sum_repl").reshape(-1)
    off = 0
    g_rep = {n: [] for n in REPLICATED}
    for layer in range(depth):
        for n in REPLICATED:
            size = 1
            for s in REPL_SHAPES[n]:
                size *= s
            g_rep[n].append(total[off:off + size].reshape(REPL_SHAPES[n]))
            off += size
    for n in REPLICATED:
        grad_w[n] = jnp.stack(g_rep[n], axis=0)
    grad_w["final_norm_g"] = total[off:off + D_MODEL]

    deltas, new_m, new_v = {}, {}, {}
    for n in WEIGHTS:
        deltas[n], new_m[n], new_v[n] = adamw(weights[n], grad_w[n], mom1[n], mom2[n], f"adamw_{n}")
    return (loss, grad_x[None], *[grad_w[n] for n in WEIGHTS], *[deltas[n] for n in WEIGHTS],
            *[new_m[n] for n in WEIGHTS], *[new_v[n] for n in WEIGHTS])
```

```python
import functools

import jax
import jax.numpy as jnp
from jax import lax
from jax.experimental import pallas as pl
from jax.experimental.pallas import tpu as pltpu

F32 = jnp.float32
BF16 = jnp.bfloat16
MESH = pl.DeviceIdType.MESH
BS = pl.BlockSpec
SDS = jax.ShapeDtypeStruct
ANY = pl.BlockSpec(memory_space=pl.ANY)

VMEM_LIMIT_BYTES = 52 * 1024 * 1024
LANES = 128

D_MODEL = 2048
HEAD_DIM = 64
EPS = 1e-6
NEG = -1e30
FOX_HEADS = 8
FOX_WIDTH = 512
CONV_WIDTH = 512
SGU_WIDTH = 512
SGU_CHUNK = 128
DIL_PATTERNS = ((128, 1), (512, 4), (2048, 16))
DIL_SPAN = 128
DIL_HEADS = 12
DIL_WIDTH = 768
DIL_OUT = 256
ROPE_THETA = 500000.0
ROPE_DIM = 16
N_BRANCH = 4
D_FF = 5632
PLE_DIM = 256
D_IN = 14600

OFF_G, OFF_A, OFF_B, OFF_C, OFF_D, OFF_F, W_AL = 0, 8192, 9728, 11264, 12288, 14592, 14848
SRC_A, SRC_F, SRC_B, SRC_C, SRC_DQ, SRC_G = 0, 1536, 1544, 3080, 4104, 6408

ADAM_LR, ADAM_B1, ADAM_B2, ADAM_EPS, ADAM_WD, ADAM_STEP = 0.001, 0.9, 0.999, 1e-08, 0.01, 10

PACK_COLS = 1024
N_CHIPS = 4
N_DEV = 8

SHARDED = ("w_in", "shortconv_w", "w_br_fox", "w_br_conv", "w_br_sgu", "w_br_dil", "w_out", "w_up",
           "ffn_conv_w", "w_down", "w_ple_gate", "w_ple_proj")
ROW_SHARDED = ("w_out", "w_down", "w_ple_gate")
REPLICATED = ("norm_mix_g", "fox_forget_b", "sgu_norm_g", "sgu_w", "sgu_b", "norm_ffn_g", "norm_ple_g")
WEIGHTS = ("norm_mix_g", "w_in", "fox_forget_b", "shortconv_w", "sgu_norm_g", "sgu_w", "sgu_b", "w_br_fox",
           "w_br_conv", "w_br_sgu", "w_br_dil", "w_out", "norm_ffn_g", "w_up", "ffn_conv_w", "w_down",
           "norm_ple_g", "w_ple_gate", "w_ple_proj", "final_norm_g")


def _params(n_grid):
    return pltpu.CompilerParams(dimension_semantics=("arbitrary",) * n_grid, vmem_limit_bytes=VMEM_LIMIT_BYTES)


def _tile(n, pref, unit=LANES):
    best = None
    t = unit
    while t <= min(n, pref):
        if n % t == 0:
            best = t
        t += unit
    return best if best is not None else n


def _sigmoid(z):
    return 1.0 / (1.0 + jnp.exp(-z))


MAX_RESIDENT_K = 2048


def matmul(a, b, mode, out_dtype, name, res=None, tm=1024, tn=1024, tk=512):
    if mode == "nn":
        (M, K), (K2, N) = a.shape, b.shape
    elif mode == "nt":
        (M, K), (N, K2) = a.shape, b.shape
    else:
        (K, M), (K2, N) = a.shape, b.shape
    assert K == K2, (name, a.shape, b.shape)
    if mode != "tn" and K <= MAX_RESIDENT_K:
        tk = K
    tm, tn, tk = _tile(M, tm), _tile(N, tn), _tile(K, tk)
    nk = K // tk
    if mode == "nn":
        a_spec, b_spec = BS((tm, tk), lambda i, j, k: (i, k)), BS((tk, tn), lambda i, j, k: (k, j))
        dims = (((1,), (0,)), ((), ()))
    elif mode == "nt":
        a_spec, b_spec = BS((tm, tk), lambda i, j, k: (i, k)), BS((tn, tk), lambda i, j, k: (j, k))
        dims = (((1,), (1,)), ((), ()))
    else:
        a_spec, b_spec = BS((tk, tm), lambda i, j, k: (k, i)), BS((tk, tn), lambda i, j, k: (k, j))
        dims = (((0,), (0,)), ((), ()))
    has_res = res is not None

    def body(*refs):
        if has_res:
            a_ref, b_ref, r_ref, o_ref, acc = refs
        else:
            a_ref, b_ref, o_ref, acc = refs
        k = pl.program_id(2)

        @pl.when(k == 0)
        def _():
            acc[...] = jnp.zeros_like(acc)

        acc[...] += lax.dot_general(a_ref[...].astype(BF16), b_ref[...].astype(BF16), dims,
                                    preferred_element_type=F32)

        @pl.when(k == nk - 1)
        def _():
            r = acc[...]
            if has_res:
                r = r + r_ref[...]
            o_ref[...] = r.astype(o_ref.dtype)

    in_specs = [a_spec, b_spec]
    args = [a, b]
    if has_res:
        in_specs.append(BS((tm, tn), lambda i, j, k: (i, j)))
        args.append(res)
    return pl.pallas_call(
        body, name=name, grid=(M // tm, N // tn, nk), in_specs=in_specs,
        out_specs=BS((tm, tn), lambda i, j, k: (i, j)), out_shape=SDS((M, N), out_dtype),
        scratch_shapes=[pltpu.VMEM((tm, tn), F32)], compiler_params=_params(3))(*args)


def rmsnorm_fwd(x, g, name):
    S, Dm = x.shape
    tm = _tile(S, 256, 8)

    def body(x_ref, g_ref, y_ref):
        xf = x_ref[...]
        r = lax.rsqrt(jnp.mean(xf * xf, axis=-1, keepdims=True) + EPS)
        y_ref[...] = ((xf * r) * g_ref[...]).astype(y_ref.dtype)

    return pl.pallas_call(
        body, name=name, grid=(S // tm,),
        in_specs=[BS((tm, Dm), lambda i: (i, 0)), BS((1, Dm), lambda i: (0, 0))],
        out_specs=BS((tm, Dm), lambda i: (i, 0)), out_shape=SDS((S, Dm), BF16),
        compiler_params=_params(1))(x, g.reshape(1, Dm))


def rmsnorm_bwd(x, g, dy, dres, name):
    S, Dm = x.shape
    tm = _tile(S, 256, 8)

    def body(x_ref, g_ref, dy_ref, dres_ref, dx_ref, dg_ref):
        xf = x_ref[...]
        r = lax.rsqrt(jnp.mean(xf * xf, axis=-1, keepdims=True) + EPS)
        xh = xf * r
        dy = dy_ref[...].astype(F32)
        dxh = dy * g_ref[...]
        dx_ref[...] = r * (dxh - xh * jnp.mean(dxh * xh, axis=-1, keepdims=True)) + dres_ref[...]

        @pl.when(pl.program_id(0) == 0)
        def _():
            dg_ref[...] = jnp.zeros_like(dg_ref)

        dg_ref[...] += jnp.sum(dy * xh, axis=0, keepdims=True)

    row = BS((tm, Dm), lambda i: (i, 0))
    vec = BS((1, Dm), lambda i: (0, 0))
    return pl.pallas_call(
        body, name=name, grid=(S // tm,), in_specs=[row, vec, row, row], out_specs=[row, vec],
        out_shape=[SDS((S, Dm), F32), SDS((1, Dm), F32)], compiler_params=_params(1))(x, g.reshape(1, Dm), dy, dres)


def final_loss(x, g, target, name):
    S, Dm = x.shape
    tm = _tile(S, 256, 8)

    def body(x_ref, g_ref, t_ref, loss_ref, dx_ref, dg_ref):
        xf = x_ref[...]
        r = lax.rsqrt(jnp.mean(xf * xf, axis=-1, keepdims=True) + EPS)
        xh = xf * r
        gv = g_ref[...]
        err = xh * gv - t_ref[...]
        dy = err * (1.0 / Dm)
        dxh = dy * gv
        dx_ref[...] = r * (dxh - xh * jnp.mean(dxh * xh, axis=-1, keepdims=True))

        @pl.when(pl.program_id(0) == 0)
        def _():
            dg_ref[...] = jnp.zeros_like(dg_ref)
            loss_ref[...] = jnp.zeros_like(loss_ref)

        dg_ref[...] += jnp.sum(dy * xh, axis=0, keepdims=True)
        part = 0.5 * jnp.sum(jnp.mean(err * err, axis=-1, keepdims=True), axis=0, keepdims=True)
        loss_ref[...] += jnp.broadcast_to(part, loss_ref.shape)

    row = BS((tm, Dm), lambda i: (i, 0))
    vec = BS((1, Dm), lambda i: (0, 0))
    return pl.pallas_call(
        body, name=name, grid=(S // tm,), in_specs=[row, vec, row],
        out_specs=[BS((1, LANES), lambda i: (0, 0)), row, vec],
        out_shape=[SDS((1, LANES), F32), SDS((S, Dm), F32), SDS((1, Dm), F32)],
        compiler_params=_params(1))(x, g.reshape(1, Dm), target)


def _dot_f32(a, b):
    return jnp.dot(a, b, preferred_element_type=F32, precision=lax.Precision.HIGHEST)


def _dot(a, b):
    return jnp.dot(a, b, preferred_element_type=F32)


def _dot_nt(a, b):
    return lax.dot_general(a, b, (((1,), (1,)), ((), ())), preferred_element_type=F32)


def fox_prep_fwd(af_t, bias, name):
    H, S = af_t.shape
    nc = S // LANES

    def body(a_ref, b_ref, f_ref):
        z = a_ref[...] + b_ref[...]
        logf = jnp.minimum(z, 0.0) - jnp.log(1.0 + jnp.exp(-jnp.abs(z)))
        row = lax.broadcasted_iota(jnp.int32, (LANES, LANES), 0)
        col = lax.broadcasted_iota(jnp.int32, (LANES, LANES), 1)
        upper = (row <= col).astype(F32)
        carry = jnp.zeros((H, 1), F32)
        for c in range(nc):
            chunk = logf[:, c * LANES:(c + 1) * LANES]
            f_ref[:, c * LANES:(c + 1) * LANES] = _dot_f32(chunk, upper) + carry
            carry = carry + jnp.sum(chunk, axis=1, keepdims=True)

    full = BS((H, S), lambda i: (0, 0))
    return pl.pallas_call(
        body, name=name, grid=(1,), in_specs=[full, BS((H, 1), lambda i: (0, 0))], out_specs=full,
        out_shape=SDS((H, S), F32), compiler_params=_params(1))(af_t, bias)


def fox_prep_bwd(af_t, bias, d_fq, d_fk, name):
    H, S = af_t.shape
    nc = S // LANES

    def body(a_ref, b_ref, dfq_ref, dfk_ref, da_ref, db_ref):
        row = lax.broadcasted_iota(jnp.int32, (LANES, LANES), 0)
        col = lax.broadcasted_iota(jnp.int32, (LANES, LANES), 1)
        lower = (row >= col).astype(F32)
        carry = jnp.zeros((H, 1), F32)
        dbias = jnp.zeros((H, 1), F32)
        for c in range(nc - 1, -1, -1):
            sl = slice(c * LANES, (c + 1) * LANES)
            chunk = dfq_ref[:, sl] + dfk_ref[:, sl]
            dlogf = _dot_f32(chunk, lower) + carry
            carry = carry + jnp.sum(chunk, axis=1, keepdims=True)
            z = a_ref[:, sl] + b_ref[...]
            da = dlogf * _sigmoid(-z)
            da_ref[:, sl] = da
            dbias = dbias + jnp.sum(da, axis=1, keepdims=True)
        db_ref[...] = dbias

    full = BS((H, S), lambda i: (0, 0))
    vec = BS((H, 1), lambda i: (0, 0))
    return pl.pallas_call(
        body, name=name, grid=(1,), in_specs=[full, vec, full, full], out_specs=[full, vec],
        out_shape=[SDS((H, S), F32), SDS((H, 1), F32)], compiler_params=_params(1))(af_t, bias, d_fq, d_fk)


FOX_T = 256
FOX_HP = 2


def _causal_tile(T):
    return lax.broadcasted_iota(jnp.int32, (T, T), 1) <= lax.broadcasted_iota(jnp.int32, (T, T), 0)


class SideExchange:
    def __init__(self, operands, out_shapes, n_sems, start, finish):
        self.operands, self.out_shapes, self.n_sems, self.start, self.finish = operands, out_shapes, n_sems, start, finish


def _call_with_side(body, side, name, grid, in_specs, out_specs, out_shape, args):
    if side is None:
        return pl.pallas_call(body, name=name, grid=grid, in_specs=in_specs, out_specs=out_specs,
                              out_shape=out_shape, compiler_params=_params(2))(*args), []
    n_in, n_out = len(in_specs), len(out_specs)
    s_in, s_out = len(side.operands), len(side.out_shapes)

    def wrapped(*refs):
        main_in, side_in = refs[:n_in], refs[n_in:n_in + s_in]
        main_out = refs[n_in + s_in:n_in + s_in + n_out]
        side_out = refs[n_in + s_in + n_out:n_in + s_in + n_out + s_out]
        send_sems, recv_sems = refs[-2:]
        first = jnp.logical_and(pl.program_id(0) == 0, pl.program_id(1) == 0)
        last = jnp.logical_and(pl.program_id(0) == grid[0] - 1, pl.program_id(1) == grid[1] - 1)

        @pl.when(first)
        def _():
            side.start(side_in, side_out, send_sems, recv_sems)

        body(*main_in, *main_out)

        @pl.when(last)
        def _():
            side.finish(side_in, side_out, send_sems, recv_sems)

    outs = pl.pallas_call(
        wrapped, name=name, grid=grid, in_specs=list(in_specs) + [ANY] * s_in,
        out_specs=list(out_specs) + [ANY] * s_out, out_shape=list(out_shape) + list(side.out_shapes),
        scratch_shapes=[pltpu.SemaphoreType.DMA((side.n_sems,)), pltpu.SemaphoreType.DMA((side.n_sems,))],
        compiler_params=_params(2))(*args, *side.operands)
    return outs[:n_out], outs[n_out:]


def _fox_specs(H, S, Dh, T):
    nq = S // T
    blk = BS((FOX_HP, T, Dh), lambda h, i: (h, i, 0))
    full = BS((FOX_HP, S, Dh), lambda h, i: (h, 0, 0))
    colb = BS((FOX_HP, T, 1), lambda h, i: (h, i, 0))
    rowf = BS((FOX_HP, nq, 1, T), lambda h, i: (h, 0, 0, 0))
    return blk, full, colb, rowf, (H // FOX_HP, nq)


def fox_attn_fwd(q, k, v, f_col, f_row, name, side=None):
    H, S, Dh = q.shape
    T = min(FOX_T, S)
    scale = Dh ** -0.5

    def body(q_ref, k_ref, v_ref, fc_ref, fr_ref, o_ref, lse_ref):
        qi = pl.program_id(1)
        qs = [q_ref[h] for h in range(FOX_HP)]
        fqs = [fc_ref[h] for h in range(FOX_HP)]

        def step(j, carry, diagonal):
            off = pl.multiple_of(j * T, T)
            out = []
            for h in range(FOX_HP):
                m, l, acc = carry[h]
                kv = k_ref[h, pl.ds(off, T), :]
                vv = v_ref[h, pl.ds(off, T), :]
                s = _dot_nt(qs[h], kv) * scale + (fqs[h] - fr_ref[h, j])
                if diagonal:
                    s = jnp.where(_causal_tile(T), s, NEG)
                m_new = jnp.maximum(m, jnp.max(s, axis=-1, keepdims=True))
                p = jnp.exp(s - m_new)
                alpha = jnp.exp(m - m_new)
                l = alpha * l + jnp.sum(p, axis=-1, keepdims=True)
                acc = alpha * acc + _dot(p.astype(BF16), vv)
                out.append((m_new, l, acc))
            return tuple(out)

        init = tuple((jnp.full((T, 1), NEG, F32), jnp.zeros((T, 1), F32), jnp.zeros((T, Dh), F32))
                     for _ in range(FOX_HP))
        carry = lax.fori_loop(0, qi, functools.partial(step, diagonal=False), init)
        carry = step(qi, carry, True)
        for h in range(FOX_HP):
            m, l, acc = carry[h]
            o_ref[h] = (acc / l).astype(o_ref.dtype)
            lse_ref[h] = m + jnp.log(l)

    blk, full, colb, rowf, grid = _fox_specs(H, S, Dh, T)
    return _call_with_side(body, side, name, grid, [blk, full, full, colb, rowf], [blk, colb],
                           [SDS((H, S, Dh), BF16), SDS((H, S, 1), F32)], (q, k, v, f_col, f_row))


def fox_attn_bwd_dq(q, k, v, f_col, f_row, o, lse, do, name, side=None):
    H, S, Dh = q.shape
    T = min(FOX_T, S)
    scale = Dh ** -0.5

    def body(q_ref, k_ref, v_ref, fc_ref, fr_ref, o_ref, lse_ref, do_ref, dq_ref, dl_ref, df_ref):
        qi = pl.program_id(1)
        qs = [q_ref[h] for h in range(FOX_HP)]
        fqs = [fc_ref[h] for h in range(FOX_HP)]
        lses = [lse_ref[h] for h in range(FOX_HP)]
        dos = [do_ref[h] for h in range(FOX_HP)]
        deltas = [jnp.sum(dos[h].astype(F32) * o_ref[h].astype(F32), axis=-1, keepdims=True) for h in range(FOX_HP)]
        for h in range(FOX_HP):
            dl_ref[h] = deltas[h]

        def step(j, carry, diagonal):
            off = pl.multiple_of(j * T, T)
            out = []
            for h in range(FOX_HP):
                dq, dfq = carry[h]
                kv = k_ref[h, pl.ds(off, T), :]
                vv = v_ref[h, pl.ds(off, T), :]
                s = _dot_nt(qs[h], kv) * scale + (fqs[h] - fr_ref[h, j])
                if diagonal:
                    s = jnp.where(_causal_tile(T), s, NEG)
                p = jnp.exp(s - lses[h])
                ds = p * (_dot_nt(dos[h], vv) - deltas[h])
                out.append((dq + _dot(ds.astype(BF16), kv), dfq + jnp.sum(ds, axis=-1, keepdims=True)))
            return tuple(out)

        init = tuple((jnp.zeros((T, Dh), F32), jnp.zeros((T, 1), F32)) for _ in range(FOX_HP))
        carry = lax.fori_loop(0, qi, functools.partial(step, diagonal=False), init)
        carry = step(qi, carry, True)
        for h in range(FOX_HP):
            dq_ref[h] = carry[h][0] * scale
            df_ref[h] = carry[h][1]

    blk, full, colb, rowf, grid = _fox_specs(H, S, Dh, T)
    return _call_with_side(body, side, name, grid, [blk, full, full, colb, rowf, blk, colb, blk], [blk, colb, colb],
                           [SDS((H, S, Dh), F32), SDS((H, S, 1), F32), SDS((H, S, 1), F32)],
                           (q, k, v, f_col, f_row, o, lse, do))


def fox_attn_bwd_dkv(q, k, v, f_col, f_row, lse_row, delta_row, do, name, side=None):
    H, S, Dh = q.shape
    T = min(FOX_T, S)
    nq = S // T
    scale = Dh ** -0.5

    def body(q_ref, k_ref, v_ref, fc_ref, fr_ref, lse_ref, dl_ref, do_ref, dk_ref, dv_ref, df_ref):
        kj = pl.program_id(1)
        ks = [k_ref[h] for h in range(FOX_HP)]
        vs = [v_ref[h] for h in range(FOX_HP)]
        fks = [fc_ref[h] for h in range(FOX_HP)]

        def step(i, carry, diagonal):
            off = pl.multiple_of(i * T, T)
            out = []
            for h in range(FOX_HP):
                dk, dv, dfk = carry[h]
                qv = q_ref[h, pl.ds(off, T), :]
                dov = do_ref[h, pl.ds(off, T), :]
                st = _dot_nt(ks[h], qv) * scale + (fr_ref[h, i] - fks[h])
                if diagonal:
                    st = jnp.where(lax.broadcasted_iota(jnp.int32, (T, T), 0)
                                   <= lax.broadcasted_iota(jnp.int32, (T, T), 1), st, NEG)
                pt = jnp.exp(st - lse_ref[h, i])
                dv = dv + _dot(pt.astype(BF16), dov)
                dst = pt * (_dot_nt(vs[h], dov) - dl_ref[h, i])
                dk = dk + _dot(dst.astype(BF16), qv)
                out.append((dk, dv, dfk + jnp.sum(dst, axis=-1, keepdims=True)))
            return tuple(out)

        init = tuple((jnp.zeros((T, Dh), F32), jnp.zeros((T, Dh), F32), jnp.zeros((T, 1), F32))
                     for _ in range(FOX_HP))
        carry = step(kj, init, True)
        carry = lax.fori_loop(kj + 1, nq, functools.partial(step, diagonal=False), carry)
        for h in range(FOX_HP):
            dk_ref[h] = carry[h][0] * scale
            dv_ref[h] = carry[h][1]
            df_ref[h] = -carry[h][2]

    blk, full, colb, rowf, grid = _fox_specs(H, S, Dh, T)
    return _call_with_side(body, side, name, grid, [full, blk, blk, colb, rowf, rowf, rowf, full], [blk, blk, colb],
                           [SDS((H, S, Dh), F32), SDS((H, S, Dh), F32), SDS((H, S, 1), F32)],
                           (q, k, v, f_col, f_row, lse_row, delta_row, do))


HALO = 8
CONV_TM = 512


def _conv_ext(e, w_ref):
    return w_ref[0:1, :] * pltpu.roll(e, 2, 0) + w_ref[1:2, :] * pltpu.roll(e, 1, 0) + w_ref[2:3, :] * e


def _conv_t_ext(d, w_ref):
    n = d.shape[0]
    return w_ref[2:3, :] * d + w_ref[1:2, :] * pltpu.roll(d, n - 1, 0) + w_ref[0:1, :] * pltpu.roll(d, n - 2, 0)


def _time_specs(S, tm, width, col_block):
    per = tm // HALO
    last = S // HALO - 1
    cur = BS((tm, width), lambda j, i: (i, col_block(j)))
    prev = BS((HALO, width), lambda j, i: (jnp.maximum(i * per - 1, 0), col_block(j)))
    nxt = BS((HALO, width), lambda j, i: (jnp.minimum((i + 1) * per, last), col_block(j)))
    return cur, prev, nxt


def shortconv_fwd(proj, w, name):
    S = proj.shape[0]
    tm = _tile(S, CONV_TM, 8)
    nb = CONV_WIDTH // LANES
    b0 = OFF_B // LANES

    def body(xb_ref, xbp_ref, gb_ref, gc_ref, gcp_ref, w_ref, o_ref):
        keep = (pl.program_id(1) > 0).astype(F32)
        e = jnp.concatenate([gcp_ref[...] * xbp_ref[...] * keep, gc_ref[...] * xb_ref[...]], axis=0)
        o_ref[...] = (gb_ref[...] * _conv_ext(e, w_ref)[HALO:, :]).astype(o_ref.dtype)

    xb, xbp, _ = _time_specs(S, tm, LANES, lambda j: b0 + j)
    gb, _, _ = _time_specs(S, tm, LANES, lambda j: b0 + nb + j)
    gc, gcp, _ = _time_specs(S, tm, LANES, lambda j: b0 + 2 * nb + j)
    return pl.pallas_call(
        body, name=name, grid=(nb, S // tm),
        in_specs=[xb, xbp, gb, gc, gcp, BS((3, LANES), lambda j, i: (0, j))],
        out_specs=BS((tm, LANES), lambda j, i: (i, j)), out_shape=SDS((S, CONV_WIDTH), BF16),
        compiler_params=_params(2))(proj, proj, proj, proj, proj, w)


def shortconv_bwd(proj, w, do_b, name):
    S = proj.shape[0]
    tm = _tile(S, CONV_TM, 8)
    nt = S // tm
    nb = CONV_WIDTH // LANES
    b0 = OFF_B // LANES

    def body(xb_ref, xbp_ref, gb_ref, gbn_ref, gc_ref, gcp_ref, do_ref, don_ref, w_ref,
             dxb_ref, dgb_ref, dgc_ref, dw_ref):
        i = pl.program_id(1)
        keep_prev = (i > 0).astype(F32)
        keep_next = (i < nt - 1).astype(F32)
        xb, gb, gc = xb_ref[...], gb_ref[...], gc_ref[...]
        do = do_ref[...].astype(F32)
        u = gc * xb
        e = jnp.concatenate([gcp_ref[...] * xbp_ref[...] * keep_prev, u], axis=0)
        dgb_ref[...] = (do * _conv_ext(e, w_ref)[HALO:, :]).astype(dgb_ref.dtype)
        dcv = do * gb
        d_ext = jnp.concatenate([dcv, don_ref[...].astype(F32) * gbn_ref[...] * keep_next], axis=0)
        du = _conv_t_ext(d_ext, w_ref)[:tm, :]
        dgc_ref[...] = (du * xb).astype(dgc_ref.dtype)
        dxb_ref[...] = (du * gc).astype(dxb_ref.dtype)

        @pl.when(i == 0)
        def _():
            dw_ref[...] = jnp.zeros_like(dw_ref)

        dw_ref[0:1, :] += jnp.sum(dcv * pltpu.roll(e, 2, 0)[HALO:, :], axis=0, keepdims=True)
        dw_ref[1:2, :] += jnp.sum(dcv * pltpu.roll(e, 1, 0)[HALO:, :], axis=0, keepdims=True)
        dw_ref[2:3, :] += jnp.sum(dcv * u, axis=0, keepdims=True)

    xb, xbp, _ = _time_specs(S, tm, LANES, lambda j: b0 + j)
    gb, _, gbn = _time_specs(S, tm, LANES, lambda j: b0 + nb + j)
    gc, gcp, _ = _time_specs(S, tm, LANES, lambda j: b0 + 2 * nb + j)
    do, _, don = _time_specs(S, tm, LANES, lambda j: j)
    out = BS((tm, LANES), lambda j, i: (i, j))
    wspec = BS((3, LANES), lambda j, i: (0, j))
    return pl.pallas_call(
        body, name=name, grid=(nb, nt), in_specs=[xb, xbp, gb, gbn, gc, gcp, do, don, wspec],
        out_specs=[out, out, out, wspec],
        out_shape=[SDS((S, CONV_WIDTH), BF16)] * 3 + [SDS((3, CONV_WIDTH), F32)],
        compiler_params=_params(2))(proj, proj, proj, proj, proj, proj, do_b, do_b, w)


_GELU_C = 0.7978845608028654


def _gelu(x):
    return 0.5 * x * (1.0 + jnp.tanh(_GELU_C * (x + 0.044715 * x * x * x)))


def _gelu_grad(x):
    t = jnp.tanh(_GELU_C * (x + 0.044715 * x * x * x))
    return 0.5 * (1.0 + t) + 0.5 * x * (1.0 - t * t) * _GELU_C * (1.0 + 3.0 * 0.044715 * x * x)


def _tril_masks():
    row = lax.broadcasted_iota(jnp.int32, (SGU_CHUNK, SGU_CHUNK), 0)
    col = lax.broadcasted_iota(jnp.int32, (SGU_CHUNK, SGU_CHUNK), 1)
    return row >= col, row <= col


def _lane_column(mat, g):
    lane = lax.broadcasted_iota(jnp.int32, mat.shape, 1)
    return jnp.sum(jnp.where(lane == g, mat, 0.0), axis=-1, keepdims=True)


def sgu_fwd(proj, norm_g, w_s, b_t, name):
    S = proj.shape[0]
    T = SGU_CHUNK
    G = SGU_WIDTH // T
    c0 = OFF_C // (2 * SGU_WIDTH)

    def body(c_ref, g_ref, w_ref, b_ref, o_ref):
        u = _gelu(c_ref[:, 0:SGU_WIDTH])
        v = _gelu(c_ref[:, SGU_WIDTH:2 * SGU_WIDTH])
        r = lax.rsqrt(jnp.mean(v * v, axis=-1, keepdims=True) + EPS)
        vn = ((v * r) * g_ref[...]).astype(BF16)
        mask, _ = _tril_masks()
        bias = b_ref[...]
        for g in range(G):
            sl = slice(g * T, (g + 1) * T)
            wt = jnp.where(mask, w_ref[g], 0.0).astype(BF16)
            mixed = _dot(wt, vn[:, sl]) + _lane_column(bias, g)
            o_ref[:, sl] = (u[:, sl] * mixed).astype(o_ref.dtype)

    return pl.pallas_call(
        body, name=name, grid=(S // T,),
        in_specs=[BS((T, 2 * SGU_WIDTH), lambda i: (i, c0)), BS((1, SGU_WIDTH), lambda i: (0, 0)),
                  BS((G, T, T), lambda i: (0, 0, 0)), BS((T, T), lambda i: (0, 0))],
        out_specs=BS((T, SGU_WIDTH), lambda i: (i, 0)), out_shape=SDS((S, SGU_WIDTH), BF16),
        compiler_params=_params(1))(proj, norm_g, w_s, b_t)


def sgu_bwd(proj, norm_g, w_s, w_st, b_t, do_c, name):
    S = proj.shape[0]
    T = SGU_CHUNK
    G = SGU_WIDTH // T
    c0 = OFF_C // (2 * SGU_WIDTH)

    def body(c_ref, g_ref, w_ref, wt_ref, b_ref, do_ref, dc_ref, dg_ref, dw_ref, db_ref):
        cu = c_ref[:, 0:SGU_WIDTH]
        cv = c_ref[:, SGU_WIDTH:2 * SGU_WIDTH]
        u = _gelu(cu)
        v = _gelu(cv)
        r = lax.rsqrt(jnp.mean(v * v, axis=-1, keepdims=True) + EPS)
        xh = v * r
        gv = g_ref[...]
        vn = (xh * gv).astype(BF16)
        do = do_ref[...].astype(F32)
        mask, mask_t = _tril_masks()
        bias = b_ref[...]
        lane = lax.broadcasted_iota(jnp.int32, (T, T), 1)

        @pl.when(pl.program_id(0) == 0)
        def _():
            dg_ref[...] = jnp.zeros_like(dg_ref)
            dw_ref[...] = jnp.zeros_like(dw_ref)
            db_ref[...] = jnp.zeros_like(db_ref)

        dvn_parts = []
        du_parts = []
        dbias = jnp.zeros((T, T), F32)
        for g in range(G):
            sl = slice(g * T, (g + 1) * T)
            wt = jnp.where(mask, w_ref[g], 0.0).astype(BF16)
            mixed = _dot(wt, vn[:, sl]) + _lane_column(bias, g)
            du_parts.append(do[:, sl] * mixed)
            dmix = do[:, sl] * u[:, sl]
            dmix_b = dmix.astype(BF16)
            dw_ref[g] += jnp.where(mask, _dot_nt(dmix_b, vn[:, sl]), 0.0)
            dbias = dbias + jnp.where(lane == g, jnp.sum(dmix, axis=-1, keepdims=True), 0.0)
            wtt = jnp.where(mask_t, wt_ref[g], 0.0).astype(BF16)
            dvn_parts.append(_dot(wtt, dmix_b))
        db_ref[...] += dbias
        dvn = jnp.concatenate(dvn_parts, axis=-1)
        du = jnp.concatenate(du_parts, axis=-1)
        dg_ref[...] += jnp.sum(dvn * xh, axis=0, keepdims=True)
        dxh = dvn * gv
        dv = r * (dxh - xh * jnp.mean(dxh * xh, axis=-1, keepdims=True))
        dc_ref[:, 0:SGU_WIDTH] = (du * _gelu_grad(cu)).astype(dc_ref.dtype)
        dc_ref[:, SGU_WIDTH:2 * SGU_WIDTH] = (dv * _gelu_grad(cv)).astype(dc_ref.dtype)

    wspec = BS((G, T, T), lambda i: (0, 0, 0))
    gspec = BS((1, SGU_WIDTH), lambda i: (0, 0))
    return pl.pallas_call(
        body, name=name, grid=(S // T,),
        in_specs=[BS((T, 2 * SGU_WIDTH), lambda i: (i, c0)), gspec, wspec, wspec, BS((T, T), lambda i: (0, 0)),
                  BS((T, SGU_WIDTH), lambda i: (i, 0))],
        out_specs=[BS((T, 2 * SGU_WIDTH), lambda i: (i, 0)), gspec, wspec, BS((T, T), lambda i: (0, 0))],
        out_shape=[SDS((S, 2 * SGU_WIDTH), BF16), SDS((1, SGU_WIDTH), F32), SDS((G, T, T), F32), SDS((T, T), F32)],
        compiler_params=_params(1))(proj, norm_g, w_s, w_st, b_t, do_c)


def rope_tables(positions):
    half = ROPE_DIM // 2
    inv = ROPE_THETA ** (-jnp.arange(half, dtype=F32) * (2.0 / ROPE_DIM))
    ang = positions.astype(F32)[:, None] * inv
    cos, sin = jnp.cos(ang), jnp.sin(ang)
    S = positions.shape[0]
    ones = jnp.ones((S, HEAD_DIM - ROPE_DIM), F32)
    zeros = jnp.zeros((S, HEAD_DIM - ROPE_DIM), F32)
    zh = jnp.zeros((S, half), F32)
    c = jnp.concatenate([cos, cos, ones], axis=-1)
    sa = jnp.concatenate([zh, sin, zeros], axis=-1)
    sb = jnp.concatenate([-sin, zh, zeros], axis=-1)
    tile2 = lambda t: jnp.concatenate([t, t], axis=-1)
    return tile2(c), tile2(sa), tile2(sb)


def rope_apply(x, col0, ncols, cos, sa, sb, out_dtype, name):
    S = x.shape[0]
    tm = _tile(S, 512, 8)
    half = ROPE_DIM // 2
    b0 = col0 // LANES

    def body(x_ref, c_ref, sa_ref, sb_ref, o_ref):
        xv = x_ref[...].astype(F32)
        o_ref[...] = (xv * c_ref[...] + pltpu.roll(xv, half, 1) * sa_ref[...]
                      + pltpu.roll(xv, LANES - half, 1) * sb_ref[...]).astype(o_ref.dtype)

    tab = BS((tm, LANES), lambda i, j: (i, 0))
    return pl.pallas_call(
        body, name=name, grid=(S // tm, ncols // LANES),
        in_specs=[BS((tm, LANES), lambda i, j: (i, b0 + j)), tab, tab, tab],
        out_specs=BS((tm, LANES), lambda i, j: (i, j)), out_shape=SDS((S, ncols), out_dtype),
        compiler_params=_params(2))(x, cos, sa, sb)


def _dil_masks(first):
    qi = lax.broadcasted_iota(jnp.int32, (DIL_SPAN, DIL_SPAN), 0)
    ki = lax.broadcasted_iota(jnp.int32, (DIL_SPAN, DIL_SPAN), 1)
    return ki >= qi + first.astype(jnp.int32) * DIL_SPAN, ki <= qi


def dil_attn_fwd(q, k, v, nb, name):
    H, S, Dh = q.shape
    T = DIL_SPAN
    nblk = S // T
    scale = Dh ** -0.5

    def body(q_ref, kp_ref, kc_ref, vp_ref, vc_ref, o_ref, lse_ref):
        b = pl.program_id(1)
        mp, mc = _dil_masks(b % nb == 0)
        qv = q_ref[...]
        sp = jnp.where(mp, _dot_nt(qv, kp_ref[...]) * scale, NEG)
        sc = jnp.where(mc, _dot_nt(qv, kc_ref[...]) * scale, NEG)
        m = jnp.maximum(jnp.max(sp, axis=-1, keepdims=True), jnp.max(sc, axis=-1, keepdims=True))
        ep = jnp.exp(sp - m)
        ec = jnp.exp(sc - m)
        l = jnp.sum(ep, axis=-1, keepdims=True) + jnp.sum(ec, axis=-1, keepdims=True)
        o_ref[...] = (_dot(ep.astype(BF16), vp_ref[...]) + _dot(ec.astype(BF16), vc_ref[...])) / l
        lse_ref[...] = m + jnp.log(l)

    cur = BS((None, T, Dh), lambda h, b: (h, b, 0))
    prev = BS((None, T, Dh), lambda h, b: (h, jnp.maximum(b - 1, 0), 0))
    colb = BS((None, T, 1), lambda h, b: (h, b, 0))
    return pl.pallas_call(
        body, name=name, grid=(H, nblk), in_specs=[cur, prev, cur, prev, cur], out_specs=[cur, colb],
        out_shape=[SDS((H, S, Dh), F32), SDS((H, S, 1), F32)], compiler_params=_params(2))(q, k, k, v, v)


def dil_attn_bwd_dq(q, k, v, o, lse, do, dlse, nb, name):
    H, S, Dh = q.shape
    T = DIL_SPAN
    nblk = S // T
    scale = Dh ** -0.5

    def body(q_ref, kp_ref, kc_ref, vp_ref, vc_ref, o_ref, lse_ref, do_ref, dlse_ref, dq_ref, dl_ref):
        b = pl.program_id(1)
        mp, mc = _dil_masks(b % nb == 0)
        qv = q_ref[...]
        kp, kc = kp_ref[...], kc_ref[...]
        dov = do_ref[...]
        delta = jnp.sum(dov * o_ref[...], axis=-1, keepdims=True) - dlse_ref[...]
        dl_ref[...] = delta
        dob = dov.astype(BF16)
        lse_v = lse_ref[...]
        pp = jnp.exp(jnp.where(mp, _dot_nt(qv, kp) * scale, NEG) - lse_v)
        pc = jnp.exp(jnp.where(mc, _dot_nt(qv, kc) * scale, NEG) - lse_v)
        dsp = pp * (_dot_nt(dob, vp_ref[...]) - delta)
        dsc = pc * (_dot_nt(dob, vc_ref[...]) - delta)
        dq_ref[...] = (_dot(dsp.astype(BF16), kp) + _dot(dsc.astype(BF16), kc)) * scale

    cur = BS((None, T, Dh), lambda h, b: (h, b, 0))
    prev = BS((None, T, Dh), lambda h, b: (h, jnp.maximum(b - 1, 0), 0))
    colb = BS((None, T, 1), lambda h, b: (h, b, 0))
    return pl.pallas_call(
        body, name=name, grid=(H, nblk), in_specs=[cur, prev, cur, prev, cur, cur, colb, cur, colb],
        out_specs=[cur, colb], out_shape=[SDS((H, S, Dh), F32), SDS((H, S, 1), F32)],
        compiler_params=_params(2))(q, k, k, v, v, o, lse, do, dlse)


def dil_attn_bwd_dkv(q, k, v, lse_row, delta_row, do, nb, name):
    H, S, Dh = q.shape
    T = DIL_SPAN
    nblk = S // T
    scale = Dh ** -0.5

    def body(k_ref, v_ref, qc_ref, qn_ref, doc_ref, don_ref, lc_ref, ln_ref, dc_ref, dn_ref, dk_ref, dv_ref):
        c = pl.program_id(1)
        no_next = ((c + 1) % nb == 0).astype(jnp.int32)
        si = lax.broadcasted_iota(jnp.int32, (T, T), 0)
        ti = lax.broadcasted_iota(jnp.int32, (T, T), 1)
        m_cur = si <= ti
        m_next = si >= ti + no_next * T
        kv, vv = k_ref[...], v_ref[...]
        qc, qn = qc_ref[...], qn_ref[...]
        doc, don = doc_ref[...].astype(BF16), don_ref[...].astype(BF16)
        pt = jnp.exp(jnp.where(m_cur, _dot_nt(kv, qc) * scale, NEG) - lc_ref[...])
        ptn = jnp.exp(jnp.where(m_next, _dot_nt(kv, qn) * scale, NEG) - ln_ref[...])
        dv_ref[...] = _dot(pt.astype(BF16), doc) + _dot(ptn.astype(BF16), don)
        dst = pt * (_dot_nt(vv, doc) - dc_ref[...])
        dstn = ptn * (_dot_nt(vv, don) - dn_ref[...])
        dk_ref[...] = (_dot(dst.astype(BF16), qc) + _dot(dstn.astype(BF16), qn)) * scale

    cur = BS((None, T, Dh), lambda h, b: (h, b, 0))
    nxt = BS((None, T, Dh), lambda h, b: (h, jnp.minimum(b + 1, nblk - 1), 0))
    rcur = BS((None, None, 1, T), lambda h, b: (h, b, 0, 0))
    rnxt = BS((None, None, 1, T), lambda h, b: (h, jnp.minimum(b + 1, nblk - 1), 0, 0))
    return pl.pallas_call(
        body, name=name, grid=(H, nblk), in_specs=[cur, cur, cur, nxt, cur, nxt, rcur, rnxt, rcur, rnxt],
        out_specs=[cur, cur], out_shape=[SDS((H, S, Dh), F32), SDS((H, S, Dh), F32)],
        compiler_params=_params(2))(k, v, q, q, do, do, lse_row, lse_row, delta_row, delta_row)


def dil_merge_fwd(outs, lses, name):
    H, S, Dh = outs[0].shape
    tm = _tile(S, 512, 8)

    def body(o0, o1, o2, l0, l1, l2, out_ref):
        ls = [l0[...], l1[...], l2[...]]
        m = jnp.maximum(jnp.maximum(ls[0], ls[1]), ls[2])
        es = [jnp.exp(l - m) for l in ls]
        den = es[0] + es[1] + es[2]
        out_ref[...] = (es[0] * o0[...] + es[1] * o1[...] + es[2] * o2[...]) / den

    blk = BS((None, tm, Dh), lambda h, i: (h, i, 0))
    colb = BS((None, tm, 1), lambda h, i: (h, i, 0))
    return pl.pallas_call(
        body, name=name, grid=(H, S // tm), in_specs=[blk] * 3 + [colb] * 3, out_specs=blk,
        out_shape=SDS((H, S, Dh), F32), compiler_params=_params(2))(*outs, *lses)


def dil_merge_bwd(outs, lses, d_out, name):
    H, S, Dh = outs[0].shape
    tm = _tile(S, 512, 8)

    def body(o0, o1, o2, l0, l1, l2, d_ref, do0, do1, do2, dl0, dl1, dl2):
        ls = [l0[...], l1[...], l2[...]]
        m = jnp.maximum(jnp.maximum(ls[0], ls[1]), ls[2])
        es = [jnp.exp(l - m) for l in ls]
        den = es[0] + es[1] + es[2]
        ws = [e / den for e in es]
        dv = d_ref[...]
        dws = [jnp.sum(dv * o[...], axis=-1, keepdims=True) for o in (o0, o1, o2)]
        mean = ws[0] * dws[0] + ws[1] * dws[1] + ws[2] * dws[2]
        for w, dw, do_ref, dl_ref in zip(ws, dws, (do0, do1, do2), (dl0, dl1, dl2)):
            do_ref[...] = w * dv
            dl_ref[...] = w * (dw - mean)

    blk = BS((None, tm, Dh), lambda h, i: (h, i, 0))
    colb = BS((None, tm, 1), lambda h, i: (h, i, 0))
    return pl.pallas_call(
        body, name=name, grid=(H, S // tm), in_specs=[blk] * 3 + [colb] * 3 + [blk],
        out_specs=[blk] * 3 + [colb] * 3,
        out_shape=[SDS((H, S, Dh), F32)] * 3 + [SDS((H, S, 1), F32)] * 3,
        compiler_params=_params(2))(*outs, *lses, d_out)


MERGE_TN = 512


def merge_fwd(o_stack, w_br, proj, name):
    _, S, K = o_stack.shape
    tm = _tile(S, 512, 8)
    tn = MERGE_TN
    nj = D_MODEL // tn

    def body(o_ref, w_ref, gl_ref, m_ref, acc):
        br = pl.program_id(2)

        @pl.when(br == 0)
        def _():
            acc[...] = jnp.zeros_like(acc)

        acc[...] += _sigmoid(gl_ref[...]) * _dot(o_ref[...], w_ref[...])

        @pl.when(br == N_BRANCH - 1)
        def _():
            m_ref[...] = acc[...].astype(m_ref.dtype)

    return pl.pallas_call(
        body, name=name, grid=(S // tm, nj, N_BRANCH),
        in_specs=[BS((None, tm, K), lambda i, j, b: (b, i, 0)), BS((None, K, tn), lambda i, j, b: (b, 0, j)),
                  BS((tm, tn), lambda i, j, b: (i, b * nj + j))],
        out_specs=BS((tm, tn), lambda i, j, b: (i, j)), out_shape=SDS((S, D_MODEL), BF16),
        scratch_shapes=[pltpu.VMEM((tm, tn), F32)], compiler_params=_params(3))(o_stack, w_br, proj)


def merge_bwd(o_stack, w_br, proj, d_merged, name):
    _, S, K = o_stack.shape
    tm = _tile(S, 512, 8)
    tn = MERGE_TN
    nj = D_MODEL // tn

    def body(o_ref, w_ref, gl_ref, dm_ref, dgl_ref, dy_ref):
        gate = _sigmoid(gl_ref[...])
        y = _dot(o_ref[...], w_ref[...])
        dm = dm_ref[...].astype(F32)
        dgl_ref[...] = (dm * y * gate * (1.0 - gate)).astype(dgl_ref.dtype)
        dy_ref[...] = (dm * gate).astype(dy_ref.dtype)

    return pl.pallas_call(
        body, name=name, grid=(S // tm, nj, N_BRANCH),
        in_specs=[BS((None, tm, K), lambda i, j, b: (b, i, 0)), BS((None, K, tn), lambda i, j, b: (b, 0, j)),
                  BS((tm, tn), lambda i, j, b: (i, b * nj + j)), BS((tm, tn), lambda i, j, b: (i, j))],
        out_specs=[BS((tm, tn), lambda i, j, b: (i, b * nj + j)), BS((None, tm, tn), lambda i, j, b: (b, i, j))],
        out_shape=[SDS((S, N_BRANCH * D_MODEL), BF16), SDS((N_BRANCH, S, D_MODEL), BF16)],
        compiler_params=_params(3))(o_stack, w_br, proj, d_merged)


FFN_CW = 256


def ffn_mid_fwd(pre_g, pre_v, w_g, w_v, name):
    S = pre_g.shape[0]
    tm = _tile(S, CONV_TM, 8)

    def body(g_ref, gp_ref, v_ref, vp_ref, wg_ref, wv_ref, a_ref):
        keep = (pl.program_id(1) > 0).astype(F32)
        ug = _conv_ext(jnp.concatenate([gp_ref[...] * keep, g_ref[...]], axis=0), wg_ref)[HALO:, :]
        uv = _conv_ext(jnp.concatenate([vp_ref[...] * keep, v_ref[...]], axis=0), wv_ref)[HALO:, :]
        a_ref[...] = (ug * _sigmoid(ug) * uv).astype(a_ref.dtype)

    cur, prev, _ = _time_specs(S, tm, FFN_CW, lambda j: j)
    wspec = BS((3, FFN_CW), lambda j, i: (0, j))
    return pl.pallas_call(
        body, name=name, grid=(D_FF // FFN_CW, S // tm), in_specs=[cur, prev, cur, prev, wspec, wspec],
        out_specs=cur, out_shape=SDS((S, D_FF), BF16), compiler_params=_params(2))(
            pre_g, pre_g, pre_v, pre_v, w_g, w_v)


def ffn_mid_bwd(pre_g, pre_v, w_g, w_v, d_a, name):
    S = pre_g.shape[0]
    tm = _tile(S, CONV_TM, 8)
    nt = S // tm

    def body(g_ref, gp_ref, gn_ref, v_ref, vp_ref, vn_ref, wg_ref, wv_ref, da_ref, dan_ref,
             dg_ref, dv_ref, dwg_ref, dwv_ref):
        i = pl.program_id(1)
        keep_prev = (i > 0).astype(F32)
        keep_next = (i < nt - 1).astype(F32)
        eg = jnp.concatenate([gp_ref[...] * keep_prev, g_ref[...], gn_ref[...]], axis=0)
        ev = jnp.concatenate([vp_ref[...] * keep_prev, v_ref[...], vn_ref[...]], axis=0)
        ug = _conv_ext(eg, wg_ref)
        uv = _conv_ext(ev, wv_ref)
        da = jnp.concatenate([jnp.zeros((HALO, FFN_CW), F32), da_ref[...], dan_ref[...] * keep_next], axis=0)
        sg = _sigmoid(ug)
        dug = da * uv * (sg * (1.0 + ug * (1.0 - sg)))
        duv = da * (ug * sg)
        dg_ref[...] = _conv_t_ext(dug, wg_ref)[HALO:HALO + tm, :].astype(dg_ref.dtype)
        dv_ref[...] = _conv_t_ext(duv, wv_ref)[HALO:HALO + tm, :].astype(dv_ref.dtype)

        @pl.when(i == 0)
        def _():
            dwg_ref[...] = jnp.zeros_like(dwg_ref)
            dwv_ref[...] = jnp.zeros_like(dwv_ref)

        for dup_e, e, dw_ref in ((dug, eg, dwg_ref), (duv, ev, dwv_ref)):
            dup = dup_e[HALO:HALO + tm, :]
            dw_ref[0:1, :] += jnp.sum(dup * pltpu.roll(e, 2, 0)[HALO:HALO + tm, :], axis=0, keepdims=True)
            dw_ref[1:2, :] += jnp.sum(dup * pltpu.roll(e, 1, 0)[HALO:HALO + tm, :], axis=0, keepdims=True)
            dw_ref[2:3, :] += jnp.sum(dup * e[HALO:HALO + tm, :], axis=0, keepdims=True)

    cur, prev, nxt = _time_specs(S, tm, FFN_CW, lambda j: j)
    wspec = BS((3, FFN_CW), lambda j, i: (0, j))
    return pl.pallas_call(
        body, name=name, grid=(D_FF // FFN_CW, nt),
        in_specs=[cur, prev, nxt, cur, prev, nxt, wspec, wspec, cur, nxt],
        out_specs=[cur, cur, wspec, wspec],
        out_shape=[SDS((S, D_FF), BF16)] * 2 + [SDS((3, D_FF), F32)] * 2,
        compiler_params=_params(2))(pre_g, pre_g, pre_g, pre_v, pre_v, pre_v, w_g, w_v, d_a, d_a)


def ple_fwd(x, a, e, name):
    S, Dm = x.shape
    tm = _tile(S, 256, 8)

    def body(x_ref, a_ref, e_ref, o_ref):
        o_ref[...] = x_ref[...] + _sigmoid(a_ref[...]) * e_ref[...]

    row = BS((tm, Dm), lambda i: (i, 0))
    return pl.pallas_call(body, name=name, grid=(S // tm,), in_specs=[row] * 3, out_specs=row,
                          out_shape=SDS((S, Dm), F32), compiler_params=_params(1))(x, a, e)


def ple_bwd(a, e, dx, name):
    S, Dm = a.shape
    tm = _tile(S, 256, 8)

    def body(a_ref, e_ref, dx_ref, da_ref, de_ref):
        s = _sigmoid(a_ref[...])
        d = dx_ref[...]
        da_ref[...] = (d * e_ref[...] * s * (1.0 - s)).astype(da_ref.dtype)
        de_ref[...] = (d * s).astype(de_ref.dtype)

    row = BS((tm, Dm), lambda i: (i, 0))
    return pl.pallas_call(body, name=name, grid=(S // tm,), in_specs=[row] * 3, out_specs=[row, row],
                          out_shape=[SDS((S, Dm), BF16)] * 2, compiler_params=_params(1))(a, e, dx)


def to_heads(x, n_heads, dil):
    S = x.shape[0]
    x = x.reshape(S // dil, dil, n_heads, HEAD_DIM).transpose(2, 1, 0, 3)
    return x.reshape(n_heads, S, HEAD_DIM)


def from_heads(y, dil):
    H, S, C = y.shape
    y = y.reshape(H, dil, S // dil, C).transpose(2, 1, 0, 3)
    return y.reshape(S, H * C)


def rows_of(col, t):
    H, S, _ = col.shape
    return col.reshape(H, S // t, 1, t)


IN_SEGMENTS = ((SRC_A, SRC_F, OFF_A), (SRC_F, SRC_B, OFF_F), (SRC_B, SRC_C, OFF_B), (SRC_C, SRC_DQ, OFF_C),
               (SRC_DQ, SRC_G, OFF_D), (SRC_G, D_IN, OFF_G))
IN_SHARD = D_IN // N_CHIPS


def w_in_aligned_from_chips(t):
    pieces = []
    for ref_lo, ref_hi, _ in sorted(IN_SEGMENTS, key=lambda seg: seg[2]):
        for k in range(N_CHIPS):
            lo, hi = max(ref_lo, k * IN_SHARD), min(ref_hi, (k + 1) * IN_SHARD)
            if lo < hi:
                pieces.append(t[k][:, lo - k * IN_SHARD:hi - k * IN_SHARD])
    pieces.append(jnp.zeros((t[0].shape[0], W_AL - D_IN), t[0].dtype))
    return jnp.concatenate(pieces, axis=1)


def w_in_chips_from_aligned(g):
    slots = []
    for k in range(N_CHIPS):
        pieces = []
        for ref_lo, ref_hi, al in IN_SEGMENTS:
            lo, hi = max(ref_lo, k * IN_SHARD), min(ref_hi, (k + 1) * IN_SHARD)
            if lo < hi:
                pieces.append(g[:, al + lo - ref_lo:al + hi - ref_lo])
        slots.append(jnp.concatenate(pieces, axis=1))
    return jnp.stack(slots, axis=0)


def chips_to_full(t, name):
    return jnp.concatenate([t[k] for k in range(N_CHIPS)], axis=0 if name in ROW_SHARDED else 1)


def full_to_chips(g, name):
    if name in ROW_SHARDED:
        return g.reshape(N_CHIPS, g.shape[0] // N_CHIPS, g.shape[1])
    return g.reshape(g.shape[0], N_CHIPS, g.shape[1] // N_CHIPS).transpose(1, 0, 2)


def halves_from_chips(t):
    return jnp.concatenate([t[0], t[1]], axis=1), jnp.concatenate([t[2], t[3]], axis=1)


def chips_from_halves(g, v):
    c = g.shape[1] // 2
    return jnp.stack([g[:, :c], g[:, c:], v[:, :c], v[:, c:]], axis=0)


def layer_fwd(x, p_l, rope, w, tag, side=None):
    S = x.shape[0]
    sv = {"x0": x}
    h = rmsnorm_fwd(x, w["norm_mix_g"], f"{tag}_norm_mix")
    proj = matmul(h, w["w_in_al"], "nn", F32, f"{tag}_proj")
    sv["h"], sv["proj"] = h, proj

    af_t = proj[:, OFF_F:OFF_F + FOX_HEADS].T
    f_cum = fox_prep_fwd(af_t, w["fox_forget_b"].reshape(FOX_HEADS, 1), f"{tag}_fox_prep")
    T = min(FOX_T, S)
    f_col = f_cum.reshape(FOX_HEADS, S, 1)
    f_row = f_cum.reshape(FOX_HEADS, S // T, 1, T)
    qkv = to_heads(proj[:, OFF_A:OFF_B].astype(BF16), 3 * FOX_HEADS, 1)
    qa, ka, va = qkv[:FOX_HEADS], qkv[FOX_HEADS:2 * FOX_HEADS], qkv[2 * FOX_HEADS:]
    (oa_h, lse_a), side_out = fox_attn_fwd(qa, ka, va, f_col, f_row, f"{tag}_fox_fwd", side)
    o_a = from_heads(oa_h, 1)
    sv.update(af_t=af_t, f_col=f_col, f_row=f_row, qa=qa, ka=ka, va=va, oa_h=oa_h, lse_a=lse_a)

    o_b = shortconv_fwd(proj, w["shortconv_w"], f"{tag}_sconv_fwd")

    o_c = sgu_fwd(proj, w["sgu_norm_g"].reshape(1, SGU_WIDTH), w["sgu_w"], _sgu_bias(w["sgu_b"]), f"{tag}_sgu_fwd")

    cos, sa, sb = rope
    qk = rope_apply(proj, OFF_D, 2 * DIL_WIDTH, cos, sa, sb, BF16, f"{tag}_rope_fwd")
    vd = proj[:, OFF_D + 2 * DIL_WIDTH:OFF_D + 3 * DIL_WIDTH].astype(BF16)
    outs, lses, dil_sv = [], [], []
    for g, (window, dil) in enumerate(DIL_PATTERNS):
        sl = slice(g * DIL_OUT, (g + 1) * DIL_OUT)
        qg = to_heads(qk[:, sl], 4, dil)
        kg = to_heads(qk[:, DIL_WIDTH:][:, sl], 4, dil)
        vg = to_heads(vd[:, sl], 4, dil)
        nb = (S // dil) // DIL_SPAN
        og, lg = dil_attn_fwd(qg, kg, vg, nb, f"{tag}_dil{g}_fwd")
        dil_sv.append((qg, kg, vg, og, lg, nb))
        outs.append(_heads_unperm(og, dil))
        lses.append(_col_unperm(lg, dil))
    od_h = dil_merge_fwd(outs, lses, f"{tag}_dil_merge_fwd")
    o_d = from_heads(od_h, 1)
    sv.update(dil=dil_sv, outs=outs, lses=lses)

    o_d_pad = jnp.concatenate([o_d.astype(BF16), jnp.zeros((S, FOX_WIDTH - DIL_OUT), BF16)], axis=-1)
    o_stack = jnp.stack([o_a, o_b, o_c, o_d_pad], axis=0)
    merged = merge_fwd(o_stack, w["w_br"], proj, f"{tag}_merge_fwd")
    x1 = matmul(merged, w["w_out"], "nn", F32, f"{tag}_out_proj", res=x)
    sv.update(o_stack=o_stack, merged=merged, x1=x1)

    h2 = rmsnorm_fwd(x1, w["norm_ffn_g"], f"{tag}_norm_ffn")
    pre = (matmul(h2, w["w_up_g"], "nn", F32, f"{tag}_up_g"), matmul(h2, w["w_up_v"], "nn", F32, f"{tag}_up_v"))
    a = ffn_mid_fwd(pre[0], pre[1], w["ffn_conv_g"], w["ffn_conv_v"], f"{tag}_ffn_mid_fwd")
    x2 = matmul(a, w["w_down"], "nn", F32, f"{tag}_down", res=x1)
    sv.update(h2=h2, pre=pre, a=a, x2=x2)

    n3 = rmsnorm_fwd(x2, w["norm_ple_g"], f"{tag}_norm_ple")
    pg = matmul(n3, w["w_ple_gate"], "nn", F32, f"{tag}_ple_gate")
    pe = matmul(p_l, w["w_ple_proj"], "nn", F32, f"{tag}_ple_proj")
    x3 = ple_fwd(x2, pg, pe, f"{tag}_ple_fwd")
    sv.update(n3=n3, pg=pg, pe=pe, p_l=p_l)
    return x3, sv, side_out


def _sgu_bias(b):
    return jnp.pad(b.T, ((0, 0), (0, SGU_CHUNK - b.shape[0])))


def _col_unperm(col, dil):
    H, S, _ = col.shape
    return col.reshape(H, dil, S // dil).transpose(0, 2, 1).reshape(H, S, 1)


def _col_perm(col, dil):
    H, S, _ = col.shape
    return col.reshape(H, S // dil, dil).transpose(0, 2, 1).reshape(H, S, 1)


def _heads_perm(y, dil):
    H, S, C = y.shape
    return y.reshape(H, S // dil, dil, C).transpose(0, 2, 1, 3).reshape(H, S, C)


def _heads_unperm(y, dil):
    H, S, C = y.shape
    return y.reshape(H, dil, S // dil, C).transpose(0, 2, 1, 3).reshape(H, S, C)


def layer_bwd(dx3, sv, rope, w, tag, side=None):
    S = dx3.shape[0]
    gr = {}
    da, de = ple_bwd(sv["pg"], sv["pe"], dx3, f"{tag}_ple_bwd")
    gr["w_ple_proj"] = matmul(sv["p_l"], de, "tn", F32, f"{tag}_dw_ple_proj")
    gr["w_ple_gate"] = matmul(sv["n3"], da, "tn", F32, f"{tag}_dw_ple_gate")
    dn3 = matmul(da, w["w_ple_gate"], "nt", BF16, f"{tag}_dn3")
    dx2, gr["norm_ple_g"] = rmsnorm_bwd(sv["x2"], w["norm_ple_g"], dn3, dx3, f"{tag}_norm_ple_bwd")

    d_a = matmul(dx2, w["w_down"], "nt", F32, f"{tag}_da")
    gr["w_down"] = matmul(sv["a"], dx2, "tn", F32, f"{tag}_dw_down")
    dpre_g, dpre_v, dwc_g, dwc_v = ffn_mid_bwd(sv["pre"][0], sv["pre"][1], w["ffn_conv_g"], w["ffn_conv_v"], d_a,
                                               f"{tag}_ffn_mid_bwd")
    gr["ffn_conv_w"] = (dwc_g, dwc_v)
    gr["w_up"] = (matmul(sv["h2"], dpre_g, "tn", F32, f"{tag}_dw_up_g"),
                  matmul(sv["h2"], dpre_v, "tn", F32, f"{tag}_dw_up_v"))
    dh2_g = matmul(dpre_g, w["w_up_g"], "nt", F32, f"{tag}_dh2_g")
    dh2 = matmul(dpre_v, w["w_up_v"], "nt", BF16, f"{tag}_dh2_v", res=dh2_g)
    dx1, gr["norm_ffn_g"] = rmsnorm_bwd(sv["x1"], w["norm_ffn_g"], dh2, dx2, f"{tag}_norm_ffn_bwd")

    d_merged = matmul(dx1, w["w_out"], "nt", BF16, f"{tag}_dmerged")
    gr["w_out"] = matmul(sv["merged"], dx1, "tn", F32, f"{tag}_dw_out")
    proj = sv["proj"]
    dgl, dy = merge_bwd(sv["o_stack"], w["w_br"], proj, d_merged, f"{tag}_merge_bwd")
    d_o, d_wbr = [], []
    for b in range(N_BRANCH):
        d_o.append(matmul(dy[b], w["w_br"][b], "nt", F32, f"{tag}_do{b}"))
        d_wbr.append(matmul(sv["o_stack"][b], dy[b], "tn", F32, f"{tag}_dw_br{b}"))
    gr["w_br"] = d_wbr

    do_a = to_heads(d_o[0].astype(BF16), FOX_HEADS, 1)
    T = min(FOX_T, S)
    (dqa, delta_a, d_fq), side_out = fox_attn_bwd_dq(sv["qa"], sv["ka"], sv["va"], sv["f_col"], sv["f_row"],
                                                     sv["oa_h"], sv["lse_a"], do_a, f"{tag}_fox_dq", side)
    (dka, dva, d_fk), _ = fox_attn_bwd_dkv(sv["qa"], sv["ka"], sv["va"], sv["f_col"], sv["f_row"],
                                           rows_of(sv["lse_a"], T), rows_of(delta_a, T), do_a, f"{tag}_fox_dkv")
    daf_t, dfb = fox_prep_bwd(sv["af_t"], w["fox_forget_b"].reshape(FOX_HEADS, 1), d_fq.reshape(FOX_HEADS, S),
                              d_fk.reshape(FOX_HEADS, S), f"{tag}_fox_prep_bwd")
    gr["fox_forget_b"] = dfb.reshape(FOX_HEADS)
    d_proj_a = from_heads(jnp.concatenate([dqa, dka, dva], axis=0), 1).astype(BF16)

    dxb, dgb, dgc, gr["shortconv_w"] = shortconv_bwd(proj, w["shortconv_w"], d_o[1], f"{tag}_sconv_bwd")

    d_c, dsg, dsw, dsb = sgu_bwd(proj, w["sgu_norm_g"].reshape(1, SGU_WIDTH), w["sgu_w"],
                                 jnp.swapaxes(w["sgu_w"], 1, 2), _sgu_bias(w["sgu_b"]), d_o[2], f"{tag}_sgu_bwd")
    gr["sgu_norm_g"] = dsg.reshape(SGU_WIDTH)
    gr["sgu_w"] = dsw
    gr["sgu_b"] = dsb[:, :SGU_WIDTH // SGU_CHUNK].T

    d_od = to_heads(d_o[3][:, :DIL_OUT], 4, 1)
    d_outs_lses = dil_merge_bwd(sv["outs"], sv["lses"], d_od, f"{tag}_dil_merge_bwd")
    d_outs, d_lses = d_outs_lses[:3], d_outs_lses[3:]
    dq_parts, dk_parts, dv_parts = [], [], []
    for g, (window, dil) in enumerate(DIL_PATTERNS):
        qg, kg, vg, og, lg, nb = sv["dil"][g]
        do_g = _heads_perm(d_outs[g], dil)
        dl_g = _col_perm(d_lses[g], dil)
        dqg, delta_g = dil_attn_bwd_dq(qg, kg, vg, og, lg, do_g, dl_g, nb, f"{tag}_dil{g}_dq")
        dkg, dvg = dil_attn_bwd_dkv(qg, kg, vg, rows_of(lg, DIL_SPAN), rows_of(delta_g, DIL_SPAN), do_g, nb,
                                    f"{tag}_dil{g}_dkv")
        dq_parts.append(from_heads(dqg, dil))
        dk_parts.append(from_heads(dkg, dil))
        dv_parts.append(from_heads(dvg, dil))
    cos, sa, sb = rope
    d_qk_rot = jnp.concatenate(dq_parts + dk_parts, axis=-1)
    d_qk = rope_apply(d_qk_rot, 0, 2 * DIL_WIDTH, cos, -sa, -sb, BF16, f"{tag}_rope_bwd")
    d_vd = jnp.concatenate(dv_parts, axis=-1).astype(BF16)

    d_f_cols = jnp.concatenate([daf_t.T.astype(BF16), jnp.zeros((S, W_AL - OFF_F - FOX_HEADS), BF16)], axis=-1)
    d_proj = jnp.concatenate([dgl, d_proj_a, dxb, dgb, dgc, d_c, d_qk, d_vd, d_f_cols], axis=-1)
    gr["w_in_al"] = matmul(sv["h"], d_proj, "tn", F32, f"{tag}_dw_in", tm=2048, tn=512)
    dh = matmul(d_proj, w["w_in_al"], "nt", BF16, f"{tag}_dh")
    dx0, gr["norm_mix_g"] = rmsnorm_bwd(sv["x0"], w["norm_mix_g"], dh, dx1, f"{tag}_norm_mix_bwd")
    return dx0, gr, side_out


def local_weights(chips, repl, layer):
    w = {n: repl[n][layer] for n in REPLICATED}
    cast = lambda n, dtype: [chips[n][k].astype(dtype) for k in range(N_CHIPS)]
    full = {n: chips_to_full(cast(n, BF16), n)
            for n in ("w_br_fox", "w_br_conv", "w_br_sgu", "w_br_dil", "w_out", "w_down", "w_ple_gate", "w_ple_proj")}
    w["w_in_al"] = w_in_aligned_from_chips(cast("w_in", BF16))
    w["shortconv_w"] = chips_to_full(cast("shortconv_w", F32), "shortconv_w")
    pad = jnp.zeros((FOX_WIDTH - DIL_OUT, D_MODEL), BF16)
    w["w_br"] = jnp.stack([full["w_br_fox"], full["w_br_conv"], full["w_br_sgu"],
                           jnp.concatenate([full["w_br_dil"], pad], axis=0)], axis=0)
    w["w_up_g"], w["w_up_v"] = halves_from_chips(cast("w_up", BF16))
    w["ffn_conv_g"], w["ffn_conv_v"] = halves_from_chips(cast("ffn_conv_w", F32))
    for n in ("w_out", "w_down", "w_ple_gate", "w_ple_proj"):
        w[n] = full[n]
    return w


def grads_to_chips(gr):
    out = {n: gr[n] for n in ("fox_forget_b", "sgu_norm_g", "sgu_w", "sgu_b")}
    out["norm_mix_g"] = gr["norm_mix_g"].reshape(D_MODEL)
    out["norm_ffn_g"] = gr["norm_ffn_g"].reshape(D_MODEL)
    out["norm_ple_g"] = gr["norm_ple_g"].reshape(D_MODEL)
    out["w_in"] = w_in_chips_from_aligned(gr["w_in_al"])
    out["w_up"] = chips_from_halves(*gr["w_up"])
    out["ffn_conv_w"] = chips_from_halves(*gr["ffn_conv_w"])
    for b, n in enumerate(("w_br_fox", "w_br_conv", "w_br_sgu")):
        out[n] = full_to_chips(gr["w_br"][b], n)
    out["w_br_dil"] = full_to_chips(gr["w_br"][3][:DIL_OUT], "w_br_dil")
    for n in ("shortconv_w", "w_out", "w_down", "w_ple_gate", "w_ple_proj"):
        out[n] = full_to_chips(gr[n], n)
    return out


def local_step(x, p, positions, repl, final_norm_g, loss_target, exch):
    depth = p.shape[0]
    rope = rope_tables(positions)
    saved, ws = [], []
    chips = exch.first_weights()
    for layer in range(depth):
        w = local_weights(chips, repl, layer)
        side = exch.weights_exchange(layer + 1) if layer + 1 < depth else None
        x, sv, side_out = layer_fwd(x, p[layer].astype(BF16), rope, w, f"l{layer}", side)
        if layer + 1 < depth:
            chips = exch.weights_arrived(layer + 1, side_out)
        saved.append(sv)
        ws.append(w)
    loss_part, dx, dgf = final_loss(x, final_norm_g, loss_target, "final_loss")
    for layer in range(depth - 1, -1, -1):
        dx, gr, side_out = layer_bwd(dx, saved[layer], rope, ws[layer], f"l{layer}", exch.grads_exchange())
        exch.grads_arrived(side_out)
        exch.grads_ready(layer, grads_to_chips(gr))
    exch.grads_flush()
    return loss_part[0, 0], dx, dgf.reshape(-1)


def _position():
    return lax.axis_index("x"), lax.axis_index("y"), lax.axis_index("c")


def _other_chips(x, y):
    return [(1 - x, y), (x, 1 - y), (1 - x, 1 - y)]


def _remote(src, dst, send_sem, recv_sem, device):
    return pltpu.make_async_remote_copy(src_ref=src, dst_ref=dst, send_sem=send_sem, recv_sem=recv_sem,
                                        device_id=device, device_id_type=MESH)


def _chip_index():
    return 2 * lax.axis_index("x") + lax.axis_index("y")


def _block_rows(rows, cols, unit):
    return _tile(rows, max(unit, (1 << 19) // cols // unit * unit), unit)


def gather_chip_shards(packs, name):
    return _run_exchange(gather_exchange(packs), name)


def gather_exchange(packs):
    n = len(packs)
    halves = [p.shape[0] // 2 for p in packs]

    def half(outs, t, chip, core):
        return outs[t].at[chip, pl.ds(core * halves[t], halves[t]), :]

    def ici_sends(srcs, outs, send_sems, recv_sems):
        x, y, c = _position()
        me = 2 * x + y
        return [_remote(srcs[t].at[pl.ds(c * halves[t], halves[t]), :], half(outs, t, me, c),
                        send_sems.at[6 * t + j], recv_sems.at[6 * t + j], (px, py, c))
                for t in range(n) for j, (px, py) in enumerate(_other_chips(x, y))]

    def start(srcs, outs, send_sems, recv_sems):
        for cp in ici_sends(srcs, outs, send_sems, recv_sems):
            cp.start()

    def finish(srcs, outs, send_sems, recv_sems):
        x, y, c = _position()
        sibling = (x, y, 1 - c)
        chips = _other_chips(x, y)
        passed = []
        for t in range(n):
            for j, (px, py) in enumerate(chips):
                k = 2 * px + py
                s = 6 * t + j
                landed = half(outs, t, k, c)
                _remote(landed, landed, send_sems.at[s], recv_sems.at[s], (px, py, c)).wait_recv()
                fwd = _remote(landed, landed, send_sems.at[s + 3], recv_sems.at[s + 3], sibling)
                fwd.start()
                passed.append(fwd)
        for t in range(n):
            for j, (px, py) in enumerate(chips):
                s = 6 * t + j + 3
                theirs = half(outs, t, 2 * px + py, 1 - c)
                _remote(theirs, theirs, send_sems.at[s], recv_sems.at[s], sibling).wait_recv()
        for cp in ici_sends(srcs, outs, send_sems, recv_sems) + passed:
            cp.wait_send()

    return SideExchange(list(packs), [SDS((N_CHIPS,) + p.shape, p.dtype) for p in packs], 6 * n, start, finish)


def _run_exchange(side, name):
    n_in, n_out = len(side.operands), len(side.out_shapes)

    def body(*refs):
        srcs, outs, (send_sems, recv_sems) = refs[:n_in], refs[n_in:n_in + n_out], refs[n_in + n_out:]
        side.start(srcs, outs, send_sems, recv_sems)
        side.finish(srcs, outs, send_sems, recv_sems)

    return pl.pallas_call(
        body, name=name, in_specs=[ANY] * n_in, out_specs=[ANY] * n_out, out_shape=side.out_shapes,
        scratch_shapes=[pltpu.SemaphoreType.DMA((side.n_sems,)), pltpu.SemaphoreType.DMA((side.n_sems,))],
    )(*side.operands)


def swap_halves_with_sibling(gs, name):
    n = len(gs)
    halves = [g.shape[1] // 2 for g in gs]

    def body(*refs):
        srcs, lands, (send_sems, recv_sems) = refs[:n], refs[n:2 * n], refs[2 * n:]
        x, y, c = _position()
        copies = [_remote(srcs[t].at[:, pl.ds((1 - c) * halves[t], halves[t]), :], lands[t], send_sems.at[t],
                          recv_sems.at[t], (x, y, 1 - c)) for t in range(n)]
        for cp in copies:
            cp.start()
        for cp in copies:
            cp.wait()

    return pl.pallas_call(
        body, name=name, in_specs=[ANY] * n, out_specs=[ANY] * n,
        out_shape=[SDS((g.shape[0], h, g.shape[2]), g.dtype) for g, h in zip(gs, halves)],
        scratch_shapes=[pltpu.SemaphoreType.DMA((n,)), pltpu.SemaphoreType.DMA((n,))])(*gs)


def add_my_half(g, other, out_dtype, name):
    n, R, C = g.shape
    H = R // 2
    tr = _block_rows(H, C, 16) if H % 16 == 0 else H
    nb = H // tr
    core = lax.axis_index("c").astype(jnp.int32).reshape(1)

    def body(c_ref, g_ref, o_ref, out_ref):
        out_ref[...] = (g_ref[...] + o_ref[...]).astype(out_ref.dtype)

    grid_spec = pltpu.PrefetchScalarGridSpec(
        num_scalar_prefetch=1, grid=(n, nb),
        in_specs=[BS((None, tr, C), lambda s, i, c_ref: (s, c_ref[0] * nb + i, 0)),
                  BS((None, tr, C), lambda s, i, c_ref: (s, i, 0))],
        out_specs=BS((None, tr, C), lambda s, i, c_ref: (s, i, 0)))
    return pl.pallas_call(body, name=name, grid_spec=grid_spec, out_shape=SDS((n, H, C), out_dtype),
                          compiler_params=_params(2))(core, g, other)


def exchange_slots_between_chips(parts, name):
    return _run_exchange(slot_exchange(parts), name)


def slot_exchange(parts):
    n = len(parts)

    def sends(srcs, lands, send_sems, recv_sems):
        x, y, c = _position()
        me = 2 * x + y
        return [_remote(srcs[t].at[2 * px + py], lands[t].at[me], send_sems.at[3 * t + j], recv_sems.at[3 * t + j],
                        (px, py, c)) for t in range(n) for j, (px, py) in enumerate(_other_chips(x, y))]

    def start(srcs, lands, send_sems, recv_sems):
        for cp in sends(srcs, lands, send_sems, recv_sems):
            cp.start()

    def finish(srcs, lands, send_sems, recv_sems):
        x, y, c = _position()
        for t in range(n):
            for j, (px, py) in enumerate(_other_chips(x, y)):
                k = 2 * px + py
                _remote(srcs[t].at[k], lands[t].at[k], send_sems.at[3 * t + j], recv_sems.at[3 * t + j],
                        (px, py, c)).wait_recv()
        for cp in sends(srcs, lands, send_sems, recv_sems):
            cp.wait_send()

    return SideExchange(list(parts), [SDS(p.shape, p.dtype) for p in parts], 3 * n, start, finish)


def sum_slots_into_my_half(landed, mine, name):
    n, H, C = landed.shape
    tr = _block_rows(H, C, 16) if H % 16 == 0 else H
    nb = H // tr
    where = jnp.stack([lax.axis_index("c"), _chip_index()]).astype(jnp.int32)

    def body(w_ref, l_ref, m_ref, o_ref):
        me = w_ref[1]
        o_ref[...] = jnp.zeros_like(o_ref)
        for k in range(n):
            @pl.when(me == k)
            def _():
                o_ref[...] += m_ref[k].astype(F32)

            @pl.when(me != k)
            def _():
                o_ref[...] += l_ref[k].astype(F32)

    slots = BS((n, tr, C), lambda i, w_ref: (0, i, 0))
    grid_spec = pltpu.PrefetchScalarGridSpec(
        num_scalar_prefetch=1, grid=(nb,), in_specs=[slots, slots],
        out_specs=BS((tr, C), lambda i, w_ref: (w_ref[0] * nb + i, 0)))
    return pl.pallas_call(body, name=name, grid_spec=grid_spec, out_shape=SDS((2 * H, C), F32),
                          compiler_params=_params(1))(where, landed, mine)


def sum_slots(parts, name):
    n, H, C = parts.shape
    tr = _tile(H, 256, 16)

    def body(p_ref, o_ref):
        acc = p_ref[0].astype(F32)
        for k in range(1, n):
            acc = acc + p_ref[k].astype(F32)
        o_ref[...] = acc

    return pl.pallas_call(
        body, name=name, grid=(H // tr,), in_specs=[BS((n, tr, C), lambda i: (0, i, 0))],
        out_specs=BS((tr, C), lambda i: (i, 0)), out_shape=SDS((H, C), F32), compiler_params=_params(1))(parts)


def join_halves_with_sibling(arrs, name):
    n = len(arrs)
    halves = [a.shape[0] // 2 for a in arrs]

    def body(*refs):
        outs, (send_sems, recv_sems) = refs[n:2 * n], refs[2 * n:]
        x, y, c = _position()

        def half(t, core):
            return outs[t].at[pl.ds(core * halves[t], halves[t]), :]

        sends = [_remote(half(t, c), half(t, c), send_sems.at[t], recv_sems.at[t], (x, y, 1 - c)) for t in range(n)]
        for cp in sends:
            cp.start()
        for t in range(n):
            _remote(half(t, 1 - c), half(t, 1 - c), send_sems.at[t], recv_sems.at[t], (x, y, 1 - c)).wait_recv()
        for cp in sends:
            cp.wait_send()

    return pl.pallas_call(
        body, name=name, in_specs=[ANY] * n, out_specs=[ANY] * n, out_shape=[SDS(a.shape, a.dtype) for a in arrs],
        input_output_aliases={t: t for t in range(n)},
        scratch_shapes=[pltpu.SemaphoreType.DMA((n,)), pltpu.SemaphoreType.DMA((n,))])(*arrs)


def reduce_scatter_pair_sums(gs, tag):
    n = len(gs)
    others = swap_halves_with_sibling(gs, f"{tag}_swap")
    return [add_my_half(g, o, BF16 if t < n - 1 else F32, f"{tag}_pair_sum{t}")
            for t, (g, o) in enumerate(zip(gs, others))]


def reduce_scatter_finish(parts, landed, tag):
    sums = [sum_slots_into_my_half(l, p, f"{tag}_chip_sum{t}") for t, (l, p) in enumerate(zip(landed, parts))]
    return join_halves_with_sibling(sums, f"{tag}_join")


def gather_all_devices(pack, name):
    R, C = pack.shape

    def body(src, out, send_sems, recv_sems, local_sem):
        x, y, c = _position()
        me = 4 * x + 2 * y + c
        local = pltpu.make_async_copy(src, out.at[me], local_sem)
        local.start()
        peers = []
        for m in range(1, N_DEV):
            px = 1 - x if m & 4 else x
            py = 1 - y if m & 2 else y
            pc = 1 - c if m & 1 else c
            peers.append((px, py, pc))
        sends = [_remote(src, out.at[me], send_sems.at[j], recv_sems.at[j], peer) for j, peer in enumerate(peers)]
        for cp in sends:
            cp.start()
        for j, (px, py, pc) in enumerate(peers):
            k = 4 * px + 2 * py + pc
            _remote(src, out.at[k], send_sems.at[j], recv_sems.at[j], (px, py, pc)).wait_recv()
        for cp in sends:
            cp.wait_send()
        local.wait()

    return pl.pallas_call(
        body, name=name, in_specs=[ANY], out_specs=ANY, out_shape=SDS((N_DEV, R, C), pack.dtype),
        scratch_shapes=[pltpu.SemaphoreType.DMA((N_DEV - 1,)), pltpu.SemaphoreType.DMA((N_DEV - 1,)),
                        pltpu.SemaphoreType.DMA(())])(pack)


def adamw(w, g, m, v, name):
    shape = w.shape
    cols = shape[-1] if len(shape) > 1 else shape[0]
    rows = w.size // cols
    two = lambda t: t.reshape(rows, cols)
    tr = rows
    if rows * cols * 4 > (1 << 21):
        tr = _tile(rows, max(8, ((1 << 21) // (cols * 4)) // 8 * 8), 8)
    c1 = 1.0 / (1.0 - ADAM_B1 ** ADAM_STEP)
    c2 = 1.0 / (1.0 - ADAM_B2 ** ADAM_STEP)

    def body(w_ref, g_ref, m_ref, v_ref, d_ref, mo_ref, vo_ref):
        gv = g_ref[...]
        mn = ADAM_B1 * m_ref[...] + (1.0 - ADAM_B1) * gv
        vn = ADAM_B2 * v_ref[...] + (1.0 - ADAM_B2) * (gv * gv)
        d_ref[...] = -ADAM_LR * ((mn * c1) / (jnp.sqrt(vn * c2) + ADAM_EPS) + ADAM_WD * w_ref[...])
        mo_ref[...] = mn
        vo_ref[...] = vn

    blk = BS((tr, cols), lambda i: (i, 0))
    d, mo, vo = pl.pallas_call(
        body, name=name, grid=(rows // tr,), in_specs=[blk] * 4, out_specs=[blk] * 3,
        out_shape=[SDS((rows, cols), F32)] * 3, compiler_params=_params(1))(two(w), two(g), two(m), two(v))
    return d.reshape(shape), mo.reshape(shape), vo.reshape(shape)


def _rows_for(n, unit):
    rows = -(-n // PACK_COLS)
    return -(-rows // unit) * unit


ROWS_GROUP = ("w_out", "w_ple_gate", "w_down")
COLS_GROUP = ("w_br_fox", "w_br_conv", "w_br_sgu", "w_br_dil", "w_ple_proj")
SMALL_GROUP = ("shortconv_w", "ffn_conv_w")
SMALL_ROWS = 16


def group_shards(t, dtype):
    lead = t["w_in"].shape[:-2]
    small = jnp.concatenate([t[n].astype(F32).reshape(lead + (-1,)) for n in SMALL_GROUP], axis=-1)
    pad = jnp.zeros(lead + (SMALL_ROWS * PACK_COLS - small.shape[-1],), F32)
    small = jnp.concatenate([small, pad], axis=-1).reshape(lead + (SMALL_ROWS, PACK_COLS))
    return [t["w_in"].astype(dtype), t["w_up"].astype(dtype),
            jnp.concatenate([t[n].astype(dtype) for n in ROWS_GROUP], axis=-2),
            jnp.concatenate([t[n].astype(dtype) for n in COLS_GROUP], axis=-2), small]


def ungroup_shards(arrs, shard_shapes):
    w_in_s, w_up_s, rows, cols, small = arrs
    lead = w_in_s.shape[:-2]
    out = {"w_in": w_in_s, "w_up": w_up_s}
    for group, arr in ((ROWS_GROUP, rows), (COLS_GROUP, cols)):
        off = 0
        for n in group:
            r = shard_shapes[n][0]
            out[n] = arr[..., off:off + r, :]
            off += r
    flat = small.reshape(lead + (-1,))
    off = 0
    for n in SMALL_GROUP:
        size = shard_shapes[n][0] * shard_shapes[n][1]
        out[n] = flat[..., off:off + size].reshape(lead + shard_shapes[n])
        off += size
    return out


class ShardExchange:
    def __init__(self, weights, depth, shard_shapes):
        self.shard_shapes = shard_shapes
        self.packs = [group_shards({n: weights[n][layer] for n in SHARDED}, BF16) for layer in range(depth)]
        self.me = _chip_index()
        self.pending = None
        self.shard_grads = [None] * depth
        self.repl_grads = [None] * depth

    def _chips(self, layer, gathered):
        per_chip = [ungroup_shards([jnp.where(self.me == k, pk, g[k]) for g, pk in zip(gathered, self.packs[layer])],
                                   self.shard_shapes) for k in range(N_CHIPS)]
        return {n: [per_chip[k][n] for k in range(N_CHIPS)] for n in SHARDED}

    def first_weights(self):
        return self._chips(0, gather_chip_shards(self.packs[0], "gather_w0"))

    def weights_exchange(self, layer):
        return gather_exchange(self.packs[layer])

    def weights_arrived(self, layer, gathered):
        return self._chips(layer, gathered)

    def grads_exchange(self):
        return slot_exchange(self.pending[1]) if self.pending is not None else None

    def grads_arrived(self, landed):
        if self.pending is not None:
            self._finish(landed)

    def grads_ready(self, layer, gr):
        self.repl_grads[layer] = {n: gr[n] for n in REPLICATED}
        slots = group_shards({n: gr[n] for n in SHARDED}, F32)
        self.pending = (layer, reduce_scatter_pair_sums(slots, f"rs{layer}"))

    def grads_flush(self):
        layer, parts = self.pending
        self._finish(exchange_slots_between_chips(parts, f"rs{layer}_ici"))

    def _finish(self, landed):
        layer, parts = self.pending
        self.shard_grads[layer] = ungroup_shards(reduce_scatter_finish(parts, landed, f"rs{layer}"),
                                                 self.shard_shapes)
        self.pending = None


REPL_SHAPES = {"norm_mix_g": (D_MODEL,), "fox_forget_b": (FOX_HEADS,), "sgu_norm_g": (SGU_WIDTH,),
               "sgu_w": (4, SGU_CHUNK, SGU_CHUNK), "sgu_b": (4, SGU_CHUNK), "norm_ffn_g": (D_MODEL,),
               "norm_ple_g": (D_MODEL,)}


def kernel(x, p, positions, norm_mix_g, w_in, fox_forget_b, shortconv_w, sgu_norm_g, sgu_w, sgu_b, w_br_fox, w_br_conv, w_br_sgu, w_br_dil, w_out, norm_ffn_g, w_up, ffn_conv_w, w_down, norm_ple_g, w_ple_gate, w_ple_proj, final_norm_g, loss_target, m_norm_mix_g, m_w_in, m_fox_forget_b, m_shortconv_w, m_sgu_norm_g, m_sgu_w, m_sgu_b, m_w_br_fox, m_w_br_conv, m_w_br_sgu, m_w_br_dil, m_w_out, m_norm_ffn_g, m_w_up, m_ffn_conv_w, m_w_down, m_norm_ple_g, m_w_ple_gate, m_w_ple_proj, m_final_norm_g, v_norm_mix_g, v_w_in, v_fox_forget_b, v_shortconv_w, v_sgu_norm_g, v_sgu_w, v_sgu_b, v_w_br_fox, v_w_br_conv, v_w_br_sgu, v_w_br_dil, v_w_out, v_norm_ffn_g, v_w_up, v_ffn_conv_w, v_w_down, v_norm_ple_g, v_w_ple_gate, v_w_ple_proj, v_final_norm_g):
    weights = dict(norm_mix_g=norm_mix_g, w_in=w_in, fox_forget_b=fox_forget_b, shortconv_w=shortconv_w,
                   sgu_norm_g=sgu_norm_g, sgu_w=sgu_w, sgu_b=sgu_b, w_br_fox=w_br_fox, w_br_conv=w_br_conv,
                   w_br_sgu=w_br_sgu, w_br_dil=w_br_dil, w_out=w_out, norm_ffn_g=norm_ffn_g, w_up=w_up,
                   ffn_conv_w=ffn_conv_w, w_down=w_down, norm_ple_g=norm_ple_g, w_ple_gate=w_ple_gate,
                   w_ple_proj=w_ple_proj, final_norm_g=final_norm_g)
    mom1 = dict(norm_mix_g=m_norm_mix_g, w_in=m_w_in, fox_forget_b=m_fox_forget_b, shortconv_w=m_shortconv_w,
                sgu_norm_g=m_sgu_norm_g, sgu_w=m_sgu_w, sgu_b=m_sgu_b, w_br_fox=m_w_br_fox, w_br_conv=m_w_br_conv,
                w_br_sgu=m_w_br_sgu, w_br_dil=m_w_br_dil, w_out=m_w_out, norm_ffn_g=m_norm_ffn_g, w_up=m_w_up,
                ffn_conv_w=m_ffn_conv_w, w_down=m_w_down, norm_ple_g=m_norm_ple_g, w_ple_gate=m_w_ple_gate,
                w_ple_proj=m_w_ple_proj, final_norm_g=m_final_norm_g)
    mom2 = dict(norm_mix_g=v_norm_mix_g, w_in=v_w_in, fox_forget_b=v_fox_forget_b, shortconv_w=v_shortconv_w,
                sgu_norm_g=v_sgu_norm_g, sgu_w=v_sgu_w, sgu_b=v_sgu_b, w_br_fox=v_w_br_fox, w_br_conv=v_w_br_conv,
                w_br_sgu=v_w_br_sgu, w_br_dil=v_w_br_dil, w_out=v_w_out, norm_ffn_g=v_norm_ffn_g, w_up=v_w_up,
                ffn_conv_w=v_ffn_conv_w, w_down=v_w_down, norm_ple_g=v_norm_ple_g, w_ple_gate=v_w_ple_gate,
                w_ple_proj=v_w_ple_proj, final_norm_g=v_final_norm_g)
    depth = w_in.shape[0]
    shard_shapes = {n: tuple(weights[n].shape[1:]) for n in SHARDED}

    exch = ShardExchange(weights, depth, shard_shapes)
    repl = {n: weights[n] for n in REPLICATED}
    loss_part, grad_x, d_final = local_step(x[0], p[:, 0], positions[0], repl, final_norm_g, loss_target[0], exch)
    loss = lax.psum(loss_part, ("x", "y", "c"))
    grads = exch.repl_grads
    grad_w = {n: jnp.stack([exch.shard_grads[layer][n] for layer in range(depth)], axis=0) for n in SHARDED}

    flat = jnp.concatenate([grads[layer][n].astype(F32).reshape(-1) for layer in range(depth) for n in REPLICATED]
                           + [d_final])
    Rr = _rows_for(flat.shape[0], 16)
    packed = jnp.concatenate([flat, jnp.zeros((Rr * PACK_COLS - flat.shape[0],), F32)]).reshape(Rr, PACK_COLS)
    total = sum_slots(gather_all_devices(packed, "gather_repl"), "sum_repl").reshape(-1)
    off = 0
    g_rep = {n: [] for n in REPLICATED}
    for layer in range(depth):
        for n in REPLICATED:
            size = 1
            for s in REPL_SHAPES[n]:
                size *= s
            g_rep[n].append(total[off:off + size].reshape(REPL_SHAPES[n]))
            off += size
    for n in REPLICATED:
        grad_w[n] = jnp.stack(g_rep[n], axis=0)
    grad_w["final_norm_g"] = total[off:off + D_MODEL]

    deltas, new_m, new_v = {}, {}, {}
    for n in WEIGHTS:
        deltas[n], new_m[n], new_v[n] = adamw(weights[n], grad_w[n], mom1[n], mom2[n], f"adamw_{n}")
    return (loss, grad_x[None], *[grad_w[n] for n in WEIGHTS], *[deltas[n] for n in WEIGHTS],
            *[new_m[n] for n in WEIGHTS], *[new_v[n] for n in WEIGHTS])
```

```python
import functools

import jax
import jax.numpy as jnp
from jax import lax
from jax.experimental import pallas as pl
from jax.experimental.pallas import tpu as pltpu

F32 = jnp.float32
BF16 = jnp.bfloat16
MESH = pl.DeviceIdType.MESH
BS = pl.BlockSpec
SDS = jax.ShapeDtypeStruct
ANY = pl.BlockSpec(memory_space=pl.ANY)

VMEM_LIMIT_BYTES = 52 * 1024 * 1024
LANES = 128

D_MODEL = 2048
HEAD_DIM = 64
EPS = 1e-6
NEG = -1e30
FOX_HEADS = 8
FOX_WIDTH = 512
CONV_WIDTH = 512
SGU_WIDTH = 512
SGU_CHUNK = 128
DIL_PATTERNS = ((128, 1), (512, 4), (2048, 16))
DIL_SPAN = 128
DIL_HEADS = 12
DIL_WIDTH = 768
DIL_OUT = 256
ROPE_THETA = 500000.0
ROPE_DIM = 16
N_BRANCH = 4
D_FF = 5632
PLE_DIM = 256
D_IN = 14600

OFF_G, OFF_A, OFF_B, OFF_C, OFF_D, OFF_F, W_AL = 0, 8192, 9728, 11264, 12288, 14592, 14848
SRC_A, SRC_F, SRC_B, SRC_C, SRC_DQ, SRC_G = 0, 1536, 1544, 3080, 4104, 6408

ADAM_LR, ADAM_B1, ADAM_B2, ADAM_EPS, ADAM_WD, ADAM_STEP = 0.001, 0.9, 0.999, 1e-08, 0.01, 10

PACK_COLS = 1024
N_CHIPS = 4
N_DEV = 8

SHARDED = ("w_in", "shortconv_w", "w_br_fox", "w_br_conv", "w_br_sgu", "w_br_dil", "w_out", "w_up",
           "ffn_conv_w", "w_down", "w_ple_gate", "w_ple_proj")
ROW_SHARDED = ("w_out", "w_down", "w_ple_gate")
REPLICATED = ("norm_mix_g", "fox_forget_b", "sgu_norm_g", "sgu_w", "sgu_b", "norm_ffn_g", "norm_ple_g")
WEIGHTS = ("norm_mix_g", "w_in", "fox_forget_b", "shortconv_w", "sgu_norm_g", "sgu_w", "sgu_b", "w_br_fox",
           "w_br_conv", "w_br_sgu", "w_br_dil", "w_out", "norm_ffn_g", "w_up", "ffn_conv_w", "w_down",
           "norm_ple_g", "w_ple_gate", "w_ple_proj", "final_norm_g")


def _params(n_grid):
    return pltpu.CompilerParams(dimension_semantics=("arbitrary",) * n_grid, vmem_limit_bytes=VMEM_LIMIT_BYTES)


def _tile(n, pref, unit=LANES):
    best = None
    t = unit
    while t <= min(n, pref):
        if n % t == 0:
            best = t
        t += unit
    return best if best is not None else n


def _sigmoid(z):
    return 1.0 / (1.0 + jnp.exp(-z))


MAX_RESIDENT_K = 2048


def matmul(a, b, mode, out_dtype, name, res=None, tm=1536, tn=1024, tk=512, side=None):
    if mode == "tn":
        a, mode = a.astype(BF16).T, "nn"
    if mode == "nn":
        (M, K), (K2, N) = a.shape, b.shape
    else:
        (M, K), (N, K2) = a.shape, b.shape
    assert K == K2, (name, a.shape, b.shape)
    if K <= MAX_RESIDENT_K:
        tk = K
    tm, tn, tk = _tile(M, tm), _tile(N, tn), _tile(K, tk)
    nk = K // tk
    if mode == "nn":
        a_spec, b_spec = BS((tm, tk), lambda i, j, k: (i, k)), BS((tk, tn), lambda i, j, k: (k, j))
        dims = (((1,), (0,)), ((), ()))
    else:
        a_spec, b_spec = BS((tm, tk), lambda i, j, k: (i, k)), BS((tn, tk), lambda i, j, k: (j, k))
        dims = (((1,), (1,)), ((), ()))
    has_res = res is not None

    def body(*refs):
        if has_res:
            a_ref, b_ref, r_ref, o_ref, acc = refs
        else:
            a_ref, b_ref, o_ref, acc = refs
        k = pl.program_id(2)

        @pl.when(k == 0)
        def _():
            acc[...] = jnp.zeros_like(acc)

        acc[...] += lax.dot_general(a_ref[...].astype(BF16), b_ref[...].astype(BF16), dims,
                                    preferred_element_type=F32)

        @pl.when(k == nk - 1)
        def _():
            r = acc[...]
            if has_res:
                r = r + r_ref[...]
            o_ref[...] = r.astype(o_ref.dtype)

    in_specs = [a_spec, b_spec]
    args = [a, b]
    if has_res:
        in_specs.append(BS((tm, tn), lambda i, j, k: (i, j)))
        args.append(res)
    (out,), side_out = _call_with_side(
        body, side, name, (M // tm, N // tn, nk), in_specs, [BS((tm, tn), lambda i, j, k: (i, j))],
        [SDS((M, N), out_dtype)], args, scratch=[pltpu.VMEM((tm, tn), F32)])
    return out if side is None else (out, side_out)


def rmsnorm_fwd(x, g, name):
    S, Dm = x.shape
    tm = _tile(S, 256, 8)

    def body(x_ref, g_ref, y_ref):
        xf = x_ref[...]
        r = lax.rsqrt(jnp.mean(xf * xf, axis=-1, keepdims=True) + EPS)
        y_ref[...] = ((xf * r) * g_ref[...]).astype(y_ref.dtype)

    return pl.pallas_call(
        body, name=name, grid=(S // tm,),
        in_specs=[BS((tm, Dm), lambda i: (i, 0)), BS((1, Dm), lambda i: (0, 0))],
        out_specs=BS((tm, Dm), lambda i: (i, 0)), out_shape=SDS((S, Dm), BF16),
        compiler_params=_params(1))(x, g.reshape(1, Dm))


def rmsnorm_bwd(x, g, dy, dres, name):
    S, Dm = x.shape
    tm = _tile(S, 256, 8)

    def body(x_ref, g_ref, dy_ref, dres_ref, dx_ref, dg_ref):
        xf = x_ref[...]
        r = lax.rsqrt(jnp.mean(xf * xf, axis=-1, keepdims=True) + EPS)
        xh = xf * r
        dy = dy_ref[...].astype(F32)
        dxh = dy * g_ref[...]
        dx_ref[...] = r * (dxh - xh * jnp.mean(dxh * xh, axis=-1, keepdims=True)) + dres_ref[...]

        @pl.when(pl.program_id(0) == 0)
        def _():
            dg_ref[...] = jnp.zeros_like(dg_ref)

        dg_ref[...] += jnp.sum(dy * xh, axis=0, keepdims=True)

    row = BS((tm, Dm), lambda i: (i, 0))
    vec = BS((1, Dm), lambda i: (0, 0))
    return pl.pallas_call(
        body, name=name, grid=(S // tm,), in_specs=[row, vec, row, row], out_specs=[row, vec],
        out_shape=[SDS((S, Dm), F32), SDS((1, Dm), F32)], compiler_params=_params(1))(x, g.reshape(1, Dm), dy, dres)


def final_loss(x, g, target, name):
    S, Dm = x.shape
    tm = _tile(S, 256, 8)

    def body(x_ref, g_ref, t_ref, loss_ref, dx_ref, dg_ref):
        xf = x_ref[...]
        r = lax.rsqrt(jnp.mean(xf * xf, axis=-1, keepdims=True) + EPS)
        xh = xf * r
        gv = g_ref[...]
        err = xh * gv - t_ref[...]
        dy = err * (1.0 / Dm)
        dxh = dy * gv
        dx_ref[...] = r * (dxh - xh * jnp.mean(dxh * xh, axis=-1, keepdims=True))

        @pl.when(pl.program_id(0) == 0)
        def _():
            dg_ref[...] = jnp.zeros_like(dg_ref)
            loss_ref[...] = jnp.zeros_like(loss_ref)

        dg_ref[...] += jnp.sum(dy * xh, axis=0, keepdims=True)
        part = 0.5 * jnp.sum(jnp.mean(err * err, axis=-1, keepdims=True), axis=0, keepdims=True)
        loss_ref[...] += jnp.broadcast_to(part, loss_ref.shape)

    row = BS((tm, Dm), lambda i: (i, 0))
    vec = BS((1, Dm), lambda i: (0, 0))
    return pl.pallas_call(
        body, name=name, grid=(S // tm,), in_specs=[row, vec, row],
        out_specs=[BS((1, LANES), lambda i: (0, 0)), row, vec],
        out_shape=[SDS((1, LANES), F32), SDS((S, Dm), F32), SDS((1, Dm), F32)],
        compiler_params=_params(1))(x, g.reshape(1, Dm), target)


def _dot_f32(a, b):
    return jnp.dot(a, b, preferred_element_type=F32, precision=lax.Precision.HIGHEST)


def _dot(a, b):
    return jnp.dot(a, b, preferred_element_type=F32)


def _dot_nt(a, b):
    return lax.dot_general(a, b, (((1,), (1,)), ((), ())), preferred_element_type=F32)


def fox_prep_fwd(af_t, bias, name):
    H, S = af_t.shape
    nc = S // LANES

    def body(a_ref, b_ref, f_ref):
        z = a_ref[...] + b_ref[...]
        logf = jnp.minimum(z, 0.0) - jnp.log(1.0 + jnp.exp(-jnp.abs(z)))
        row = lax.broadcasted_iota(jnp.int32, (LANES, LANES), 0)
        col = lax.broadcasted_iota(jnp.int32, (LANES, LANES), 1)
        upper = (row <= col).astype(F32)
        carry = jnp.zeros((H, 1), F32)
        for c in range(nc):
            chunk = logf[:, c * LANES:(c + 1) * LANES]
            f_ref[:, c * LANES:(c + 1) * LANES] = _dot_f32(chunk, upper) + carry
            carry = carry + jnp.sum(chunk, axis=1, keepdims=True)

    full = BS((H, S), lambda i: (0, 0))
    return pl.pallas_call(
        body, name=name, grid=(1,), in_specs=[full, BS((H, 1), lambda i: (0, 0))], out_specs=full,
        out_shape=SDS((H, S), F32), compiler_params=_params(1))(af_t, bias)


def fox_prep_bwd(af_t, bias, d_fq, d_fk, name):
    H, S = af_t.shape
    nc = S // LANES

    def body(a_ref, b_ref, dfq_ref, dfk_ref, da_ref, db_ref):
        row = lax.broadcasted_iota(jnp.int32, (LANES, LANES), 0)
        col = lax.broadcasted_iota(jnp.int32, (LANES, LANES), 1)
        lower = (row >= col).astype(F32)
        carry = jnp.zeros((H, 1), F32)
        dbias = jnp.zeros((H, 1), F32)
        for c in range(nc - 1, -1, -1):
            sl = slice(c * LANES, (c + 1) * LANES)
            chunk = dfq_ref[:, sl] + dfk_ref[:, sl]
            dlogf = _dot_f32(chunk, lower) + carry
            carry = carry + jnp.sum(chunk, axis=1, keepdims=True)
            z = a_ref[:, sl] + b_ref[...]
            da = dlogf * _sigmoid(-z)
            da_ref[:, sl] = da
            dbias = dbias + jnp.sum(da, axis=1, keepdims=True)
        db_ref[...] = dbias

    full = BS((H, S), lambda i: (0, 0))
    vec = BS((H, 1), lambda i: (0, 0))
    return pl.pallas_call(
        body, name=name, grid=(1,), in_specs=[full, vec, full, full], out_specs=[full, vec],
        out_shape=[SDS((H, S), F32), SDS((H, 1), F32)], compiler_params=_params(1))(af_t, bias, d_fq, d_fk)


FOX_T = 256
FOX_HP = 2


def _causal_tile(T):
    return lax.broadcasted_iota(jnp.int32, (T, T), 1) <= lax.broadcasted_iota(jnp.int32, (T, T), 0)


class SideExchange:
    def __init__(self, operands, out_shapes, n_sems, start, finish):
        self.operands, self.out_shapes, self.n_sems, self.start, self.finish = operands, out_shapes, n_sems, start, finish


def _call_with_side(body, side, name, grid, in_specs, out_specs, out_shape, args, scratch=()):
    scratch = list(scratch)
    if side is None:
        return pl.pallas_call(body, name=name, grid=grid, in_specs=in_specs, out_specs=out_specs,
                              out_shape=out_shape, scratch_shapes=scratch,
                              compiler_params=_params(len(grid)))(*args), []
    n_in, n_out = len(in_specs), len(out_specs)
    s_in, s_out = len(side.operands), len(side.out_shapes)

    def wrapped(*refs):
        main_in, side_in = refs[:n_in], refs[n_in:n_in + s_in]
        main_out = refs[n_in + s_in:n_in + s_in + n_out]
        side_out = refs[n_in + s_in + n_out:n_in + s_in + n_out + s_out]
        main_scratch = refs[n_in + s_in + n_out + s_out:-2]
        send_sems, recv_sems = refs[-2:]
        first = pl.program_id(0) == 0
        last = pl.program_id(0) == grid[0] - 1
        for axis in range(1, len(grid)):
            first = jnp.logical_and(first, pl.program_id(axis) == 0)
            last = jnp.logical_and(last, pl.program_id(axis) == grid[axis] - 1)

        @pl.when(first)
        def _():
            side.start(side_in, side_out, send_sems, recv_sems)

        body(*main_in, *main_out, *main_scratch)

        @pl.when(last)
        def _():
            side.finish(side_in, side_out, send_sems, recv_sems)

    outs = pl.pallas_call(
        wrapped, name=name, grid=grid, in_specs=list(in_specs) + [ANY] * s_in,
        out_specs=list(out_specs) + [ANY] * s_out, out_shape=list(out_shape) + list(side.out_shapes),
        scratch_shapes=scratch + [pltpu.SemaphoreType.DMA((side.n_sems,)), pltpu.SemaphoreType.DMA((side.n_sems,))],
        compiler_params=_params(len(grid)))(*args, *side.operands)
    return outs[:n_out], outs[n_out:]


def _fox_specs(H, S, Dh, T):
    nq = S // T
    blk = BS((FOX_HP, T, Dh), lambda h, i: (h, i, 0))
    full = BS((FOX_HP, S, Dh), lambda h, i: (h, 0, 0))
    colb = BS((FOX_HP, T, 1), lambda h, i: (h, i, 0))
    rowf = BS((FOX_HP, nq, 1, T), lambda h, i: (h, 0, 0, 0))
    return blk, full, colb, rowf, (H // FOX_HP, nq)


def fox_attn_fwd(q, k, v, f_col, f_row, name, side=None):
    H, S, Dh = q.shape
    T = min(FOX_T, S)
    scale = Dh ** -0.5

    def body(q_ref, k_ref, v_ref, fc_ref, fr_ref, o_ref, lse_ref):
        qi = pl.program_id(1)
        qs = [q_ref[h] for h in range(FOX_HP)]
        fqs = [fc_ref[h] for h in range(FOX_HP)]

        def step(j, carry, diagonal):
            off = pl.multiple_of(j * T, T)
            out = []
            for h in range(FOX_HP):
                m, l, acc = carry[h]
                kv = k_ref[h, pl.ds(off, T), :]
                vv = v_ref[h, pl.ds(off, T), :]
                s = _dot_nt(qs[h], kv) * scale + (fqs[h] - fr_ref[h, j])
                if diagonal:
                    s = jnp.where(_causal_tile(T), s, NEG)
                m_new = jnp.maximum(m, jnp.max(s, axis=-1, keepdims=True))
                p = jnp.exp(s - m_new)
                alpha = jnp.exp(m - m_new)
                l = alpha * l + jnp.sum(p, axis=-1, keepdims=True)
                acc = alpha * acc + _dot(p.astype(BF16), vv)
                out.append((m_new, l, acc))
            return tuple(out)

        init = tuple((jnp.full((T, 1), NEG, F32), jnp.zeros((T, 1), F32), jnp.zeros((T, Dh), F32))
                     for _ in range(FOX_HP))
        carry = lax.fori_loop(0, qi, functools.partial(step, diagonal=False), init)
        carry = step(qi, carry, True)
        for h in range(FOX_HP):
            m, l, acc = carry[h]
            o_ref[h] = (acc / l).astype(o_ref.dtype)
            lse_ref[h] = m + jnp.log(l)

    blk, full, colb, rowf, grid = _fox_specs(H, S, Dh, T)
    return _call_with_side(body, side, name, grid, [blk, full, full, colb, rowf], [blk, colb],
                           [SDS((H, S, Dh), BF16), SDS((H, S, 1), F32)], (q, k, v, f_col, f_row))


def fox_attn_bwd_dq(q, k, v, f_col, f_row, o, lse, do, name, side=None):
    H, S, Dh = q.shape
    T = min(FOX_T, S)
    scale = Dh ** -0.5

    def body(q_ref, k_ref, v_ref, fc_ref, fr_ref, o_ref, lse_ref, do_ref, dq_ref, dl_ref, df_ref):
        qi = pl.program_id(1)
        qs = [q_ref[h] for h in range(FOX_HP)]
        fqs = [fc_ref[h] for h in range(FOX_HP)]
        lses = [lse_ref[h] for h in range(FOX_HP)]
        dos = [do_ref[h] for h in range(FOX_HP)]
        deltas = [jnp.sum(dos[h].astype(F32) * o_ref[h].astype(F32), axis=-1, keepdims=True) for h in range(FOX_HP)]
        for h in range(FOX_HP):
            dl_ref[h] = deltas[h]

        def step(j, carry, diagonal):
            off = pl.multiple_of(j * T, T)
            out = []
            for h in range(FOX_HP):
                dq, dfq = carry[h]
                kv = k_ref[h, pl.ds(off, T), :]
                vv = v_ref[h, pl.ds(off, T), :]
                s = _dot_nt(qs[h], kv) * scale + (fqs[h] - fr_ref[h, j])
                if diagonal:
                    s = jnp.where(_causal_tile(T), s, NEG)
                p = jnp.exp(s - lses[h])
                ds = p * (_dot_nt(dos[h], vv) - deltas[h])
                out.append((dq + _dot(ds.astype(BF16), kv), dfq + jnp.sum(ds, axis=-1, keepdims=True)))
            return tuple(out)

        init = tuple((jnp.zeros((T, Dh), F32), jnp.zeros((T, 1), F32)) for _ in range(FOX_HP))
        carry = lax.fori_loop(0, qi, functools.partial(step, diagonal=False), init)
        carry = step(qi, carry, True)
        for h in range(FOX_HP):
            dq_ref[h] = carry[h][0] * scale
            df_ref[h] = carry[h][1]

    blk, full, colb, rowf, grid = _fox_specs(H, S, Dh, T)
    return _call_with_side(body, side, name, grid, [blk, full, full, colb, rowf, blk, colb, blk], [blk, colb, colb],
                           [SDS((H, S, Dh), F32), SDS((H, S, 1), F32), SDS((H, S, 1), F32)],
                           (q, k, v, f_col, f_row, o, lse, do))


def fox_attn_bwd_dkv(q, k, v, f_col, f_row, lse_row, delta_row, do, name, side=None):
    H, S, Dh = q.shape
    T = min(FOX_T, S)
    nq = S // T
    scale = Dh ** -0.5

    def body(q_ref, k_ref, v_ref, fc_ref, fr_ref, lse_ref, dl_ref, do_ref, dk_ref, dv_ref, df_ref):
        kj = pl.program_id(1)
        ks = [k_ref[h] for h in range(FOX_HP)]
        vs = [v_ref[h] for h in range(FOX_HP)]
        fks = [fc_ref[h] for h in range(FOX_HP)]

        def step(i, carry, diagonal):
            off = pl.multiple_of(i * T, T)
            out = []
            for h in range(FOX_HP):
                dk, dv, dfk = carry[h]
                qv = q_ref[h, pl.ds(off, T), :]
                dov = do_ref[h, pl.ds(off, T), :]
                st = _dot_nt(ks[h], qv) * scale + (fr_ref[h, i] - fks[h])
                if diagonal:
                    st = jnp.where(lax.broadcasted_iota(jnp.int32, (T, T), 0)
                                   <= lax.broadcasted_iota(jnp.int32, (T, T), 1), st, NEG)
                pt = jnp.exp(st - lse_ref[h, i])
                dv = dv + _dot(pt.astype(BF16), dov)
                dst = pt * (_dot_nt(vs[h], dov) - dl_ref[h, i])
                dk = dk + _dot(dst.astype(BF16), qv)
                out.append((dk, dv, dfk + jnp.sum(dst, axis=-1, keepdims=True)))
            return tuple(out)

        init = tuple((jnp.zeros((T, Dh), F32), jnp.zeros((T, Dh), F32), jnp.zeros((T, 1), F32))
                     for _ in range(FOX_HP))
        carry = step(kj, init, True)
        carry = lax.fori_loop(kj + 1, nq, functools.partial(step, diagonal=False), carry)
        for h in range(FOX_HP):
            dk_ref[h] = carry[h][0] * scale
            dv_ref[h] = carry[h][1]
            df_ref[h] = -carry[h][2]

    blk, full, colb, rowf, grid = _fox_specs(H, S, Dh, T)
    return _call_with_side(body, side, name, grid, [full, blk, blk, colb, rowf, rowf, rowf, full], [blk, blk, colb],
                           [SDS((H, S, Dh), F32), SDS((H, S, Dh), F32), SDS((H, S, 1), F32)],
                           (q, k, v, f_col, f_row, lse_row, delta_row, do))


HALO = 8
CONV_TM = 512


def _conv_ext(e, w_ref):
    return w_ref[0:1, :] * pltpu.roll(e, 2, 0) + w_ref[1:2, :] * pltpu.roll(e, 1, 0) + w_ref[2:3, :] * e


def _conv_t_ext(d, w_ref):
    n = d.shape[0]
    return w_ref[2:3, :] * d + w_ref[1:2, :] * pltpu.roll(d, n - 1, 0) + w_ref[0:1, :] * pltpu.roll(d, n - 2, 0)


def _time_specs(S, tm, width, col_block):
    per = tm // HALO
    last = S // HALO - 1
    cur = BS((tm, width), lambda j, i: (i, col_block(j)))
    prev = BS((HALO, width), lambda j, i: (jnp.maximum(i * per - 1, 0), col_block(j)))
    nxt = BS((HALO, width), lambda j, i: (jnp.minimum((i + 1) * per, last), col_block(j)))
    return cur, prev, nxt


def shortconv_fwd(proj, w, name):
    S = proj.shape[0]
    tm = _tile(S, CONV_TM, 8)
    nb = CONV_WIDTH // LANES
    b0 = OFF_B // LANES

    def body(xb_ref, xbp_ref, gb_ref, gc_ref, gcp_ref, w_ref, o_ref):
        keep = (pl.program_id(1) > 0).astype(F32)
        e = jnp.concatenate([gcp_ref[...] * xbp_ref[...] * keep, gc_ref[...] * xb_ref[...]], axis=0)
        o_ref[...] = (gb_ref[...] * _conv_ext(e, w_ref)[HALO:, :]).astype(o_ref.dtype)

    xb, xbp, _ = _time_specs(S, tm, LANES, lambda j: b0 + j)
    gb, _, _ = _time_specs(S, tm, LANES, lambda j: b0 + nb + j)
    gc, gcp, _ = _time_specs(S, tm, LANES, lambda j: b0 + 2 * nb + j)
    return pl.pallas_call(
        body, name=name, grid=(nb, S // tm),
        in_specs=[xb, xbp, gb, gc, gcp, BS((3, LANES), lambda j, i: (0, j))],
        out_specs=BS((tm, LANES), lambda j, i: (i, j)), out_shape=SDS((S, CONV_WIDTH), BF16),
        compiler_params=_params(2))(proj, proj, proj, proj, proj, w)


def shortconv_bwd(proj, w, do_b, name):
    S = proj.shape[0]
    tm = _tile(S, CONV_TM, 8)
    nt = S // tm
    nb = CONV_WIDTH // LANES
    b0 = OFF_B // LANES

    def body(xb_ref, xbp_ref, gb_ref, gbn_ref, gc_ref, gcp_ref, do_ref, don_ref, w_ref,
             dxb_ref, dgb_ref, dgc_ref, dw_ref):
        i = pl.program_id(1)
        keep_prev = (i > 0).astype(F32)
        keep_next = (i < nt - 1).astype(F32)
        xb, gb, gc = xb_ref[...], gb_ref[...], gc_ref[...]
        do = do_ref[...].astype(F32)
        u = gc * xb
        e = jnp.concatenate([gcp_ref[...] * xbp_ref[...] * keep_prev, u], axis=0)
        dgb_ref[...] = (do * _conv_ext(e, w_ref)[HALO:, :]).astype(dgb_ref.dtype)
        dcv = do * gb
        d_ext = jnp.concatenate([dcv, don_ref[...].astype(F32) * gbn_ref[...] * keep_next], axis=0)
        du = _conv_t_ext(d_ext, w_ref)[:tm, :]
        dgc_ref[...] = (du * xb).astype(dgc_ref.dtype)
        dxb_ref[...] = (du * gc).astype(dxb_ref.dtype)

        @pl.when(i == 0)
        def _():
            dw_ref[...] = jnp.zeros_like(dw_ref)

        dw_ref[0:1, :] += jnp.sum(dcv * pltpu.roll(e, 2, 0)[HALO:, :], axis=0, keepdims=True)
        dw_ref[1:2, :] += jnp.sum(dcv * pltpu.roll(e, 1, 0)[HALO:, :], axis=0, keepdims=True)
        dw_ref[2:3, :] += jnp.sum(dcv * u, axis=0, keepdims=True)

    xb, xbp, _ = _time_specs(S, tm, LANES, lambda j: b0 + j)
    gb, _, gbn = _time_specs(S, tm, LANES, lambda j: b0 + nb + j)
    gc, gcp, _ = _time_specs(S, tm, LANES, lambda j: b0 + 2 * nb + j)
    do, _, don = _time_specs(S, tm, LANES, lambda j: j)
    out = BS((tm, LANES), lambda j, i: (i, j))
    wspec = BS((3, LANES), lambda j, i: (0, j))
    return pl.pallas_call(
        body, name=name, grid=(nb, nt), in_specs=[xb, xbp, gb, gbn, gc, gcp, do, don, wspec],
        out_specs=[out, out, out, wspec],
        out_shape=[SDS((S, CONV_WIDTH), BF16)] * 3 + [SDS((3, CONV_WIDTH), F32)],
        compiler_params=_params(2))(proj, proj, proj, proj, proj, proj, do_b, do_b, w)


_GELU_C = 0.7978845608028654


def _gelu(x):
    return 0.5 * x * (1.0 + jnp.tanh(_GELU_C * (x + 0.044715 * x * x * x)))


def _gelu_grad(x):
    t = jnp.tanh(_GELU_C * (x + 0.044715 * x * x * x))
    return 0.5 * (1.0 + t) + 0.5 * x * (1.0 - t * t) * _GELU_C * (1.0 + 3.0 * 0.044715 * x * x)


def _tril_masks():
    row = lax.broadcasted_iota(jnp.int32, (SGU_CHUNK, SGU_CHUNK), 0)
    col = lax.broadcasted_iota(jnp.int32, (SGU_CHUNK, SGU_CHUNK), 1)
    return row >= col, row <= col


def _lane_column(mat, g):
    lane = lax.broadcasted_iota(jnp.int32, mat.shape, 1)
    return jnp.sum(jnp.where(lane == g, mat, 0.0), axis=-1, keepdims=True)


def sgu_fwd(proj, norm_g, w_s, b_t, name):
    S = proj.shape[0]
    T = SGU_CHUNK
    G = SGU_WIDTH // T
    c0 = OFF_C // (2 * SGU_WIDTH)

    def body(c_ref, g_ref, w_ref, b_ref, o_ref):
        u = _gelu(c_ref[:, 0:SGU_WIDTH])
        v = _gelu(c_ref[:, SGU_WIDTH:2 * SGU_WIDTH])
        r = lax.rsqrt(jnp.mean(v * v, axis=-1, keepdims=True) + EPS)
        vn = ((v * r) * g_ref[...]).astype(BF16)
        mask, _ = _tril_masks()
        bias = b_ref[...]
        for g in range(G):
            sl = slice(g * T, (g + 1) * T)
            wt = jnp.where(mask, w_ref[g], 0.0).astype(BF16)
            mixed = _dot(wt, vn[:, sl]) + _lane_column(bias, g)
            o_ref[:, sl] = (u[:, sl] * mixed).astype(o_ref.dtype)

    return pl.pallas_call(
        body, name=name, grid=(S // T,),
        in_specs=[BS((T, 2 * SGU_WIDTH), lambda i: (i, c0)), BS((1, SGU_WIDTH), lambda i: (0, 0)),
                  BS((G, T, T), lambda i: (0, 0, 0)), BS((T, T), lambda i: (0, 0))],
        out_specs=BS((T, SGU_WIDTH), lambda i: (i, 0)), out_shape=SDS((S, SGU_WIDTH), BF16),
        compiler_params=_params(1))(proj, norm_g, w_s, b_t)


def sgu_bwd(proj, norm_g, w_s, w_st, b_t, do_c, name):
    S = proj.shape[0]
    T = SGU_CHUNK
    G = SGU_WIDTH // T
    c0 = OFF_C // (2 * SGU_WIDTH)

    def body(c_ref, g_ref, w_ref, wt_ref, b_ref, do_ref, dc_ref, dg_ref, dw_ref, db_ref):
        cu = c_ref[:, 0:SGU_WIDTH]
        cv = c_ref[:, SGU_WIDTH:2 * SGU_WIDTH]
        u = _gelu(cu)
        v = _gelu(cv)
        r = lax.rsqrt(jnp.mean(v * v, axis=-1, keepdims=True) + EPS)
        xh = v * r
        gv = g_ref[...]
        vn = (xh * gv).astype(BF16)
        do = do_ref[...].astype(F32)
        mask, mask_t = _tril_masks()
        bias = b_ref[...]
        lane = lax.broadcasted_iota(jnp.int32, (T, T), 1)

        @pl.when(pl.program_id(0) == 0)
        def _():
            dg_ref[...] = jnp.zeros_like(dg_ref)
            dw_ref[...] = jnp.zeros_like(dw_ref)
            db_ref[...] = jnp.zeros_like(db_ref)

        dvn_parts = []
        du_parts = []
        dbias = jnp.zeros((T, T), F32)
        for g in range(G):
            sl = slice(g * T, (g + 1) * T)
            wt = jnp.where(mask, w_ref[g], 0.0).astype(BF16)
            mixed = _dot(wt, vn[:, sl]) + _lane_column(bias, g)
            du_parts.append(do[:, sl] * mixed)
            dmix = do[:, sl] * u[:, sl]
            dmix_b = dmix.astype(BF16)
            dw_ref[g] += jnp.where(mask, _dot_nt(dmix_b, vn[:, sl]), 0.0)
            dbias = dbias + jnp.where(lane == g, jnp.sum(dmix, axis=-1, keepdims=True), 0.0)
            wtt = jnp.where(mask_t, wt_ref[g], 0.0).astype(BF16)
            dvn_parts.append(_dot(wtt, dmix_b))
        db_ref[...] += dbias
        dvn = jnp.concatenate(dvn_parts, axis=-1)
        du = jnp.concatenate(du_parts, axis=-1)
        dg_ref[...] += jnp.sum(dvn * xh, axis=0, keepdims=True)
        dxh = dvn * gv
        dv = r * (dxh - xh * jnp.mean(dxh * xh, axis=-1, keepdims=True))
        dc_ref[:, 0:SGU_WIDTH] = (du * _gelu_grad(cu)).astype(dc_ref.dtype)
        dc_ref[:, SGU_WIDTH:2 * SGU_WIDTH] = (dv * _gelu_grad(cv)).astype(dc_ref.dtype)

    wspec = BS((G, T, T), lambda i: (0, 0, 0))
    gspec = BS((1, SGU_WIDTH), lambda i: (0, 0))
    return pl.pallas_call(
        body, name=name, grid=(S // T,),
        in_specs=[BS((T, 2 * SGU_WIDTH), lambda i: (i, c0)), gspec, wspec, wspec, BS((T, T), lambda i: (0, 0)),
                  BS((T, SGU_WIDTH), lambda i: (i, 0))],
        out_specs=[BS((T, 2 * SGU_WIDTH), lambda i: (i, 0)), gspec, wspec, BS((T, T), lambda i: (0, 0))],
        out_shape=[SDS((S, 2 * SGU_WIDTH), BF16), SDS((1, SGU_WIDTH), F32), SDS((G, T, T), F32), SDS((T, T), F32)],
        compiler_params=_params(1))(proj, norm_g, w_s, w_st, b_t, do_c)


def rope_tables(positions):
    half = ROPE_DIM // 2
    inv = ROPE_THETA ** (-jnp.arange(half, dtype=F32) * (2.0 / ROPE_DIM))
    ang = positions.astype(F32)[:, None] * inv
    cos, sin = jnp.cos(ang), jnp.sin(ang)
    S = positions.shape[0]
    ones = jnp.ones((S, HEAD_DIM - ROPE_DIM), F32)
    zeros = jnp.zeros((S, HEAD_DIM - ROPE_DIM), F32)
    zh = jnp.zeros((S, half), F32)
    c = jnp.concatenate([cos, cos, ones], axis=-1)
    sa = jnp.concatenate([zh, sin, zeros], axis=-1)
    sb = jnp.concatenate([-sin, zh, zeros], axis=-1)
    tile2 = lambda t: jnp.concatenate([t, t], axis=-1)
    return tile2(c), tile2(sa), tile2(sb)


def rope_apply(x, col0, ncols, cos, sa, sb, out_dtype, name):
    S = x.shape[0]
    tm = _tile(S, 512, 8)
    half = ROPE_DIM // 2
    b0 = col0 // LANES

    def body(x_ref, c_ref, sa_ref, sb_ref, o_ref):
        xv = x_ref[...].astype(F32)
        o_ref[...] = (xv * c_ref[...] + pltpu.roll(xv, half, 1) * sa_ref[...]
                      + pltpu.roll(xv, LANES - half, 1) * sb_ref[...]).astype(o_ref.dtype)

    tab = BS((tm, LANES), lambda i, j: (i, 0))
    return pl.pallas_call(
        body, name=name, grid=(S // tm, ncols // LANES),
        in_specs=[BS((tm, LANES), lambda i, j: (i, b0 + j)), tab, tab, tab],
        out_specs=BS((tm, LANES), lambda i, j: (i, j)), out_shape=SDS((S, ncols), out_dtype),
        compiler_params=_params(2))(x, cos, sa, sb)


def _dil_masks(first):
    qi = lax.broadcasted_iota(jnp.int32, (DIL_SPAN, DIL_SPAN), 0)
    ki = lax.broadcasted_iota(jnp.int32, (DIL_SPAN, DIL_SPAN), 1)
    return ki >= qi + first.astype(jnp.int32) * DIL_SPAN, ki <= qi


def dil_attn_fwd(q, k, v, nb, name):
    H, S, Dh = q.shape
    T = DIL_SPAN
    nblk = S // T
    scale = Dh ** -0.5

    def body(q_ref, kp_ref, kc_ref, vp_ref, vc_ref, o_ref, lse_ref):
        mp, mc = _dil_masks(pl.program_id(0) % nb == 0)
        for h in range(H):
            qv = q_ref[h]
            sp = jnp.where(mp, _dot_nt(qv, kp_ref[h]) * scale, NEG)
            sc = jnp.where(mc, _dot_nt(qv, kc_ref[h]) * scale, NEG)
            m = jnp.maximum(jnp.max(sp, axis=-1, keepdims=True), jnp.max(sc, axis=-1, keepdims=True))
            ep = jnp.exp(sp - m)
            ec = jnp.exp(sc - m)
            l = jnp.sum(ep, axis=-1, keepdims=True) + jnp.sum(ec, axis=-1, keepdims=True)
            o_ref[h] = (_dot(ep.astype(BF16), vp_ref[h]) + _dot(ec.astype(BF16), vc_ref[h])) / l
            lse_ref[h] = m + jnp.log(l)

    cur = BS((H, T, Dh), lambda b: (0, b, 0))
    prev = BS((H, T, Dh), lambda b: (0, jnp.maximum(b - 1, 0), 0))
    colb = BS((H, T, 1), lambda b: (0, b, 0))
    return pl.pallas_call(
        body, name=name, grid=(nblk,), in_specs=[cur, prev, cur, prev, cur], out_specs=[cur, colb],
        out_shape=[SDS((H, S, Dh), F32), SDS((H, S, 1), F32)], compiler_params=_params(1))(q, k, k, v, v)


def dil_attn_bwd_dq(q, k, v, o, lse, do, dlse, nb, name):
    H, S, Dh = q.shape
    T = DIL_SPAN
    nblk = S // T
    scale = Dh ** -0.5

    def body(q_ref, kp_ref, kc_ref, vp_ref, vc_ref, o_ref, lse_ref, do_ref, dlse_ref, dq_ref, dl_ref):
        mp, mc = _dil_masks(pl.program_id(0) % nb == 0)
        for h in range(H):
            qv = q_ref[h]
            kp, kc = kp_ref[h], kc_ref[h]
            dov = do_ref[h]
            delta = jnp.sum(dov * o_ref[h], axis=-1, keepdims=True) - dlse_ref[h]
            dl_ref[h] = delta
            dob = dov.astype(BF16)
            lse_v = lse_ref[h]
            pp = jnp.exp(jnp.where(mp, _dot_nt(qv, kp) * scale, NEG) - lse_v)
            pc = jnp.exp(jnp.where(mc, _dot_nt(qv, kc) * scale, NEG) - lse_v)
            dsp = pp * (_dot_nt(dob, vp_ref[h]) - delta)
            dsc = pc * (_dot_nt(dob, vc_ref[h]) - delta)
            dq_ref[h] = (_dot(dsp.astype(BF16), kp) + _dot(dsc.astype(BF16), kc)) * scale

    cur = BS((H, T, Dh), lambda b: (0, b, 0))
    prev = BS((H, T, Dh), lambda b: (0, jnp.maximum(b - 1, 0), 0))
    colb = BS((H, T, 1), lambda b: (0, b, 0))
    return pl.pallas_call(
        body, name=name, grid=(nblk,), in_specs=[cur, prev, cur, prev, cur, cur, colb, cur, colb],
        out_specs=[cur, colb], out_shape=[SDS((H, S, Dh), F32), SDS((H, S, 1), F32)],
        compiler_params=_params(1))(q, k, k, v, v, o, lse, do, dlse)


def dil_attn_bwd_dkv(q, k, v, lse_row, delta_row, do, nb, name):
    H, S, Dh = q.shape
    T = DIL_SPAN
    nblk = S // T
    scale = Dh ** -0.5

    def body(k_ref, v_ref, qc_ref, qn_ref, doc_ref, don_ref, lc_ref, ln_ref, dc_ref, dn_ref, dk_ref, dv_ref):
        no_next = ((pl.program_id(0) + 1) % nb == 0).astype(jnp.int32)
        si = lax.broadcasted_iota(jnp.int32, (T, T), 0)
        ti = lax.broadcasted_iota(jnp.int32, (T, T), 1)
        m_cur = si <= ti
        m_next = si >= ti + no_next * T
        for h in range(H):
            kv, vv = k_ref[h], v_ref[h]
            qc, qn = qc_ref[h], qn_ref[h]
            doc, don = doc_ref[h].astype(BF16), don_ref[h].astype(BF16)
            pt = jnp.exp(jnp.where(m_cur, _dot_nt(kv, qc) * scale, NEG) - lc_ref[h])
            ptn = jnp.exp(jnp.where(m_next, _dot_nt(kv, qn) * scale, NEG) - ln_ref[h])
            dv_ref[h] = _dot(pt.astype(BF16), doc) + _dot(ptn.astype(BF16), don)
            dst = pt * (_dot_nt(vv, doc) - dc_ref[h])
            dstn = ptn * (_dot_nt(vv, don) - dn_ref[h])
            dk_ref[h] = (_dot(dst.astype(BF16), qc) + _dot(dstn.astype(BF16), qn)) * scale

    cur = BS((H, T, Dh), lambda b: (0, b, 0))
    nxt = BS((H, T, Dh), lambda b: (0, jnp.minimum(b + 1, nblk - 1), 0))
    rcur = BS((H, None, 1, T), lambda b: (0, b, 0, 0))
    rnxt = BS((H, None, 1, T), lambda b: (0, jnp.minimum(b + 1, nblk - 1), 0, 0))
    return pl.pallas_call(
        body, name=name, grid=(nblk,), in_specs=[cur, cur, cur, nxt, cur, nxt, rcur, rnxt, rcur, rnxt],
        out_specs=[cur, cur], out_shape=[SDS((H, S, Dh), F32), SDS((H, S, Dh), F32)],
        compiler_params=_params(1))(k, v, q, q, do, do, lse_row, lse_row, delta_row, delta_row)


def dil_merge_fwd(outs, lses, name):
    H, S, Dh = outs[0].shape
    tm = _tile(S, 512, 8)

    def body(o0, o1, o2, l0, l1, l2, out_ref):
        ls = [l0[...], l1[...], l2[...]]
        m = jnp.maximum(jnp.maximum(ls[0], ls[1]), ls[2])
        es = [jnp.exp(l - m) for l in ls]
        den = es[0] + es[1] + es[2]
        out_ref[...] = (es[0] * o0[...] + es[1] * o1[...] + es[2] * o2[...]) / den

    blk = BS((None, tm, Dh), lambda h, i: (h, i, 0))
    colb = BS((None, tm, 1), lambda h, i: (h, i, 0))
    return pl.pallas_call(
        body, name=name, grid=(H, S // tm), in_specs=[blk] * 3 + [colb] * 3, out_specs=blk,
        out_shape=SDS((H, S, Dh), F32), compiler_params=_params(2))(*outs, *lses)


def dil_merge_bwd(outs, lses, d_out, name):
    H, S, Dh = outs[0].shape
    tm = _tile(S, 512, 8)

    def body(o0, o1, o2, l0, l1, l2, d_ref, do0, do1, do2, dl0, dl1, dl2):
        ls = [l0[...], l1[...], l2[...]]
        m = jnp.maximum(jnp.maximum(ls[0], ls[1]), ls[2])
        es = [jnp.exp(l - m) for l in ls]
        den = es[0] + es[1] + es[2]
        ws = [e / den for e in es]
        dv = d_ref[...]
        dws = [jnp.sum(dv * o[...], axis=-1, keepdims=True) for o in (o0, o1, o2)]
        mean = ws[0] * dws[0] + ws[1] * dws[1] + ws[2] * dws[2]
        for w, dw, do_ref, dl_ref in zip(ws, dws, (do0, do1, do2), (dl0, dl1, dl2)):
            do_ref[...] = w * dv
            dl_ref[...] = w * (dw - mean)

    blk = BS((None, tm, Dh), lambda h, i: (h, i, 0))
    colb = BS((None, tm, 1), lambda h, i: (h, i, 0))
    return pl.pallas_call(
        body, name=name, grid=(H, S // tm), in_specs=[blk] * 3 + [colb] * 3 + [blk],
        out_specs=[blk] * 3 + [colb] * 3,
        out_shape=[SDS((H, S, Dh), F32)] * 3 + [SDS((H, S, 1), F32)] * 3,
        compiler_params=_params(2))(*outs, *lses, d_out)


MERGE_TN = 512


def merge_fwd(o_stack, w_br, proj, name):
    _, S, K = o_stack.shape
    tm = _tile(S, 512, 8)
    tn = MERGE_TN
    nj = D_MODEL // tn

    def body(o_ref, w_ref, gl_ref, m_ref, acc):
        br = pl.program_id(2)

        @pl.when(br == 0)
        def _():
            acc[...] = jnp.zeros_like(acc)

        acc[...] += _sigmoid(gl_ref[...]) * _dot(o_ref[...], w_ref[...])

        @pl.when(br == N_BRANCH - 1)
        def _():
            m_ref[...] = acc[...].astype(m_ref.dtype)

    return pl.pallas_call(
        body, name=name, grid=(S // tm, nj, N_BRANCH),
        in_specs=[BS((None, tm, K), lambda i, j, b: (b, i, 0)), BS((None, K, tn), lambda i, j, b: (b, 0, j)),
                  BS((tm, tn), lambda i, j, b: (i, b * nj + j))],
        out_specs=BS((tm, tn), lambda i, j, b: (i, j)), out_shape=SDS((S, D_MODEL), BF16),
        scratch_shapes=[pltpu.VMEM((tm, tn), F32)], compiler_params=_params(3))(o_stack, w_br, proj)


def merge_bwd(o_stack, w_br, proj, d_merged, name):
    _, S, K = o_stack.shape
    tm = _tile(S, 512, 8)
    tn = MERGE_TN
    nj = D_MODEL // tn

    def body(o_ref, w_ref, gl_ref, dm_ref, dgl_ref, dy_ref):
        gate = _sigmoid(gl_ref[...])
        y = _dot(o_ref[...], w_ref[...])
        dm = dm_ref[...].astype(F32)
        dgl_ref[...] = (dm * y * gate * (1.0 - gate)).astype(dgl_ref.dtype)
        dy_ref[...] = (dm * gate).astype(dy_ref.dtype)

    return pl.pallas_call(
        body, name=name, grid=(S // tm, nj, N_BRANCH),
        in_specs=[BS((None, tm, K), lambda i, j, b: (b, i, 0)), BS((None, K, tn), lambda i, j, b: (b, 0, j)),
                  BS((tm, tn), lambda i, j, b: (i, b * nj + j)), BS((tm, tn), lambda i, j, b: (i, j))],
        out_specs=[BS((tm, tn), lambda i, j, b: (i, b * nj + j)), BS((None, tm, tn), lambda i, j, b: (b, i, j))],
        out_shape=[SDS((S, N_BRANCH * D_MODEL), BF16), SDS((N_BRANCH, S, D_MODEL), BF16)],
        compiler_params=_params(3))(o_stack, w_br, proj, d_merged)


FFN_CW = 256


def ffn_mid_fwd(pre_g, pre_v, w_g, w_v, name, side=None):
    S = pre_g.shape[0]
    tm = _tile(S, CONV_TM, 8)

    def body(g_ref, gp_ref, v_ref, vp_ref, wg_ref, wv_ref, a_ref):
        keep = (pl.program_id(1) > 0).astype(F32)
        ug = _conv_ext(jnp.concatenate([gp_ref[...] * keep, g_ref[...]], axis=0), wg_ref)[HALO:, :]
        uv = _conv_ext(jnp.concatenate([vp_ref[...] * keep, v_ref[...]], axis=0), wv_ref)[HALO:, :]
        a_ref[...] = (ug * _sigmoid(ug) * uv).astype(a_ref.dtype)

    cur, prev, _ = _time_specs(S, tm, FFN_CW, lambda j: j)
    wspec = BS((3, FFN_CW), lambda j, i: (0, j))
    (a,), side_out = _call_with_side(body, side, name, (D_FF // FFN_CW, S // tm),
                                     [cur, prev, cur, prev, wspec, wspec], [cur], [SDS((S, D_FF), BF16)],
                                     (pre_g, pre_g, pre_v, pre_v, w_g, w_v))
    return a, side_out


def ffn_mid_bwd(pre_g, pre_v, w_g, w_v, d_a, name):
    S = pre_g.shape[0]
    tm = _tile(S, CONV_TM, 8)
    nt = S // tm

    def body(g_ref, gp_ref, gn_ref, v_ref, vp_ref, vn_ref, wg_ref, wv_ref, da_ref, dan_ref,
             dg_ref, dv_ref, dwg_ref, dwv_ref):
        i = pl.program_id(1)
        keep_prev = (i > 0).astype(F32)
        keep_next = (i < nt - 1).astype(F32)
        eg = jnp.concatenate([gp_ref[...] * keep_prev, g_ref[...], gn_ref[...]], axis=0)
        ev = jnp.concatenate([vp_ref[...] * keep_prev, v_ref[...], vn_ref[...]], axis=0)
        ug = _conv_ext(eg, wg_ref)
        uv = _conv_ext(ev, wv_ref)
        da = jnp.concatenate([jnp.zeros((HALO, FFN_CW), F32), da_ref[...], dan_ref[...] * keep_next], axis=0)
        sg = _sigmoid(ug)
        dug = da * uv * (sg * (1.0 + ug * (1.0 - sg)))
        duv = da * (ug * sg)
        dg_ref[...] = _conv_t_ext(dug, wg_ref)[HALO:HALO + tm, :].astype(dg_ref.dtype)
        dv_ref[...] = _conv_t_ext(duv, wv_ref)[HALO:HALO + tm, :].astype(dv_ref.dtype)

        @pl.when(i == 0)
        def _():
            dwg_ref[...] = jnp.zeros_like(dwg_ref)
            dwv_ref[...] = jnp.zeros_like(dwv_ref)

        for dup_e, e, dw_ref in ((dug, eg, dwg_ref), (duv, ev, dwv_ref)):
            dup = dup_e[HALO:HALO + tm, :]
            dw_ref[0:1, :] += jnp.sum(dup * pltpu.roll(e, 2, 0)[HALO:HALO + tm, :], axis=0, keepdims=True)
            dw_ref[1:2, :] += jnp.sum(dup * pltpu.roll(e, 1, 0)[HALO:HALO + tm, :], axis=0, keepdims=True)
            dw_ref[2:3, :] += jnp.sum(dup * e[HALO:HALO + tm, :], axis=0, keepdims=True)

    cur, prev, nxt = _time_specs(S, tm, FFN_CW, lambda j: j)
    wspec = BS((3, FFN_CW), lambda j, i: (0, j))
    return pl.pallas_call(
        body, name=name, grid=(D_FF // FFN_CW, nt),
        in_specs=[cur, prev, nxt, cur, prev, nxt, wspec, wspec, cur, nxt],
        out_specs=[cur, cur, wspec, wspec],
        out_shape=[SDS((S, D_FF), BF16)] * 2 + [SDS((3, D_FF), F32)] * 2,
        compiler_params=_params(2))(pre_g, pre_g, pre_g, pre_v, pre_v, pre_v, w_g, w_v, d_a, d_a)


def ple_fwd(x, a, e, name):
    S, Dm = x.shape
    tm = _tile(S, 256, 8)

    def body(x_ref, a_ref, e_ref, o_ref):
        o_ref[...] = x_ref[...] + _sigmoid(a_ref[...]) * e_ref[...]

    row = BS((tm, Dm), lambda i: (i, 0))
    return pl.pallas_call(body, name=name, grid=(S // tm,), in_specs=[row] * 3, out_specs=row,
                          out_shape=SDS((S, Dm), F32), compiler_params=_params(1))(x, a, e)


def ple_bwd(a, e, dx, name):
    S, Dm = a.shape
    tm = _tile(S, 256, 8)

    def body(a_ref, e_ref, dx_ref, da_ref, de_ref):
        s = _sigmoid(a_ref[...])
        d = dx_ref[...]
        da_ref[...] = (d * e_ref[...] * s * (1.0 - s)).astype(da_ref.dtype)
        de_ref[...] = (d * s).astype(de_ref.dtype)

    row = BS((tm, Dm), lambda i: (i, 0))
    return pl.pallas_call(body, name=name, grid=(S // tm,), in_specs=[row] * 3, out_specs=[row, row],
                          out_shape=[SDS((S, Dm), BF16)] * 2, compiler_params=_params(1))(a, e, dx)


def to_heads(x, n_heads, dil):
    S = x.shape[0]
    x = x.reshape(S // dil, dil, n_heads, HEAD_DIM).transpose(2, 1, 0, 3)
    return x.reshape(n_heads, S, HEAD_DIM)


def from_heads(y, dil):
    H, S, C = y.shape
    y = y.reshape(H, dil, S // dil, C).transpose(2, 1, 0, 3)
    return y.reshape(S, H * C)


def rows_of(col, t):
    H, S, _ = col.shape
    return col.reshape(H, S // t, 1, t)


IN_SEGMENTS = ((SRC_A, SRC_F, OFF_A), (SRC_F, SRC_B, OFF_F), (SRC_B, SRC_C, OFF_B), (SRC_C, SRC_DQ, OFF_C),
               (SRC_DQ, SRC_G, OFF_D), (SRC_G, D_IN, OFF_G))
IN_SHARD = D_IN // N_CHIPS


def w_in_aligned_from_chips(t):
    pieces = []
    for ref_lo, ref_hi, _ in sorted(IN_SEGMENTS, key=lambda seg: seg[2]):
        for k in range(N_CHIPS):
            lo, hi = max(ref_lo, k * IN_SHARD), min(ref_hi, (k + 1) * IN_SHARD)
            if lo < hi:
                pieces.append(t[k][:, lo - k * IN_SHARD:hi - k * IN_SHARD])
    pieces.append(jnp.zeros((t[0].shape[0], W_AL - D_IN), t[0].dtype))
    return jnp.concatenate(pieces, axis=1)


def w_in_chips_from_aligned(g):
    slots = []
    for k in range(N_CHIPS):
        pieces = []
        for ref_lo, ref_hi, al in IN_SEGMENTS:
            lo, hi = max(ref_lo, k * IN_SHARD), min(ref_hi, (k + 1) * IN_SHARD)
            if lo < hi:
                pieces.append(g[:, al + lo - ref_lo:al + hi - ref_lo])
        slots.append(jnp.concatenate(pieces, axis=1))
    return jnp.stack(slots, axis=0)


def chips_to_full(t, name):
    return jnp.concatenate([t[k] for k in range(N_CHIPS)], axis=0 if name in ROW_SHARDED else 1)


def full_to_chips(g, name):
    if name in ROW_SHARDED:
        return g.reshape(N_CHIPS, g.shape[0] // N_CHIPS, g.shape[1])
    return g.reshape(g.shape[0], N_CHIPS, g.shape[1] // N_CHIPS).transpose(1, 0, 2)


def halves_from_chips(t):
    return jnp.concatenate([t[0], t[1]], axis=1), jnp.concatenate([t[2], t[3]], axis=1)


def chips_from_halves(g, v):
    c = g.shape[1] // 2
    return jnp.stack([g[:, :c], g[:, c:], v[:, :c], v[:, c:]], axis=0)


def layer_fwd(x, p_l, rope, w, tag, sides=None):
    S = x.shape[0]
    sides = sides or {}
    side_out = {}
    sv = {"x0": x}
    h = rmsnorm_fwd(x, w["norm_mix_g"], f"{tag}_norm_mix")
    proj = matmul(h, w["w_in_al"], "nn", F32, f"{tag}_proj", side=sides.get("proj"))
    if "proj" in sides:
        proj, side_out["proj"] = proj
    sv["h"], sv["proj"] = h, proj

    af_t = proj[:, OFF_F:OFF_F + FOX_HEADS].T
    f_cum = fox_prep_fwd(af_t, w["fox_forget_b"].reshape(FOX_HEADS, 1), f"{tag}_fox_prep")
    T = min(FOX_T, S)
    f_col = f_cum.reshape(FOX_HEADS, S, 1)
    f_row = f_cum.reshape(FOX_HEADS, S // T, 1, T)
    qkv = to_heads(proj[:, OFF_A:OFF_B].astype(BF16), 3 * FOX_HEADS, 1)
    qa, ka, va = qkv[:FOX_HEADS], qkv[FOX_HEADS:2 * FOX_HEADS], qkv[2 * FOX_HEADS:]
    (oa_h, lse_a), side_out["fox"] = fox_attn_fwd(qa, ka, va, f_col, f_row, f"{tag}_fox_fwd", sides.get("fox"))
    o_a = from_heads(oa_h, 1)
    sv.update(af_t=af_t, f_col=f_col, f_row=f_row, qa=qa, ka=ka, va=va, oa_h=oa_h, lse_a=lse_a)

    o_b = shortconv_fwd(proj, w["shortconv_w"], f"{tag}_sconv_fwd")

    o_c = sgu_fwd(proj, w["sgu_norm_g"].reshape(1, SGU_WIDTH), w["sgu_w"], _sgu_bias(w["sgu_b"]), f"{tag}_sgu_fwd")

    cos, sa, sb = rope
    qk = rope_apply(proj, OFF_D, 2 * DIL_WIDTH, cos, sa, sb, BF16, f"{tag}_rope_fwd")
    vd = proj[:, OFF_D + 2 * DIL_WIDTH:OFF_D + 3 * DIL_WIDTH].astype(BF16)
    outs, lses, dil_sv = [], [], []
    for g, (window, dil) in enumerate(DIL_PATTERNS):
        sl = slice(g * DIL_OUT, (g + 1) * DIL_OUT)
        qg = to_heads(qk[:, sl], 4, dil)
        kg = to_heads(qk[:, DIL_WIDTH:][:, sl], 4, dil)
        vg = to_heads(vd[:, sl], 4, dil)
        nb = (S // dil) // DIL_SPAN
        og, lg = dil_attn_fwd(qg, kg, vg, nb, f"{tag}_dil{g}_fwd")
        dil_sv.append((qg, kg, vg, og, lg, nb))
        outs.append(_heads_unperm(og, dil))
        lses.append(_col_unperm(lg, dil))
    od_h = dil_merge_fwd(outs, lses, f"{tag}_dil_merge_fwd")
    o_d = from_heads(od_h, 1)
    sv.update(dil=dil_sv, outs=outs, lses=lses)

    o_d_pad = jnp.concatenate([o_d.astype(BF16), jnp.zeros((S, FOX_WIDTH - DIL_OUT), BF16)], axis=-1)
    o_stack = jnp.stack([o_a, o_b, o_c, o_d_pad], axis=0)
    merged = merge_fwd(o_stack, w["w_br"], proj, f"{tag}_merge_fwd")
    x1 = matmul(merged, w["w_out"], "nn", F32, f"{tag}_out_proj", res=x)
    sv.update(o_stack=o_stack, merged=merged, x1=x1)

    h2 = rmsnorm_fwd(x1, w["norm_ffn_g"], f"{tag}_norm_ffn")
    pre = (matmul(h2, w["w_up_g"], "nn", F32, f"{tag}_up_g"), matmul(h2, w["w_up_v"], "nn", F32, f"{tag}_up_v"))
    a, side_out["ffn"] = ffn_mid_fwd(pre[0], pre[1], w["ffn_conv_g"], w["ffn_conv_v"], f"{tag}_ffn_mid_fwd",
                                     sides.get("ffn"))
    x2 = matmul(a, w["w_down"], "nn", F32, f"{tag}_down", res=x1)
    sv.update(h2=h2, pre=pre, a=a, x2=x2)

    n3 = rmsnorm_fwd(x2, w["norm_ple_g"], f"{tag}_norm_ple")
    pg = matmul(n3, w["w_ple_gate"], "nn", F32, f"{tag}_ple_gate")
    pe = matmul(p_l, w["w_ple_proj"], "nn", F32, f"{tag}_ple_proj")
    x3 = ple_fwd(x2, pg, pe, f"{tag}_ple_fwd")
    sv.update(n3=n3, pg=pg, pe=pe, p_l=p_l)
    return x3, sv, side_out


def _sgu_bias(b):
    return jnp.pad(b.T, ((0, 0), (0, SGU_CHUNK - b.shape[0])))


def _col_unperm(col, dil):
    H, S, _ = col.shape
    return col.reshape(H, dil, S // dil).transpose(0, 2, 1).reshape(H, S, 1)


def _col_perm(col, dil):
    H, S, _ = col.shape
    return col.reshape(H, S // dil, dil).transpose(0, 2, 1).reshape(H, S, 1)


def _heads_perm(y, dil):
    H, S, C = y.shape
    return y.reshape(H, S // dil, dil, C).transpose(0, 2, 1, 3).reshape(H, S, C)


def _heads_unperm(y, dil):
    H, S, C = y.shape
    return y.reshape(H, dil, S // dil, C).transpose(0, 2, 1, 3).reshape(H, S, C)


def layer_bwd(dx3, sv, rope, w, tag, sides=None):
    sides = sides or {}
    side_out = {}
    S = dx3.shape[0]
    gr = {}
    da, de = ple_bwd(sv["pg"], sv["pe"], dx3, f"{tag}_ple_bwd")
    gr["w_ple_proj"] = matmul(sv["p_l"], de, "tn", F32, f"{tag}_dw_ple_proj")
    gr["w_ple_gate"] = matmul(sv["n3"], da, "tn", F32, f"{tag}_dw_ple_gate")
    dn3 = matmul(da, w["w_ple_gate"], "nt", BF16, f"{tag}_dn3")
    dx2, gr["norm_ple_g"] = rmsnorm_bwd(sv["x2"], w["norm_ple_g"], dn3, dx3, f"{tag}_norm_ple_bwd")

    d_a = matmul(dx2, w["w_down"], "nt", F32, f"{tag}_da")
    gr["w_down"] = matmul(sv["a"], dx2, "tn", F32, f"{tag}_dw_down")
    dpre_g, dpre_v, dwc_g, dwc_v = ffn_mid_bwd(sv["pre"][0], sv["pre"][1], w["ffn_conv_g"], w["ffn_conv_v"], d_a,
                                               f"{tag}_ffn_mid_bwd")
    gr["ffn_conv_w"] = (dwc_g, dwc_v)
    h2_t = sv["h2"].T
    gr["w_up"] = (matmul(h2_t, dpre_g, "nn", F32, f"{tag}_dw_up_g", tm=2048, tn=512),
                  matmul(h2_t, dpre_v, "nn", F32, f"{tag}_dw_up_v", tm=2048, tn=512))
    dh2_g = matmul(dpre_g, w["w_up_g"], "nt", F32, f"{tag}_dh2_g")
    dh2 = matmul(dpre_v, w["w_up_v"], "nt", BF16, f"{tag}_dh2_v", res=dh2_g)
    dx1, gr["norm_ffn_g"] = rmsnorm_bwd(sv["x1"], w["norm_ffn_g"], dh2, dx2, f"{tag}_norm_ffn_bwd")

    d_merged = matmul(dx1, w["w_out"], "nt", BF16, f"{tag}_dmerged")
    gr["w_out"] = matmul(sv["merged"], dx1, "tn", F32, f"{tag}_dw_out")
    proj = sv["proj"]
    dgl, dy = merge_bwd(sv["o_stack"], w["w_br"], proj, d_merged, f"{tag}_merge_bwd")
    d_o, d_wbr = [], []
    for b in range(N_BRANCH):
        d_o.append(matmul(dy[b], w["w_br"][b], "nt", F32, f"{tag}_do{b}"))
        d_wbr.append(matmul(sv["o_stack"][b], dy[b], "tn", F32, f"{tag}_dw_br{b}"))
    gr["w_br"] = d_wbr

    do_a = to_heads(d_o[0].astype(BF16), FOX_HEADS, 1)
    T = min(FOX_T, S)
    (dqa, delta_a, d_fq), side_out["dq"] = fox_attn_bwd_dq(
        sv["qa"], sv["ka"], sv["va"], sv["f_col"], sv["f_row"], sv["oa_h"], sv["lse_a"], do_a, f"{tag}_fox_dq",
        sides.get("dq"))
    (dka, dva, d_fk), side_out["dkv"] = fox_attn_bwd_dkv(
        sv["qa"], sv["ka"], sv["va"], sv["f_col"], sv["f_row"], rows_of(sv["lse_a"], T), rows_of(delta_a, T), do_a,
        f"{tag}_fox_dkv", sides.get("dkv"))
    daf_t, dfb = fox_prep_bwd(sv["af_t"], w["fox_forget_b"].reshape(FOX_HEADS, 1), d_fq.reshape(FOX_HEADS, S),
                              d_fk.reshape(FOX_HEADS, S), f"{tag}_fox_prep_bwd")
    gr["fox_forget_b"] = dfb.reshape(FOX_HEADS)
    d_proj_a = from_heads(jnp.concatenate([dqa, dka, dva], axis=0), 1).astype(BF16)

    dxb, dgb, dgc, gr["shortconv_w"] = shortconv_bwd(proj, w["shortconv_w"], d_o[1], f"{tag}_sconv_bwd")

    d_c, dsg, dsw, dsb = sgu_bwd(proj, w["sgu_norm_g"].reshape(1, SGU_WIDTH), w["sgu_w"],
                                 jnp.swapaxes(w["sgu_w"], 1, 2), _sgu_bias(w["sgu_b"]), d_o[2], f"{tag}_sgu_bwd")
    gr["sgu_norm_g"] = dsg.reshape(SGU_WIDTH)
    gr["sgu_w"] = dsw
    gr["sgu_b"] = dsb[:, :SGU_WIDTH // SGU_CHUNK].T

    d_od = to_heads(d_o[3][:, :DIL_OUT], 4, 1)
    d_outs_lses = dil_merge_bwd(sv["outs"], sv["lses"], d_od, f"{tag}_dil_merge_bwd")
    d_outs, d_lses = d_outs_lses[:3], d_outs_lses[3:]
    dq_parts, dk_parts, dv_parts = [], [], []
    for g, (window, dil) in enumerate(DIL_PATTERNS):
        qg, kg, vg, og, lg, nb = sv["dil"][g]
        do_g = _heads_perm(d_outs[g], dil)
        dl_g = _col_perm(d_lses[g], dil)
        dqg, delta_g = dil_attn_bwd_dq(qg, kg, vg, og, lg, do_g, dl_g, nb, f"{tag}_dil{g}_dq")
        dkg, dvg = dil_attn_bwd_dkv(qg, kg, vg, rows_of(lg, DIL_SPAN), rows_of(delta_g, DIL_SPAN), do_g, nb,
                                    f"{tag}_dil{g}_dkv")
        dq_parts.append(from_heads(dqg, dil))
        dk_parts.append(from_heads(dkg, dil))
        dv_parts.append(from_heads(dvg, dil))
    cos, sa, sb = rope
    d_qk_rot = jnp.concatenate(dq_parts + dk_parts, axis=-1)
    d_qk = rope_apply(d_qk_rot, 0, 2 * DIL_WIDTH, cos, -sa, -sb, BF16, f"{tag}_rope_bwd")
    d_vd = jnp.concatenate(dv_parts, axis=-1).astype(BF16)

    d_f_cols = jnp.concatenate([daf_t.T.astype(BF16), jnp.zeros((S, W_AL - OFF_F - FOX_HEADS), BF16)], axis=-1)
    d_proj = jnp.concatenate([dgl, d_proj_a, dxb, dgb, dgc, d_c, d_qk, d_vd, d_f_cols], axis=-1)
    gr["w_in_al"] = matmul(sv["h"], d_proj, "tn", F32, f"{tag}_dw_in", tm=2048, tn=512)
    dh = matmul(d_proj, w["w_in_al"], "nt", BF16, f"{tag}_dh")
    dx0, gr["norm_mix_g"] = rmsnorm_bwd(sv["x0"], w["norm_mix_g"], dh, dx1, f"{tag}_norm_mix_bwd")
    return dx0, gr, side_out


def local_weights(chips, repl, layer):
    w = {n: repl[n][layer] for n in REPLICATED}
    cast = lambda n, dtype: [chips[n][k].astype(dtype) for k in range(N_CHIPS)]
    full = {n: chips_to_full(cast(n, BF16), n)
            for n in ("w_br_fox", "w_br_conv", "w_br_sgu", "w_br_dil", "w_out", "w_down", "w_ple_gate", "w_ple_proj")}
    w["w_in_al"] = w_in_aligned_from_chips(cast("w_in", BF16))
    w["shortconv_w"] = chips_to_full(cast("shortconv_w", F32), "shortconv_w")
    pad = jnp.zeros((FOX_WIDTH - DIL_OUT, D_MODEL), BF16)
    w["w_br"] = jnp.stack([full["w_br_fox"], full["w_br_conv"], full["w_br_sgu"],
                           jnp.concatenate([full["w_br_dil"], pad], axis=0)], axis=0)
    w["w_up_g"], w["w_up_v"] = halves_from_chips(cast("w_up", BF16))
    w["ffn_conv_g"], w["ffn_conv_v"] = halves_from_chips(cast("ffn_conv_w", F32))
    for n in ("w_out", "w_down", "w_ple_gate", "w_ple_proj"):
        w[n] = full[n]
    return w


def grads_to_chips(gr):
    out = {n: gr[n] for n in ("fox_forget_b", "sgu_norm_g", "sgu_w", "sgu_b")}
    out["norm_mix_g"] = gr["norm_mix_g"].reshape(D_MODEL)
    out["norm_ffn_g"] = gr["norm_ffn_g"].reshape(D_MODEL)
    out["norm_ple_g"] = gr["norm_ple_g"].reshape(D_MODEL)
    out["w_in"] = w_in_chips_from_aligned(gr["w_in_al"])
    out["w_up"] = chips_from_halves(*gr["w_up"])
    out["ffn_conv_w"] = chips_from_halves(*gr["ffn_conv_w"])
    for b, n in enumerate(("w_br_fox", "w_br_conv", "w_br_sgu")):
        out[n] = full_to_chips(gr["w_br"][b], n)
    out["w_br_dil"] = full_to_chips(gr["w_br"][3][:DIL_OUT], "w_br_dil")
    for n in ("shortconv_w", "w_out", "w_down", "w_ple_gate", "w_ple_proj"):
        out[n] = full_to_chips(gr[n], n)
    return out


def local_step(x, p, positions, repl, final_norm_g, loss_target, exch):
    depth = p.shape[0]
    rope = rope_tables(positions)
    saved, ws = [], []
    chips = exch.first_weights()
    for layer in range(depth):
        w = local_weights(chips, repl, layer)
        side = exch.weights_exchange(layer + 1) if layer + 1 < depth else None
        x, sv, side_out = layer_fwd(x, p[layer].astype(BF16), rope, w, f"l{layer}", side)
        if layer + 1 < depth:
            chips = exch.weights_arrived(layer + 1, side_out)
        saved.append(sv)
        ws.append(w)
    loss_part, dx, dgf = final_loss(x, final_norm_g, loss_target, "final_loss")
    for layer in range(depth - 1, -1, -1):
        dx, gr, side_out = layer_bwd(dx, saved[layer], rope, ws[layer], f"l{layer}", exch.grads_exchange())
        exch.grads_arrived(side_out)
        exch.grads_ready(layer, grads_to_chips(gr))
    exch.grads_flush()
    return loss_part[0, 0], dx, dgf.reshape(-1)


def _position():
    return lax.axis_index("x"), lax.axis_index("y"), lax.axis_index("c")


def _other_chips(x, y):
    return [(1 - x, y), (x, 1 - y), (1 - x, 1 - y)]


def _remote(src, dst, send_sem, recv_sem, device):
    return pltpu.make_async_remote_copy(src_ref=src, dst_ref=dst, send_sem=send_sem, recv_sem=recv_sem,
                                        device_id=device, device_id_type=MESH)


def _chip_index():
    return 2 * lax.axis_index("x") + lax.axis_index("y")


def _block_rows(rows, cols, unit):
    return _tile(rows, max(unit, (1 << 19) // cols // unit * unit), unit)


def gather_chip_shards(packs, name):
    return _run_exchange(gather_exchange(packs), name)


def gather_exchange(packs):
    n = len(packs)
    halves = [p.shape[0] // 2 for p in packs]

    def half(outs, t, chip, core):
        return outs[t].at[chip, pl.ds(core * halves[t], halves[t]), :]

    def ici_sends(srcs, outs, send_sems, recv_sems):
        x, y, c = _position()
        me = 2 * x + y
        return [_remote(srcs[t].at[pl.ds(c * halves[t], halves[t]), :], half(outs, t, me, c),
                        send_sems.at[6 * t + j], recv_sems.at[6 * t + j], (px, py, c))
                for t in range(n) for j, (px, py) in enumerate(_other_chips(x, y))]

    def start(srcs, outs, send_sems, recv_sems):
        for cp in ici_sends(srcs, outs, send_sems, recv_sems):
            cp.start()

    def finish(srcs, outs, send_sems, recv_sems):
        x, y, c = _position()
        sibling = (x, y, 1 - c)
        chips = _other_chips(x, y)
        passed = []
        for t in range(n):
            for j, (px, py) in enumerate(chips):
                k = 2 * px + py
                s = 6 * t + j
                landed = half(outs, t, k, c)
                _remote(landed, landed, send_sems.at[s], recv_sems.at[s], (px, py, c)).wait_recv()
                fwd = _remote(landed, landed, send_sems.at[s + 3], recv_sems.at[s + 3], sibling)
                fwd.start()
                passed.append(fwd)
        for t in range(n):
            for j, (px, py) in enumerate(chips):
                s = 6 * t + j + 3
                theirs = half(outs, t, 2 * px + py, 1 - c)
                _remote(theirs, theirs, send_sems.at[s], recv_sems.at[s], sibling).wait_recv()
        for cp in ici_sends(srcs, outs, send_sems, recv_sems) + passed:
            cp.wait_send()

    return SideExchange(list(packs), [SDS((N_CHIPS,) + p.shape, p.dtype) for p in packs], 6 * n, start, finish)


def _run_exchange(side, name):
    n_in, n_out = len(side.operands), len(side.out_shapes)

    def body(*refs):
        srcs, outs, (send_sems, recv_sems) = refs[:n_in], refs[n_in:n_in + n_out], refs[n_in + n_out:]
        side.start(srcs, outs, send_sems, recv_sems)
        side.finish(srcs, outs, send_sems, recv_sems)

    return pl.pallas_call(
        body, name=name, in_specs=[ANY] * n_in, out_specs=[ANY] * n_out, out_shape=side.out_shapes,
        scratch_shapes=[pltpu.SemaphoreType.DMA((side.n_sems,)), pltpu.SemaphoreType.DMA((side.n_sems,))],
    )(*side.operands)


def swap_halves_with_sibling(gs, name):
    n = len(gs)
    halves = [g.shape[1] // 2 for g in gs]

    def body(*refs):
        srcs, lands, (send_sems, recv_sems) = refs[:n], refs[n:2 * n], refs[2 * n:]
        x, y, c = _position()
        copies = [_remote(srcs[t].at[:, pl.ds((1 - c) * halves[t], halves[t]), :], lands[t], send_sems.at[t],
                          recv_sems.at[t], (x, y, 1 - c)) for t in range(n)]
        for cp in copies:
            cp.start()
        for cp in copies:
            cp.wait()

    return pl.pallas_call(
        body, name=name, in_specs=[ANY] * n, out_specs=[ANY] * n,
        out_shape=[SDS((g.shape[0], h, g.shape[2]), g.dtype) for g, h in zip(gs, halves)],
        scratch_shapes=[pltpu.SemaphoreType.DMA((n,)), pltpu.SemaphoreType.DMA((n,))])(*gs)


def add_my_half(g, other, out_dtype, name):
    n, R, C = g.shape
    H = R // 2
    tr = _block_rows(H, C, 16) if H % 16 == 0 else H
    nb = H // tr
    core = lax.axis_index("c").astype(jnp.int32).reshape(1)

    def body(c_ref, g_ref, o_ref, out_ref):
        out_ref[...] = (g_ref[...] + o_ref[...]).astype(out_ref.dtype)

    grid_spec = pltpu.PrefetchScalarGridSpec(
        num_scalar_prefetch=1, grid=(n, nb),
        in_specs=[BS((None, tr, C), lambda s, i, c_ref: (s, c_ref[0] * nb + i, 0)),
                  BS((None, tr, C), lambda s, i, c_ref: (s, i, 0))],
        out_specs=BS((None, tr, C), lambda s, i, c_ref: (s, i, 0)))
    return pl.pallas_call(body, name=name, grid_spec=grid_spec, out_shape=SDS((n, H, C), out_dtype),
                          compiler_params=_params(2))(core, g, other)


def exchange_slots_between_chips(parts, name):
    return _run_exchange(slot_exchange(parts), name)


def slot_exchange(parts):
    n = len(parts)

    def sends(srcs, lands, send_sems, recv_sems):
        x, y, c = _position()
        me = 2 * x + y
        return [_remote(srcs[t].at[2 * px + py], lands[t].at[me], send_sems.at[3 * t + j], recv_sems.at[3 * t + j],
                        (px, py, c)) for t in range(n) for j, (px, py) in enumerate(_other_chips(x, y))]

    def start(srcs, lands, send_sems, recv_sems):
        for cp in sends(srcs, lands, send_sems, recv_sems):
            cp.start()

    def finish(srcs, lands, send_sems, recv_sems):
        x, y, c = _position()
        for t in range(n):
            for j, (px, py) in enumerate(_other_chips(x, y)):
                k = 2 * px + py
                _remote(srcs[t].at[k], lands[t].at[k], send_sems.at[3 * t + j], recv_sems.at[3 * t + j],
                        (px, py, c)).wait_recv()
        for cp in sends(srcs, lands, send_sems, recv_sems):
            cp.wait_send()

    return SideExchange(list(parts), [SDS(p.shape, p.dtype) for p in parts], 3 * n, start, finish)


def sum_slots_into_my_half(landed, mine, name):
    n, H, C = landed.shape
    tr = _block_rows(H, C, 16) if H % 16 == 0 else H
    nb = H // tr
    where = jnp.stack([lax.axis_index("c"), _chip_index()]).astype(jnp.int32)

    def body(w_ref, l_ref, m_ref, o_ref):
        me = w_ref[1]
        o_ref[...] = jnp.zeros_like(o_ref)
        for k in range(n):
            @pl.when(me == k)
            def _():
                o_ref[...] += m_ref[k].astype(F32)

            @pl.when(me != k)
            def _():
                o_ref[...] += l_ref[k].astype(F32)

    slots = BS((n, tr, C), lambda i, w_ref: (0, i, 0))
    grid_spec = pltpu.PrefetchScalarGridSpec(
        num_scalar_prefetch=1, grid=(nb,), in_specs=[slots, slots],
        out_specs=BS((tr, C), lambda i, w_ref: (w_ref[0] * nb + i, 0)))
    return pl.pallas_call(body, name=name, grid_spec=grid_spec, out_shape=SDS((2 * H, C), F32),
                          compiler_params=_params(1))(where, landed, mine)


def sum_slots(parts, name):
    n, H, C = parts.shape
    tr = _tile(H, 256, 16)

    def body(p_ref, o_ref):
        acc = p_ref[0].astype(F32)
        for k in range(1, n):
            acc = acc + p_ref[k].astype(F32)
        o_ref[...] = acc

    return pl.pallas_call(
        body, name=name, grid=(H // tr,), in_specs=[BS((n, tr, C), lambda i: (0, i, 0))],
        out_specs=BS((tr, C), lambda i: (i, 0)), out_shape=SDS((H, C), F32), compiler_params=_params(1))(parts)


def join_halves_with_sibling(arrs, name):
    n = len(arrs)
    halves = [a.shape[0] // 2 for a in arrs]

    def body(*refs):
        outs, (send_sems, recv_sems) = refs[n:2 * n], refs[2 * n:]
        x, y, c = _position()

        def half(t, core):
            return outs[t].at[pl.ds(core * halves[t], halves[t]), :]

        sends = [_remote(half(t, c), half(t, c), send_sems.at[t], recv_sems.at[t], (x, y, 1 - c)) for t in range(n)]
        for cp in sends:
            cp.start()
        for t in range(n):
            _remote(half(t, 1 - c), half(t, 1 - c), send_sems.at[t], recv_sems.at[t], (x, y, 1 - c)).wait_recv()
        for cp in sends:
            cp.wait_send()

    return pl.pallas_call(
        body, name=name, in_specs=[ANY] * n, out_specs=[ANY] * n, out_shape=[SDS(a.shape, a.dtype) for a in arrs],
        input_output_aliases={t: t for t in range(n)},
        scratch_shapes=[pltpu.SemaphoreType.DMA((n,)), pltpu.SemaphoreType.DMA((n,))])(*arrs)


def reduce_scatter_pair_sums(gs, tag):
    n = len(gs)
    others = swap_halves_with_sibling(gs, f"{tag}_swap")
    return [add_my_half(g, o, BF16 if t < n - 1 else F32, f"{tag}_pair_sum{t}")
            for t, (g, o) in enumerate(zip(gs, others))]


def reduce_scatter_finish(parts, landed, tag):
    sums = [sum_slots_into_my_half(l, p, f"{tag}_chip_sum{t}") for t, (l, p) in enumerate(zip(landed, parts))]
    return join_halves_with_sibling(sums, f"{tag}_join")


def gather_all_devices(pack, name):
    R, C = pack.shape

    def body(src, out, send_sems, recv_sems, local_sem):
        x, y, c = _position()
        me = 4 * x + 2 * y + c
        local = pltpu.make_async_copy(src, out.at[me], local_sem)
        local.start()
        peers = []
        for m in range(1, N_DEV):
            px = 1 - x if m & 4 else x
            py = 1 - y if m & 2 else y
            pc = 1 - c if m & 1 else c
            peers.append((px, py, pc))
        sends = [_remote(src, out.at[me], send_sems.at[j], recv_sems.at[j], peer) for j, peer in enumerate(peers)]
        for cp in sends:
            cp.start()
        for j, (px, py, pc) in enumerate(peers):
            k = 4 * px + 2 * py + pc
            _remote(src, out.at[k], send_sems.at[j], recv_sems.at[j], (px, py, pc)).wait_recv()
        for cp in sends:
            cp.wait_send()
        local.wait()

    return pl.pallas_call(
        body, name=name, in_specs=[ANY], out_specs=ANY, out_shape=SDS((N_DEV, R, C), pack.dtype),
        scratch_shapes=[pltpu.SemaphoreType.DMA((N_DEV - 1,)), pltpu.SemaphoreType.DMA((N_DEV - 1,)),
                        pltpu.SemaphoreType.DMA(())])(pack)


def adamw(w, g, m, v, name):
    shape = w.shape
    cols = shape[-1] if len(shape) > 1 else shape[0]
    rows = w.size // cols
    two = lambda t: t.reshape(rows, cols)
    tr = rows
    if rows * cols * 4 > (1 << 21):
        tr = _tile(rows, max(8, ((1 << 21) // (cols * 4)) // 8 * 8), 8)
    c1 = 1.0 / (1.0 - ADAM_B1 ** ADAM_STEP)
    c2 = 1.0 / (1.0 - ADAM_B2 ** ADAM_STEP)

    def body(w_ref, g_ref, m_ref, v_ref, d_ref, mo_ref, vo_ref):
        gv = g_ref[...]
        mn = ADAM_B1 * m_ref[...] + (1.0 - ADAM_B1) * gv
        vn = ADAM_B2 * v_ref[...] + (1.0 - ADAM_B2) * (gv * gv)
        d_ref[...] = -ADAM_LR * ((mn * c1) / (jnp.sqrt(vn * c2) + ADAM_EPS) + ADAM_WD * w_ref[...])
        mo_ref[...] = mn
        vo_ref[...] = vn

    blk = BS((tr, cols), lambda i: (i, 0))
    d, mo, vo = pl.pallas_call(
        body, name=name, grid=(rows // tr,), in_specs=[blk] * 4, out_specs=[blk] * 3,
        out_shape=[SDS((rows, cols), F32)] * 3, compiler_params=_params(1))(two(w), two(g), two(m), two(v))
    return d.reshape(shape), mo.reshape(shape), vo.reshape(shape)


def _rows_for(n, unit):
    rows = -(-n // PACK_COLS)
    return -(-rows // unit) * unit


ROWS_GROUP = ("w_out", "w_ple_gate", "w_down")
COLS_GROUP = ("w_br_fox", "w_br_conv", "w_br_sgu", "w_br_dil", "w_ple_proj")
SMALL_GROUP = ("shortconv_w", "ffn_conv_w")
SMALL_ROWS = 16


def group_shards(t, dtype):
    lead = t["w_in"].shape[:-2]
    small = jnp.concatenate([t[n].astype(F32).reshape(lead + (-1,)) for n in SMALL_GROUP], axis=-1)
    pad = jnp.zeros(lead + (SMALL_ROWS * PACK_COLS - small.shape[-1],), F32)
    small = jnp.concatenate([small, pad], axis=-1).reshape(lead + (SMALL_ROWS, PACK_COLS))
    return [t["w_in"].astype(dtype), t["w_up"].astype(dtype),
            jnp.concatenate([t[n].astype(dtype) for n in ROWS_GROUP], axis=-2),
            jnp.concatenate([t[n].astype(dtype) for n in COLS_GROUP], axis=-2), small]


def ungroup_shards(arrs, shard_shapes):
    w_in_s, w_up_s, rows, cols, small = arrs
    lead = w_in_s.shape[:-2]
    out = {"w_in": w_in_s, "w_up": w_up_s}
    for group, arr in ((ROWS_GROUP, rows), (COLS_GROUP, cols)):
        off = 0
        for n in group:
            r = shard_shapes[n][0]
            out[n] = arr[..., off:off + r, :]
            off += r
    flat = small.reshape(lead + (-1,))
    off = 0
    for n in SMALL_GROUP:
        size = shard_shapes[n][0] * shard_shapes[n][1]
        out[n] = flat[..., off:off + size].reshape(lead + shard_shapes[n])
        off += size
    return out


class ShardExchange:
    def __init__(self, weights, depth, shard_shapes):
        self.shard_shapes = shard_shapes
        self.packs = [group_shards({n: weights[n][layer] for n in SHARDED}, BF16) for layer in range(depth)]
        self.me = _chip_index()
        self.pending = None
        self.shard_grads = [None] * depth
        self.repl_grads = [None] * depth

    def _chips(self, layer, gathered):
        per_chip = [ungroup_shards([jnp.where(self.me == k, pk, g[k]) for g, pk in zip(gathered, self.packs[layer])],
                                   self.shard_shapes) for k in range(N_CHIPS)]
        return {n: [per_chip[k][n] for k in range(N_CHIPS)] for n in SHARDED}

    def first_weights(self):
        return self._chips(0, gather_chip_shards(self.packs[0], "gather_w0"))

    FWD_HOSTS = {"fox": (0, 3, 4), "proj": (1,), "ffn": (2,)}
    BWD_HOSTS = {"dq": (0, 3, 4), "dkv": (1, 2)}

    @staticmethod
    def _split(hosts, arrs, make):
        return {host: make([arrs[t] for t in idx]) for host, idx in hosts.items()}

    @staticmethod
    def _join(hosts, outs):
        arrs = [None] * sum(len(idx) for idx in hosts.values())
        for host, idx in hosts.items():
            for t, arr in zip(idx, outs[host]):
                arrs[t] = arr
        return arrs

    def weights_exchange(self, layer):
        return self._split(self.FWD_HOSTS, self.packs[layer], gather_exchange)

    def weights_arrived(self, layer, outs):
        return self._chips(layer, self._join(self.FWD_HOSTS, outs))

    def grads_exchange(self):
        if self.pending is None:
            return None
        return self._split(self.BWD_HOSTS, self.pending[1], slot_exchange)

    def grads_arrived(self, outs):
        if self.pending is not None:
            self._finish(self._join(self.BWD_HOSTS, outs))

    def grads_ready(self, layer, gr):
        self.repl_grads[layer] = {n: gr[n] for n in REPLICATED}
        slots = group_shards({n: gr[n] for n in SHARDED}, F32)
        self.pending = (layer, reduce_scatter_pair_sums(slots, f"rs{layer}"))

    def grads_flush(self):
        layer, parts = self.pending
        self._finish(exchange_slots_between_chips(parts, f"rs{layer}_ici"))

    def _finish(self, landed):
        layer, parts = self.pending
        self.shard_grads[layer] = ungroup_shards(reduce_scatter_finish(parts, landed, f"rs{layer}"),
                                                 self.shard_shapes)
        self.pending = None


REPL_SHAPES = {"norm_mix_g": (D_MODEL,), "fox_forget_b": (FOX_HEADS,), "sgu_norm_g": (SGU_WIDTH,),
               "sgu_w": (4, SGU_CHUNK, SGU_CHUNK), "sgu_b": (4, SGU_CHUNK), "norm_ffn_g": (D_MODEL,),
               "norm_ple_g": (D_MODEL,)}


def kernel(x, p, positions, norm_mix_g, w_in, fox_forget_b, shortconv_w, sgu_norm_g, sgu_w, sgu_b, w_br_fox, w_br_conv, w_br_sgu, w_br_dil, w_out, norm_ffn_g, w_up, ffn_conv_w, w_down, norm_ple_g, w_ple_gate, w_ple_proj, final_norm_g, loss_target, m_norm_mix_g, m_w_in, m_fox_forget_b, m_shortconv_w, m_sgu_norm_g, m_sgu_w, m_sgu_b, m_w_br_fox, m_w_br_conv, m_w_br_sgu, m_w_br_dil, m_w_out, m_norm_ffn_g, m_w_up, m_ffn_conv_w, m_w_down, m_norm_ple_g, m_w_ple_gate, m_w_ple_proj, m_final_norm_g, v_norm_mix_g, v_w_in, v_fox_forget_b, v_shortconv_w, v_sgu_norm_g, v_sgu_w, v_sgu_b, v_w_br_fox, v_w_br_conv, v_w_br_sgu, v_w_br_dil, v_w_out, v_norm_ffn_g, v_w_up, v_ffn_conv_w, v_w_down, v_norm_ple_g, v_w_ple_gate, v_w_ple_proj, v_final_norm_g):
    weights = dict(norm_mix_g=norm_mix_g, w_in=w_in, fox_forget_b=fox_forget_b, shortconv_w=shortconv_w,
                   sgu_norm_g=sgu_norm_g, sgu_w=sgu_w, sgu_b=sgu_b, w_br_fox=w_br_fox, w_br_conv=w_br_conv,
                   w_br_sgu=w_br_sgu, w_br_dil=w_br_dil, w_out=w_out, norm_ffn_g=norm_ffn_g, w_up=w_up,
                   ffn_conv_w=ffn_conv_w, w_down=w_down, norm_ple_g=norm_ple_g, w_ple_gate=w_ple_gate,
                   w_ple_proj=w_ple_proj, final_norm_g=final_norm_g)
    mom1 = dict(norm_mix_g=m_norm_mix_g, w_in=m_w_in, fox_forget_b=m_fox_forget_b, shortconv_w=m_shortconv_w,
                sgu_norm_g=m_sgu_norm_g, sgu_w=m_sgu_w, sgu_b=m_sgu_b, w_br_fox=m_w_br_fox, w_br_conv=m_w_br_conv,
                w_br_sgu=m_w_br_sgu, w_br_dil=m_w_br_dil, w_out=m_w_out, norm_ffn_g=m_norm_ffn_g, w_up=m_w_up,
                ffn_conv_w=m_ffn_conv_w, w_down=m_w_down, norm_ple_g=m_norm_ple_g, w_ple_gate=m_w_ple_gate,
                w_ple_proj=m_w_ple_proj, final_norm_g=m_final_norm_g)
    mom2 = dict(norm_mix_g=v_norm_mix_g, w_in=v_w_in, fox_forget_b=v_fox_forget_b, shortconv_w=v_shortconv_w,
                sgu_norm_g=v_sgu_norm_g, sgu_w=v_sgu_w, sgu_b=v_sgu_b, w_br_fox=v_w_br_fox, w_br_conv=v_w_br_conv,
                w_br_sgu=v_w_br_sgu, w_br_dil=v_w_br_dil, w_out=v_w_out, norm_ffn_g=v_norm_ffn_g, w_up=v_w_up,
                ffn_conv_w=v_ffn_conv_w, w_down=v_w_down, norm_ple_g=v_norm_ple_g, w_ple_gate=v_w_ple_gate,
                w_ple_proj=v_w_ple_proj, final_norm_g=v_final_norm_g)
    depth = w_in.shape[0]
    shard_shapes = {n: tuple(weights[n].shape[1:]) for n in SHARDED}

    exch = ShardExchange(weights, depth, shard_shapes)
    repl = {n: weights[n] for n in REPLICATED}
    loss_part, grad_x, d_final = local_step(x[0], p[:, 0], positions[0], repl, final_norm_g, loss_target[0], exch)
    loss = lax.psum(loss_part, ("x", "y", "c"))
    grads = exch.repl_grads
    grad_w = {n: jnp.stack([exch.shard_grads[layer][n] for layer in range(depth)], axis=0) for n in SHARDED}

    flat = jnp.concatenate([grads[layer][n].astype(F32).reshape(-1) for layer in range(depth) for n in REPLICATED]
                           + [d_final])
    Rr = _rows_for(flat.shape[0], 16)
    packed = jnp.concatenate([flat, jnp.zeros((Rr * PACK_COLS - flat.shape[0],), F32)]).reshape(Rr, PACK_COLS)
    total = sum_slots(gather_all_devices(packed, "gather_repl"), "sum_repl").reshape(-1)
    off = 0
    g_rep = {n: [] for n in REPLICATED}
    for layer in range(depth):
        for n in REPLICATED:
            size = 1
            for s in REPL_SHAPES[n]:
                size *= s
            g_rep[n].append(total[off:off + size].reshape(REPL_SHAPES[n]))
            off += size
    for n in REPLICATED:
        grad_w[n] = jnp.stack(g_rep[n], axis=0)
    grad_w["final_norm_g"] = total[off:off + D_MODEL]

    deltas, new_m, new_v = {}, {}, {}
    for n in WEIGHTS:
        deltas[n], new_m[n], new_v[n] = adamw(weights[n], grad_w[n], mom1[n], mom2[n], f"adamw_{n}")
    return (loss, grad_x[None], *[grad_w[n] for n in WEIGHTS], *[deltas[n] for n in WEIGHTS],
            *[new_m[n] for n in WEIGHTS], *[new_v[n] for n in WEIGHTS])
```

```python
import functools

import jax
import jax.numpy as jnp
from jax import lax
from jax.experimental import pallas as pl
from jax.experimental.pallas import tpu as pltpu

F32 = jnp.float32
BF16 = jnp.bfloat16
MESH = pl.DeviceIdType.MESH
BS = pl.BlockSpec
SDS = jax.ShapeDtypeStruct
ANY = pl.BlockSpec(memory_space=pl.ANY)

VMEM_LIMIT_BYTES = 52 * 1024 * 1024
LANES = 128

D_MODEL = 2048
HEAD_DIM = 64
EPS = 1e-6
NEG = -1e30
FOX_HEADS = 8
FOX_WIDTH = 512
CONV_WIDTH = 512
SGU_WIDTH = 512
SGU_CHUNK = 128
DIL_PATTERNS = ((128, 1), (512, 4), (2048, 16))
DIL_SPAN = 128
DIL_HEADS = 12
DIL_WIDTH = 768
DIL_OUT = 256
ROPE_THETA = 500000.0
ROPE_DIM = 16
N_BRANCH = 4
D_FF = 5632
PLE_DIM = 256
D_IN = 14600

OFF_G, OFF_A, OFF_B, OFF_C, OFF_D, OFF_F, W_AL = 0, 8192, 9728, 11264, 12288, 14592, 14848
SRC_A, SRC_F, SRC_B, SRC_C, SRC_DQ, SRC_G = 0, 1536, 1544, 3080, 4104, 6408

ADAM_LR, ADAM_B1, ADAM_B2, ADAM_EPS, ADAM_WD, ADAM_STEP = 0.001, 0.9, 0.999, 1e-08, 0.01, 10

PACK_COLS = 1024
N_CHIPS = 4
N_DEV = 8

SHARDED = ("w_in", "shortconv_w", "w_br_fox", "w_br_conv", "w_br_sgu", "w_br_dil", "w_out", "w_up",
           "ffn_conv_w", "w_down", "w_ple_gate", "w_ple_proj")
ROW_SHARDED = ("w_out", "w_down", "w_ple_gate")
REPLICATED = ("norm_mix_g", "fox_forget_b", "sgu_norm_g", "sgu_w", "sgu_b", "norm_ffn_g", "norm_ple_g")
WEIGHTS = ("norm_mix_g", "w_in", "fox_forget_b", "shortconv_w", "sgu_norm_g", "sgu_w", "sgu_b", "w_br_fox",
           "w_br_conv", "w_br_sgu", "w_br_dil", "w_out", "norm_ffn_g", "w_up", "ffn_conv_w", "w_down",
           "norm_ple_g", "w_ple_gate", "w_ple_proj", "final_norm_g")


def _params(n_grid):
    return pltpu.CompilerParams(dimension_semantics=("arbitrary",) * n_grid, vmem_limit_bytes=VMEM_LIMIT_BYTES)


def _tile(n, pref, unit=LANES):
    best = None
    t = unit
    while t <= min(n, pref):
        if n % t == 0:
            best = t
        t += unit
    return best if best is not None else n


def _sigmoid(z):
    return 1.0 / (1.0 + jnp.exp(-z))


MAX_RESIDENT_K = 2048


def matmul(a, b, mode, out_dtype, name, res=None, tm=1536, tn=1024, tk=1536, side=None):
    if mode == "tn":
        a, mode = a.astype(BF16).T, "nn"
    if mode == "nn":
        (M, K), (K2, N) = a.shape, b.shape
    else:
        (M, K), (N, K2) = a.shape, b.shape
    assert K == K2, (name, a.shape, b.shape)
    if K <= MAX_RESIDENT_K:
        tk = K
    tm, tn, tk = _tile(M, tm), _tile(N, tn), _tile(K, tk)
    nk = K // tk
    if mode == "nn":
        a_spec, b_spec = BS((tm, tk), lambda i, j, k: (i, k)), BS((tk, tn), lambda i, j, k: (k, j))
        dims = (((1,), (0,)), ((), ()))
    else:
        a_spec, b_spec = BS((tm, tk), lambda i, j, k: (i, k)), BS((tn, tk), lambda i, j, k: (j, k))
        dims = (((1,), (1,)), ((), ()))
    has_res = res is not None

    def body(*refs):
        if has_res:
            a_ref, b_ref, r_ref, o_ref, acc = refs
        else:
            a_ref, b_ref, o_ref, acc = refs
        k = pl.program_id(2)

        @pl.when(k == 0)
        def _():
            acc[...] = jnp.zeros_like(acc)

        acc[...] += lax.dot_general(a_ref[...].astype(BF16), b_ref[...].astype(BF16), dims,
                                    preferred_element_type=F32)

        @pl.when(k == nk - 1)
        def _():
            r = acc[...]
            if has_res:
                r = r + r_ref[...]
            o_ref[...] = r.astype(o_ref.dtype)

    in_specs = [a_spec, b_spec]
    args = [a, b]
    if has_res:
        in_specs.append(BS((tm, tn), lambda i, j, k: (i, j)))
        args.append(res)
    (out,), side_out = _call_with_side(
        body, side, name, (M // tm, N // tn, nk), in_specs, [BS((tm, tn), lambda i, j, k: (i, j))],
        [SDS((M, N), out_dtype)], args, scratch=[pltpu.VMEM((tm, tn), F32)])
    return out if side is None else (out, side_out)


def rmsnorm_fwd(x, g, name):
    S, Dm = x.shape
    tm = _tile(S, 256, 8)

    def body(x_ref, g_ref, y_ref):
        xf = x_ref[...]
        r = lax.rsqrt(jnp.mean(xf * xf, axis=-1, keepdims=True) + EPS)
        y_ref[...] = ((xf * r) * g_ref[...]).astype(y_ref.dtype)

    return pl.pallas_call(
        body, name=name, grid=(S // tm,),
        in_specs=[BS((tm, Dm), lambda i: (i, 0)), BS((1, Dm), lambda i: (0, 0))],
        out_specs=BS((tm, Dm), lambda i: (i, 0)), out_shape=SDS((S, Dm), BF16),
        compiler_params=_params(1))(x, g.reshape(1, Dm))


def rmsnorm_bwd(x, g, dy, dres, name):
    S, Dm = x.shape
    tm = _tile(S, 256, 8)

    def body(x_ref, g_ref, dy_ref, dres_ref, dx_ref, dg_ref):
        xf = x_ref[...]
        r = lax.rsqrt(jnp.mean(xf * xf, axis=-1, keepdims=True) + EPS)
        xh = xf * r
        dy = dy_ref[...].astype(F32)
        dxh = dy * g_ref[...]
        dx_ref[...] = r * (dxh - xh * jnp.mean(dxh * xh, axis=-1, keepdims=True)) + dres_ref[...]

        @pl.when(pl.program_id(0) == 0)
        def _():
            dg_ref[...] = jnp.zeros_like(dg_ref)

        dg_ref[...] += jnp.sum(dy * xh, axis=0, keepdims=True)

    row = BS((tm, Dm), lambda i: (i, 0))
    vec = BS((1, Dm), lambda i: (0, 0))
    return pl.pallas_call(
        body, name=name, grid=(S // tm,), in_specs=[row, vec, row, row], out_specs=[row, vec],
        out_shape=[SDS((S, Dm), F32), SDS((1, Dm), F32)], compiler_params=_params(1))(x, g.reshape(1, Dm), dy, dres)


def final_loss(x, g, target, name):
    S, Dm = x.shape
    tm = _tile(S, 256, 8)

    def body(x_ref, g_ref, t_ref, loss_ref, dx_ref, dg_ref):
        xf = x_ref[...]
        r = lax.rsqrt(jnp.mean(xf * xf, axis=-1, keepdims=True) + EPS)
        xh = xf * r
        gv = g_ref[...]
        err = xh * gv - t_ref[...]
        dy = err * (1.0 / Dm)
        dxh = dy * gv
        dx_ref[...] = r * (dxh - xh * jnp.mean(dxh * xh, axis=-1, keepdims=True))

        @pl.when(pl.program_id(0) == 0)
        def _():
            dg_ref[...] = jnp.zeros_like(dg_ref)
            loss_ref[...] = jnp.zeros_like(loss_ref)

        dg_ref[...] += jnp.sum(dy * xh, axis=0, keepdims=True)
        part = 0.5 * jnp.sum(jnp.mean(err * err, axis=-1, keepdims=True), axis=0, keepdims=True)
        loss_ref[...] += jnp.broadcast_to(part, loss_ref.shape)

    row = BS((tm, Dm), lambda i: (i, 0))
    vec = BS((1, Dm), lambda i: (0, 0))
    return pl.pallas_call(
        body, name=name, grid=(S // tm,), in_specs=[row, vec, row],
        out_specs=[BS((1, LANES), lambda i: (0, 0)), row, vec],
        out_shape=[SDS((1, LANES), F32), SDS((S, Dm), F32), SDS((1, Dm), F32)],
        compiler_params=_params(1))(x, g.reshape(1, Dm), target)


def _dot_f32(a, b):
    return jnp.dot(a, b, preferred_element_type=F32, precision=lax.Precision.HIGHEST)


def _dot(a, b):
    return jnp.dot(a, b, preferred_element_type=F32)


def _dot_nt(a, b):
    return lax.dot_general(a, b, (((1,), (1,)), ((), ())), preferred_element_type=F32)


def fox_prep_fwd(af_t, bias, name):
    H, S = af_t.shape
    nc = S // LANES

    def body(a_ref, b_ref, f_ref):
        z = a_ref[...] + b_ref[...]
        logf = jnp.minimum(z, 0.0) - jnp.log(1.0 + jnp.exp(-jnp.abs(z)))
        row = lax.broadcasted_iota(jnp.int32, (LANES, LANES), 0)
        col = lax.broadcasted_iota(jnp.int32, (LANES, LANES), 1)
        upper = (row <= col).astype(F32)
        carry = jnp.zeros((H, 1), F32)
        for c in range(nc):
            chunk = logf[:, c * LANES:(c + 1) * LANES]
            f_ref[:, c * LANES:(c + 1) * LANES] = _dot_f32(chunk, upper) + carry
            carry = carry + jnp.sum(chunk, axis=1, keepdims=True)

    full = BS((H, S), lambda i: (0, 0))
    return pl.pallas_call(
        body, name=name, grid=(1,), in_specs=[full, BS((H, 1), lambda i: (0, 0))], out_specs=full,
        out_shape=SDS((H, S), F32), compiler_params=_params(1))(af_t, bias)


def fox_prep_bwd(af_t, bias, d_fq, d_fk, name):
    H, S = af_t.shape
    nc = S // LANES

    def body(a_ref, b_ref, dfq_ref, dfk_ref, da_ref, db_ref):
        row = lax.broadcasted_iota(jnp.int32, (LANES, LANES), 0)
        col = lax.broadcasted_iota(jnp.int32, (LANES, LANES), 1)
        lower = (row >= col).astype(F32)
        carry = jnp.zeros((H, 1), F32)
        dbias = jnp.zeros((H, 1), F32)
        for c in range(nc - 1, -1, -1):
            sl = slice(c * LANES, (c + 1) * LANES)
            chunk = dfq_ref[:, sl] + dfk_ref[:, sl]
            dlogf = _dot_f32(chunk, lower) + carry
            carry = carry + jnp.sum(chunk, axis=1, keepdims=True)
            z = a_ref[:, sl] + b_ref[...]
            da = dlogf * _sigmoid(-z)
            da_ref[:, sl] = da
            dbias = dbias + jnp.sum(da, axis=1, keepdims=True)
        db_ref[...] = dbias

    full = BS((H, S), lambda i: (0, 0))
    vec = BS((H, 1), lambda i: (0, 0))
    return pl.pallas_call(
        body, name=name, grid=(1,), in_specs=[full, vec, full, full], out_specs=[full, vec],
        out_shape=[SDS((H, S), F32), SDS((H, 1), F32)], compiler_params=_params(1))(af_t, bias, d_fq, d_fk)


FOX_T = 256
FOX_HP = 2


def _causal_tile(T):
    return lax.broadcasted_iota(jnp.int32, (T, T), 1) <= lax.broadcasted_iota(jnp.int32, (T, T), 0)


class SideExchange:
    def __init__(self, operands, out_shapes, n_sems, start, finish):
        self.operands, self.out_shapes, self.n_sems, self.start, self.finish = operands, out_shapes, n_sems, start, finish


def _call_with_side(body, side, name, grid, in_specs, out_specs, out_shape, args, scratch=()):
    scratch = list(scratch)
    if side is None:
        return pl.pallas_call(body, name=name, grid=grid, in_specs=in_specs, out_specs=out_specs,
                              out_shape=out_shape, scratch_shapes=scratch,
                              compiler_params=_params(len(grid)))(*args), []
    n_in, n_out = len(in_specs), len(out_specs)
    s_in, s_out = len(side.operands), len(side.out_shapes)

    def wrapped(*refs):
        main_in, side_in = refs[:n_in], refs[n_in:n_in + s_in]
        main_out = refs[n_in + s_in:n_in + s_in + n_out]
        side_out = refs[n_in + s_in + n_out:n_in + s_in + n_out + s_out]
        main_scratch = refs[n_in + s_in + n_out + s_out:-2]
        send_sems, recv_sems = refs[-2:]
        first = pl.program_id(0) == 0
        last = pl.program_id(0) == grid[0] - 1
        for axis in range(1, len(grid)):
            first = jnp.logical_and(first, pl.program_id(axis) == 0)
            last = jnp.logical_and(last, pl.program_id(axis) == grid[axis] - 1)

        @pl.when(first)
        def _():
            side.start(side_in, side_out, send_sems, recv_sems)

        body(*main_in, *main_out, *main_scratch)

        @pl.when(last)
        def _():
            side.finish(side_in, side_out, send_sems, recv_sems)

    outs = pl.pallas_call(
        wrapped, name=name, grid=grid, in_specs=list(in_specs) + [ANY] * s_in,
        out_specs=list(out_specs) + [ANY] * s_out, out_shape=list(out_shape) + list(side.out_shapes),
        scratch_shapes=scratch + [pltpu.SemaphoreType.DMA((side.n_sems,)), pltpu.SemaphoreType.DMA((side.n_sems,))],
        compiler_params=_params(len(grid)))(*args, *side.operands)
    return outs[:n_out], outs[n_out:]


def _fox_specs(H, S, Dh, T):
    nq = S // T
    blk = BS((FOX_HP, T, Dh), lambda h, i: (h, i, 0))
    full = BS((FOX_HP, S, Dh), lambda h, i: (h, 0, 0))
    colb = BS((FOX_HP, T, 1), lambda h, i: (h, i, 0))
    rowf = BS((FOX_HP, nq, 1, T), lambda h, i: (h, 0, 0, 0))
    return blk, full, colb, rowf, (H // FOX_HP, nq)


def fox_attn_fwd(q, k, v, f_col, f_row, name, side=None):
    H, S, Dh = q.shape
    T = min(FOX_T, S)
    scale = Dh ** -0.5

    def body(q_ref, k_ref, v_ref, fc_ref, fr_ref, o_ref, lse_ref):
        qi = pl.program_id(1)
        qs = [q_ref[h] for h in range(FOX_HP)]
        fqs = [fc_ref[h] for h in range(FOX_HP)]

        def step(j, carry, diagonal):
            off = pl.multiple_of(j * T, T)
            out = []
            for h in range(FOX_HP):
                m, l, acc = carry[h]
                kv = k_ref[h, pl.ds(off, T), :]
                vv = v_ref[h, pl.ds(off, T), :]
                s = _dot_nt(qs[h], kv) * scale + (fqs[h] - fr_ref[h, j])
                if diagonal:
                    s = jnp.where(_causal_tile(T), s, NEG)
                m_new = jnp.maximum(m, jnp.max(s, axis=-1, keepdims=True))
                p = jnp.exp(s - m_new)
                alpha = jnp.exp(m - m_new)
                l = alpha * l + jnp.sum(p, axis=-1, keepdims=True)
                acc = alpha * acc + _dot(p.astype(BF16), vv)
                out.append((m_new, l, acc))
            return tuple(out)

        init = tuple((jnp.full((T, 1), NEG, F32), jnp.zeros((T, 1), F32), jnp.zeros((T, Dh), F32))
                     for _ in range(FOX_HP))
        carry = lax.fori_loop(0, qi, functools.partial(step, diagonal=False), init)
        carry = step(qi, carry, True)
        for h in range(FOX_HP):
            m, l, acc = carry[h]
            o_ref[h] = (acc / l).astype(o_ref.dtype)
            lse_ref[h] = m + jnp.log(l)

    blk, full, colb, rowf, grid = _fox_specs(H, S, Dh, T)
    return _call_with_side(body, side, name, grid, [blk, full, full, colb, rowf], [blk, colb],
                           [SDS((H, S, Dh), BF16), SDS((H, S, 1), F32)], (q, k, v, f_col, f_row))


def fox_attn_bwd_dq(q, k, v, f_col, f_row, o, lse, do, name, side=None):
    H, S, Dh = q.shape
    T = min(FOX_T, S)
    scale = Dh ** -0.5

    def body(q_ref, k_ref, v_ref, fc_ref, fr_ref, o_ref, lse_ref, do_ref, dq_ref, dl_ref, df_ref):
        qi = pl.program_id(1)
        qs = [q_ref[h] for h in range(FOX_HP)]
        fqs = [fc_ref[h] for h in range(FOX_HP)]
        lses = [lse_ref[h] for h in range(FOX_HP)]
        dos = [do_ref[h] for h in range(FOX_HP)]
        deltas = [jnp.sum(dos[h].astype(F32) * o_ref[h].astype(F32), axis=-1, keepdims=True) for h in range(FOX_HP)]
        for h in range(FOX_HP):
            dl_ref[h] = deltas[h]

        def step(j, carry, diagonal):
            off = pl.multiple_of(j * T, T)
            out = []
            for h in range(FOX_HP):
                dq, dfq = carry[h]
                kv = k_ref[h, pl.ds(off, T), :]
                vv = v_ref[h, pl.ds(off, T), :]
                s = _dot_nt(qs[h], kv) * scale + (fqs[h] - fr_ref[h, j])
                if diagonal:
                    s = jnp.where(_causal_tile(T), s, NEG)
                p = jnp.exp(s - lses[h])
                ds = p * (_dot_nt(dos[h], vv) - deltas[h])
                out.append((dq + _dot(ds.astype(BF16), kv), dfq + jnp.sum(ds, axis=-1, keepdims=True)))
            return tuple(out)

        init = tuple((jnp.zeros((T, Dh), F32), jnp.zeros((T, 1), F32)) for _ in range(FOX_HP))
        carry = lax.fori_loop(0, qi, functools.partial(step, diagonal=False), init)
        carry = step(qi, carry, True)
        for h in range(FOX_HP):
            dq_ref[h] = carry[h][0] * scale
            df_ref[h] = carry[h][1]

    blk, full, colb, rowf, grid = _fox_specs(H, S, Dh, T)
    return _call_with_side(body, side, name, grid, [blk, full, full, colb, rowf, blk, colb, blk], [blk, colb, colb],
                           [SDS((H, S, Dh), F32), SDS((H, S, 1), F32), SDS((H, S, 1), F32)],
                           (q, k, v, f_col, f_row, o, lse, do))


def fox_attn_bwd_dkv(q, k, v, f_col, f_row, lse_row, delta_row, do, name, side=None):
    H, S, Dh = q.shape
    T = min(FOX_T, S)
    nq = S // T
    scale = Dh ** -0.5

    def body(q_ref, k_ref, v_ref, fc_ref, fr_ref, lse_ref, dl_ref, do_ref, dk_ref, dv_ref, df_ref):
        kj = pl.program_id(1)
        ks = [k_ref[h] for h in range(FOX_HP)]
        vs = [v_ref[h] for h in range(FOX_HP)]
        fks = [fc_ref[h] for h in range(FOX_HP)]

        def step(i, carry, diagonal):
            off = pl.multiple_of(i * T, T)
            out = []
            for h in range(FOX_HP):
                dk, dv, dfk = carry[h]
                qv = q_ref[h, pl.ds(off, T), :]
                dov = do_ref[h, pl.ds(off, T), :]
                st = _dot_nt(ks[h], qv) * scale + (fr_ref[h, i] - fks[h])
                if diagonal:
                    st = jnp.where(lax.broadcasted_iota(jnp.int32, (T, T), 0)
                                   <= lax.broadcasted_iota(jnp.int32, (T, T), 1), st, NEG)
                pt = jnp.exp(st - lse_ref[h, i])
                dv = dv + _dot(pt.astype(BF16), dov)
                dst = pt * (_dot_nt(vs[h], dov) - dl_ref[h, i])
                dk = dk + _dot(dst.astype(BF16), qv)
                out.append((dk, dv, dfk + jnp.sum(dst, axis=-1, keepdims=True)))
            return tuple(out)

        init = tuple((jnp.zeros((T, Dh), F32), jnp.zeros((T, Dh), F32), jnp.zeros((T, 1), F32))
                     for _ in range(FOX_HP))
        carry = step(kj, init, True)
        carry = lax.fori_loop(kj + 1, nq, functools.partial(step, diagonal=False), carry)
        for h in range(FOX_HP):
            dk_ref[h] = carry[h][0] * scale
            dv_ref[h] = carry[h][1]
            df_ref[h] = -carry[h][2]

    blk, full, colb, rowf, grid = _fox_specs(H, S, Dh, T)
    return _call_with_side(body, side, name, grid, [full, blk, blk, colb, rowf, rowf, rowf, full], [blk, blk, colb],
                           [SDS((H, S, Dh), F32), SDS((H, S, Dh), F32), SDS((H, S, 1), F32)],
                           (q, k, v, f_col, f_row, lse_row, delta_row, do))


HALO = 8
CONV_TM = 512


def _conv_ext(e, w_ref):
    return w_ref[0:1, :] * pltpu.roll(e, 2, 0) + w_ref[1:2, :] * pltpu.roll(e, 1, 0) + w_ref[2:3, :] * e


def _conv_t_ext(d, w_ref):
    n = d.shape[0]
    return w_ref[2:3, :] * d + w_ref[1:2, :] * pltpu.roll(d, n - 1, 0) + w_ref[0:1, :] * pltpu.roll(d, n - 2, 0)


def _time_specs(S, tm, width, col_block):
    per = tm // HALO
    last = S // HALO - 1
    cur = BS((tm, width), lambda j, i: (i, col_block(j)))
    prev = BS((HALO, width), lambda j, i: (jnp.maximum(i * per - 1, 0), col_block(j)))
    nxt = BS((HALO, width), lambda j, i: (jnp.minimum((i + 1) * per, last), col_block(j)))
    return cur, prev, nxt


def shortconv_fwd(proj, w, name):
    S = proj.shape[0]
    tm = _tile(S, CONV_TM, 8)
    nb = CONV_WIDTH // LANES
    b0 = OFF_B // LANES

    def body(xb_ref, xbp_ref, gb_ref, gc_ref, gcp_ref, w_ref, o_ref):
        keep = (pl.program_id(1) > 0).astype(F32)
        e = jnp.concatenate([gcp_ref[...] * xbp_ref[...] * keep, gc_ref[...] * xb_ref[...]], axis=0)
        o_ref[...] = (gb_ref[...] * _conv_ext(e, w_ref)[HALO:, :]).astype(o_ref.dtype)

    xb, xbp, _ = _time_specs(S, tm, LANES, lambda j: b0 + j)
    gb, _, _ = _time_specs(S, tm, LANES, lambda j: b0 + nb + j)
    gc, gcp, _ = _time_specs(S, tm, LANES, lambda j: b0 + 2 * nb + j)
    return pl.pallas_call(
        body, name=name, grid=(nb, S // tm),
        in_specs=[xb, xbp, gb, gc, gcp, BS((3, LANES), lambda j, i: (0, j))],
        out_specs=BS((tm, LANES), lambda j, i: (i, j)), out_shape=SDS((S, CONV_WIDTH), BF16),
        compiler_params=_params(2))(proj, proj, proj, proj, proj, w)


def shortconv_bwd(proj, w, do_b, name):
    S = proj.shape[0]
    tm = _tile(S, CONV_TM, 8)
    nt = S // tm
    nb = CONV_WIDTH // LANES
    b0 = OFF_B // LANES

    def body(xb_ref, xbp_ref, gb_ref, gbn_ref, gc_ref, gcp_ref, do_ref, don_ref, w_ref,
             dxb_ref, dgb_ref, dgc_ref, dw_ref):
        i = pl.program_id(1)
        keep_prev = (i > 0).astype(F32)
        keep_next = (i < nt - 1).astype(F32)
        xb, gb, gc = xb_ref[...], gb_ref[...], gc_ref[...]
        do = do_ref[...].astype(F32)
        u = gc * xb
        e = jnp.concatenate([gcp_ref[...] * xbp_ref[...] * keep_prev, u], axis=0)
        dgb_ref[...] = (do * _conv_ext(e, w_ref)[HALO:, :]).astype(dgb_ref.dtype)
        dcv = do * gb
        d_ext = jnp.concatenate([dcv, don_ref[...].astype(F32) * gbn_ref[...] * keep_next], axis=0)
        du = _conv_t_ext(d_ext, w_ref)[:tm, :]
        dgc_ref[...] = (du * xb).astype(dgc_ref.dtype)
        dxb_ref[...] = (du * gc).astype(dxb_ref.dtype)

        @pl.when(i == 0)
        def _():
            dw_ref[...] = jnp.zeros_like(dw_ref)

        dw_ref[0:1, :] += jnp.sum(dcv * pltpu.roll(e, 2, 0)[HALO:, :], axis=0, keepdims=True)
        dw_ref[1:2, :] += jnp.sum(dcv * pltpu.roll(e, 1, 0)[HALO:, :], axis=0, keepdims=True)
        dw_ref[2:3, :] += jnp.sum(dcv * u, axis=0, keepdims=True)

    xb, xbp, _ = _time_specs(S, tm, LANES, lambda j: b0 + j)
    gb, _, gbn = _time_specs(S, tm, LANES, lambda j: b0 + nb + j)
    gc, gcp, _ = _time_specs(S, tm, LANES, lambda j: b0 + 2 * nb + j)
    do, _, don = _time_specs(S, tm, LANES, lambda j: j)
    out = BS((tm, LANES), lambda j, i: (i, j))
    wspec = BS((3, LANES), lambda j, i: (0, j))
    return pl.pallas_call(
        body, name=name, grid=(nb, nt), in_specs=[xb, xbp, gb, gbn, gc, gcp, do, don, wspec],
        out_specs=[out, out, out, wspec],
        out_shape=[SDS((S, CONV_WIDTH), BF16)] * 3 + [SDS((3, CONV_WIDTH), F32)],
        compiler_params=_params(2))(proj, proj, proj, proj, proj, proj, do_b, do_b, w)


_GELU_C = 0.7978845608028654


def _gelu(x):
    return 0.5 * x * (1.0 + jnp.tanh(_GELU_C * (x + 0.044715 * x * x * x)))


def _gelu_grad(x):
    t = jnp.tanh(_GELU_C * (x + 0.044715 * x * x * x))
    return 0.5 * (1.0 + t) + 0.5 * x * (1.0 - t * t) * _GELU_C * (1.0 + 3.0 * 0.044715 * x * x)


def _tril_masks():
    row = lax.broadcasted_iota(jnp.int32, (SGU_CHUNK, SGU_CHUNK), 0)
    col = lax.broadcasted_iota(jnp.int32, (SGU_CHUNK, SGU_CHUNK), 1)
    return row >= col, row <= col


def _lane_column(mat, g):
    lane = lax.broadcasted_iota(jnp.int32, mat.shape, 1)
    return jnp.sum(jnp.where(lane == g, mat, 0.0), axis=-1, keepdims=True)


def sgu_fwd(proj, norm_g, w_s, b_t, name):
    S = proj.shape[0]
    T = SGU_CHUNK
    G = SGU_WIDTH // T
    c0 = OFF_C // (2 * SGU_WIDTH)

    def body(c_ref, g_ref, w_ref, b_ref, o_ref):
        u = _gelu(c_ref[:, 0:SGU_WIDTH])
        v = _gelu(c_ref[:, SGU_WIDTH:2 * SGU_WIDTH])
        r = lax.rsqrt(jnp.mean(v * v, axis=-1, keepdims=True) + EPS)
        vn = ((v * r) * g_ref[...]).astype(BF16)
        mask, _ = _tril_masks()
        bias = b_ref[...]
        for g in range(G):
            sl = slice(g * T, (g + 1) * T)
            wt = jnp.where(mask, w_ref[g], 0.0).astype(BF16)
            mixed = _dot(wt, vn[:, sl]) + _lane_column(bias, g)
            o_ref[:, sl] = (u[:, sl] * mixed).astype(o_ref.dtype)

    return pl.pallas_call(
        body, name=name, grid=(S // T,),
        in_specs=[BS((T, 2 * SGU_WIDTH), lambda i: (i, c0)), BS((1, SGU_WIDTH), lambda i: (0, 0)),
                  BS((G, T, T), lambda i: (0, 0, 0)), BS((T, T), lambda i: (0, 0))],
        out_specs=BS((T, SGU_WIDTH), lambda i: (i, 0)), out_shape=SDS((S, SGU_WIDTH), BF16),
        compiler_params=_params(1))(proj, norm_g, w_s, b_t)


def sgu_bwd(proj, norm_g, w_s, w_st, b_t, do_c, name):
    S = proj.shape[0]
    T = SGU_CHUNK
    G = SGU_WIDTH // T
    c0 = OFF_C // (2 * SGU_WIDTH)

    def body(c_ref, g_ref, w_ref, wt_ref, b_ref, do_ref, dc_ref, dg_ref, dw_ref, db_ref):
        cu = c_ref[:, 0:SGU_WIDTH]
        cv = c_ref[:, SGU_WIDTH:2 * SGU_WIDTH]
        u = _gelu(cu)
        v = _gelu(cv)
        r = lax.rsqrt(jnp.mean(v * v, axis=-1, keepdims=True) + EPS)
        xh = v * r
        gv = g_ref[...]
        vn = (xh * gv).astype(BF16)
        do = do_ref[...].astype(F32)
        mask, mask_t = _tril_masks()
        bias = b_ref[...]
        lane = lax.broadcasted_iota(jnp.int32, (T, T), 1)

        @pl.when(pl.program_id(0) == 0)
        def _():
            dg_ref[...] = jnp.zeros_like(dg_ref)
            dw_ref[...] = jnp.zeros_like(dw_ref)
            db_ref[...] = jnp.zeros_like(db_ref)

        dvn_parts = []
        du_parts = []
        dbias = jnp.zeros((T, T), F32)
        for g in range(G):
            sl = slice(g * T, (g + 1) * T)
            wt = jnp.where(mask, w_ref[g], 0.0).astype(BF16)
            mixed = _dot(wt, vn[:, sl]) + _lane_column(bias, g)
            du_parts.append(do[:, sl] * mixed)
            dmix = do[:, sl] * u[:, sl]
            dmix_b = dmix.astype(BF16)
            dw_ref[g] += jnp.where(mask, _dot_nt(dmix_b, vn[:, sl]), 0.0)
            dbias = dbias + jnp.where(lane == g, jnp.sum(dmix, axis=-1, keepdims=True), 0.0)
            wtt = jnp.where(mask_t, wt_ref[g], 0.0).astype(BF16)
            dvn_parts.append(_dot(wtt, dmix_b))
        db_ref[...] += dbias
        dvn = jnp.concatenate(dvn_parts, axis=-1)
        du = jnp.concatenate(du_parts, axis=-1)
        dg_ref[...] += jnp.sum(dvn * xh, axis=0, keepdims=True)
        dxh = dvn * gv
        dv = r * (dxh - xh * jnp.mean(dxh * xh, axis=-1, keepdims=True))
        dc_ref[:, 0:SGU_WIDTH] = (du * _gelu_grad(cu)).astype(dc_ref.dtype)
        dc_ref[:, SGU_WIDTH:2 * SGU_WIDTH] = (dv * _gelu_grad(cv)).astype(dc_ref.dtype)

    wspec = BS((G, T, T), lambda i: (0, 0, 0))
    gspec = BS((1, SGU_WIDTH), lambda i: (0, 0))
    return pl.pallas_call(
        body, name=name, grid=(S // T,),
        in_specs=[BS((T, 2 * SGU_WIDTH), lambda i: (i, c0)), gspec, wspec, wspec, BS((T, T), lambda i: (0, 0)),
                  BS((T, SGU_WIDTH), lambda i: (i, 0))],
        out_specs=[BS((T, 2 * SGU_WIDTH), lambda i: (i, 0)), gspec, wspec, BS((T, T), lambda i: (0, 0))],
        out_shape=[SDS((S, 2 * SGU_WIDTH), BF16), SDS((1, SGU_WIDTH), F32), SDS((G, T, T), F32), SDS((T, T), F32)],
        compiler_params=_params(1))(proj, norm_g, w_s, w_st, b_t, do_c)


def rope_tables(positions):
    half = ROPE_DIM // 2
    inv = ROPE_THETA ** (-jnp.arange(half, dtype=F32) * (2.0 / ROPE_DIM))
    ang = positions.astype(F32)[:, None] * inv
    cos, sin = jnp.cos(ang), jnp.sin(ang)
    S = positions.shape[0]
    ones = jnp.ones((S, HEAD_DIM - ROPE_DIM), F32)
    zeros = jnp.zeros((S, HEAD_DIM - ROPE_DIM), F32)
    zh = jnp.zeros((S, half), F32)
    c = jnp.concatenate([cos, cos, ones], axis=-1)
    sa = jnp.concatenate([zh, sin, zeros], axis=-1)
    sb = jnp.concatenate([-sin, zh, zeros], axis=-1)
    tile2 = lambda t: jnp.concatenate([t, t], axis=-1)
    return tile2(c), tile2(sa), tile2(sb)


def rope_apply(x, col0, ncols, cos, sa, sb, out_dtype, name):
    S = x.shape[0]
    tm = _tile(S, 512, 8)
    half = ROPE_DIM // 2
    b0 = col0 // LANES

    def body(x_ref, c_ref, sa_ref, sb_ref, o_ref):
        xv = x_ref[...].astype(F32)
        o_ref[...] = (xv * c_ref[...] + pltpu.roll(xv, half, 1) * sa_ref[...]
                      + pltpu.roll(xv, LANES - half, 1) * sb_ref[...]).astype(o_ref.dtype)

    tab = BS((tm, LANES), lambda i, j: (i, 0))
    return pl.pallas_call(
        body, name=name, grid=(S // tm, ncols // LANES),
        in_specs=[BS((tm, LANES), lambda i, j: (i, b0 + j)), tab, tab, tab],
        out_specs=BS((tm, LANES), lambda i, j: (i, j)), out_shape=SDS((S, ncols), out_dtype),
        compiler_params=_params(2))(x, cos, sa, sb)


def _dil_masks(first):
    qi = lax.broadcasted_iota(jnp.int32, (DIL_SPAN, DIL_SPAN), 0)
    ki = lax.broadcasted_iota(jnp.int32, (DIL_SPAN, DIL_SPAN), 1)
    return ki >= qi + first.astype(jnp.int32) * DIL_SPAN, ki <= qi


def dil_attn_fwd(q, k, v, nb, name):
    H, S, Dh = q.shape
    T = DIL_SPAN
    nblk = S // T
    scale = Dh ** -0.5

    def body(q_ref, kp_ref, kc_ref, vp_ref, vc_ref, o_ref, lse_ref):
        mp, mc = _dil_masks(pl.program_id(0) % nb == 0)
        for h in range(H):
            qv = q_ref[h]
            sp = jnp.where(mp, _dot_nt(qv, kp_ref[h]) * scale, NEG)
            sc = jnp.where(mc, _dot_nt(qv, kc_ref[h]) * scale, NEG)
            m = jnp.maximum(jnp.max(sp, axis=-1, keepdims=True), jnp.max(sc, axis=-1, keepdims=True))
            ep = jnp.exp(sp - m)
            ec = jnp.exp(sc - m)
            l = jnp.sum(ep, axis=-1, keepdims=True) + jnp.sum(ec, axis=-1, keepdims=True)
            o_ref[h] = (_dot(ep.astype(BF16), vp_ref[h]) + _dot(ec.astype(BF16), vc_ref[h])) / l
            lse_ref[h] = m + jnp.log(l)

    cur = BS((H, T, Dh), lambda b: (0, b, 0))
    prev = BS((H, T, Dh), lambda b: (0, jnp.maximum(b - 1, 0), 0))
    colb = BS((H, T, 1), lambda b: (0, b, 0))
    return pl.pallas_call(
        body, name=name, grid=(nblk,), in_specs=[cur, prev, cur, prev, cur], out_specs=[cur, colb],
        out_shape=[SDS((H, S, Dh), F32), SDS((H, S, 1), F32)], compiler_params=_params(1))(q, k, k, v, v)


def dil_attn_bwd_dq(q, k, v, o, lse, do, dlse, nb, name):
    H, S, Dh = q.shape
    T = DIL_SPAN
    nblk = S // T
    scale = Dh ** -0.5

    def body(q_ref, kp_ref, kc_ref, vp_ref, vc_ref, o_ref, lse_ref, do_ref, dlse_ref, dq_ref, dl_ref):
        mp, mc = _dil_masks(pl.program_id(0) % nb == 0)
        for h in range(H):
            qv = q_ref[h]
            kp, kc = kp_ref[h], kc_ref[h]
            dov = do_ref[h]
            delta = jnp.sum(dov * o_ref[h], axis=-1, keepdims=True) - dlse_ref[h]
            dl_ref[h] = delta
            dob = dov.astype(BF16)
            lse_v = lse_ref[h]
            pp = jnp.exp(jnp.where(mp, _dot_nt(qv, kp) * scale, NEG) - lse_v)
            pc = jnp.exp(jnp.where(mc, _dot_nt(qv, kc) * scale, NEG) - lse_v)
            dsp = pp * (_dot_nt(dob, vp_ref[h]) - delta)
            dsc = pc * (_dot_nt(dob, vc_ref[h]) - delta)
            dq_ref[h] = (_dot(dsp.astype(BF16), kp) + _dot(dsc.astype(BF16), kc)) * scale

    cur = BS((H, T, Dh), lambda b: (0, b, 0))
    prev = BS((H, T, Dh), lambda b: (0, jnp.maximum(b - 1, 0), 0))
    colb = BS((H, T, 1), lambda b: (0, b, 0))
    return pl.pallas_call(
        body, name=name, grid=(nblk,), in_specs=[cur, prev, cur, prev, cur, cur, colb, cur, colb],
        out_specs=[cur, colb], out_shape=[SDS((H, S, Dh), F32), SDS((H, S, 1), F32)],
        compiler_params=_params(1))(q, k, k, v, v, o, lse, do, dlse)


def dil_attn_bwd_dkv(q, k, v, lse_row, delta_row, do, nb, name):
    H, S, Dh = q.shape
    T = DIL_SPAN
    nblk = S // T
    scale = Dh ** -0.5

    def body(k_ref, v_ref, qc_ref, qn_ref, doc_ref, don_ref, lc_ref, ln_ref, dc_ref, dn_ref, dk_ref, dv_ref):
        no_next = ((pl.program_id(0) + 1) % nb == 0).astype(jnp.int32)
        si = lax.broadcasted_iota(jnp.int32, (T, T), 0)
        ti = lax.broadcasted_iota(jnp.int32, (T, T), 1)
        m_cur = si <= ti
        m_next = si >= ti + no_next * T
        for h in range(H):
            kv, vv = k_ref[h], v_ref[h]
            qc, qn = qc_ref[h], qn_ref[h]
            doc, don = doc_ref[h].astype(BF16), don_ref[h].astype(BF16)
            pt = jnp.exp(jnp.where(m_cur, _dot_nt(kv, qc) * scale, NEG) - lc_ref[h])
            ptn = jnp.exp(jnp.where(m_next, _dot_nt(kv, qn) * scale, NEG) - ln_ref[h])
            dv_ref[h] = _dot(pt.astype(BF16), doc) + _dot(ptn.astype(BF16), don)
            dst = pt * (_dot_nt(vv, doc) - dc_ref[h])
            dstn = ptn * (_dot_nt(vv, don) - dn_ref[h])
            dk_ref[h] = (_dot(dst.astype(BF16), qc) + _dot(dstn.astype(BF16), qn)) * scale

    cur = BS((H, T, Dh), lambda b: (0, b, 0))
    nxt = BS((H, T, Dh), lambda b: (0, jnp.minimum(b + 1, nblk - 1), 0))
    rcur = BS((H, None, 1, T), lambda b: (0, b, 0, 0))
    rnxt = BS((H, None, 1, T), lambda b: (0, jnp.minimum(b + 1, nblk - 1), 0, 0))
    return pl.pallas_call(
        body, name=name, grid=(nblk,), in_specs=[cur, cur, cur, nxt, cur, nxt, rcur, rnxt, rcur, rnxt],
        out_specs=[cur, cur], out_shape=[SDS((H, S, Dh), F32), SDS((H, S, Dh), F32)],
        compiler_params=_params(1))(k, v, q, q, do, do, lse_row, lse_row, delta_row, delta_row)


def dil_merge_fwd(outs, lses, name):
    H, S, Dh = outs[0].shape
    tm = _tile(S, 512, 8)

    def body(o0, o1, o2, l0, l1, l2, out_ref):
        ls = [l0[...], l1[...], l2[...]]
        m = jnp.maximum(jnp.maximum(ls[0], ls[1]), ls[2])
        es = [jnp.exp(l - m) for l in ls]
        den = es[0] + es[1] + es[2]
        out_ref[...] = (es[0] * o0[...] + es[1] * o1[...] + es[2] * o2[...]) / den

    blk = BS((None, tm, Dh), lambda h, i: (h, i, 0))
    colb = BS((None, tm, 1), lambda h, i: (h, i, 0))
    return pl.pallas_call(
        body, name=name, grid=(H, S // tm), in_specs=[blk] * 3 + [colb] * 3, out_specs=blk,
        out_shape=SDS((H, S, Dh), F32), compiler_params=_params(2))(*outs, *lses)


def dil_merge_bwd(outs, lses, d_out, name):
    H, S, Dh = outs[0].shape
    tm = _tile(S, 512, 8)

    def body(o0, o1, o2, l0, l1, l2, d_ref, do0, do1, do2, dl0, dl1, dl2):
        ls = [l0[...], l1[...], l2[...]]
        m = jnp.maximum(jnp.maximum(ls[0], ls[1]), ls[2])
        es = [jnp.exp(l - m) for l in ls]
        den = es[0] + es[1] + es[2]
        ws = [e / den for e in es]
        dv = d_ref[...]
        dws = [jnp.sum(dv * o[...], axis=-1, keepdims=True) for o in (o0, o1, o2)]
        mean = ws[0] * dws[0] + ws[1] * dws[1] + ws[2] * dws[2]
        for w, dw, do_ref, dl_ref in zip(ws, dws, (do0, do1, do2), (dl0, dl1, dl2)):
            do_ref[...] = w * dv
            dl_ref[...] = w * (dw - mean)

    blk = BS((None, tm, Dh), lambda h, i: (h, i, 0))
    colb = BS((None, tm, 1), lambda h, i: (h, i, 0))
    return pl.pallas_call(
        body, name=name, grid=(H, S // tm), in_specs=[blk] * 3 + [colb] * 3 + [blk],
        out_specs=[blk] * 3 + [colb] * 3,
        out_shape=[SDS((H, S, Dh), F32)] * 3 + [SDS((H, S, 1), F32)] * 3,
        compiler_params=_params(2))(*outs, *lses, d_out)


MERGE_TN = 512


def merge_fwd(o_stack, w_br, proj, name):
    _, S, K = o_stack.shape
    tm = _tile(S, 512, 8)
    tn = MERGE_TN
    nj = D_MODEL // tn

    def body(o_ref, w_ref, gl_ref, m_ref, acc):
        br = pl.program_id(2)

        @pl.when(br == 0)
        def _():
            acc[...] = jnp.zeros_like(acc)

        acc[...] += _sigmoid(gl_ref[...]) * _dot(o_ref[...], w_ref[...])

        @pl.when(br == N_BRANCH - 1)
        def _():
            m_ref[...] = acc[...].astype(m_ref.dtype)

    return pl.pallas_call(
        body, name=name, grid=(S // tm, nj, N_BRANCH),
        in_specs=[BS((None, tm, K), lambda i, j, b: (b, i, 0)), BS((None, K, tn), lambda i, j, b: (b, 0, j)),
                  BS((tm, tn), lambda i, j, b: (i, b * nj + j))],
        out_specs=BS((tm, tn), lambda i, j, b: (i, j)), out_shape=SDS((S, D_MODEL), BF16),
        scratch_shapes=[pltpu.VMEM((tm, tn), F32)], compiler_params=_params(3))(o_stack, w_br, proj)


def merge_bwd(o_stack, w_br, proj, d_merged, name):
    _, S, K = o_stack.shape
    tm = _tile(S, 512, 8)
    tn = MERGE_TN
    nj = D_MODEL // tn

    def body(o_ref, w_ref, gl_ref, dm_ref, dgl_ref, dy_ref):
        gate = _sigmoid(gl_ref[...])
        y = _dot(o_ref[...], w_ref[...])
        dm = dm_ref[...].astype(F32)
        dgl_ref[...] = (dm * y * gate * (1.0 - gate)).astype(dgl_ref.dtype)
        dy_ref[...] = (dm * gate).astype(dy_ref.dtype)

    return pl.pallas_call(
        body, name=name, grid=(S // tm, nj, N_BRANCH),
        in_specs=[BS((None, tm, K), lambda i, j, b: (b, i, 0)), BS((None, K, tn), lambda i, j, b: (b, 0, j)),
                  BS((tm, tn), lambda i, j, b: (i, b * nj + j)), BS((tm, tn), lambda i, j, b: (i, j))],
        out_specs=[BS((tm, tn), lambda i, j, b: (i, b * nj + j)), BS((None, tm, tn), lambda i, j, b: (b, i, j))],
        out_shape=[SDS((S, N_BRANCH * D_MODEL), BF16), SDS((N_BRANCH, S, D_MODEL), BF16)],
        compiler_params=_params(3))(o_stack, w_br, proj, d_merged)


FFN_CW = 256


def ffn_mid_fwd(pre_g, pre_v, w_g, w_v, name, side=None):
    S = pre_g.shape[0]
    tm = _tile(S, CONV_TM, 8)

    def body(g_ref, gp_ref, v_ref, vp_ref, wg_ref, wv_ref, a_ref):
        keep = (pl.program_id(1) > 0).astype(F32)
        ug = _conv_ext(jnp.concatenate([gp_ref[...] * keep, g_ref[...]], axis=0), wg_ref)[HALO:, :]
        uv = _conv_ext(jnp.concatenate([vp_ref[...] * keep, v_ref[...]], axis=0), wv_ref)[HALO:, :]
        a_ref[...] = (ug * _sigmoid(ug) * uv).astype(a_ref.dtype)

    cur, prev, _ = _time_specs(S, tm, FFN_CW, lambda j: j)
    wspec = BS((3, FFN_CW), lambda j, i: (0, j))
    (a,), side_out = _call_with_side(body, side, name, (D_FF // FFN_CW, S // tm),
                                     [cur, prev, cur, prev, wspec, wspec], [cur], [SDS((S, D_FF), BF16)],
                                     (pre_g, pre_g, pre_v, pre_v, w_g, w_v))
    return a, side_out


def ffn_mid_bwd(pre_g, pre_v, w_g, w_v, d_a, name, side=None):
    S = pre_g.shape[0]
    tm = _tile(S, CONV_TM, 8)
    nt = S // tm

    def body(g_ref, gp_ref, gn_ref, v_ref, vp_ref, vn_ref, wg_ref, wv_ref, da_ref, dan_ref,
             dg_ref, dv_ref, dwg_ref, dwv_ref):
        i = pl.program_id(1)
        keep_prev = (i > 0).astype(F32)
        keep_next = (i < nt - 1).astype(F32)
        eg = jnp.concatenate([gp_ref[...] * keep_prev, g_ref[...], gn_ref[...]], axis=0)
        ev = jnp.concatenate([vp_ref[...] * keep_prev, v_ref[...], vn_ref[...]], axis=0)
        ug = _conv_ext(eg, wg_ref)
        uv = _conv_ext(ev, wv_ref)
        da = jnp.concatenate([jnp.zeros((HALO, FFN_CW), F32), da_ref[...], dan_ref[...] * keep_next], axis=0)
        sg = _sigmoid(ug)
        dug = da * uv * (sg * (1.0 + ug * (1.0 - sg)))
        duv = da * (ug * sg)
        dg_ref[...] = _conv_t_ext(dug, wg_ref)[HALO:HALO + tm, :].astype(dg_ref.dtype)
        dv_ref[...] = _conv_t_ext(duv, wv_ref)[HALO:HALO + tm, :].astype(dv_ref.dtype)

        @pl.when(i == 0)
        def _():
            dwg_ref[...] = jnp.zeros_like(dwg_ref)
            dwv_ref[...] = jnp.zeros_like(dwv_ref)

        for dup_e, e, dw_ref in ((dug, eg, dwg_ref), (duv, ev, dwv_ref)):
            dup = dup_e[HALO:HALO + tm, :]
            dw_ref[0:1, :] += jnp.sum(dup * pltpu.roll(e, 2, 0)[HALO:HALO + tm, :], axis=0, keepdims=True)
            dw_ref[1:2, :] += jnp.sum(dup * pltpu.roll(e, 1, 0)[HALO:HALO + tm, :], axis=0, keepdims=True)
            dw_ref[2:3, :] += jnp.sum(dup * e[HALO:HALO + tm, :], axis=0, keepdims=True)

    cur, prev, nxt = _time_specs(S, tm, FFN_CW, lambda j: j)
    wspec = BS((3, FFN_CW), lambda j, i: (0, j))
    return _call_with_side(
        body, side, name, (D_FF // FFN_CW, nt), [cur, prev, nxt, cur, prev, nxt, wspec, wspec, cur, nxt],
        [cur, cur, wspec, wspec], [SDS((S, D_FF), BF16)] * 2 + [SDS((3, D_FF), F32)] * 2,
        (pre_g, pre_g, pre_g, pre_v, pre_v, pre_v, w_g, w_v, d_a, d_a))


def ple_fwd(x, a, e, name):
    S, Dm = x.shape
    tm = _tile(S, 256, 8)

    def body(x_ref, a_ref, e_ref, o_ref):
        o_ref[...] = x_ref[...] + _sigmoid(a_ref[...]) * e_ref[...]

    row = BS((tm, Dm), lambda i: (i, 0))
    return pl.pallas_call(body, name=name, grid=(S // tm,), in_specs=[row] * 3, out_specs=row,
                          out_shape=SDS((S, Dm), F32), compiler_params=_params(1))(x, a, e)


def ple_bwd(a, e, dx, name):
    S, Dm = a.shape
    tm = _tile(S, 256, 8)

    def body(a_ref, e_ref, dx_ref, da_ref, de_ref):
        s = _sigmoid(a_ref[...])
        d = dx_ref[...]
        da_ref[...] = (d * e_ref[...] * s * (1.0 - s)).astype(da_ref.dtype)
        de_ref[...] = (d * s).astype(de_ref.dtype)

    row = BS((tm, Dm), lambda i: (i, 0))
    return pl.pallas_call(body, name=name, grid=(S // tm,), in_specs=[row] * 3, out_specs=[row, row],
                          out_shape=[SDS((S, Dm), BF16)] * 2, compiler_params=_params(1))(a, e, dx)


def to_heads(x, n_heads, dil):
    S = x.shape[0]
    x = x.reshape(S // dil, dil, n_heads, HEAD_DIM).transpose(2, 1, 0, 3)
    return x.reshape(n_heads, S, HEAD_DIM)


def from_heads(y, dil):
    H, S, C = y.shape
    y = y.reshape(H, dil, S // dil, C).transpose(2, 1, 0, 3)
    return y.reshape(S, H * C)


def _columns(x, lo, hi):
    return lax.optimization_barrier(x[:, lo:hi])


def rows_of(col, t):
    H, S, _ = col.shape
    return col.reshape(H, S // t, 1, t)


IN_SEGMENTS = ((SRC_A, SRC_F, OFF_A), (SRC_F, SRC_B, OFF_F), (SRC_B, SRC_C, OFF_B), (SRC_C, SRC_DQ, OFF_C),
               (SRC_DQ, SRC_G, OFF_D), (SRC_G, D_IN, OFF_G))
IN_SHARD = D_IN // N_CHIPS


def w_in_aligned_from_chips(t):
    pieces = []
    for ref_lo, ref_hi, _ in sorted(IN_SEGMENTS, key=lambda seg: seg[2]):
        for k in range(N_CHIPS):
            lo, hi = max(ref_lo, k * IN_SHARD), min(ref_hi, (k + 1) * IN_SHARD)
            if lo < hi:
                pieces.append(t[k][:, lo - k * IN_SHARD:hi - k * IN_SHARD])
    pieces.append(jnp.zeros((t[0].shape[0], W_AL - D_IN), t[0].dtype))
    return jnp.concatenate(pieces, axis=1)


def w_in_chips_from_aligned(g):
    slots = []
    for k in range(N_CHIPS):
        pieces = []
        for ref_lo, ref_hi, al in IN_SEGMENTS:
            lo, hi = max(ref_lo, k * IN_SHARD), min(ref_hi, (k + 1) * IN_SHARD)
            if lo < hi:
                pieces.append(g[:, al + lo - ref_lo:al + hi - ref_lo])
        slots.append(jnp.concatenate(pieces, axis=1))
    return jnp.stack(slots, axis=0)


def chips_to_full(t, name):
    return jnp.concatenate([t[k] for k in range(N_CHIPS)], axis=0 if name in ROW_SHARDED else 1)


def full_to_chips(g, name):
    if name in ROW_SHARDED:
        return g.reshape(N_CHIPS, g.shape[0] // N_CHIPS, g.shape[1])
    return g.reshape(g.shape[0], N_CHIPS, g.shape[1] // N_CHIPS).transpose(1, 0, 2)


def halves_from_chips(t):
    return jnp.concatenate([t[0], t[1]], axis=1), jnp.concatenate([t[2], t[3]], axis=1)


def chips_from_halves(g, v):
    c = g.shape[1] // 2
    return jnp.stack([g[:, :c], g[:, c:], v[:, :c], v[:, c:]], axis=0)


def layer_fwd(x, p_l, rope, w, tag, sides=None):
    S = x.shape[0]
    sides = sides or {}
    side_out = {}
    sv = {"x0": x}
    h = rmsnorm_fwd(x, w["norm_mix_g"], f"{tag}_norm_mix")
    proj = matmul(h, w["w_in_al"], "nn", F32, f"{tag}_proj", side=sides.get("proj"))
    if "proj" in sides:
        proj, side_out["proj"] = proj
    sv["h"], sv["proj"] = h, proj

    af_t = _columns(proj, OFF_F, OFF_F + FOX_HEADS).T
    f_cum = fox_prep_fwd(af_t, w["fox_forget_b"].reshape(FOX_HEADS, 1), f"{tag}_fox_prep")
    T = min(FOX_T, S)
    f_col = f_cum.reshape(FOX_HEADS, S, 1)
    f_row = f_cum.reshape(FOX_HEADS, S // T, 1, T)
    qkv = to_heads(_columns(proj, OFF_A, OFF_B).astype(BF16), 3 * FOX_HEADS, 1)
    qa, ka, va = qkv[:FOX_HEADS], qkv[FOX_HEADS:2 * FOX_HEADS], qkv[2 * FOX_HEADS:]
    (oa_h, lse_a), side_out["fox"] = fox_attn_fwd(qa, ka, va, f_col, f_row, f"{tag}_fox_fwd", sides.get("fox"))
    o_a = from_heads(oa_h, 1)
    sv.update(af_t=af_t, f_col=f_col, f_row=f_row, qa=qa, ka=ka, va=va, oa_h=oa_h, lse_a=lse_a)

    o_b = shortconv_fwd(proj, w["shortconv_w"], f"{tag}_sconv_fwd")

    o_c = sgu_fwd(proj, w["sgu_norm_g"].reshape(1, SGU_WIDTH), w["sgu_w"], _sgu_bias(w["sgu_b"]), f"{tag}_sgu_fwd")

    cos, sa, sb = rope
    qk = rope_apply(proj, OFF_D, 2 * DIL_WIDTH, cos, sa, sb, BF16, f"{tag}_rope_fwd")
    vd = _columns(proj, OFF_D + 2 * DIL_WIDTH, OFF_D + 3 * DIL_WIDTH).astype(BF16)
    outs, lses, dil_sv = [], [], []
    for g, (window, dil) in enumerate(DIL_PATTERNS):
        sl = slice(g * DIL_OUT, (g + 1) * DIL_OUT)
        qg = to_heads(qk[:, sl], 4, dil)
        kg = to_heads(qk[:, DIL_WIDTH:][:, sl], 4, dil)
        vg = to_heads(vd[:, sl], 4, dil)
        nb = (S // dil) // DIL_SPAN
        og, lg = dil_attn_fwd(qg, kg, vg, nb, f"{tag}_dil{g}_fwd")
        dil_sv.append((qg, kg, vg, og, lg, nb))
        outs.append(_heads_unperm(og, dil))
        lses.append(_col_unperm(lg, dil))
    od_h = dil_merge_fwd(outs, lses, f"{tag}_dil_merge_fwd")
    o_d = from_heads(od_h, 1)
    sv.update(dil=dil_sv, outs=outs, lses=lses)

    o_d_pad = jnp.concatenate([o_d.astype(BF16), jnp.zeros((S, FOX_WIDTH - DIL_OUT), BF16)], axis=-1)
    o_stack = jnp.stack([o_a, o_b, o_c, o_d_pad], axis=0)
    merged = merge_fwd(o_stack, w["w_br"], proj, f"{tag}_merge_fwd")
    x1 = matmul(merged, w["w_out"], "nn", F32, f"{tag}_out_proj", res=x)
    sv.update(o_stack=o_stack, merged=merged, x1=x1)

    h2 = rmsnorm_fwd(x1, w["norm_ffn_g"], f"{tag}_norm_ffn")
    pre = (matmul(h2, w["w_up_g"], "nn", F32, f"{tag}_up_g"), matmul(h2, w["w_up_v"], "nn", F32, f"{tag}_up_v"))
    a, side_out["ffn"] = ffn_mid_fwd(pre[0], pre[1], w["ffn_conv_g"], w["ffn_conv_v"], f"{tag}_ffn_mid_fwd",
                                     sides.get("ffn"))
    x2 = matmul(a, w["w_down"], "nn", F32, f"{tag}_down", res=x1)
    sv.update(h2=h2, pre=pre, a=a, x2=x2)

    n3 = rmsnorm_fwd(x2, w["norm_ple_g"], f"{tag}_norm_ple")
    pg = matmul(n3, w["w_ple_gate"], "nn", F32, f"{tag}_ple_gate")
    pe = matmul(p_l, w["w_ple_proj"], "nn", F32, f"{tag}_ple_proj")
    x3 = ple_fwd(x2, pg, pe, f"{tag}_ple_fwd")
    sv.update(n3=n3, pg=pg, pe=pe, p_l=p_l)
    return x3, sv, side_out


def _sgu_bias(b):
    return jnp.pad(b.T, ((0, 0), (0, SGU_CHUNK - b.shape[0])))


def _col_unperm(col, dil):
    H, S, _ = col.shape
    return col.reshape(H, dil, S // dil).transpose(0, 2, 1).reshape(H, S, 1)


def _col_perm(col, dil):
    H, S, _ = col.shape
    return col.reshape(H, S // dil, dil).transpose(0, 2, 1).reshape(H, S, 1)


def _heads_perm(y, dil):
    H, S, C = y.shape
    return y.reshape(H, S // dil, dil, C).transpose(0, 2, 1, 3).reshape(H, S, C)


def _heads_unperm(y, dil):
    H, S, C = y.shape
    return y.reshape(H, dil, S // dil, C).transpose(0, 2, 1, 3).reshape(H, S, C)


def layer_bwd(dx3, sv, rope, w, tag, exch):
    side_out = {}
    S = dx3.shape[0]
    gr = {}
    da, de = ple_bwd(sv["pg"], sv["pe"], dx3, f"{tag}_ple_bwd")
    gr["w_ple_proj"] = matmul(sv["p_l"], de, "tn", F32, f"{tag}_dw_ple_proj")
    gr["w_ple_gate"] = matmul(sv["n3"], da, "tn", F32, f"{tag}_dw_ple_gate")
    dn3 = matmul(da, w["w_ple_gate"], "nt", BF16, f"{tag}_dn3")
    dx2, gr["norm_ple_g"] = rmsnorm_bwd(sv["x2"], w["norm_ple_g"], dn3, dx3, f"{tag}_norm_ple_bwd")

    d_a = matmul(dx2, w["w_down"], "nt", F32, f"{tag}_da")
    gr["w_down"] = matmul(sv["a"], dx2, "tn", F32, f"{tag}_dw_down")
    (dpre_g, dpre_v, dwc_g, dwc_v), swapped = ffn_mid_bwd(sv["pre"][0], sv["pre"][1], w["ffn_conv_g"],
                                                          w["ffn_conv_v"], d_a, f"{tag}_ffn_mid_bwd",
                                                          exch.swap_exchange())
    sides = exch.ici_exchanges(swapped)
    gr["ffn_conv_w"] = (dwc_g, dwc_v)
    h2_t = sv["h2"].T
    gr["w_up"] = (matmul(h2_t, dpre_g, "nn", F32, f"{tag}_dw_up_g", tm=2048, tn=512),
                  matmul(h2_t, dpre_v, "nn", F32, f"{tag}_dw_up_v", tm=2048, tn=512))
    dh2_g = matmul(dpre_g, w["w_up_g"], "nt", F32, f"{tag}_dh2_g")
    dh2 = matmul(dpre_v, w["w_up_v"], "nt", BF16, f"{tag}_dh2_v", res=dh2_g)
    dx1, gr["norm_ffn_g"] = rmsnorm_bwd(sv["x1"], w["norm_ffn_g"], dh2, dx2, f"{tag}_norm_ffn_bwd")

    d_merged = matmul(dx1, w["w_out"], "nt", BF16, f"{tag}_dmerged")
    gr["w_out"] = matmul(sv["merged"], dx1, "tn", F32, f"{tag}_dw_out")
    proj = sv["proj"]
    dgl, dy = merge_bwd(sv["o_stack"], w["w_br"], proj, d_merged, f"{tag}_merge_bwd")
    d_o, d_wbr = [], []
    for b in range(N_BRANCH):
        d_o.append(matmul(dy[b], w["w_br"][b], "nt", F32, f"{tag}_do{b}"))
        d_wbr.append(matmul(sv["o_stack"][b], dy[b], "tn", F32, f"{tag}_dw_br{b}"))
    gr["w_br"] = d_wbr

    do_a = to_heads(d_o[0].astype(BF16), FOX_HEADS, 1)
    T = min(FOX_T, S)
    (dqa, delta_a, d_fq), side_out["dq"] = fox_attn_bwd_dq(
        sv["qa"], sv["ka"], sv["va"], sv["f_col"], sv["f_row"], sv["oa_h"], sv["lse_a"], do_a, f"{tag}_fox_dq",
        sides.get("dq"))
    (dka, dva, d_fk), side_out["dkv"] = fox_attn_bwd_dkv(
        sv["qa"], sv["ka"], sv["va"], sv["f_col"], sv["f_row"], rows_of(sv["lse_a"], T), rows_of(delta_a, T), do_a,
        f"{tag}_fox_dkv", sides.get("dkv"))
    daf_t, dfb = fox_prep_bwd(sv["af_t"], w["fox_forget_b"].reshape(FOX_HEADS, 1), d_fq.reshape(FOX_HEADS, S),
                              d_fk.reshape(FOX_HEADS, S), f"{tag}_fox_prep_bwd")
    gr["fox_forget_b"] = dfb.reshape(FOX_HEADS)
    d_proj_a = from_heads(jnp.concatenate([dqa, dka, dva], axis=0), 1).astype(BF16)

    dxb, dgb, dgc, gr["shortconv_w"] = shortconv_bwd(proj, w["shortconv_w"], d_o[1], f"{tag}_sconv_bwd")

    d_c, dsg, dsw, dsb = sgu_bwd(proj, w["sgu_norm_g"].reshape(1, SGU_WIDTH), w["sgu_w"],
                                 jnp.swapaxes(w["sgu_w"], 1, 2), _sgu_bias(w["sgu_b"]), d_o[2], f"{tag}_sgu_bwd")
    gr["sgu_norm_g"] = dsg.reshape(SGU_WIDTH)
    gr["sgu_w"] = dsw
    gr["sgu_b"] = dsb[:, :SGU_WIDTH // SGU_CHUNK].T

    d_od = to_heads(d_o[3][:, :DIL_OUT], 4, 1)
    d_outs_lses = dil_merge_bwd(sv["outs"], sv["lses"], d_od, f"{tag}_dil_merge_bwd")
    d_outs, d_lses = d_outs_lses[:3], d_outs_lses[3:]
    dq_parts, dk_parts, dv_parts = [], [], []
    for g, (window, dil) in enumerate(DIL_PATTERNS):
        qg, kg, vg, og, lg, nb = sv["dil"][g]
        do_g = _heads_perm(d_outs[g], dil)
        dl_g = _col_perm(d_lses[g], dil)
        dqg, delta_g = dil_attn_bwd_dq(qg, kg, vg, og, lg, do_g, dl_g, nb, f"{tag}_dil{g}_dq")
        dkg, dvg = dil_attn_bwd_dkv(qg, kg, vg, rows_of(lg, DIL_SPAN), rows_of(delta_g, DIL_SPAN), do_g, nb,
                                    f"{tag}_dil{g}_dkv")
        dq_parts.append(from_heads(dqg, dil))
        dk_parts.append(from_heads(dkg, dil))
        dv_parts.append(from_heads(dvg, dil))
    cos, sa, sb = rope
    d_qk_rot = jnp.concatenate(dq_parts + dk_parts, axis=-1)
    d_qk = rope_apply(d_qk_rot, 0, 2 * DIL_WIDTH, cos, -sa, -sb, BF16, f"{tag}_rope_bwd")
    d_vd = jnp.concatenate(dv_parts, axis=-1).astype(BF16)

    d_f_cols = jnp.concatenate([daf_t.T.astype(BF16), jnp.zeros((S, W_AL - OFF_F - FOX_HEADS), BF16)], axis=-1)
    d_proj = jnp.concatenate([dgl, d_proj_a, dxb, dgb, dgc, d_c, d_qk, d_vd, d_f_cols], axis=-1)
    gr["w_in_al"] = matmul(sv["h"], d_proj, "tn", F32, f"{tag}_dw_in", tm=2048, tn=512)
    dh = matmul(d_proj, w["w_in_al"], "nt", BF16, f"{tag}_dh", tk=W_AL // 4)
    dx0, gr["norm_mix_g"] = rmsnorm_bwd(sv["x0"], w["norm_mix_g"], dh, dx1, f"{tag}_norm_mix_bwd")
    exch.ici_arrived(side_out)
    return dx0, gr


def local_weights(chips, repl, layer):
    w = {n: repl[n][layer] for n in REPLICATED}
    cast = lambda n, dtype: [chips[n][k].astype(dtype) for k in range(N_CHIPS)]
    full = {n: chips_to_full(cast(n, BF16), n)
            for n in ("w_br_fox", "w_br_conv", "w_br_sgu", "w_br_dil", "w_out", "w_down", "w_ple_gate", "w_ple_proj")}
    w["w_in_al"] = w_in_aligned_from_chips(cast("w_in", BF16))
    w["shortconv_w"] = chips_to_full(cast("shortconv_w", F32), "shortconv_w")
    pad = jnp.zeros((FOX_WIDTH - DIL_OUT, D_MODEL), BF16)
    w["w_br"] = jnp.stack([full["w_br_fox"], full["w_br_conv"], full["w_br_sgu"],
                           jnp.concatenate([full["w_br_dil"], pad], axis=0)], axis=0)
    w["w_up_g"], w["w_up_v"] = halves_from_chips(cast("w_up", BF16))
    w["ffn_conv_g"], w["ffn_conv_v"] = halves_from_chips(cast("ffn_conv_w", F32))
    for n in ("w_out", "w_down", "w_ple_gate", "w_ple_proj"):
        w[n] = full[n]
    return w


def grads_to_chips(gr):
    out = {n: gr[n] for n in ("fox_forget_b", "sgu_norm_g", "sgu_w", "sgu_b")}
    out["norm_mix_g"] = gr["norm_mix_g"].reshape(D_MODEL)
    out["norm_ffn_g"] = gr["norm_ffn_g"].reshape(D_MODEL)
    out["norm_ple_g"] = gr["norm_ple_g"].reshape(D_MODEL)
    out["w_in"] = w_in_chips_from_aligned(gr["w_in_al"])
    out["w_up"] = chips_from_halves(*gr["w_up"])
    out["ffn_conv_w"] = chips_from_halves(*gr["ffn_conv_w"])
    for b, n in enumerate(("w_br_fox", "w_br_conv", "w_br_sgu")):
        out[n] = full_to_chips(gr["w_br"][b], n)
    out["w_br_dil"] = full_to_chips(gr["w_br"][3][:DIL_OUT], "w_br_dil")
    for n in ("shortconv_w", "w_out", "w_down", "w_ple_gate", "w_ple_proj"):
        out[n] = full_to_chips(gr[n], n)
    return out


def local_step(x, p, positions, repl, final_norm_g, loss_target, exch):
    depth = p.shape[0]
    rope = rope_tables(positions)
    saved, ws = [], []
    chips = exch.first_weights()
    for layer in range(depth):
        w = local_weights(chips, repl, layer)
        side = exch.weights_exchange(layer + 1) if layer + 1 < depth else None
        x, sv, side_out = layer_fwd(x, p[layer].astype(BF16), rope, w, f"l{layer}", side)
        if layer + 1 < depth:
            chips = exch.weights_arrived(layer + 1, side_out)
        saved.append(sv)
        ws.append(w)
    loss_part, dx, dgf = final_loss(x, final_norm_g, loss_target, "final_loss")
    for layer in range(depth - 1, -1, -1):
        dx, gr = layer_bwd(dx, saved[layer], rope, ws[layer], f"l{layer}", exch)
        exch.grads_ready(layer, grads_to_chips(gr))
    exch.grads_flush()
    return loss_part[0, 0], dx, dgf.reshape(-1)


def _position():
    return lax.axis_index("x"), lax.axis_index("y"), lax.axis_index("c")


def _other_chips(x, y):
    return [(1 - x, y), (x, 1 - y), (1 - x, 1 - y)]


def _remote(src, dst, send_sem, recv_sem, device):
    return pltpu.make_async_remote_copy(src_ref=src, dst_ref=dst, send_sem=send_sem, recv_sem=recv_sem,
                                        device_id=device, device_id_type=MESH)


def _chip_index():
    return 2 * lax.axis_index("x") + lax.axis_index("y")


def _block_rows(rows, cols, unit):
    return _tile(rows, max(unit, (1 << 19) // cols // unit * unit), unit)


def gather_chip_shards(packs, name):
    return _run_exchange(gather_exchange(packs), name)


def gather_exchange(packs):
    n = len(packs)
    halves = [p.shape[0] // 2 for p in packs]

    def half(outs, t, chip, core):
        return outs[t].at[chip, pl.ds(core * halves[t], halves[t]), :]

    def ici_sends(srcs, outs, send_sems, recv_sems):
        x, y, c = _position()
        me = 2 * x + y
        return [_remote(srcs[t].at[pl.ds(c * halves[t], halves[t]), :], half(outs, t, me, c),
                        send_sems.at[6 * t + j], recv_sems.at[6 * t + j], (px, py, c))
                for t in range(n) for j, (px, py) in enumerate(_other_chips(x, y))]

    def start(srcs, outs, send_sems, recv_sems):
        for cp in ici_sends(srcs, outs, send_sems, recv_sems):
            cp.start()

    def finish(srcs, outs, send_sems, recv_sems):
        x, y, c = _position()
        sibling = (x, y, 1 - c)
        chips = _other_chips(x, y)
        passed = []
        for t in range(n):
            for j, (px, py) in enumerate(chips):
                k = 2 * px + py
                s = 6 * t + j
                landed = half(outs, t, k, c)
                _remote(landed, landed, send_sems.at[s], recv_sems.at[s], (px, py, c)).wait_recv()
                fwd = _remote(landed, landed, send_sems.at[s + 3], recv_sems.at[s + 3], sibling)
                fwd.start()
                passed.append(fwd)
        for t in range(n):
            for j, (px, py) in enumerate(chips):
                s = 6 * t + j + 3
                theirs = half(outs, t, 2 * px + py, 1 - c)
                _remote(theirs, theirs, send_sems.at[s], recv_sems.at[s], sibling).wait_recv()
        for cp in ici_sends(srcs, outs, send_sems, recv_sems) + passed:
            cp.wait_send()

    return SideExchange(list(packs), [SDS((N_CHIPS,) + p.shape, p.dtype) for p in packs], 6 * n, start, finish)


def _run_exchange(side, name):
    n_in, n_out = len(side.operands), len(side.out_shapes)

    def body(*refs):
        srcs, outs, (send_sems, recv_sems) = refs[:n_in], refs[n_in:n_in + n_out], refs[n_in + n_out:]
        side.start(srcs, outs, send_sems, recv_sems)
        side.finish(srcs, outs, send_sems, recv_sems)

    return pl.pallas_call(
        body, name=name, in_specs=[ANY] * n_in, out_specs=[ANY] * n_out, out_shape=side.out_shapes,
        scratch_shapes=[pltpu.SemaphoreType.DMA((side.n_sems,)), pltpu.SemaphoreType.DMA((side.n_sems,))],
    )(*side.operands)


def swap_halves_with_sibling(gs, name):
    return _run_exchange(swap_exchange(gs), name)


def swap_exchange(gs):
    n = len(gs)
    halves = [g.shape[1] // 2 for g in gs]

    def copies(srcs, lands, send_sems, recv_sems):
        x, y, c = _position()
        return [_remote(srcs[t].at[:, pl.ds((1 - c) * halves[t], halves[t]), :], lands[t], send_sems.at[t],
                        recv_sems.at[t], (x, y, 1 - c)) for t in range(n)]

    def start(srcs, lands, send_sems, recv_sems):
        for cp in copies(srcs, lands, send_sems, recv_sems):
            cp.start()

    def finish(srcs, lands, send_sems, recv_sems):
        for cp in copies(srcs, lands, send_sems, recv_sems):
            cp.wait()

    return SideExchange(list(gs), [SDS((g.shape[0], h, g.shape[2]), g.dtype) for g, h in zip(gs, halves)], n,
                        start, finish)


def add_my_half(g, other, out_dtype, name):
    n, R, C = g.shape
    H = R // 2
    tr = _block_rows(H, C, 16) if H % 16 == 0 else H
    nb = H // tr
    core = lax.axis_index("c").astype(jnp.int32).reshape(1)

    def body(c_ref, g_ref, o_ref, out_ref):
        out_ref[...] = (g_ref[...] + o_ref[...]).astype(out_ref.dtype)

    grid_spec = pltpu.PrefetchScalarGridSpec(
        num_scalar_prefetch=1, grid=(n, nb),
        in_specs=[BS((None, tr, C), lambda s, i, c_ref: (s, c_ref[0] * nb + i, 0)),
                  BS((None, tr, C), lambda s, i, c_ref: (s, i, 0))],
        out_specs=BS((None, tr, C), lambda s, i, c_ref: (s, i, 0)))
    return pl.pallas_call(body, name=name, grid_spec=grid_spec, out_shape=SDS((n, H, C), out_dtype),
                          compiler_params=_params(2))(core, g, other)


def exchange_slots_between_chips(parts, name):
    return _run_exchange(slot_exchange(parts), name)


def slot_exchange(parts):
    n = len(parts)

    def sends(srcs, lands, send_sems, recv_sems):
        x, y, c = _position()
        me = 2 * x + y
        return [_remote(srcs[t].at[2 * px + py], lands[t].at[me], send_sems.at[3 * t + j], recv_sems.at[3 * t + j],
                        (px, py, c)) for t in range(n) for j, (px, py) in enumerate(_other_chips(x, y))]

    def start(srcs, lands, send_sems, recv_sems):
        for cp in sends(srcs, lands, send_sems, recv_sems):
            cp.start()

    def finish(srcs, lands, send_sems, recv_sems):
        x, y, c = _position()
        for t in range(n):
            for j, (px, py) in enumerate(_other_chips(x, y)):
                k = 2 * px + py
                _remote(srcs[t].at[k], lands[t].at[k], send_sems.at[3 * t + j], recv_sems.at[3 * t + j],
                        (px, py, c)).wait_recv()
        for cp in sends(srcs, lands, send_sems, recv_sems):
            cp.wait_send()

    return SideExchange(list(parts), [SDS(p.shape, p.dtype) for p in parts], 3 * n, start, finish)


def sum_slots_into_my_half(landed, mine, name):
    n, H, C = landed.shape
    tr = _block_rows(H, C, 16) if H % 16 == 0 else H
    nb = H // tr
    where = jnp.stack([lax.axis_index("c"), _chip_index()]).astype(jnp.int32)

    def body(w_ref, l_ref, m_ref, o_ref):
        me = w_ref[1]
        o_ref[...] = jnp.zeros_like(o_ref)
        for k in range(n):
            @pl.when(me == k)
            def _():
                o_ref[...] += m_ref[k].astype(F32)

            @pl.when(me != k)
            def _():
                o_ref[...] += l_ref[k].astype(F32)

    slots = BS((n, tr, C), lambda i, w_ref: (0, i, 0))
    grid_spec = pltpu.PrefetchScalarGridSpec(
        num_scalar_prefetch=1, grid=(nb,), in_specs=[slots, slots],
        out_specs=BS((tr, C), lambda i, w_ref: (w_ref[0] * nb + i, 0)))
    return pl.pallas_call(body, name=name, grid_spec=grid_spec, out_shape=SDS((2 * H, C), F32),
                          compiler_params=_params(1))(where, landed, mine)


def sum_slots(parts, name):
    n, H, C = parts.shape
    tr = _tile(H, 256, 16)

    def body(p_ref, o_ref):
        acc = p_ref[0].astype(F32)
        for k in range(1, n):
            acc = acc + p_ref[k].astype(F32)
        o_ref[...] = acc

    return pl.pallas_call(
        body, name=name, grid=(H // tr,), in_specs=[BS((n, tr, C), lambda i: (0, i, 0))],
        out_specs=BS((tr, C), lambda i: (i, 0)), out_shape=SDS((H, C), F32), compiler_params=_params(1))(parts)


def join_halves_with_sibling(arrs, name):
    n = len(arrs)
    halves = [a.shape[0] // 2 for a in arrs]

    def body(*refs):
        outs, (send_sems, recv_sems) = refs[n:2 * n], refs[2 * n:]
        x, y, c = _position()

        def half(t, core):
            return outs[t].at[pl.ds(core * halves[t], halves[t]), :]

        sends = [_remote(half(t, c), half(t, c), send_sems.at[t], recv_sems.at[t], (x, y, 1 - c)) for t in range(n)]
        for cp in sends:
            cp.start()
        for t in range(n):
            _remote(half(t, 1 - c), half(t, 1 - c), send_sems.at[t], recv_sems.at[t], (x, y, 1 - c)).wait_recv()
        for cp in sends:
            cp.wait_send()

    return pl.pallas_call(
        body, name=name, in_specs=[ANY] * n, out_specs=[ANY] * n, out_shape=[SDS(a.shape, a.dtype) for a in arrs],
        input_output_aliases={t: t for t in range(n)},
        scratch_shapes=[pltpu.SemaphoreType.DMA((n,)), pltpu.SemaphoreType.DMA((n,))])(*arrs)


def reduce_scatter_pair_sums(gs, others, tag):
    n = len(gs)
    return [add_my_half(g, o, BF16 if t < n - 1 else F32, f"{tag}_pair_sum{t}")
            for t, (g, o) in enumerate(zip(gs, others))]


def reduce_scatter_finish(parts, landed, tag):
    sums = [sum_slots_into_my_half(l, p, f"{tag}_chip_sum{t}") for t, (l, p) in enumerate(zip(landed, parts))]
    return join_halves_with_sibling(sums, f"{tag}_join")


def gather_all_devices(pack, name):
    R, C = pack.shape

    def body(src, out, send_sems, recv_sems, local_sem):
        x, y, c = _position()
        me = 4 * x + 2 * y + c
        local = pltpu.make_async_copy(src, out.at[me], local_sem)
        local.start()
        peers = []
        for m in range(1, N_DEV):
            px = 1 - x if m & 4 else x
            py = 1 - y if m & 2 else y
            pc = 1 - c if m & 1 else c
            peers.append((px, py, pc))
        sends = [_remote(src, out.at[me], send_sems.at[j], recv_sems.at[j], peer) for j, peer in enumerate(peers)]
        for cp in sends:
            cp.start()
        for j, (px, py, pc) in enumerate(peers):
            k = 4 * px + 2 * py + pc
            _remote(src, out.at[k], send_sems.at[j], recv_sems.at[j], (px, py, pc)).wait_recv()
        for cp in sends:
            cp.wait_send()
        local.wait()

    return pl.pallas_call(
        body, name=name, in_specs=[ANY], out_specs=ANY, out_shape=SDS((N_DEV, R, C), pack.dtype),
        scratch_shapes=[pltpu.SemaphoreType.DMA((N_DEV - 1,)), pltpu.SemaphoreType.DMA((N_DEV - 1,)),
                        pltpu.SemaphoreType.DMA(())])(pack)


def _adamw_update(w, g, m, v):
    c1 = 1.0 / (1.0 - ADAM_B1 ** ADAM_STEP)
    c2 = 1.0 / (1.0 - ADAM_B2 ** ADAM_STEP)
    mn = ADAM_B1 * m + (1.0 - ADAM_B1) * g
    vn = ADAM_B2 * v + (1.0 - ADAM_B2) * (g * g)
    return -ADAM_LR * ((mn * c1) / (jnp.sqrt(vn * c2) + ADAM_EPS) + ADAM_WD * w), mn, vn


def adamw_layers(w, gs, m, v, name):
    L, r, c = w.shape
    tr = r if r * c * 4 <= (1 << 20) else _tile(r, max(8, ((1 << 20) // (c * 4)) // 8 * 8), 8)
    nb = r // tr

    def g_spec(l):
        return BS((tr, c), lambda layer, i: (jnp.where(layer == l, i, jnp.where(layer < l, 0, nb - 1)), 0))

    def body(w_ref, m_ref, v_ref, *rest):
        g_refs, (go_ref, d_ref, mo_ref, vo_ref) = rest[:L], rest[L:]
        layer = pl.program_id(0)
        for l in range(L):
            @pl.when(layer == l)
            def _(g_ref=g_refs[l]):
                gv = g_ref[...]
                go_ref[...] = gv
                d_ref[...], mo_ref[...], vo_ref[...] = _adamw_update(w_ref[...], gv, m_ref[...], v_ref[...])

    blk = BS((None, tr, c), lambda layer, i: (layer, i, 0))
    return pl.pallas_call(
        body, name=name, grid=(L, nb), in_specs=[blk] * 3 + [g_spec(l) for l in range(L)], out_specs=[blk] * 4,
        out_shape=[SDS((L, r, c), F32)] * 4, compiler_params=_params(2))(w, m, v, *gs)


def adamw(w, g, m, v, name):
    shape = w.shape
    cols = shape[-1] if len(shape) > 1 else shape[0]
    rows = w.size // cols
    two = lambda t: t.reshape(rows, cols)
    tr = rows
    if rows * cols * 4 > (1 << 21):
        tr = _tile(rows, max(8, ((1 << 21) // (cols * 4)) // 8 * 8), 8)

    def body(w_ref, g_ref, m_ref, v_ref, d_ref, mo_ref, vo_ref):
        d_ref[...], mo_ref[...], vo_ref[...] = _adamw_update(w_ref[...], g_ref[...], m_ref[...], v_ref[...])

    blk = BS((tr, cols), lambda i: (i, 0))
    d, mo, vo = pl.pallas_call(
        body, name=name, grid=(rows // tr,), in_specs=[blk] * 4, out_specs=[blk] * 3,
        out_shape=[SDS((rows, cols), F32)] * 3, compiler_params=_params(1))(two(w), two(g), two(m), two(v))
    return d.reshape(shape), mo.reshape(shape), vo.reshape(shape)


def _rows_for(n, unit):
    rows = -(-n // PACK_COLS)
    return -(-rows // unit) * unit


ROWS_GROUP = ("w_out", "w_ple_gate", "w_down")
COLS_GROUP = ("w_br_fox", "w_br_conv", "w_br_sgu", "w_br_dil", "w_ple_proj")
SMALL_GROUP = ("shortconv_w", "ffn_conv_w")
SMALL_ROWS = 16


def group_shards(t, dtype):
    lead = t["w_in"].shape[:-2]
    small = jnp.concatenate([t[n].astype(F32).reshape(lead + (-1,)) for n in SMALL_GROUP], axis=-1)
    pad = jnp.zeros(lead + (SMALL_ROWS * PACK_COLS - small.shape[-1],), F32)
    small = jnp.concatenate([small, pad], axis=-1).reshape(lead + (SMALL_ROWS, PACK_COLS))
    return [t["w_in"].astype(dtype), t["w_up"].astype(dtype),
            jnp.concatenate([t[n].astype(dtype) for n in ROWS_GROUP], axis=-2),
            jnp.concatenate([t[n].astype(dtype) for n in COLS_GROUP], axis=-2), small]


def ungroup_shards(arrs, shard_shapes):
    w_in_s, w_up_s, rows, cols, small = arrs
    lead = w_in_s.shape[:-2]
    out = {"w_in": w_in_s, "w_up": w_up_s}
    for group, arr in ((ROWS_GROUP, rows), (COLS_GROUP, cols)):
        off = 0
        for n in group:
            r = shard_shapes[n][0]
            out[n] = arr[..., off:off + r, :]
            off += r
    flat = small.reshape(lead + (-1,))
    off = 0
    for n in SMALL_GROUP:
        size = shard_shapes[n][0] * shard_shapes[n][1]
        out[n] = flat[..., off:off + size].reshape(lead + shard_shapes[n])
        off += size
    return out


class ShardExchange:
    def __init__(self, weights, depth, shard_shapes):
        self.shard_shapes = shard_shapes
        self.packs = [group_shards({n: weights[n][layer] for n in SHARDED}, BF16) for layer in range(depth)]
        self.me = _chip_index()
        self.pending = None
        self.parts = None
        self.shard_grads = [None] * depth
        self.repl_grads = [None] * depth

    def _chips(self, layer, gathered):
        per_chip = [ungroup_shards([jnp.where(self.me == k, pk, g[k]) for g, pk in zip(gathered, self.packs[layer])],
                                   self.shard_shapes) for k in range(N_CHIPS)]
        return {n: [per_chip[k][n] for k in range(N_CHIPS)] for n in SHARDED}

    def first_weights(self):
        return self._chips(0, gather_chip_shards(self.packs[0], "gather_w0"))

    FWD_HOSTS = {"fox": (0, 3, 4), "proj": (1,), "ffn": (2,)}
    BWD_HOSTS = {"dq": (0, 3, 4), "dkv": (1, 2)}

    @staticmethod
    def _split(hosts, arrs, make):
        return {host: make([arrs[t] for t in idx]) for host, idx in hosts.items()}

    @staticmethod
    def _join(hosts, outs):
        arrs = [None] * sum(len(idx) for idx in hosts.values())
        for host, idx in hosts.items():
            for t, arr in zip(idx, outs[host]):
                arrs[t] = arr
        return arrs

    def weights_exchange(self, layer):
        return self._split(self.FWD_HOSTS, self.packs[layer], gather_exchange)

    def weights_arrived(self, layer, outs):
        return self._chips(layer, self._join(self.FWD_HOSTS, outs))

    def grads_ready(self, layer, gr):
        self.repl_grads[layer] = {n: gr[n] for n in REPLICATED}
        self.pending = (layer, group_shards({n: gr[n] for n in SHARDED}, F32))

    def swap_exchange(self):
        return swap_exchange(self.pending[1]) if self.pending is not None else None

    def ici_exchanges(self, swapped):
        if self.pending is None:
            return {}
        layer, slots = self.pending
        self.parts = reduce_scatter_pair_sums(slots, swapped, f"rs{layer}")
        return self._split(self.BWD_HOSTS, self.parts, slot_exchange)

    def ici_arrived(self, outs):
        if self.pending is not None:
            self._finish(self._join(self.BWD_HOSTS, outs))

    def grads_flush(self):
        layer, slots = self.pending
        self.parts = reduce_scatter_pair_sums(slots, swap_halves_with_sibling(slots, f"rs{layer}_swap"), f"rs{layer}")
        self._finish(exchange_slots_between_chips(self.parts, f"rs{layer}_ici"))

    def _finish(self, landed):
        layer = self.pending[0]
        self.shard_grads[layer] = ungroup_shards(reduce_scatter_finish(self.parts, landed, f"rs{layer}"),
                                                 self.shard_shapes)
        self.pending = None


REPL_SHAPES = {"norm_mix_g": (D_MODEL,), "fox_forget_b": (FOX_HEADS,), "sgu_norm_g": (SGU_WIDTH,),
               "sgu_w": (4, SGU_CHUNK, SGU_CHUNK), "sgu_b": (4, SGU_CHUNK), "norm_ffn_g": (D_MODEL,),
               "norm_ple_g": (D_MODEL,)}


def kernel(x, p, positions, norm_mix_g, w_in, fox_forget_b, shortconv_w, sgu_norm_g, sgu_w, sgu_b, w_br_fox, w_br_conv, w_br_sgu, w_br_dil, w_out, norm_ffn_g, w_up, ffn_conv_w, w_down, norm_ple_g, w_ple_gate, w_ple_proj, final_norm_g, loss_target, m_norm_mix_g, m_w_in, m_fox_forget_b, m_shortconv_w, m_sgu_norm_g, m_sgu_w, m_sgu_b, m_w_br_fox, m_w_br_conv, m_w_br_sgu, m_w_br_dil, m_w_out, m_norm_ffn_g, m_w_up, m_ffn_conv_w, m_w_down, m_norm_ple_g, m_w_ple_gate, m_w_ple_proj, m_final_norm_g, v_norm_mix_g, v_w_in, v_fox_forget_b, v_shortconv_w, v_sgu_norm_g, v_sgu_w, v_sgu_b, v_w_br_fox, v_w_br_conv, v_w_br_sgu, v_w_br_dil, v_w_out, v_norm_ffn_g, v_w_up, v_ffn_conv_w, v_w_down, v_norm_ple_g, v_w_ple_gate, v_w_ple_proj, v_final_norm_g):
    weights = dict(norm_mix_g=norm_mix_g, w_in=w_in, fox_forget_b=fox_forget_b, shortconv_w=shortconv_w,
                   sgu_norm_g=sgu_norm_g, sgu_w=sgu_w, sgu_b=sgu_b, w_br_fox=w_br_fox, w_br_conv=w_br_conv,
                   w_br_sgu=w_br_sgu, w_br_dil=w_br_dil, w_out=w_out, norm_ffn_g=norm_ffn_g, w_up=w_up,
                   ffn_conv_w=ffn_conv_w, w_down=w_down, norm_ple_g=norm_ple_g, w_ple_gate=w_ple_gate,
                   w_ple_proj=w_ple_proj, final_norm_g=final_norm_g)
    mom1 = dict(norm_mix_g=m_norm_mix_g, w_in=m_w_in, fox_forget_b=m_fox_forget_b, shortconv_w=m_shortconv_w,
                sgu_norm_g=m_sgu_norm_g, sgu_w=m_sgu_w, sgu_b=m_sgu_b, w_br_fox=m_w_br_fox, w_br_conv=m_w_br_conv,
                w_br_sgu=m_w_br_sgu, w_br_dil=m_w_br_dil, w_out=m_w_out, norm_ffn_g=m_norm_ffn_g, w_up=m_w_up,
                ffn_conv_w=m_ffn_conv_w, w_down=m_w_down, norm_ple_g=m_norm_ple_g, w_ple_gate=m_w_ple_gate,
                w_ple_proj=m_w_ple_proj, final_norm_g=m_final_norm_g)
    mom2 = dict(norm_mix_g=v_norm_mix_g, w_in=v_w_in, fox_forget_b=v_fox_forget_b, shortconv_w=v_shortconv_w,
                sgu_norm_g=v_sgu_norm_g, sgu_w=v_sgu_w, sgu_b=v_sgu_b, w_br_fox=v_w_br_fox, w_br_conv=v_w_br_conv,
                w_br_sgu=v_w_br_sgu, w_br_dil=v_w_br_dil, w_out=v_w_out, norm_ffn_g=v_norm_ffn_g, w_up=v_w_up,
                ffn_conv_w=v_ffn_conv_w, w_down=v_w_down, norm_ple_g=v_norm_ple_g, w_ple_gate=v_w_ple_gate,
                w_ple_proj=v_w_ple_proj, final_norm_g=v_final_norm_g)
    depth = w_in.shape[0]
    shard_shapes = {n: tuple(weights[n].shape[1:]) for n in SHARDED}

    exch = ShardExchange(weights, depth, shard_shapes)
    repl = {n: weights[n] for n in REPLICATED}
    loss_part, grad_x, d_final = local_step(x[0], p[:, 0], positions[0], repl, final_norm_g, loss_target[0], exch)
    loss = lax.psum(loss_part, ("x", "y", "c"))
    grads = exch.repl_grads
    grad_w, deltas, new_m, new_v = {}, {}, {}, {}
    for n in SHARDED:
        grad_w[n], deltas[n], new_m[n], new_v[n] = adamw_layers(
            weights[n], [exch.shard_grads[layer][n] for layer in range(depth)], mom1[n], mom2[n], f"adamw_{n}")

    flat = jnp.concatenate([grads[layer][n].astype(F32).reshape(-1) for layer in range(depth) for n in REPLICATED]
                           + [d_final])
    Rr = _rows_for(flat.shape[0], 16)
    packed = jnp.concatenate([flat, jnp.zeros((Rr * PACK_COLS - flat.shape[0],), F32)]).reshape(Rr, PACK_COLS)
    total = sum_slots(gather_all_devices(packed, "gather_repl"), "sum_repl").reshape(-1)
    off = 0
    g_rep = {n: [] for n in REPLICATED}
    for layer in range(depth):
        for n in REPLICATED:
            size = 1
            for s in REPL_SHAPES[n]:
                size *= s
            g_rep[n].append(total[off:off + size].reshape(REPL_SHAPES[n]))
            off += size
    for n in REPLICATED:
        grad_w[n] = jnp.stack(g_rep[n], axis=0)
    grad_w["final_norm_g"] = total[off:off + D_MODEL]

    for n in REPLICATED + ("final_norm_g",):
        deltas[n], new_m[n], new_v[n] = adamw(weights[n], grad_w[n], mom1[n], mom2[n], f"adamw_{n}")
    return (loss, grad_x[None], *[grad_w[n] for n in WEIGHTS], *[deltas[n] for n in WEIGHTS],
            *[new_m[n] for n in WEIGHTS], *[new_v[n] for n in WEIGHTS])
```

```python
import functools

import jax
import jax.numpy as jnp
from jax import lax
from jax.experimental import pallas as pl
from jax.experimental.pallas import tpu as pltpu

F32 = jnp.float32
BF16 = jnp.bfloat16
MESH = pl.DeviceIdType.MESH
BS = pl.BlockSpec
SDS = jax.ShapeDtypeStruct
ANY = pl.BlockSpec(memory_space=pl.ANY)

VMEM_LIMIT_BYTES = 52 * 1024 * 1024
LANES = 128

D_MODEL = 2048
HEAD_DIM = 64
EPS = 1e-6
NEG = -1e30
FOX_HEADS = 8
FOX_WIDTH = 512
CONV_WIDTH = 512
SGU_WIDTH = 512
SGU_CHUNK = 128
DIL_PATTERNS = ((128, 1), (512, 4), (2048, 16))
DIL_SPAN = 128
DIL_HEADS = 12
DIL_WIDTH = 768
DIL_OUT = 256
ROPE_THETA = 500000.0
ROPE_DIM = 16
N_BRANCH = 4
D_FF = 5632
PLE_DIM = 256
D_IN = 14600

OFF_G, OFF_A, OFF_B, OFF_C, OFF_D, OFF_F, W_AL = 0, 8192, 9728, 11264, 12288, 14592, 14848
SRC_A, SRC_F, SRC_B, SRC_C, SRC_DQ, SRC_G = 0, 1536, 1544, 3080, 4104, 6408

ADAM_LR, ADAM_B1, ADAM_B2, ADAM_EPS, ADAM_WD, ADAM_STEP = 0.001, 0.9, 0.999, 1e-08, 0.01, 10

PACK_COLS = 1024
N_CHIPS = 4
N_DEV = 8

SHARDED = ("w_in", "shortconv_w", "w_br_fox", "w_br_conv", "w_br_sgu", "w_br_dil", "w_out", "w_up",
           "ffn_conv_w", "w_down", "w_ple_gate", "w_ple_proj")
ROW_SHARDED = ("w_out", "w_down", "w_ple_gate")
REPLICATED = ("norm_mix_g", "fox_forget_b", "sgu_norm_g", "sgu_w", "sgu_b", "norm_ffn_g", "norm_ple_g")
WEIGHTS = ("norm_mix_g", "w_in", "fox_forget_b", "shortconv_w", "sgu_norm_g", "sgu_w", "sgu_b", "w_br_fox",
           "w_br_conv", "w_br_sgu", "w_br_dil", "w_out", "norm_ffn_g", "w_up", "ffn_conv_w", "w_down",
           "norm_ple_g", "w_ple_gate", "w_ple_proj", "final_norm_g")


def _params(n_grid):
    return pltpu.CompilerParams(dimension_semantics=("arbitrary",) * n_grid, vmem_limit_bytes=VMEM_LIMIT_BYTES)


def _tile(n, pref, unit=LANES):
    best = None
    t = unit
    while t <= min(n, pref):
        if n % t == 0:
            best = t
        t += unit
    return best if best is not None else n


def _sigmoid(z):
    return 1.0 / (1.0 + jnp.exp(-z))


MAX_RESIDENT_K = 2048


def matmul(a, b, mode, out_dtype, name, res=None, tm=1536, tn=1024, tk=1536, side=None):
    if mode == "tn":
        a, mode = a.astype(BF16).T, "nn"
    if mode == "nn":
        (M, K), (K2, N) = a.shape, b.shape
    else:
        (M, K), (N, K2) = a.shape, b.shape
    assert K == K2, (name, a.shape, b.shape)
    if K <= MAX_RESIDENT_K:
        tk = K
    tm, tn, tk = _tile(M, tm), _tile(N, tn), _tile(K, tk)
    nk = K // tk
    if mode == "nn":
        a_spec, b_spec = BS((tm, tk), lambda i, j, k: (i, k)), BS((tk, tn), lambda i, j, k: (k, j))
        dims = (((1,), (0,)), ((), ()))
    else:
        a_spec, b_spec = BS((tm, tk), lambda i, j, k: (i, k)), BS((tn, tk), lambda i, j, k: (j, k))
        dims = (((1,), (1,)), ((), ()))
    has_res = res is not None

    def body(*refs):
        if has_res:
            a_ref, b_ref, r_ref, o_ref, acc = refs
        else:
            a_ref, b_ref, o_ref, acc = refs
        k = pl.program_id(2)

        @pl.when(k == 0)
        def _():
            acc[...] = jnp.zeros_like(acc)

        acc[...] += lax.dot_general(a_ref[...].astype(BF16), b_ref[...].astype(BF16), dims,
                                    preferred_element_type=F32)

        @pl.when(k == nk - 1)
        def _():
            r = acc[...]
            if has_res:
                r = r + r_ref[...]
            o_ref[...] = r.astype(o_ref.dtype)

    in_specs = [a_spec, b_spec]
    args = [a, b]
    if has_res:
        in_specs.append(BS((tm, tn), lambda i, j, k: (i, j)))
        args.append(res)
    (out,), side_out = _call_with_side(
        body, side, name, (M // tm, N // tn, nk), in_specs, [BS((tm, tn), lambda i, j, k: (i, j))],
        [SDS((M, N), out_dtype)], args, scratch=[pltpu.VMEM((tm, tn), F32)])
    return out if side is None else (out, side_out)


def rmsnorm_fwd(x, g, name):
    S, Dm = x.shape
    tm = _tile(S, 256, 8)

    def body(x_ref, g_ref, y_ref):
        xf = x_ref[...]
        r = lax.rsqrt(jnp.mean(xf * xf, axis=-1, keepdims=True) + EPS)
        y_ref[...] = ((xf * r) * g_ref[...]).astype(y_ref.dtype)

    return pl.pallas_call(
        body, name=name, grid=(S // tm,),
        in_specs=[BS((tm, Dm), lambda i: (i, 0)), BS((1, Dm), lambda i: (0, 0))],
        out_specs=BS((tm, Dm), lambda i: (i, 0)), out_shape=SDS((S, Dm), BF16),
        compiler_params=_params(1))(x, g.reshape(1, Dm))


def rmsnorm_bwd(x, g, dy, dres, name):
    S, Dm = x.shape
    tm = _tile(S, 256, 8)

    def body(x_ref, g_ref, dy_ref, dres_ref, dx_ref, dg_ref):
        xf = x_ref[...]
        r = lax.rsqrt(jnp.mean(xf * xf, axis=-1, keepdims=True) + EPS)
        xh = xf * r
        dy = dy_ref[...].astype(F32)
        dxh = dy * g_ref[...]
        dx_ref[...] = r * (dxh - xh * jnp.mean(dxh * xh, axis=-1, keepdims=True)) + dres_ref[...]

        @pl.when(pl.program_id(0) == 0)
        def _():
            dg_ref[...] = jnp.zeros_like(dg_ref)

        dg_ref[...] += jnp.sum(dy * xh, axis=0, keepdims=True)

    row = BS((tm, Dm), lambda i: (i, 0))
    vec = BS((1, Dm), lambda i: (0, 0))
    return pl.pallas_call(
        body, name=name, grid=(S // tm,), in_specs=[row, vec, row, row], out_specs=[row, vec],
        out_shape=[SDS((S, Dm), F32), SDS((1, Dm), F32)], compiler_params=_params(1))(x, g.reshape(1, Dm), dy, dres)


def final_loss(x, g, target, name):
    S, Dm = x.shape
    tm = _tile(S, 256, 8)

    def body(x_ref, g_ref, t_ref, loss_ref, dx_ref, dg_ref):
        xf = x_ref[...]
        r = lax.rsqrt(jnp.mean(xf * xf, axis=-1, keepdims=True) + EPS)
        xh = xf * r
        gv = g_ref[...]
        err = xh * gv - t_ref[...]
        dy = err * (1.0 / Dm)
        dxh = dy * gv
        dx_ref[...] = r * (dxh - xh * jnp.mean(dxh * xh, axis=-1, keepdims=True))

        @pl.when(pl.program_id(0) == 0)
        def _():
            dg_ref[...] = jnp.zeros_like(dg_ref)
            loss_ref[...] = jnp.zeros_like(loss_ref)

        dg_ref[...] += jnp.sum(dy * xh, axis=0, keepdims=True)
        part = 0.5 * jnp.sum(jnp.mean(err * err, axis=-1, keepdims=True), axis=0, keepdims=True)
        loss_ref[...] += jnp.broadcast_to(part, loss_ref.shape)

    row = BS((tm, Dm), lambda i: (i, 0))
    vec = BS((1, Dm), lambda i: (0, 0))
    return pl.pallas_call(
        body, name=name, grid=(S // tm,), in_specs=[row, vec, row],
        out_specs=[BS((1, LANES), lambda i: (0, 0)), row, vec],
        out_shape=[SDS((1, LANES), F32), SDS((S, Dm), F32), SDS((1, Dm), F32)],
        compiler_params=_params(1))(x, g.reshape(1, Dm), target)


def _dot_f32(a, b):
    return jnp.dot(a, b, preferred_element_type=F32, precision=lax.Precision.HIGHEST)


def _dot(a, b):
    return jnp.dot(a, b, preferred_element_type=F32)


def _dot_nt(a, b):
    return lax.dot_general(a, b, (((1,), (1,)), ((), ())), preferred_element_type=F32)


def fox_prep_fwd(proj, bias, name):
    S = proj.shape[0]
    H = FOX_HEADS
    nc = S // LANES

    def body(p_ref, b_ref, a_ref, f_ref):
        row = lax.broadcasted_iota(jnp.int32, (LANES, LANES), 0)
        col = lax.broadcasted_iota(jnp.int32, (LANES, LANES), 1)
        upper = (row <= col).astype(F32)
        carry = jnp.zeros((H, 1), F32)
        for c in range(nc):
            sl = slice(c * LANES, (c + 1) * LANES)
            logits = p_ref[sl, :].T[0:H, :]
            a_ref[:, sl] = logits
            z = logits + b_ref[...]
            chunk = jnp.minimum(z, 0.0) - jnp.log(1.0 + jnp.exp(-jnp.abs(z)))
            f_ref[:, sl] = _dot_f32(chunk, upper) + carry
            carry = carry + jnp.sum(chunk, axis=1, keepdims=True)

    full = BS((H, S), lambda i: (0, 0))
    return pl.pallas_call(
        body, name=name, grid=(1,),
        in_specs=[BS((S, LANES), lambda i: (0, OFF_F // LANES)), BS((H, 1), lambda i: (0, 0))],
        out_specs=[full, full], out_shape=[SDS((H, S), F32)] * 2, compiler_params=_params(1))(proj, bias)


def fox_prep_bwd(af_t, bias, d_fq, d_fk, name):
    H, S = af_t.shape
    nc = S // LANES

    def body(a_ref, b_ref, dfq_ref, dfk_ref, da_ref, db_ref):
        row = lax.broadcasted_iota(jnp.int32, (LANES, LANES), 0)
        col = lax.broadcasted_iota(jnp.int32, (LANES, LANES), 1)
        lower = (row >= col).astype(F32)
        carry = jnp.zeros((H, 1), F32)
        dbias = jnp.zeros((H, 1), F32)
        for c in range(nc - 1, -1, -1):
            sl = slice(c * LANES, (c + 1) * LANES)
            chunk = dfq_ref[:, sl] + dfk_ref[:, sl]
            dlogf = _dot_f32(chunk, lower) + carry
            carry = carry + jnp.sum(chunk, axis=1, keepdims=True)
            z = a_ref[:, sl] + b_ref[...]
            da = dlogf * _sigmoid(-z)
            da_ref[:, sl] = da
            dbias = dbias + jnp.sum(da, axis=1, keepdims=True)
        db_ref[...] = dbias

    full = BS((H, S), lambda i: (0, 0))
    vec = BS((H, 1), lambda i: (0, 0))
    return pl.pallas_call(
        body, name=name, grid=(1,), in_specs=[full, vec, full, full], out_specs=[full, vec],
        out_shape=[SDS((H, S), F32), SDS((H, 1), F32)], compiler_params=_params(1))(af_t, bias, d_fq, d_fk)


FOX_T = 256
FOX_HP = 4


def _causal_tile(T):
    return lax.broadcasted_iota(jnp.int32, (T, T), 1) <= lax.broadcasted_iota(jnp.int32, (T, T), 0)


class SideExchange:
    def __init__(self, operands, out_shapes, n_sems, start, finish):
        self.operands, self.out_shapes, self.n_sems, self.start, self.finish = operands, out_shapes, n_sems, start, finish


def _call_with_side(body, side, name, grid, in_specs, out_specs, out_shape, args, scratch=()):
    scratch = list(scratch)
    if side is None:
        return pl.pallas_call(body, name=name, grid=grid, in_specs=in_specs, out_specs=out_specs,
                              out_shape=out_shape, scratch_shapes=scratch,
                              compiler_params=_params(len(grid)))(*args), []
    n_in, n_out = len(in_specs), len(out_specs)
    s_in, s_out = len(side.operands), len(side.out_shapes)

    def wrapped(*refs):
        main_in, side_in = refs[:n_in], refs[n_in:n_in + s_in]
        main_out = refs[n_in + s_in:n_in + s_in + n_out]
        side_out = refs[n_in + s_in + n_out:n_in + s_in + n_out + s_out]
        main_scratch = refs[n_in + s_in + n_out + s_out:-2]
        send_sems, recv_sems = refs[-2:]
        first = pl.program_id(0) == 0
        last = pl.program_id(0) == grid[0] - 1
        for axis in range(1, len(grid)):
            first = jnp.logical_and(first, pl.program_id(axis) == 0)
            last = jnp.logical_and(last, pl.program_id(axis) == grid[axis] - 1)

        @pl.when(first)
        def _():
            side.start(side_in, side_out, send_sems, recv_sems)

        body(*main_in, *main_out, *main_scratch)

        @pl.when(last)
        def _():
            side.finish(side_in, side_out, send_sems, recv_sems)

    outs = pl.pallas_call(
        wrapped, name=name, grid=grid, in_specs=list(in_specs) + [ANY] * s_in,
        out_specs=list(out_specs) + [ANY] * s_out, out_shape=list(out_shape) + list(side.out_shapes),
        scratch_shapes=scratch + [pltpu.SemaphoreType.DMA((side.n_sems,)), pltpu.SemaphoreType.DMA((side.n_sems,))],
        compiler_params=_params(len(grid)))(*args, *side.operands)
    return outs[:n_out], outs[n_out:]


def _fox_specs(H, S, Dh, T):
    nq = S // T
    blk = BS((FOX_HP, T, Dh), lambda h, i: (h, i, 0))
    full = BS((FOX_HP, S, Dh), lambda h, i: (h, 0, 0))
    colb = BS((FOX_HP, T, 1), lambda h, i: (h, i, 0))
    rowf = BS((FOX_HP, nq, 1, T), lambda h, i: (h, 0, 0, 0))
    return blk, full, colb, rowf, (H // FOX_HP, nq)


def fox_attn_fwd(q, k, v, f_col, f_row, name, side=None):
    H, S, Dh = q.shape
    T = min(FOX_T, S)
    scale = Dh ** -0.5

    def body(q_ref, k_ref, v_ref, fc_ref, fr_ref, o_ref, lse_ref):
        qi = pl.program_id(1)
        qs = [q_ref[h] for h in range(FOX_HP)]
        fqs = [fc_ref[h] for h in range(FOX_HP)]

        def step(j, carry, diagonal):
            off = pl.multiple_of(j * T, T)
            out = []
            for h in range(FOX_HP):
                m, l, acc = carry[h]
                kv = k_ref[h, pl.ds(off, T), :]
                vv = v_ref[h, pl.ds(off, T), :]
                s = _dot_nt(qs[h], kv) * scale + (fqs[h] - fr_ref[h, j])
                if diagonal:
                    s = jnp.where(_causal_tile(T), s, NEG)
                m_new = jnp.maximum(m, jnp.max(s, axis=-1, keepdims=True))
                p = jnp.exp(s - m_new)
                alpha = jnp.exp(m - m_new)
                l = alpha * l + jnp.sum(p, axis=-1, keepdims=True)
                acc = alpha * acc + _dot(p.astype(BF16), vv)
                out.append((m_new, l, acc))
            return tuple(out)

        init = tuple((jnp.full((T, 1), NEG, F32), jnp.zeros((T, 1), F32), jnp.zeros((T, Dh), F32))
                     for _ in range(FOX_HP))
        carry = lax.fori_loop(0, qi, functools.partial(step, diagonal=False), init)
        carry = step(qi, carry, True)
        for h in range(FOX_HP):
            m, l, acc = carry[h]
            o_ref[h] = (acc / l).astype(o_ref.dtype)
            lse_ref[h] = m + jnp.log(l)

    blk, full, colb, rowf, grid = _fox_specs(H, S, Dh, T)
    return _call_with_side(body, side, name, grid, [blk, full, full, colb, rowf], [blk, colb],
                           [SDS((H, S, Dh), BF16), SDS((H, S, 1), F32)], (q, k, v, f_col, f_row))


def fox_attn_bwd_dq(q, k, v, f_col, f_row, o, lse, do, name, side=None):
    H, S, Dh = q.shape
    T = min(FOX_T, S)
    scale = Dh ** -0.5

    def body(q_ref, k_ref, v_ref, fc_ref, fr_ref, o_ref, lse_ref, do_ref, dq_ref, dl_ref, df_ref):
        qi = pl.program_id(1)
        qs = [q_ref[h] for h in range(FOX_HP)]
        fqs = [fc_ref[h] for h in range(FOX_HP)]
        lses = [lse_ref[h] for h in range(FOX_HP)]
        dos = [do_ref[h] for h in range(FOX_HP)]
        deltas = [jnp.sum(dos[h].astype(F32) * o_ref[h].astype(F32), axis=-1, keepdims=True) for h in range(FOX_HP)]
        for h in range(FOX_HP):
            dl_ref[h] = deltas[h]

        def step(j, carry, diagonal):
            off = pl.multiple_of(j * T, T)
            out = []
            for h in range(FOX_HP):
                dq, dfq = carry[h]
                kv = k_ref[h, pl.ds(off, T), :]
                vv = v_ref[h, pl.ds(off, T), :]
                s = _dot_nt(qs[h], kv) * scale + (fqs[h] - fr_ref[h, j])
                if diagonal:
                    s = jnp.where(_causal_tile(T), s, NEG)
                p = jnp.exp(s - lses[h])
                ds = p * (_dot_nt(dos[h], vv) - deltas[h])
                out.append((dq + _dot(ds.astype(BF16), kv), dfq + jnp.sum(ds, axis=-1, keepdims=True)))
            return tuple(out)

        init = tuple((jnp.zeros((T, Dh), F32), jnp.zeros((T, 1), F32)) for _ in range(FOX_HP))
        carry = lax.fori_loop(0, qi, functools.partial(step, diagonal=False), init)
        carry = step(qi, carry, True)
        for h in range(FOX_HP):
            dq_ref[h] = carry[h][0] * scale
            df_ref[h] = carry[h][1]

    blk, full, colb, rowf, grid = _fox_specs(H, S, Dh, T)
    return _call_with_side(body, side, name, grid, [blk, full, full, colb, rowf, blk, colb, blk], [blk, colb, colb],
                           [SDS((H, S, Dh), F32), SDS((H, S, 1), F32), SDS((H, S, 1), F32)],
                           (q, k, v, f_col, f_row, o, lse, do))


def fox_attn_bwd_dkv(q, k, v, f_col, f_row, lse_row, delta_row, do, name, side=None):
    H, S, Dh = q.shape
    T = min(FOX_T, S)
    nq = S // T
    scale = Dh ** -0.5

    def body(q_ref, k_ref, v_ref, fc_ref, fr_ref, lse_ref, dl_ref, do_ref, dk_ref, dv_ref, df_ref):
        kj = pl.program_id(1)
        ks = [k_ref[h] for h in range(FOX_HP)]
        vs = [v_ref[h] for h in range(FOX_HP)]
        fks = [fc_ref[h] for h in range(FOX_HP)]

        def step(i, carry, diagonal):
            off = pl.multiple_of(i * T, T)
            out = []
            for h in range(FOX_HP):
                dk, dv, dfk = carry[h]
                qv = q_ref[h, pl.ds(off, T), :]
                dov = do_ref[h, pl.ds(off, T), :]
                st = _dot_nt(ks[h], qv) * scale + (fr_ref[h, i] - fks[h])
                if diagonal:
                    st = jnp.where(lax.broadcasted_iota(jnp.int32, (T, T), 0)
                                   <= lax.broadcasted_iota(jnp.int32, (T, T), 1), st, NEG)
                pt = jnp.exp(st - lse_ref[h, i])
                dv = dv + _dot(pt.astype(BF16), dov)
                dst = pt * (_dot_nt(vs[h], dov) - dl_ref[h, i])
                dk = dk + _dot(dst.astype(BF16), qv)
                out.append((dk, dv, dfk + jnp.sum(dst, axis=-1, keepdims=True)))
            return tuple(out)

        init = tuple((jnp.zeros((T, Dh), F32), jnp.zeros((T, Dh), F32), jnp.zeros((T, 1), F32))
                     for _ in range(FOX_HP))
        carry = step(kj, init, True)
        carry = lax.fori_loop(kj + 1, nq, functools.partial(step, diagonal=False), carry)
        for h in range(FOX_HP):
            dk_ref[h] = carry[h][0] * scale
            dv_ref[h] = carry[h][1]
            df_ref[h] = -carry[h][2]

    blk, full, colb, rowf, grid = _fox_specs(H, S, Dh, T)
    return _call_with_side(body, side, name, grid, [full, blk, blk, colb, rowf, rowf, rowf, full], [blk, blk, colb],
                           [SDS((H, S, Dh), F32), SDS((H, S, Dh), F32), SDS((H, S, 1), F32)],
                           (q, k, v, f_col, f_row, lse_row, delta_row, do))


HALO = 8
CONV_TM = 512


def _conv_ext(e, w_ref):
    return w_ref[0:1, :] * pltpu.roll(e, 2, 0) + w_ref[1:2, :] * pltpu.roll(e, 1, 0) + w_ref[2:3, :] * e


def _conv_t_ext(d, w_ref):
    n = d.shape[0]
    return w_ref[2:3, :] * d + w_ref[1:2, :] * pltpu.roll(d, n - 1, 0) + w_ref[0:1, :] * pltpu.roll(d, n - 2, 0)


def _time_specs(S, tm, width, col_block):
    per = tm // HALO
    last = S // HALO - 1
    cur = BS((tm, width), lambda j, i: (i, col_block(j)))
    prev = BS((HALO, width), lambda j, i: (jnp.maximum(i * per - 1, 0), col_block(j)))
    nxt = BS((HALO, width), lambda j, i: (jnp.minimum((i + 1) * per, last), col_block(j)))
    return cur, prev, nxt


def shortconv_fwd(proj, w, name):
    S = proj.shape[0]
    tm = _tile(S, CONV_TM, 8)
    nb = CONV_WIDTH // LANES
    b0 = OFF_B // LANES

    def body(xb_ref, xbp_ref, gb_ref, gc_ref, gcp_ref, w_ref, o_ref):
        keep = (pl.program_id(1) > 0).astype(F32)
        e = jnp.concatenate([gcp_ref[...] * xbp_ref[...] * keep, gc_ref[...] * xb_ref[...]], axis=0)
        o_ref[...] = (gb_ref[...] * _conv_ext(e, w_ref)[HALO:, :]).astype(o_ref.dtype)

    xb, xbp, _ = _time_specs(S, tm, LANES, lambda j: b0 + j)
    gb, _, _ = _time_specs(S, tm, LANES, lambda j: b0 + nb + j)
    gc, gcp, _ = _time_specs(S, tm, LANES, lambda j: b0 + 2 * nb + j)
    return pl.pallas_call(
        body, name=name, grid=(nb, S // tm),
        in_specs=[xb, xbp, gb, gc, gcp, BS((3, LANES), lambda j, i: (0, j))],
        out_specs=BS((tm, LANES), lambda j, i: (i, j)), out_shape=SDS((S, CONV_WIDTH), BF16),
        compiler_params=_params(2))(proj, proj, proj, proj, proj, w)


def shortconv_bwd(proj, w, do_b, name):
    S = proj.shape[0]
    tm = _tile(S, CONV_TM, 8)
    nt = S // tm
    nb = CONV_WIDTH // LANES
    b0 = OFF_B // LANES

    def body(xb_ref, xbp_ref, gb_ref, gbn_ref, gc_ref, gcp_ref, do_ref, don_ref, w_ref,
             dxb_ref, dgb_ref, dgc_ref, dw_ref):
        i = pl.program_id(1)
        keep_prev = (i > 0).astype(F32)
        keep_next = (i < nt - 1).astype(F32)
        xb, gb, gc = xb_ref[...], gb_ref[...], gc_ref[...]
        do = do_ref[...].astype(F32)
        u = gc * xb
        e = jnp.concatenate([gcp_ref[...] * xbp_ref[...] * keep_prev, u], axis=0)
        dgb_ref[...] = (do * _conv_ext(e, w_ref)[HALO:, :]).astype(dgb_ref.dtype)
        dcv = do * gb
        d_ext = jnp.concatenate([dcv, don_ref[...].astype(F32) * gbn_ref[...] * keep_next], axis=0)
        du = _conv_t_ext(d_ext, w_ref)[:tm, :]
        dgc_ref[...] = (du * xb).astype(dgc_ref.dtype)
        dxb_ref[...] = (du * gc).astype(dxb_ref.dtype)

        @pl.when(i == 0)
        def _():
            dw_ref[...] = jnp.zeros_like(dw_ref)

        dw_ref[0:1, :] += jnp.sum(dcv * pltpu.roll(e, 2, 0)[HALO:, :], axis=0, keepdims=True)
        dw_ref[1:2, :] += jnp.sum(dcv * pltpu.roll(e, 1, 0)[HALO:, :], axis=0, keepdims=True)
        dw_ref[2:3, :] += jnp.sum(dcv * u, axis=0, keepdims=True)

    xb, xbp, _ = _time_specs(S, tm, LANES, lambda j: b0 + j)
    gb, _, gbn = _time_specs(S, tm, LANES, lambda j: b0 + nb + j)
    gc, gcp, _ = _time_specs(S, tm, LANES, lambda j: b0 + 2 * nb + j)
    do, _, don = _time_specs(S, tm, LANES, lambda j: j)
    out = BS((tm, LANES), lambda j, i: (i, j))
    wspec = BS((3, LANES), lambda j, i: (0, j))
    return pl.pallas_call(
        body, name=name, grid=(nb, nt), in_specs=[xb, xbp, gb, gbn, gc, gcp, do, don, wspec],
        out_specs=[out, out, out, wspec],
        out_shape=[SDS((S, CONV_WIDTH), BF16)] * 3 + [SDS((3, CONV_WIDTH), F32)],
        compiler_params=_params(2))(proj, proj, proj, proj, proj, proj, do_b, do_b, w)


_GELU_C = 0.7978845608028654


def _gelu(x):
    return 0.5 * x * (1.0 + jnp.tanh(_GELU_C * (x + 0.044715 * x * x * x)))


def _gelu_grad(x):
    t = jnp.tanh(_GELU_C * (x + 0.044715 * x * x * x))
    return 0.5 * (1.0 + t) + 0.5 * x * (1.0 - t * t) * _GELU_C * (1.0 + 3.0 * 0.044715 * x * x)


def _tril_masks():
    row = lax.broadcasted_iota(jnp.int32, (SGU_CHUNK, SGU_CHUNK), 0)
    col = lax.broadcasted_iota(jnp.int32, (SGU_CHUNK, SGU_CHUNK), 1)
    return row >= col, row <= col


def _lane_column(mat, g):
    lane = lax.broadcasted_iota(jnp.int32, mat.shape, 1)
    return jnp.sum(jnp.where(lane == g, mat, 0.0), axis=-1, keepdims=True)


def sgu_fwd(proj, norm_g, w_s, b_t, name):
    S = proj.shape[0]
    T = SGU_CHUNK
    G = SGU_WIDTH // T
    c0 = OFF_C // (2 * SGU_WIDTH)

    def body(c_ref, g_ref, w_ref, b_ref, o_ref):
        u = _gelu(c_ref[:, 0:SGU_WIDTH])
        v = _gelu(c_ref[:, SGU_WIDTH:2 * SGU_WIDTH])
        r = lax.rsqrt(jnp.mean(v * v, axis=-1, keepdims=True) + EPS)
        vn = ((v * r) * g_ref[...]).astype(BF16)
        mask, _ = _tril_masks()
        bias = b_ref[...]
        for g in range(G):
            sl = slice(g * T, (g + 1) * T)
            wt = jnp.where(mask, w_ref[g], 0.0).astype(BF16)
            mixed = _dot(wt, vn[:, sl]) + _lane_column(bias, g)
            o_ref[:, sl] = (u[:, sl] * mixed).astype(o_ref.dtype)

    return pl.pallas_call(
        body, name=name, grid=(S // T,),
        in_specs=[BS((T, 2 * SGU_WIDTH), lambda i: (i, c0)), BS((1, SGU_WIDTH), lambda i: (0, 0)),
                  BS((G, T, T), lambda i: (0, 0, 0)), BS((T, T), lambda i: (0, 0))],
        out_specs=BS((T, SGU_WIDTH), lambda i: (i, 0)), out_shape=SDS((S, SGU_WIDTH), BF16),
        compiler_params=_params(1))(proj, norm_g, w_s, b_t)


def sgu_bwd(proj, norm_g, w_s, w_st, b_t, do_c, name):
    S = proj.shape[0]
    T = SGU_CHUNK
    G = SGU_WIDTH // T
    c0 = OFF_C // (2 * SGU_WIDTH)

    def body(c_ref, g_ref, w_ref, wt_ref, b_ref, do_ref, dc_ref, dg_ref, dw_ref, db_ref):
        cu = c_ref[:, 0:SGU_WIDTH]
        cv = c_ref[:, SGU_WIDTH:2 * SGU_WIDTH]
        u = _gelu(cu)
        v = _gelu(cv)
        r = lax.rsqrt(jnp.mean(v * v, axis=-1, keepdims=True) + EPS)
        xh = v * r
        gv = g_ref[...]
        vn = (xh * gv).astype(BF16)
        do = do_ref[...].astype(F32)
        mask, mask_t = _tril_masks()
        bias = b_ref[...]
        lane = lax.broadcasted_iota(jnp.int32, (T, T), 1)

        @pl.when(pl.program_id(0) == 0)
        def _():
            dg_ref[...] = jnp.zeros_like(dg_ref)
            dw_ref[...] = jnp.zeros_like(dw_ref)
            db_ref[...] = jnp.zeros_like(db_ref)

        dvn_parts = []
        du_parts = []
        dbias = jnp.zeros((T, T), F32)
        for g in range(G):
            sl = slice(g * T, (g + 1) * T)
            wt = jnp.where(mask, w_ref[g], 0.0).astype(BF16)
            mixed = _dot(wt, vn[:, sl]) + _lane_column(bias, g)
            du_parts.append(do[:, sl] * mixed)
            dmix = do[:, sl] * u[:, sl]
            dmix_b = dmix.astype(BF16)
            dw_ref[g] += jnp.where(mask, _dot_nt(dmix_b, vn[:, sl]), 0.0)
            dbias = dbias + jnp.where(lane == g, jnp.sum(dmix, axis=-1, keepdims=True), 0.0)
            wtt = jnp.where(mask_t, wt_ref[g], 0.0).astype(BF16)
            dvn_parts.append(_dot(wtt, dmix_b))
        db_ref[...] += dbias
        dvn = jnp.concatenate(dvn_parts, axis=-1)
        du = jnp.concatenate(du_parts, axis=-1)
        dg_ref[...] += jnp.sum(dvn * xh, axis=0, keepdims=True)
        dxh = dvn * gv
        dv = r * (dxh - xh * jnp.mean(dxh * xh, axis=-1, keepdims=True))
        dc_ref[:, 0:SGU_WIDTH] = (du * _gelu_grad(cu)).astype(dc_ref.dtype)
        dc_ref[:, SGU_WIDTH:2 * SGU_WIDTH] = (dv * _gelu_grad(cv)).astype(dc_ref.dtype)

    wspec = BS((G, T, T), lambda i: (0, 0, 0))
    gspec = BS((1, SGU_WIDTH), lambda i: (0, 0))
    return pl.pallas_call(
        body, name=name, grid=(S // T,),
        in_specs=[BS((T, 2 * SGU_WIDTH), lambda i: (i, c0)), gspec, wspec, wspec, BS((T, T), lambda i: (0, 0)),
                  BS((T, SGU_WIDTH), lambda i: (i, 0))],
        out_specs=[BS((T, 2 * SGU_WIDTH), lambda i: (i, 0)), gspec, wspec, BS((T, T), lambda i: (0, 0))],
        out_shape=[SDS((S, 2 * SGU_WIDTH), BF16), SDS((1, SGU_WIDTH), F32), SDS((G, T, T), F32), SDS((T, T), F32)],
        compiler_params=_params(1))(proj, norm_g, w_s, w_st, b_t, do_c)


def rope_tables(positions):
    half = ROPE_DIM // 2
    inv = ROPE_THETA ** (-jnp.arange(half, dtype=F32) * (2.0 / ROPE_DIM))
    ang = positions.astype(F32)[:, None] * inv
    cos, sin = jnp.cos(ang), jnp.sin(ang)
    S = positions.shape[0]
    ones = jnp.ones((S, HEAD_DIM - ROPE_DIM), F32)
    zeros = jnp.zeros((S, HEAD_DIM - ROPE_DIM), F32)
    zh = jnp.zeros((S, half), F32)
    c = jnp.concatenate([cos, cos, ones], axis=-1)
    sa = jnp.concatenate([zh, sin, zeros], axis=-1)
    sb = jnp.concatenate([-sin, zh, zeros], axis=-1)
    tile2 = lambda t: jnp.concatenate([t, t], axis=-1)
    return tile2(c), tile2(sa), tile2(sb)


def rope_apply(x, col0, ncols, cos, sa, sb, out_dtype, name):
    S = x.shape[0]
    tm = _tile(S, 512, 8)
    half = ROPE_DIM // 2
    b0 = col0 // LANES

    def body(x_ref, c_ref, sa_ref, sb_ref, o_ref):
        xv = x_ref[...].astype(F32)
        o_ref[...] = (xv * c_ref[...] + pltpu.roll(xv, half, 1) * sa_ref[...]
                      + pltpu.roll(xv, LANES - half, 1) * sb_ref[...]).astype(o_ref.dtype)

    tab = BS((tm, LANES), lambda i, j: (i, 0))
    return pl.pallas_call(
        body, name=name, grid=(S // tm, ncols // LANES),
        in_specs=[BS((tm, LANES), lambda i, j: (i, b0 + j)), tab, tab, tab],
        out_specs=BS((tm, LANES), lambda i, j: (i, j)), out_shape=SDS((S, ncols), out_dtype),
        compiler_params=_params(2))(x, cos, sa, sb)


def _dil_masks(first):
    qi = lax.broadcasted_iota(jnp.int32, (DIL_SPAN, DIL_SPAN), 0)
    ki = lax.broadcasted_iota(jnp.int32, (DIL_SPAN, DIL_SPAN), 1)
    return ki >= qi + first.astype(jnp.int32) * DIL_SPAN, ki <= qi


def dil_attn_fwd(q, k, v, nb, name):
    H, S, Dh = q.shape
    T = DIL_SPAN
    nblk = S // T
    scale = Dh ** -0.5

    def body(q_ref, kp_ref, kc_ref, vp_ref, vc_ref, o_ref, lse_ref):
        mp, mc = _dil_masks(pl.program_id(0) % nb == 0)
        for h in range(H):
            qv = q_ref[h]
            sp = jnp.where(mp, _dot_nt(qv, kp_ref[h]) * scale, NEG)
            sc = jnp.where(mc, _dot_nt(qv, kc_ref[h]) * scale, NEG)
            m = jnp.maximum(jnp.max(sp, axis=-1, keepdims=True), jnp.max(sc, axis=-1, keepdims=True))
            ep = jnp.exp(sp - m)
            ec = jnp.exp(sc - m)
            l = jnp.sum(ep, axis=-1, keepdims=True) + jnp.sum(ec, axis=-1, keepdims=True)
            o_ref[h] = (_dot(ep.astype(BF16), vp_ref[h]) + _dot(ec.astype(BF16), vc_ref[h])) / l
            lse_ref[h] = m + jnp.log(l)

    cur = BS((H, T, Dh), lambda b: (0, b, 0))
    prev = BS((H, T, Dh), lambda b: (0, jnp.maximum(b - 1, 0), 0))
    colb = BS((H, T, 1), lambda b: (0, b, 0))
    return pl.pallas_call(
        body, name=name, grid=(nblk,), in_specs=[cur, prev, cur, prev, cur], out_specs=[cur, colb],
        out_shape=[SDS((H, S, Dh), F32), SDS((H, S, 1), F32)], compiler_params=_params(1))(q, k, k, v, v)


def dil_attn_bwd_dq(q, k, v, o, lse, do, dlse, nb, name):
    H, S, Dh = q.shape
    T = DIL_SPAN
    nblk = S // T
    scale = Dh ** -0.5

    def body(q_ref, kp_ref, kc_ref, vp_ref, vc_ref, o_ref, lse_ref, do_ref, dlse_ref, dq_ref, dl_ref):
        mp, mc = _dil_masks(pl.program_id(0) % nb == 0)
        for h in range(H):
            qv = q_ref[h]
            kp, kc = kp_ref[h], kc_ref[h]
            dov = do_ref[h]
            delta = jnp.sum(dov * o_ref[h], axis=-1, keepdims=True) - dlse_ref[h]
            dl_ref[h] = delta
            dob = dov.astype(BF16)
            lse_v = lse_ref[h]
            pp = jnp.exp(jnp.where(mp, _dot_nt(qv, kp) * scale, NEG) - lse_v)
            pc = jnp.exp(jnp.where(mc, _dot_nt(qv, kc) * scale, NEG) - lse_v)
            dsp = pp * (_dot_nt(dob, vp_ref[h]) - delta)
            dsc = pc * (_dot_nt(dob, vc_ref[h]) - delta)
            dq_ref[h] = (_dot(dsp.astype(BF16), kp) + _dot(dsc.astype(BF16), kc)) * scale

    cur = BS((H, T, Dh), lambda b: (0, b, 0))
    prev = BS((H, T, Dh), lambda b: (0, jnp.maximum(b - 1, 0), 0))
    colb = BS((H, T, 1), lambda b: (0, b, 0))
    return pl.pallas_call(
        body, name=name, grid=(nblk,), in_specs=[cur, prev, cur, prev, cur, cur, colb, cur, colb],
        out_specs=[cur, colb], out_shape=[SDS((H, S, Dh), F32), SDS((H, S, 1), F32)],
        compiler_params=_params(1))(q, k, k, v, v, o, lse, do, dlse)


def dil_attn_bwd_dkv(q, k, v, lse_row, delta_row, do, nb, name):
    H, S, Dh = q.shape
    T = DIL_SPAN
    nblk = S // T
    scale = Dh ** -0.5

    def body(k_ref, v_ref, qc_ref, qn_ref, doc_ref, don_ref, lc_ref, ln_ref, dc_ref, dn_ref, dk_ref, dv_ref):
        no_next = ((pl.program_id(0) + 1) % nb == 0).astype(jnp.int32)
        si = lax.broadcasted_iota(jnp.int32, (T, T), 0)
        ti = lax.broadcasted_iota(jnp.int32, (T, T), 1)
        m_cur = si <= ti
        m_next = si >= ti + no_next * T
        for h in range(H):
            kv, vv = k_ref[h], v_ref[h]
            qc, qn = qc_ref[h], qn_ref[h]
            doc, don = doc_ref[h].astype(BF16), don_ref[h].astype(BF16)
            pt = jnp.exp(jnp.where(m_cur, _dot_nt(kv, qc) * scale, NEG) - lc_ref[h])
            ptn = jnp.exp(jnp.where(m_next, _dot_nt(kv, qn) * scale, NEG) - ln_ref[h])
            dv_ref[h] = _dot(pt.astype(BF16), doc) + _dot(ptn.astype(BF16), don)
            dst = pt * (_dot_nt(vv, doc) - dc_ref[h])
            dstn = ptn * (_dot_nt(vv, don) - dn_ref[h])
            dk_ref[h] = (_dot(dst.astype(BF16), qc) + _dot(dstn.astype(BF16), qn)) * scale

    cur = BS((H, T, Dh), lambda b: (0, b, 0))
    nxt = BS((H, T, Dh), lambda b: (0, jnp.minimum(b + 1, nblk - 1), 0))
    rcur = BS((H, None, 1, T), lambda b: (0, b, 0, 0))
    rnxt = BS((H, None, 1, T), lambda b: (0, jnp.minimum(b + 1, nblk - 1), 0, 0))
    return pl.pallas_call(
        body, name=name, grid=(nblk,), in_specs=[cur, cur, cur, nxt, cur, nxt, rcur, rnxt, rcur, rnxt],
        out_specs=[cur, cur], out_shape=[SDS((H, S, Dh), F32), SDS((H, S, Dh), F32)],
        compiler_params=_params(1))(k, v, q, q, do, do, lse_row, lse_row, delta_row, delta_row)


def dil_merge_fwd(outs, lses, name):
    H, S, Dh = outs[0].shape
    tm = _tile(S, 512, 8)

    def body(o0, o1, o2, l0, l1, l2, out_ref):
        ls = [l0[...], l1[...], l2[...]]
        m = jnp.maximum(jnp.maximum(ls[0], ls[1]), ls[2])
        es = [jnp.exp(l - m) for l in ls]
        den = es[0] + es[1] + es[2]
        out_ref[...] = (es[0] * o0[...] + es[1] * o1[...] + es[2] * o2[...]) / den

    blk = BS((None, tm, Dh), lambda h, i: (h, i, 0))
    colb = BS((None, tm, 1), lambda h, i: (h, i, 0))
    return pl.pallas_call(
        body, name=name, grid=(H, S // tm), in_specs=[blk] * 3 + [colb] * 3, out_specs=blk,
        out_shape=SDS((H, S, Dh), F32), compiler_params=_params(2))(*outs, *lses)


def dil_merge_bwd(outs, lses, d_out, name):
    H, S, Dh = outs[0].shape
    tm = _tile(S, 512, 8)

    def body(o0, o1, o2, l0, l1, l2, d_ref, do0, do1, do2, dl0, dl1, dl2):
        ls = [l0[...], l1[...], l2[...]]
        m = jnp.maximum(jnp.maximum(ls[0], ls[1]), ls[2])
        es = [jnp.exp(l - m) for l in ls]
        den = es[0] + es[1] + es[2]
        ws = [e / den for e in es]
        dv = d_ref[...]
        dws = [jnp.sum(dv * o[...], axis=-1, keepdims=True) for o in (o0, o1, o2)]
        mean = ws[0] * dws[0] + ws[1] * dws[1] + ws[2] * dws[2]
        for w, dw, do_ref, dl_ref in zip(ws, dws, (do0, do1, do2), (dl0, dl1, dl2)):
            do_ref[...] = w * dv
            dl_ref[...] = w * (dw - mean)

    blk = BS((None, tm, Dh), lambda h, i: (h, i, 0))
    colb = BS((None, tm, 1), lambda h, i: (h, i, 0))
    return pl.pallas_call(
        body, name=name, grid=(H, S // tm), in_specs=[blk] * 3 + [colb] * 3 + [blk],
        out_specs=[blk] * 3 + [colb] * 3,
        out_shape=[SDS((H, S, Dh), F32)] * 3 + [SDS((H, S, 1), F32)] * 3,
        compiler_params=_params(2))(*outs, *lses, d_out)


MERGE_TN = 512


def merge_fwd(o_stack, w_br, proj, name):
    _, S, K = o_stack.shape
    tm = _tile(S, 512, 8)
    tn = MERGE_TN
    nj = D_MODEL // tn

    def body(o_ref, w_ref, gl_ref, m_ref, acc):
        br = pl.program_id(2)

        @pl.when(br == 0)
        def _():
            acc[...] = jnp.zeros_like(acc)

        acc[...] += _sigmoid(gl_ref[...]) * _dot(o_ref[...], w_ref[...])

        @pl.when(br == N_BRANCH - 1)
        def _():
            m_ref[...] = acc[...].astype(m_ref.dtype)

    return pl.pallas_call(
        body, name=name, grid=(S // tm, nj, N_BRANCH),
        in_specs=[BS((None, tm, K), lambda i, j, b: (b, i, 0)), BS((None, K, tn), lambda i, j, b: (b, 0, j)),
                  BS((tm, tn), lambda i, j, b: (i, b * nj + j))],
        out_specs=BS((tm, tn), lambda i, j, b: (i, j)), out_shape=SDS((S, D_MODEL), BF16),
        scratch_shapes=[pltpu.VMEM((tm, tn), F32)], compiler_params=_params(3))(o_stack, w_br, proj)


def merge_bwd(o_stack, w_br, proj, d_merged, name):
    _, S, K = o_stack.shape
    tm = _tile(S, 512, 8)
    tn = MERGE_TN
    nj = D_MODEL // tn

    def body(o_ref, w_ref, gl_ref, dm_ref, dgl_ref, dy_ref):
        gate = _sigmoid(gl_ref[...])
        y = _dot(o_ref[...], w_ref[...])
        dm = dm_ref[...].astype(F32)
        dgl_ref[...] = (dm * y * gate * (1.0 - gate)).astype(dgl_ref.dtype)
        dy_ref[...] = (dm * gate).astype(dy_ref.dtype)

    return pl.pallas_call(
        body, name=name, grid=(S // tm, nj, N_BRANCH),
        in_specs=[BS((None, tm, K), lambda i, j, b: (b, i, 0)), BS((None, K, tn), lambda i, j, b: (b, 0, j)),
                  BS((tm, tn), lambda i, j, b: (i, b * nj + j)), BS((tm, tn), lambda i, j, b: (i, j))],
        out_specs=[BS((tm, tn), lambda i, j, b: (i, b * nj + j)), BS((None, tm, tn), lambda i, j, b: (b, i, j))],
        out_shape=[SDS((S, N_BRANCH * D_MODEL), BF16), SDS((N_BRANCH, S, D_MODEL), BF16)],
        compiler_params=_params(3))(o_stack, w_br, proj, d_merged)


FFN_CW = 256


def ffn_mid_fwd(pre_g, pre_v, w_g, w_v, name, side=None):
    S = pre_g.shape[0]
    tm = _tile(S, CONV_TM, 8)

    def body(g_ref, gp_ref, v_ref, vp_ref, wg_ref, wv_ref, a_ref):
        keep = (pl.program_id(1) > 0).astype(F32)
        ug = _conv_ext(jnp.concatenate([gp_ref[...] * keep, g_ref[...]], axis=0), wg_ref)[HALO:, :]
        uv = _conv_ext(jnp.concatenate([vp_ref[...] * keep, v_ref[...]], axis=0), wv_ref)[HALO:, :]
        a_ref[...] = (ug * _sigmoid(ug) * uv).astype(a_ref.dtype)

    cur, prev, _ = _time_specs(S, tm, FFN_CW, lambda j: j)
    wspec = BS((3, FFN_CW), lambda j, i: (0, j))
    (a,), side_out = _call_with_side(body, side, name, (D_FF // FFN_CW, S // tm),
                                     [cur, prev, cur, prev, wspec, wspec], [cur], [SDS((S, D_FF), BF16)],
                                     (pre_g, pre_g, pre_v, pre_v, w_g, w_v))
    return a, side_out


def ffn_mid_bwd(pre_g, pre_v, w_g, w_v, d_a, name, side=None):
    S = pre_g.shape[0]
    tm = _tile(S, CONV_TM, 8)
    nt = S // tm

    def body(g_ref, gp_ref, gn_ref, v_ref, vp_ref, vn_ref, wg_ref, wv_ref, da_ref, dan_ref,
             dg_ref, dv_ref, dwg_ref, dwv_ref):
        i = pl.program_id(1)
        keep_prev = (i > 0).astype(F32)
        keep_next = (i < nt - 1).astype(F32)
        eg = jnp.concatenate([gp_ref[...] * keep_prev, g_ref[...], gn_ref[...]], axis=0)
        ev = jnp.concatenate([vp_ref[...] * keep_prev, v_ref[...], vn_ref[...]], axis=0)
        ug = _conv_ext(eg, wg_ref)
        uv = _conv_ext(ev, wv_ref)
        da = jnp.concatenate([jnp.zeros((HALO, FFN_CW), F32), da_ref[...], dan_ref[...] * keep_next], axis=0)
        sg = _sigmoid(ug)
        dug = da * uv * (sg * (1.0 + ug * (1.0 - sg)))
        duv = da * (ug * sg)
        dg_ref[...] = _conv_t_ext(dug, wg_ref)[HALO:HALO + tm, :].astype(dg_ref.dtype)
        dv_ref[...] = _conv_t_ext(duv, wv_ref)[HALO:HALO + tm, :].astype(dv_ref.dtype)

        @pl.when(i == 0)
        def _():
            dwg_ref[...] = jnp.zeros_like(dwg_ref)
            dwv_ref[...] = jnp.zeros_like(dwv_ref)

        for dup_e, e, dw_ref in ((dug, eg, dwg_ref), (duv, ev, dwv_ref)):
            dup = dup_e[HALO:HALO + tm, :]
            dw_ref[0:1, :] += jnp.sum(dup * pltpu.roll(e, 2, 0)[HALO:HALO + tm, :], axis=0, keepdims=True)
            dw_ref[1:2, :] += jnp.sum(dup * pltpu.roll(e, 1, 0)[HALO:HALO + tm, :], axis=0, keepdims=True)
            dw_ref[2:3, :] += jnp.sum(dup * e[HALO:HALO + tm, :], axis=0, keepdims=True)

    cur, prev, nxt = _time_specs(S, tm, FFN_CW, lambda j: j)
    wspec = BS((3, FFN_CW), lambda j, i: (0, j))
    return _call_with_side(
        body, side, name, (D_FF // FFN_CW, nt), [cur, prev, nxt, cur, prev, nxt, wspec, wspec, cur, nxt],
        [cur, cur, wspec, wspec], [SDS((S, D_FF), BF16)] * 2 + [SDS((3, D_FF), F32)] * 2,
        (pre_g, pre_g, pre_g, pre_v, pre_v, pre_v, w_g, w_v, d_a, d_a))


def ple_fwd(x, a, e, name):
    S, Dm = x.shape
    tm = _tile(S, 256, 8)

    def body(x_ref, a_ref, e_ref, o_ref):
        o_ref[...] = x_ref[...] + _sigmoid(a_ref[...]) * e_ref[...]

    row = BS((tm, Dm), lambda i: (i, 0))
    return pl.pallas_call(body, name=name, grid=(S // tm,), in_specs=[row] * 3, out_specs=row,
                          out_shape=SDS((S, Dm), F32), compiler_params=_params(1))(x, a, e)


def ple_bwd(a, e, dx, name):
    S, Dm = a.shape
    tm = _tile(S, 256, 8)

    def body(a_ref, e_ref, dx_ref, da_ref, de_ref):
        s = _sigmoid(a_ref[...])
        d = dx_ref[...]
        da_ref[...] = (d * e_ref[...] * s * (1.0 - s)).astype(da_ref.dtype)
        de_ref[...] = (d * s).astype(de_ref.dtype)

    row = BS((tm, Dm), lambda i: (i, 0))
    return pl.pallas_call(body, name=name, grid=(S // tm,), in_specs=[row] * 3, out_specs=[row, row],
                          out_shape=[SDS((S, Dm), BF16)] * 2, compiler_params=_params(1))(a, e, dx)


def to_heads(x, n_heads, dil):
    S = x.shape[0]
    x = x.reshape(S // dil, dil, n_heads, HEAD_DIM).transpose(2, 1, 0, 3)
    return x.reshape(n_heads, S, HEAD_DIM)


def from_heads(y, dil):
    H, S, C = y.shape
    y = y.reshape(H, dil, S // dil, C).transpose(2, 1, 0, 3)
    return y.reshape(S, H * C)


def _columns(x, lo, hi):
    return lax.optimization_barrier(x[:, lo:hi])


def rows_of(col, t):
    H, S, _ = col.shape
    return col.reshape(H, S // t, 1, t)


IN_SEGMENTS = ((SRC_A, SRC_F, OFF_A), (SRC_F, SRC_B, OFF_F), (SRC_B, SRC_C, OFF_B), (SRC_C, SRC_DQ, OFF_C),
               (SRC_DQ, SRC_G, OFF_D), (SRC_G, D_IN, OFF_G))
IN_SHARD = D_IN // N_CHIPS


def w_in_aligned_from_chips(t):
    pieces = []
    for ref_lo, ref_hi, _ in sorted(IN_SEGMENTS, key=lambda seg: seg[2]):
        for k in range(N_CHIPS):
            lo, hi = max(ref_lo, k * IN_SHARD), min(ref_hi, (k + 1) * IN_SHARD)
            if lo < hi:
                pieces.append(t[k][:, lo - k * IN_SHARD:hi - k * IN_SHARD])
    pieces.append(jnp.zeros((t[0].shape[0], W_AL - D_IN), t[0].dtype))
    return jnp.concatenate(pieces, axis=1)


def w_in_chips_from_aligned(g):
    slots = []
    for k in range(N_CHIPS):
        pieces = []
        for ref_lo, ref_hi, al in IN_SEGMENTS:
            lo, hi = max(ref_lo, k * IN_SHARD), min(ref_hi, (k + 1) * IN_SHARD)
            if lo < hi:
                pieces.append(g[:, al + lo - ref_lo:al + hi - ref_lo])
        slots.append(jnp.concatenate(pieces, axis=1))
    return jnp.stack(slots, axis=0)


def chips_to_full(t, name):
    return jnp.concatenate([t[k] for k in range(N_CHIPS)], axis=0 if name in ROW_SHARDED else 1)


def full_to_chips(g, name):
    if name in ROW_SHARDED:
        return g.reshape(N_CHIPS, g.shape[0] // N_CHIPS, g.shape[1])
    return g.reshape(g.shape[0], N_CHIPS, g.shape[1] // N_CHIPS).transpose(1, 0, 2)


def halves_from_chips(t):
    return jnp.concatenate([t[0], t[1]], axis=1), jnp.concatenate([t[2], t[3]], axis=1)


def chips_from_halves(g, v):
    c = g.shape[1] // 2
    return jnp.stack([g[:, :c], g[:, c:], v[:, :c], v[:, c:]], axis=0)


def layer_fwd(x, p_l, rope, w, tag, sides=None):
    S = x.shape[0]
    sides = sides or {}
    side_out = {}
    sv = {"x0": x}
    h = rmsnorm_fwd(x, w["norm_mix_g"], f"{tag}_norm_mix")
    proj = matmul(h, w["w_in_al"], "nn", F32, f"{tag}_proj", side=sides.get("proj"))
    if "proj" in sides:
        proj, side_out["proj"] = proj
    sv["h"], sv["proj"] = h, proj

    af_t, f_cum = fox_prep_fwd(proj, w["fox_forget_b"].reshape(FOX_HEADS, 1), f"{tag}_fox_prep")
    T = min(FOX_T, S)
    f_col = f_cum.reshape(FOX_HEADS, S, 1)
    f_row = f_cum.reshape(FOX_HEADS, S // T, 1, T)
    qkv = to_heads(_columns(proj, OFF_A, OFF_B).astype(BF16), 3 * FOX_HEADS, 1)
    qa, ka, va = qkv[:FOX_HEADS], qkv[FOX_HEADS:2 * FOX_HEADS], qkv[2 * FOX_HEADS:]
    (oa_h, lse_a), side_out["fox"] = fox_attn_fwd(qa, ka, va, f_col, f_row, f"{tag}_fox_fwd", sides.get("fox"))
    o_a = from_heads(oa_h, 1)
    sv.update(af_t=af_t, f_col=f_col, f_row=f_row, qa=qa, ka=ka, va=va, oa_h=oa_h, lse_a=lse_a)

    o_b = shortconv_fwd(proj, w["shortconv_w"], f"{tag}_sconv_fwd")

    o_c = sgu_fwd(proj, w["sgu_norm_g"].reshape(1, SGU_WIDTH), w["sgu_w"], _sgu_bias(w["sgu_b"]), f"{tag}_sgu_fwd")

    cos, sa, sb = rope
    qk = rope_apply(proj, OFF_D, 2 * DIL_WIDTH, cos, sa, sb, BF16, f"{tag}_rope_fwd")
    vd = _columns(proj, OFF_D + 2 * DIL_WIDTH, OFF_D + 3 * DIL_WIDTH).astype(BF16)
    outs, lses, dil_sv = [], [], []
    for g, (window, dil) in enumerate(DIL_PATTERNS):
        sl = slice(g * DIL_OUT, (g + 1) * DIL_OUT)
        qg = to_heads(qk[:, sl], 4, dil)
        kg = to_heads(qk[:, DIL_WIDTH:][:, sl], 4, dil)
        vg = to_heads(vd[:, sl], 4, dil)
        nb = (S // dil) // DIL_SPAN
        og, lg = dil_attn_fwd(qg, kg, vg, nb, f"{tag}_dil{g}_fwd")
        dil_sv.append((qg, kg, vg, og, lg, nb))
        outs.append(_heads_unperm(og, dil))
        lses.append(_col_unperm(lg, dil))
    od_h = dil_merge_fwd(outs, lses, f"{tag}_dil_merge_fwd")
    o_d = from_heads(od_h, 1)
    sv.update(dil=dil_sv, outs=outs, lses=lses)

    o_d_pad = jnp.concatenate([o_d.astype(BF16), jnp.zeros((S, FOX_WIDTH - DIL_OUT), BF16)], axis=-1)
    o_stack = jnp.stack([o_a, o_b, o_c, o_d_pad], axis=0)
    merged = merge_fwd(o_stack, w["w_br"], proj, f"{tag}_merge_fwd")
    x1 = matmul(merged, w["w_out"], "nn", F32, f"{tag}_out_proj", res=x)
    sv.update(o_stack=o_stack, merged=merged, x1=x1)

    h2 = rmsnorm_fwd(x1, w["norm_ffn_g"], f"{tag}_norm_ffn")
    pre = (matmul(h2, w["w_up_g"], "nn", F32, f"{tag}_up_g"), matmul(h2, w["w_up_v"], "nn", F32, f"{tag}_up_v"))
    a, side_out["ffn"] = ffn_mid_fwd(pre[0], pre[1], w["ffn_conv_g"], w["ffn_conv_v"], f"{tag}_ffn_mid_fwd",
                                     sides.get("ffn"))
    x2 = matmul(a, w["w_down"], "nn", F32, f"{tag}_down", res=x1)
    sv.update(h2=h2, pre=pre, a=a, x2=x2)

    n3 = rmsnorm_fwd(x2, w["norm_ple_g"], f"{tag}_norm_ple")
    pg = matmul(n3, w["w_ple_gate"], "nn", F32, f"{tag}_ple_gate")
    pe = matmul(p_l, w["w_ple_proj"], "nn", F32, f"{tag}_ple_proj")
    x3 = ple_fwd(x2, pg, pe, f"{tag}_ple_fwd")
    sv.update(n3=n3, pg=pg, pe=pe, p_l=p_l)
    return x3, sv, side_out


def _sgu_bias(b):
    return jnp.pad(b.T, ((0, 0), (0, SGU_CHUNK - b.shape[0])))


def _col_unperm(col, dil):
    H, S, _ = col.shape
    return col.reshape(H, dil, S // dil).transpose(0, 2, 1).reshape(H, S, 1)


def _col_perm(col, dil):
    H, S, _ = col.shape
    return col.reshape(H, S // dil, dil).transpose(0, 2, 1).reshape(H, S, 1)


def _heads_perm(y, dil):
    H, S, C = y.shape
    return y.reshape(H, S // dil, dil, C).transpose(0, 2, 1, 3).reshape(H, S, C)


def _heads_unperm(y, dil):
    H, S, C = y.shape
    return y.reshape(H, dil, S // dil, C).transpose(0, 2, 1, 3).reshape(H, S, C)


def layer_bwd(dx3, sv, rope, w, tag, exch):
    side_out = {}
    S = dx3.shape[0]
    gr = {}
    da, de = ple_bwd(sv["pg"], sv["pe"], dx3, f"{tag}_ple_bwd")
    gr["w_ple_proj"] = matmul(sv["p_l"], de, "tn", F32, f"{tag}_dw_ple_proj")
    gr["w_ple_gate"] = matmul(sv["n3"], da, "tn", F32, f"{tag}_dw_ple_gate")
    dn3 = matmul(da, w["w_ple_gate"], "nt", BF16, f"{tag}_dn3")
    dx2, gr["norm_ple_g"] = rmsnorm_bwd(sv["x2"], w["norm_ple_g"], dn3, dx3, f"{tag}_norm_ple_bwd")

    d_a = matmul(dx2, w["w_down"], "nt", F32, f"{tag}_da")
    gr["w_down"] = matmul(sv["a"], dx2, "tn", F32, f"{tag}_dw_down")
    (dpre_g, dpre_v, dwc_g, dwc_v), swapped = ffn_mid_bwd(sv["pre"][0], sv["pre"][1], w["ffn_conv_g"],
                                                          w["ffn_conv_v"], d_a, f"{tag}_ffn_mid_bwd",
                                                          exch.swap_exchange())
    sides = exch.ici_exchanges(swapped)
    gr["ffn_conv_w"] = (dwc_g, dwc_v)
    h2_t = sv["h2"].T
    gr["w_up"] = (matmul(h2_t, dpre_g, "nn", F32, f"{tag}_dw_up_g", tm=2048, tn=512),
                  matmul(h2_t, dpre_v, "nn", F32, f"{tag}_dw_up_v", tm=2048, tn=512))
    dh2_g = matmul(dpre_g, w["w_up_g"], "nt", F32, f"{tag}_dh2_g")
    dh2 = matmul(dpre_v, w["w_up_v"], "nt", BF16, f"{tag}_dh2_v", res=dh2_g)
    dx1, gr["norm_ffn_g"] = rmsnorm_bwd(sv["x1"], w["norm_ffn_g"], dh2, dx2, f"{tag}_norm_ffn_bwd")

    d_merged = matmul(dx1, w["w_out"], "nt", BF16, f"{tag}_dmerged")
    gr["w_out"] = matmul(sv["merged"], dx1, "tn", F32, f"{tag}_dw_out")
    proj = sv["proj"]
    dgl, dy = merge_bwd(sv["o_stack"], w["w_br"], proj, d_merged, f"{tag}_merge_bwd")
    d_o, d_wbr = [], []
    for b in range(N_BRANCH):
        d_o.append(matmul(dy[b], w["w_br"][b], "nt", F32, f"{tag}_do{b}"))
        d_wbr.append(matmul(sv["o_stack"][b], dy[b], "tn", F32, f"{tag}_dw_br{b}"))
    gr["w_br"] = d_wbr

    do_a = to_heads(d_o[0].astype(BF16), FOX_HEADS, 1)
    T = min(FOX_T, S)
    (dqa, delta_a, d_fq), side_out["dq"] = fox_attn_bwd_dq(
        sv["qa"], sv["ka"], sv["va"], sv["f_col"], sv["f_row"], sv["oa_h"], sv["lse_a"], do_a, f"{tag}_fox_dq",
        sides.get("dq"))
    (dka, dva, d_fk), side_out["dkv"] = fox_attn_bwd_dkv(
        sv["qa"], sv["ka"], sv["va"], sv["f_col"], sv["f_row"], rows_of(sv["lse_a"], T), rows_of(delta_a, T), do_a,
        f"{tag}_fox_dkv", sides.get("dkv"))
    daf_t, dfb = fox_prep_bwd(sv["af_t"], w["fox_forget_b"].reshape(FOX_HEADS, 1), d_fq.reshape(FOX_HEADS, S),
                              d_fk.reshape(FOX_HEADS, S), f"{tag}_fox_prep_bwd")
    gr["fox_forget_b"] = dfb.reshape(FOX_HEADS)
    d_proj_a = from_heads(jnp.concatenate([dqa, dka, dva], axis=0), 1).astype(BF16)

    dxb, dgb, dgc, gr["shortconv_w"] = shortconv_bwd(proj, w["shortconv_w"], d_o[1], f"{tag}_sconv_bwd")

    d_c, dsg, dsw, dsb = sgu_bwd(proj, w["sgu_norm_g"].reshape(1, SGU_WIDTH), w["sgu_w"],
                                 jnp.swapaxes(w["sgu_w"], 1, 2), _sgu_bias(w["sgu_b"]), d_o[2], f"{tag}_sgu_bwd")
    gr["sgu_norm_g"] = dsg.reshape(SGU_WIDTH)
    gr["sgu_w"] = dsw
    gr["sgu_b"] = dsb[:, :SGU_WIDTH // SGU_CHUNK].T

    d_od = to_heads(d_o[3][:, :DIL_OUT], 4, 1)
    d_outs_lses = dil_merge_bwd(sv["outs"], sv["lses"], d_od, f"{tag}_dil_merge_bwd")
    d_outs, d_lses = d_outs_lses[:3], d_outs_lses[3:]
    dq_parts, dk_parts, dv_parts = [], [], []
    for g, (window, dil) in enumerate(DIL_PATTERNS):
        qg, kg, vg, og, lg, nb = sv["dil"][g]
        do_g = _heads_perm(d_outs[g], dil)
        dl_g = _col_perm(d_lses[g], dil)
        dqg, delta_g = dil_attn_bwd_dq(qg, kg, vg, og, lg, do_g, dl_g, nb, f"{tag}_dil{g}_dq")
        dkg, dvg = dil_attn_bwd_dkv(qg, kg, vg, rows_of(lg, DIL_SPAN), rows_of(delta_g, DIL_SPAN), do_g, nb,
                                    f"{tag}_dil{g}_dkv")
        dq_parts.append(from_heads(dqg, dil))
        dk_parts.append(from_heads(dkg, dil))
        dv_parts.append(from_heads(dvg, dil))
    cos, sa, sb = rope
    d_qk_rot = jnp.concatenate(dq_parts + dk_parts, axis=-1)
    d_qk = rope_apply(d_qk_rot, 0, 2 * DIL_WIDTH, cos, -sa, -sb, BF16, f"{tag}_rope_bwd")
    d_vd = jnp.concatenate(dv_parts, axis=-1).astype(BF16)

    d_f_cols = jnp.concatenate([daf_t.T.astype(BF16), jnp.zeros((S, W_AL - OFF_F - FOX_HEADS), BF16)], axis=-1)
    d_proj = jnp.concatenate([dgl, d_proj_a, dxb, dgb, dgc, d_c, d_qk, d_vd, d_f_cols], axis=-1)
    gr["w_in_al"] = matmul(sv["h"], d_proj, "tn", F32, f"{tag}_dw_in", tm=2048, tn=512)
    dh = matmul(d_proj, w["w_in_al"], "nt", BF16, f"{tag}_dh", tk=W_AL // 4)
    dx0, gr["norm_mix_g"] = rmsnorm_bwd(sv["x0"], w["norm_mix_g"], dh, dx1, f"{tag}_norm_mix_bwd")
    exch.ici_arrived(side_out)
    return dx0, gr


def local_weights(chips, repl, layer):
    w = {n: repl[n][layer] for n in REPLICATED}
    cast = lambda n, dtype: [chips[n][k].astype(dtype) for k in range(N_CHIPS)]
    full = {n: chips_to_full(cast(n, BF16), n)
            for n in ("w_br_fox", "w_br_conv", "w_br_sgu", "w_br_dil", "w_out", "w_down", "w_ple_gate", "w_ple_proj")}
    w["w_in_al"] = w_in_aligned_from_chips(cast("w_in", BF16))
    w["shortconv_w"] = chips_to_full(cast("shortconv_w", F32), "shortconv_w")
    pad = jnp.zeros((FOX_WIDTH - DIL_OUT, D_MODEL), BF16)
    w["w_br"] = jnp.stack([full["w_br_fox"], full["w_br_conv"], full["w_br_sgu"],
                           jnp.concatenate([full["w_br_dil"], pad], axis=0)], axis=0)
    w["w_up_g"], w["w_up_v"] = halves_from_chips(cast("w_up", BF16))
    w["ffn_conv_g"], w["ffn_conv_v"] = halves_from_chips(cast("ffn_conv_w", F32))
    for n in ("w_out", "w_down", "w_ple_gate", "w_ple_proj"):
        w[n] = full[n]
    return w


def grads_to_chips(gr):
    out = {n: gr[n] for n in ("fox_forget_b", "sgu_norm_g", "sgu_w", "sgu_b")}
    out["norm_mix_g"] = gr["norm_mix_g"].reshape(D_MODEL)
    out["norm_ffn_g"] = gr["norm_ffn_g"].reshape(D_MODEL)
    out["norm_ple_g"] = gr["norm_ple_g"].reshape(D_MODEL)
    out["w_in"] = w_in_chips_from_aligned(gr["w_in_al"])
    out["w_up"] = chips_from_halves(*gr["w_up"])
    out["ffn_conv_w"] = chips_from_halves(*gr["ffn_conv_w"])
    for b, n in enumerate(("w_br_fox", "w_br_conv", "w_br_sgu")):
        out[n] = full_to_chips(gr["w_br"][b], n)
    out["w_br_dil"] = full_to_chips(gr["w_br"][3][:DIL_OUT], "w_br_dil")
    for n in ("shortconv_w", "w_out", "w_down", "w_ple_gate", "w_ple_proj"):
        out[n] = full_to_chips(gr[n], n)
    return out


def local_step(x, p, positions, repl, final_norm_g, loss_target, exch):
    depth = p.shape[0]
    rope = rope_tables(positions)
    saved, ws = [], []
    chips = exch.first_weights()
    for layer in range(depth):
        w = local_weights(chips, repl, layer)
        side = exch.weights_exchange(layer + 1) if layer + 1 < depth else None
        x, sv, side_out = layer_fwd(x, p[layer].astype(BF16), rope, w, f"l{layer}", side)
        if layer + 1 < depth:
            chips = exch.weights_arrived(layer + 1, side_out)
        saved.append(sv)
        ws.append(w)
    loss_part, dx, dgf = final_loss(x, final_norm_g, loss_target, "final_loss")
    for layer in range(depth - 1, -1, -1):
        dx, gr = layer_bwd(dx, saved[layer], rope, ws[layer], f"l{layer}", exch)
        exch.grads_ready(layer, grads_to_chips(gr))
    exch.grads_flush()
    return loss_part[0, 0], dx, dgf.reshape(-1)


def _position():
    return lax.axis_index("x"), lax.axis_index("y"), lax.axis_index("c")


def _other_chips(x, y):
    return [(1 - x, y), (x, 1 - y), (1 - x, 1 - y)]


def _remote(src, dst, send_sem, recv_sem, device):
    return pltpu.make_async_remote_copy(src_ref=src, dst_ref=dst, send_sem=send_sem, recv_sem=recv_sem,
                                        device_id=device, device_id_type=MESH)


def _chip_index():
    return 2 * lax.axis_index("x") + lax.axis_index("y")


def _block_rows(rows, cols, unit):
    return _tile(rows, max(unit, (1 << 19) // cols // unit * unit), unit)


def gather_chip_shards(packs, name):
    return _run_exchange(gather_exchange(packs), name)


def gather_exchange(packs):
    n = len(packs)
    halves = [p.shape[0] // 2 for p in packs]

    def half(outs, t, chip, core):
        return outs[t].at[chip, pl.ds(core * halves[t], halves[t]), :]

    def ici_sends(srcs, outs, send_sems, recv_sems):
        x, y, c = _position()
        me = 2 * x + y
        return [_remote(srcs[t].at[pl.ds(c * halves[t], halves[t]), :], half(outs, t, me, c),
                        send_sems.at[6 * t + j], recv_sems.at[6 * t + j], (px, py, c))
                for t in range(n) for j, (px, py) in enumerate(_other_chips(x, y))]

    def start(srcs, outs, send_sems, recv_sems):
        for cp in ici_sends(srcs, outs, send_sems, recv_sems):
            cp.start()

    def finish(srcs, outs, send_sems, recv_sems):
        x, y, c = _position()
        sibling = (x, y, 1 - c)
        chips = _other_chips(x, y)
        passed = []
        for t in range(n):
            for j, (px, py) in enumerate(chips):
                k = 2 * px + py
                s = 6 * t + j
                landed = half(outs, t, k, c)
                _remote(landed, landed, send_sems.at[s], recv_sems.at[s], (px, py, c)).wait_recv()
                fwd = _remote(landed, landed, send_sems.at[s + 3], recv_sems.at[s + 3], sibling)
                fwd.start()
                passed.append(fwd)
        for t in range(n):
            for j, (px, py) in enumerate(chips):
                s = 6 * t + j + 3
                theirs = half(outs, t, 2 * px + py, 1 - c)
                _remote(theirs, theirs, send_sems.at[s], recv_sems.at[s], sibling).wait_recv()
        for cp in ici_sends(srcs, outs, send_sems, recv_sems) + passed:
            cp.wait_send()

    return SideExchange(list(packs), [SDS((N_CHIPS,) + p.shape, p.dtype) for p in packs], 6 * n, start, finish)


def _run_exchange(side, name):
    n_in, n_out = len(side.operands), len(side.out_shapes)

    def body(*refs):
        srcs, outs, (send_sems, recv_sems) = refs[:n_in], refs[n_in:n_in + n_out], refs[n_in + n_out:]
        side.start(srcs, outs, send_sems, recv_sems)
        side.finish(srcs, outs, send_sems, recv_sems)

    return pl.pallas_call(
        body, name=name, in_specs=[ANY] * n_in, out_specs=[ANY] * n_out, out_shape=side.out_shapes,
        scratch_shapes=[pltpu.SemaphoreType.DMA((side.n_sems,)), pltpu.SemaphoreType.DMA((side.n_sems,))],
    )(*side.operands)


def swap_halves_with_sibling(gs, name):
    return _run_exchange(swap_exchange(gs), name)


def swap_exchange(gs):
    n = len(gs)
    halves = [g.shape[1] // 2 for g in gs]

    def copies(srcs, lands, send_sems, recv_sems):
        x, y, c = _position()
        return [_remote(srcs[t].at[:, pl.ds((1 - c) * halves[t], halves[t]), :], lands[t], send_sems.at[t],
                        recv_sems.at[t], (x, y, 1 - c)) for t in range(n)]

    def start(srcs, lands, send_sems, recv_sems):
        for cp in copies(srcs, lands, send_sems, recv_sems):
            cp.start()

    def finish(srcs, lands, send_sems, recv_sems):
        for cp in copies(srcs, lands, send_sems, recv_sems):
            cp.wait()

    return SideExchange(list(gs), [SDS((g.shape[0], h, g.shape[2]), g.dtype) for g, h in zip(gs, halves)], n,
                        start, finish)


def add_my_half(g, other, out_dtype, name):
    n, R, C = g.shape
    H = R // 2
    tr = _block_rows(H, C, 16) if H % 16 == 0 else H
    nb = H // tr
    core = lax.axis_index("c").astype(jnp.int32).reshape(1)

    def body(c_ref, g_ref, o_ref, out_ref):
        out_ref[...] = (g_ref[...] + o_ref[...]).astype(out_ref.dtype)

    grid_spec = pltpu.PrefetchScalarGridSpec(
        num_scalar_prefetch=1, grid=(n, nb),
        in_specs=[BS((None, tr, C), lambda s, i, c_ref: (s, c_ref[0] * nb + i, 0)),
                  BS((None, tr, C), lambda s, i, c_ref: (s, i, 0))],
        out_specs=BS((None, tr, C), lambda s, i, c_ref: (s, i, 0)))
    return pl.pallas_call(body, name=name, grid_spec=grid_spec, out_shape=SDS((n, H, C), out_dtype),
                          compiler_params=_params(2))(core, g, other)


def exchange_slots_between_chips(parts, name):
    return _run_exchange(slot_exchange(parts), name)


def slot_exchange(parts):
    n = len(parts)

    def sends(srcs, lands, send_sems, recv_sems):
        x, y, c = _position()
        me = 2 * x + y
        return [_remote(srcs[t].at[2 * px + py], lands[t].at[me], send_sems.at[3 * t + j], recv_sems.at[3 * t + j],
                        (px, py, c)) for t in range(n) for j, (px, py) in enumerate(_other_chips(x, y))]

    def start(srcs, lands, send_sems, recv_sems):
        for cp in sends(srcs, lands, send_sems, recv_sems):
            cp.start()

    def finish(srcs, lands, send_sems, recv_sems):
        x, y, c = _position()
        for t in range(n):
            for j, (px, py) in enumerate(_other_chips(x, y)):
                k = 2 * px + py
                _remote(srcs[t].at[k], lands[t].at[k], send_sems.at[3 * t + j], recv_sems.at[3 * t + j],
                        (px, py, c)).wait_recv()
        for cp in sends(srcs, lands, send_sems, recv_sems):
            cp.wait_send()

    return SideExchange(list(parts), [SDS(p.shape, p.dtype) for p in parts], 3 * n, start, finish)


def sum_slots_into_my_half(landed, mine, name):
    n, H, C = landed.shape
    tr = _block_rows(H, C, 16) if H % 16 == 0 else H
    nb = H // tr
    where = jnp.stack([lax.axis_index("c"), _chip_index()]).astype(jnp.int32)

    def body(w_ref, l_ref, m_ref, o_ref):
        me = w_ref[1]
        o_ref[...] = jnp.zeros_like(o_ref)
        for k in range(n):
            @pl.when(me == k)
            def _():
                o_ref[...] += m_ref[k].astype(F32)

            @pl.when(me != k)
            def _():
                o_ref[...] += l_ref[k].astype(F32)

    slots = BS((n, tr, C), lambda i, w_ref: (0, i, 0))
    grid_spec = pltpu.PrefetchScalarGridSpec(
        num_scalar_prefetch=1, grid=(nb,), in_specs=[slots, slots],
        out_specs=BS((tr, C), lambda i, w_ref: (w_ref[0] * nb + i, 0)))
    return pl.pallas_call(body, name=name, grid_spec=grid_spec, out_shape=SDS((2 * H, C), F32),
                          compiler_params=_params(1))(where, landed, mine)


def sum_slots(parts, name):
    n, H, C = parts.shape
    tr = _tile(H, 256, 16)

    def body(p_ref, o_ref):
        acc = p_ref[0].astype(F32)
        for k in range(1, n):
            acc = acc + p_ref[k].astype(F32)
        o_ref[...] = acc

    return pl.pallas_call(
        body, name=name, grid=(H // tr,), in_specs=[BS((n, tr, C), lambda i: (0, i, 0))],
        out_specs=BS((tr, C), lambda i: (i, 0)), out_shape=SDS((H, C), F32), compiler_params=_params(1))(parts)


def join_halves_with_sibling(arrs, name):
    n = len(arrs)
    halves = [a.shape[0] // 2 for a in arrs]

    def body(*refs):
        outs, (send_sems, recv_sems) = refs[n:2 * n], refs[2 * n:]
        x, y, c = _position()

        def half(t, core):
            return outs[t].at[pl.ds(core * halves[t], halves[t]), :]

        sends = [_remote(half(t, c), half(t, c), send_sems.at[t], recv_sems.at[t], (x, y, 1 - c)) for t in range(n)]
        for cp in sends:
            cp.start()
        for t in range(n):
            _remote(half(t, 1 - c), half(t, 1 - c), send_sems.at[t], recv_sems.at[t], (x, y, 1 - c)).wait_recv()
        for cp in sends:
            cp.wait_send()

    return pl.pallas_call(
        body, name=name, in_specs=[ANY] * n, out_specs=[ANY] * n, out_shape=[SDS(a.shape, a.dtype) for a in arrs],
        input_output_aliases={t: t for t in range(n)},
        scratch_shapes=[pltpu.SemaphoreType.DMA((n,)), pltpu.SemaphoreType.DMA((n,))])(*arrs)


def reduce_scatter_pair_sums(gs, others, tag):
    n = len(gs)
    return [add_my_half(g, o, BF16 if t < n - 1 else F32, f"{tag}_pair_sum{t}")
            for t, (g, o) in enumerate(zip(gs, others))]


def reduce_scatter_finish(parts, landed, tag):
    sums = [sum_slots_into_my_half(l, p, f"{tag}_chip_sum{t}") for t, (l, p) in enumerate(zip(landed, parts))]
    return join_halves_with_sibling(sums, f"{tag}_join")


def gather_all_devices(pack, name):
    R, C = pack.shape

    def body(src, out, send_sems, recv_sems, local_sem):
        x, y, c = _position()
        me = 4 * x + 2 * y + c
        local = pltpu.make_async_copy(src, out.at[me], local_sem)
        local.start()
        peers = []
        for m in range(1, N_DEV):
            px = 1 - x if m & 4 else x
            py = 1 - y if m & 2 else y
            pc = 1 - c if m & 1 else c
            peers.append((px, py, pc))
        sends = [_remote(src, out.at[me], send_sems.at[j], recv_sems.at[j], peer) for j, peer in enumerate(peers)]
        for cp in sends:
            cp.start()
        for j, (px, py, pc) in enumerate(peers):
            k = 4 * px + 2 * py + pc
            _remote(src, out.at[k], send_sems.at[j], recv_sems.at[j], (px, py, pc)).wait_recv()
        for cp in sends:
            cp.wait_send()
        local.wait()

    return pl.pallas_call(
        body, name=name, in_specs=[ANY], out_specs=ANY, out_shape=SDS((N_DEV, R, C), pack.dtype),
        scratch_shapes=[pltpu.SemaphoreType.DMA((N_DEV - 1,)), pltpu.SemaphoreType.DMA((N_DEV - 1,)),
                        pltpu.SemaphoreType.DMA(())])(pack)


def _adamw_update(w, g, m, v):
    c1 = 1.0 / (1.0 - ADAM_B1 ** ADAM_STEP)
    c2 = 1.0 / (1.0 - ADAM_B2 ** ADAM_STEP)
    mn = ADAM_B1 * m + (1.0 - ADAM_B1) * g
    vn = ADAM_B2 * v + (1.0 - ADAM_B2) * (g * g)
    return -ADAM_LR * ((mn * c1) / (jnp.sqrt(vn * c2) + ADAM_EPS) + ADAM_WD * w), mn, vn


def adamw_layers(w, gs, m, v, name):
    L, r, c = w.shape
    tr = r if r * c * 4 <= (1 << 20) else _tile(r, max(8, ((1 << 20) // (c * 4)) // 8 * 8), 8)
    nb = r // tr

    def g_spec(l):
        return BS((tr, c), lambda layer, i: (jnp.where(layer == l, i, jnp.where(layer < l, 0, nb - 1)), 0))

    def body(w_ref, m_ref, v_ref, *rest):
        g_refs, (go_ref, d_ref, mo_ref, vo_ref) = rest[:L], rest[L:]
        layer = pl.program_id(0)
        for l in range(L):
            @pl.when(layer == l)
            def _(g_ref=g_refs[l]):
                gv = g_ref[...]
                go_ref[...] = gv
                d_ref[...], mo_ref[...], vo_ref[...] = _adamw_update(w_ref[...], gv, m_ref[...], v_ref[...])

    blk = BS((None, tr, c), lambda layer, i: (layer, i, 0))
    return pl.pallas_call(
        body, name=name, grid=(L, nb), in_specs=[blk] * 3 + [g_spec(l) for l in range(L)], out_specs=[blk] * 4,
        out_shape=[SDS((L, r, c), F32)] * 4, compiler_params=_params(2))(w, m, v, *gs)


def adamw(w, g, m, v, name):
    shape = w.shape
    cols = shape[-1] if len(shape) > 1 else shape[0]
    rows = w.size // cols
    two = lambda t: t.reshape(rows, cols)
    tr = rows
    if rows * cols * 4 > (1 << 21):
        tr = _tile(rows, max(8, ((1 << 21) // (cols * 4)) // 8 * 8), 8)

    def body(w_ref, g_ref, m_ref, v_ref, d_ref, mo_ref, vo_ref):
        d_ref[...], mo_ref[...], vo_ref[...] = _adamw_update(w_ref[...], g_ref[...], m_ref[...], v_ref[...])

    blk = BS((tr, cols), lambda i: (i, 0))
    d, mo, vo = pl.pallas_call(
        body, name=name, grid=(rows // tr,), in_specs=[blk] * 4, out_specs=[blk] * 3,
        out_shape=[SDS((rows, cols), F32)] * 3, compiler_params=_params(1))(two(w), two(g), two(m), two(v))
    return d.reshape(shape), mo.reshape(shape), vo.reshape(shape)


def _rows_for(n, unit):
    rows = -(-n // PACK_COLS)
    return -(-rows // unit) * unit


ROWS_GROUP = ("w_out", "w_ple_gate", "w_down")
COLS_GROUP = ("w_br_fox", "w_br_conv", "w_br_sgu", "w_br_dil", "w_ple_proj")
SMALL_GROUP = ("shortconv_w", "ffn_conv_w")
SMALL_ROWS = 16


def group_shards(t, dtype):
    lead = t["w_in"].shape[:-2]
    small = jnp.concatenate([t[n].astype(F32).reshape(lead + (-1,)) for n in SMALL_GROUP], axis=-1)
    pad = jnp.zeros(lead + (SMALL_ROWS * PACK_COLS - small.shape[-1],), F32)
    small = jnp.concatenate([small, pad], axis=-1).reshape(lead + (SMALL_ROWS, PACK_COLS))
    return [t["w_in"].astype(dtype), t["w_up"].astype(dtype),
            jnp.concatenate([t[n].astype(dtype) for n in ROWS_GROUP], axis=-2),
            jnp.concatenate([t[n].astype(dtype) for n in COLS_GROUP], axis=-2), small]


def ungroup_shards(arrs, shard_shapes):
    w_in_s, w_up_s, rows, cols, small = arrs
    lead = w_in_s.shape[:-2]
    out = {"w_in": w_in_s, "w_up": w_up_s}
    for group, arr in ((ROWS_GROUP, rows), (COLS_GROUP, cols)):
        off = 0
        for n in group:
            r = shard_shapes[n][0]
            out[n] = arr[..., off:off + r, :]
            off += r
    flat = small.reshape(lead + (-1,))
    off = 0
    for n in SMALL_GROUP:
        size = shard_shapes[n][0] * shard_shapes[n][1]
        out[n] = flat[..., off:off + size].reshape(lead + shard_shapes[n])
        off += size
    return out


class ShardExchange:
    def __init__(self, weights, depth, shard_shapes):
        self.shard_shapes = shard_shapes
        self.packs = [group_shards({n: weights[n][layer] for n in SHARDED}, BF16) for layer in range(depth)]
        self.me = _chip_index()
        self.pending = None
        self.parts = None
        self.shard_grads = [None] * depth
        self.repl_grads = [None] * depth

    def _chips(self, layer, gathered):
        per_chip = [ungroup_shards([jnp.where(self.me == k, pk, g[k]) for g, pk in zip(gathered, self.packs[layer])],
                                   self.shard_shapes) for k in range(N_CHIPS)]
        return {n: [per_chip[k][n] for k in range(N_CHIPS)] for n in SHARDED}

    def first_weights(self):
        return self._chips(0, gather_chip_shards(self.packs[0], "gather_w0"))

    FWD_HOSTS = {"fox": (0, 3, 4), "proj": (1,), "ffn": (2,)}
    BWD_HOSTS = {"dq": (0, 3, 4), "dkv": (1, 2)}

    @staticmethod
    def _split(hosts, arrs, make):
        return {host: make([arrs[t] for t in idx]) for host, idx in hosts.items()}

    @staticmethod
    def _join(hosts, outs):
        arrs = [None] * sum(len(idx) for idx in hosts.values())
        for host, idx in hosts.items():
            for t, arr in zip(idx, outs[host]):
                arrs[t] = arr
        return arrs

    def weights_exchange(self, layer):
        return self._split(self.FWD_HOSTS, self.packs[layer], gather_exchange)

    def weights_arrived(self, layer, outs):
        return self._chips(layer, self._join(self.FWD_HOSTS, outs))

    def grads_ready(self, layer, gr):
        self.repl_grads[layer] = {n: gr[n] for n in REPLICATED}
        self.pending = (layer, group_shards({n: gr[n] for n in SHARDED}, F32))

    def swap_exchange(self):
        return swap_exchange(self.pending[1]) if self.pending is not None else None

    def ici_exchanges(self, swapped):
        if self.pending is None:
            return {}
        layer, slots = self.pending
        self.parts = reduce_scatter_pair_sums(slots, swapped, f"rs{layer}")
        return self._split(self.BWD_HOSTS, self.parts, slot_exchange)

    def ici_arrived(self, outs):
        if self.pending is not None:
            self._finish(self._join(self.BWD_HOSTS, outs))

    def grads_flush(self):
        layer, slots = self.pending
        self.parts = reduce_scatter_pair_sums(slots, swap_halves_with_sibling(slots, f"rs{layer}_swap"), f"rs{layer}")
        self._finish(exchange_slots_between_chips(self.parts, f"rs{layer}_ici"))

    def _finish(self, landed):
        layer = self.pending[0]
        self.shard_grads[layer] = ungroup_shards(reduce_scatter_finish(self.parts, landed, f"rs{layer}"),
                                                 self.shard_shapes)
        self.pending = None


REPL_SHAPES = {"norm_mix_g": (D_MODEL,), "fox_forget_b": (FOX_HEADS,), "sgu_norm_g": (SGU_WIDTH,),
               "sgu_w": (4, SGU_CHUNK, SGU_CHUNK), "sgu_b": (4, SGU_CHUNK), "norm_ffn_g": (D_MODEL,),
               "norm_ple_g": (D_MODEL,)}


def kernel(x, p, positions, norm_mix_g, w_in, fox_forget_b, shortconv_w, sgu_norm_g, sgu_w, sgu_b, w_br_fox, w_br_conv, w_br_sgu, w_br_dil, w_out, norm_ffn_g, w_up, ffn_conv_w, w_down, norm_ple_g, w_ple_gate, w_ple_proj, final_norm_g, loss_target, m_norm_mix_g, m_w_in, m_fox_forget_b, m_shortconv_w, m_sgu_norm_g, m_sgu_w, m_sgu_b, m_w_br_fox, m_w_br_conv, m_w_br_sgu, m_w_br_dil, m_w_out, m_norm_ffn_g, m_w_up, m_ffn_conv_w, m_w_down, m_norm_ple_g, m_w_ple_gate, m_w_ple_proj, m_final_norm_g, v_norm_mix_g, v_w_in, v_fox_forget_b, v_shortconv_w, v_sgu_norm_g, v_sgu_w, v_sgu_b, v_w_br_fox, v_w_br_conv, v_w_br_sgu, v_w_br_dil, v_w_out, v_norm_ffn_g, v_w_up, v_ffn_conv_w, v_w_down, v_norm_ple_g, v_w_ple_gate, v_w_ple_proj, v_final_norm_g):
    weights = dict(norm_mix_g=norm_mix_g, w_in=w_in, fox_forget_b=fox_forget_b, shortconv_w=shortconv_w,
                   sgu_norm_g=sgu_norm_g, sgu_w=sgu_w, sgu_b=sgu_b, w_br_fox=w_br_fox, w_br_conv=w_br_conv,
                   w_br_sgu=w_br_sgu, w_br_dil=w_br_dil, w_out=w_out, norm_ffn_g=norm_ffn_g, w_up=w_up,
                   ffn_conv_w=ffn_conv_w, w_down=w_down, norm_ple_g=norm_ple_g, w_ple_gate=w_ple_gate,
                   w_ple_proj=w_ple_proj, final_norm_g=final_norm_g)
    mom1 = dict(norm_mix_g=m_norm_mix_g, w_in=m_w_in, fox_forget_b=m_fox_forget_b, shortconv_w=m_shortconv_w,
                sgu_norm_g=m_sgu_norm_g, sgu_w=m_sgu_w, sgu_b=m_sgu_b, w_br_fox=m_w_br_fox, w_br_conv=m_w_br_conv,
                w_br_sgu=m_w_br_sgu, w_br_dil=m_w_br_dil, w_out=m_w_out, norm_ffn_g=m_norm_ffn_g, w_up=m_w_up,
                ffn_conv_w=m_ffn_conv_w, w_down=m_w_down, norm_ple_g=m_norm_ple_g, w_ple_gate=m_w_ple_gate,
                w_ple_proj=m_w_ple_proj, final_norm_g=m_final_norm_g)
    mom2 = dict(norm_mix_g=v_norm_mix_g, w_in=v_w_in, fox_forget_b=v_fox_forget_b, shortconv_w=v_shortconv_w,
                sgu_norm_g=v_sgu_norm_g, sgu_w=v_sgu_w, sgu_b=v_sgu_b, w_br_fox=v_w_br_fox, w_br_conv=v_w_br_conv,
                w_br_sgu=v_w_br_sgu, w_br_dil=v_w_br_dil, w_out=v_w_out, norm_ffn_g=v_norm_ffn_g, w_up=v_w_up,
                ffn_conv_w=v_ffn_conv_w, w_down=v_w_down, norm_ple_g=v_norm_ple_g, w_ple_gate=v_w_ple_gate,
                w_ple_proj=v_w_ple_proj, final_norm_g=v_final_norm_g)
    depth = w_in.shape[0]
    shard_shapes = {n: tuple(weights[n].shape[1:]) for n in SHARDED}

    exch = ShardExchange(weights, depth, shard_shapes)
    repl = {n: weights[n] for n in REPLICATED}
    loss_part, grad_x, d_final = local_step(x[0], p[:, 0], positions[0], repl, final_norm_g, loss_target[0], exch)
    loss = lax.psum(loss_part, ("x", "y", "c"))
    grads = exch.repl_grads
    grad_w, deltas, new_m, new_v = {}, {}, {}, {}
    for n in SHARDED:
        grad_w[n], deltas[n], new_m[n], new_v[n] = adamw_layers(
            weights[n], [exch.shard_grads[layer][n] for layer in range(depth)], mom1[n], mom2[n], f"adamw_{n}")

    flat = jnp.concatenate([grads[layer][n].astype(F32).reshape(-1) for layer in range(depth) for n in REPLICATED]
                           + [d_final])
    Rr = _rows_for(flat.shape[0], 16)
    packed = jnp.concatenate([flat, jnp.zeros((Rr * PACK_COLS - flat.shape[0],), F32)]).reshape(Rr, PACK_COLS)
    total = sum_slots(gather_all_devices(packed, "gather_repl"), "sum_repl").reshape(-1)
    off = 0
    g_rep = {n: [] for n in REPLICATED}
    for layer in range(depth):
        for n in REPLICATED:
            size = 1
            for s in REPL_SHAPES[n]:
                size *= s
            g_rep[n].append(total[off:off + size].reshape(REPL_SHAPES[n]))
            off += size
    for n in REPLICATED:
        grad_w[n] = jnp.stack(g_rep[n], axis=0)
    grad_w["final_norm_g"] = total[off:off + D_MODEL]

    for n in REPLICATED + ("final_norm_g",):
        deltas[n], new_m[n], new_v[n] = adamw(weights[n], grad_w[n], mom1[n], mom2[n], f"adamw_{n}")
    return (loss, grad_x[None], *[grad_w[n] for n in WEIGHTS], *[deltas[n] for n in WEIGHTS],
            *[new_m[n] for n in WEIGHTS], *[new_v[n] for n in WEIGHTS])
```

```python
import functools

import jax
import jax.numpy as jnp
from jax import lax
from jax.experimental import pallas as pl
from jax.experimental.pallas import tpu as pltpu

F32 = jnp.float32
BF16 = jnp.bfloat16
MESH = pl.DeviceIdType.MESH
BS = pl.BlockSpec
SDS = jax.ShapeDtypeStruct
ANY = pl.BlockSpec(memory_space=pl.ANY)

VMEM_LIMIT_BYTES = 52 * 1024 * 1024
LANES = 128

D_MODEL = 2048
HEAD_DIM = 64
EPS = 1e-6
NEG = -1e30
FOX_HEADS = 8
FOX_WIDTH = 512
CONV_WIDTH = 512
SGU_WIDTH = 512
SGU_CHUNK = 128
DIL_PATTERNS = ((128, 1), (512, 4), (2048, 16))
DIL_SPAN = 128
DIL_HEADS = 12
DIL_WIDTH = 768
DIL_OUT = 256
ROPE_THETA = 500000.0
ROPE_DIM = 16
N_BRANCH = 4
D_FF = 5632
PLE_DIM = 256
D_IN = 14600

OFF_G, OFF_A, OFF_B, OFF_C, OFF_D, OFF_F, W_AL = 0, 8192, 9728, 11264, 12288, 14592, 14848
SRC_A, SRC_F, SRC_B, SRC_C, SRC_DQ, SRC_G = 0, 1536, 1544, 3080, 4104, 6408

ADAM_LR, ADAM_B1, ADAM_B2, ADAM_EPS, ADAM_WD, ADAM_STEP = 0.001, 0.9, 0.999, 1e-08, 0.01, 10

PACK_COLS = 1024
N_CHIPS = 4
N_DEV = 8

SHARDED = ("w_in", "shortconv_w", "w_br_fox", "w_br_conv", "w_br_sgu", "w_br_dil", "w_out", "w_up",
           "ffn_conv_w", "w_down", "w_ple_gate", "w_ple_proj")
ROW_SHARDED = ("w_out", "w_down", "w_ple_gate")
REPLICATED = ("norm_mix_g", "fox_forget_b", "sgu_norm_g", "sgu_w", "sgu_b", "norm_ffn_g", "norm_ple_g")
WEIGHTS = ("norm_mix_g", "w_in", "fox_forget_b", "shortconv_w", "sgu_norm_g", "sgu_w", "sgu_b", "w_br_fox",
           "w_br_conv", "w_br_sgu", "w_br_dil", "w_out", "norm_ffn_g", "w_up", "ffn_conv_w", "w_down",
           "norm_ple_g", "w_ple_gate", "w_ple_proj", "final_norm_g")


def _params(n_grid):
    return pltpu.CompilerParams(dimension_semantics=("arbitrary",) * n_grid, vmem_limit_bytes=VMEM_LIMIT_BYTES)


def _tile(n, pref, unit=LANES):
    best = None
    t = unit
    while t <= min(n, pref):
        if n % t == 0:
            best = t
        t += unit
    return best if best is not None else n


def _sigmoid(z):
    return 0.5 * jnp.tanh(0.5 * z) + 0.5


MAX_RESIDENT_K = 2048


def matmul(a, b, mode, out_dtype, name, res=None, tm=1536, tn=1024, tk=1536, side=None):
    if mode == "tn":
        a, mode = a.astype(BF16).T, "nn"
    if mode == "nn":
        (M, K), (K2, N) = a.shape, b.shape
    else:
        (M, K), (N, K2) = a.shape, b.shape
    assert K == K2, (name, a.shape, b.shape)
    if K <= MAX_RESIDENT_K:
        tk = K
    tm, tn, tk = _tile(M, tm), _tile(N, tn), _tile(K, tk)
    nk = K // tk
    if mode == "nn":
        a_spec, b_spec = BS((tm, tk), lambda i, j, k: (i, k)), BS((tk, tn), lambda i, j, k: (k, j))
        dims = (((1,), (0,)), ((), ()))
    else:
        a_spec, b_spec = BS((tm, tk), lambda i, j, k: (i, k)), BS((tn, tk), lambda i, j, k: (j, k))
        dims = (((1,), (1,)), ((), ()))
    has_res = res is not None

    def body(*refs):
        if has_res:
            a_ref, b_ref, r_ref, o_ref, acc = refs
        else:
            a_ref, b_ref, o_ref, acc = refs
        k = pl.program_id(2)

        @pl.when(k == 0)
        def _():
            acc[...] = jnp.zeros_like(acc)

        acc[...] += lax.dot_general(a_ref[...].astype(BF16), b_ref[...].astype(BF16), dims,
                                    preferred_element_type=F32)

        @pl.when(k == nk - 1)
        def _():
            r = acc[...]
            if has_res:
                r = r + r_ref[...]
            o_ref[...] = r.astype(o_ref.dtype)

    in_specs = [a_spec, b_spec]
    args = [a, b]
    if has_res:
        in_specs.append(BS((tm, tn), lambda i, j, k: (i, j)))
        args.append(res)
    (out,), side_out = _call_with_side(
        body, side, name, (M // tm, N // tn, nk), in_specs, [BS((tm, tn), lambda i, j, k: (i, j))],
        [SDS((M, N), out_dtype)], args, scratch=[pltpu.VMEM((tm, tn), F32)])
    return out if side is None else (out, side_out)


def rmsnorm_fwd(x, g, name):
    S, Dm = x.shape
    tm = _tile(S, 256, 8)

    def body(x_ref, g_ref, y_ref):
        xf = x_ref[...]
        r = lax.rsqrt(jnp.mean(xf * xf, axis=-1, keepdims=True) + EPS)
        y_ref[...] = ((xf * r) * g_ref[...]).astype(y_ref.dtype)

    return pl.pallas_call(
        body, name=name, grid=(S // tm,),
        in_specs=[BS((tm, Dm), lambda i: (i, 0)), BS((1, Dm), lambda i: (0, 0))],
        out_specs=BS((tm, Dm), lambda i: (i, 0)), out_shape=SDS((S, Dm), BF16),
        compiler_params=_params(1))(x, g.reshape(1, Dm))


def rmsnorm_bwd(x, g, dy, dres, name):
    S, Dm = x.shape
    tm = _tile(S, 256, 8)

    def body(x_ref, g_ref, dy_ref, dres_ref, dx_ref, dg_ref):
        xf = x_ref[...]
        r = lax.rsqrt(jnp.mean(xf * xf, axis=-1, keepdims=True) + EPS)
        xh = xf * r
        dy = dy_ref[...].astype(F32)
        dxh = dy * g_ref[...]
        dx_ref[...] = r * (dxh - xh * jnp.mean(dxh * xh, axis=-1, keepdims=True)) + dres_ref[...]

        @pl.when(pl.program_id(0) == 0)
        def _():
            dg_ref[...] = jnp.zeros_like(dg_ref)

        dg_ref[...] += jnp.sum(dy * xh, axis=0, keepdims=True)

    row = BS((tm, Dm), lambda i: (i, 0))
    vec = BS((1, Dm), lambda i: (0, 0))
    return pl.pallas_call(
        body, name=name, grid=(S // tm,), in_specs=[row, vec, row, row], out_specs=[row, vec],
        out_shape=[SDS((S, Dm), F32), SDS((1, Dm), F32)], compiler_params=_params(1))(x, g.reshape(1, Dm), dy, dres)


def final_loss(x, g, target, name):
    S, Dm = x.shape
    tm = _tile(S, 256, 8)

    def body(x_ref, g_ref, t_ref, loss_ref, dx_ref, dg_ref):
        xf = x_ref[...]
        r = lax.rsqrt(jnp.mean(xf * xf, axis=-1, keepdims=True) + EPS)
        xh = xf * r
        gv = g_ref[...]
        err = xh * gv - t_ref[...]
        dy = err * (1.0 / Dm)
        dxh = dy * gv
        dx_ref[...] = r * (dxh - xh * jnp.mean(dxh * xh, axis=-1, keepdims=True))

        @pl.when(pl.program_id(0) == 0)
        def _():
            dg_ref[...] = jnp.zeros_like(dg_ref)
            loss_ref[...] = jnp.zeros_like(loss_ref)

        dg_ref[...] += jnp.sum(dy * xh, axis=0, keepdims=True)
        part = 0.5 * jnp.sum(jnp.mean(err * err, axis=-1, keepdims=True), axis=0, keepdims=True)
        loss_ref[...] += jnp.broadcast_to(part, loss_ref.shape)

    row = BS((tm, Dm), lambda i: (i, 0))
    vec = BS((1, Dm), lambda i: (0, 0))
    return pl.pallas_call(
        body, name=name, grid=(S // tm,), in_specs=[row, vec, row],
        out_specs=[BS((1, LANES), lambda i: (0, 0)), row, vec],
        out_shape=[SDS((1, LANES), F32), SDS((S, Dm), F32), SDS((1, Dm), F32)],
        compiler_params=_params(1))(x, g.reshape(1, Dm), target)


def _dot_f32(a, b):
    return jnp.dot(a, b, preferred_element_type=F32, precision=lax.Precision.HIGHEST)


def _dot(a, b):
    return jnp.dot(a, b, preferred_element_type=F32)


def _dot_nt(a, b):
    return lax.dot_general(a, b, (((1,), (1,)), ((), ())), preferred_element_type=F32)


def fox_prep_fwd(proj, bias, name):
    S = proj.shape[0]
    H = FOX_HEADS
    nc = S // LANES

    def body(p_ref, b_ref, a_ref, f_ref):
        row = lax.broadcasted_iota(jnp.int32, (LANES, LANES), 0)
        col = lax.broadcasted_iota(jnp.int32, (LANES, LANES), 1)
        upper = (row <= col).astype(F32)
        carry = jnp.zeros((H, 1), F32)
        for c in range(nc):
            sl = slice(c * LANES, (c + 1) * LANES)
            logits = p_ref[sl, :].T[0:H, :]
            a_ref[:, sl] = logits
            z = logits + b_ref[...]
            chunk = jnp.minimum(z, 0.0) - jnp.log(1.0 + jnp.exp(-jnp.abs(z)))
            f_ref[:, sl] = _dot_f32(chunk, upper) + carry
            carry = carry + jnp.sum(chunk, axis=1, keepdims=True)

    full = BS((H, S), lambda i: (0, 0))
    return pl.pallas_call(
        body, name=name, grid=(1,),
        in_specs=[BS((S, LANES), lambda i: (0, OFF_F // LANES)), BS((H, 1), lambda i: (0, 0))],
        out_specs=[full, full], out_shape=[SDS((H, S), F32)] * 2, compiler_params=_params(1))(proj, bias)


def fox_prep_bwd(af_t, bias, d_fq, d_fk, name):
    H, S = af_t.shape
    nc = S // LANES

    def body(a_ref, b_ref, dfq_ref, dfk_ref, da_ref, db_ref):
        row = lax.broadcasted_iota(jnp.int32, (LANES, LANES), 0)
        col = lax.broadcasted_iota(jnp.int32, (LANES, LANES), 1)
        lower = (row >= col).astype(F32)
        carry = jnp.zeros((H, 1), F32)
        dbias = jnp.zeros((H, 1), F32)
        for c in range(nc - 1, -1, -1):
            sl = slice(c * LANES, (c + 1) * LANES)
            chunk = dfq_ref[:, sl] + dfk_ref[:, sl]
            dlogf = _dot_f32(chunk, lower) + carry
            carry = carry + jnp.sum(chunk, axis=1, keepdims=True)
            z = a_ref[:, sl] + b_ref[...]
            da = dlogf / (1.0 + jnp.exp(z))
            da_ref[:, sl] = da
            dbias = dbias + jnp.sum(da, axis=1, keepdims=True)
        db_ref[...] = dbias

    full = BS((H, S), lambda i: (0, 0))
    vec = BS((H, 1), lambda i: (0, 0))
    return pl.pallas_call(
        body, name=name, grid=(1,), in_specs=[full, vec, full, full], out_specs=[full, vec],
        out_shape=[SDS((H, S), F32), SDS((H, 1), F32)], compiler_params=_params(1))(af_t, bias, d_fq, d_fk)


FOX_T = 256
FOX_HP = 4


def _causal_tile(T):
    return lax.broadcasted_iota(jnp.int32, (T, T), 1) <= lax.broadcasted_iota(jnp.int32, (T, T), 0)


class SideExchange:
    def __init__(self, operands, out_shapes, n_sems, start, finish):
        self.operands, self.out_shapes, self.n_sems, self.start, self.finish = operands, out_shapes, n_sems, start, finish


def _call_with_side(body, side, name, grid, in_specs, out_specs, out_shape, args, scratch=()):
    scratch = list(scratch)
    if side is None:
        return pl.pallas_call(body, name=name, grid=grid, in_specs=in_specs, out_specs=out_specs,
                              out_shape=out_shape, scratch_shapes=scratch,
                              compiler_params=_params(len(grid)))(*args), []
    n_in, n_out = len(in_specs), len(out_specs)
    s_in, s_out = len(side.operands), len(side.out_shapes)

    def wrapped(*refs):
        main_in, side_in = refs[:n_in], refs[n_in:n_in + s_in]
        main_out = refs[n_in + s_in:n_in + s_in + n_out]
        side_out = refs[n_in + s_in + n_out:n_in + s_in + n_out + s_out]
        main_scratch = refs[n_in + s_in + n_out + s_out:-2]
        send_sems, recv_sems = refs[-2:]
        first = pl.program_id(0) == 0
        last = pl.program_id(0) == grid[0] - 1
        for axis in range(1, len(grid)):
            first = jnp.logical_and(first, pl.program_id(axis) == 0)
            last = jnp.logical_and(last, pl.program_id(axis) == grid[axis] - 1)

        @pl.when(first)
        def _():
            side.start(side_in, side_out, send_sems, recv_sems)

        body(*main_in, *main_out, *main_scratch)

        @pl.when(last)
        def _():
            side.finish(side_in, side_out, send_sems, recv_sems)

    outs = pl.pallas_call(
        wrapped, name=name, grid=grid, in_specs=list(in_specs) + [ANY] * s_in,
        out_specs=list(out_specs) + [ANY] * s_out, out_shape=list(out_shape) + list(side.out_shapes),
        scratch_shapes=scratch + [pltpu.SemaphoreType.DMA((side.n_sems,)), pltpu.SemaphoreType.DMA((side.n_sems,))],
        compiler_params=_params(len(grid)))(*args, *side.operands)
    return outs[:n_out], outs[n_out:]


def _fox_specs(H, S, Dh, T):
    nq = S // T
    blk = BS((FOX_HP, T, Dh), lambda h, i: (h, i, 0))
    full = BS((FOX_HP, S, Dh), lambda h, i: (h, 0, 0))
    colb = BS((FOX_HP, T, 1), lambda h, i: (h, i, 0))
    rowf = BS((FOX_HP, nq, 1, T), lambda h, i: (h, 0, 0, 0))
    return blk, full, colb, rowf, (H // FOX_HP, nq)


def fox_attn_fwd(q, k, v, f_col, f_row, name, side=None):
    H, S, Dh = q.shape
    T = min(FOX_T, S)
    scale = Dh ** -0.5

    def body(q_ref, k_ref, v_ref, fc_ref, fr_ref, o_ref, lse_ref):
        qi = pl.program_id(1)
        qs = [q_ref[h] for h in range(FOX_HP)]
        fqs = [fc_ref[h] for h in range(FOX_HP)]

        def step(j, carry, diagonal):
            off = pl.multiple_of(j * T, T)
            out = []
            for h in range(FOX_HP):
                m, l, acc = carry[h]
                kv = k_ref[h, pl.ds(off, T), :]
                vv = v_ref[h, pl.ds(off, T), :]
                s = _dot_nt(qs[h], kv) * scale + (fqs[h] - fr_ref[h, j])
                if diagonal:
                    s = jnp.where(_causal_tile(T), s, NEG)
                m_new = jnp.maximum(m, jnp.max(s, axis=-1, keepdims=True))
                p = jnp.exp(s - m_new)
                alpha = jnp.exp(m - m_new)
                l = alpha * l + jnp.sum(p, axis=-1, keepdims=True)
                acc = alpha * acc + _dot(p.astype(BF16), vv)
                out.append((m_new, l, acc))
            return tuple(out)

        init = tuple((jnp.full((T, 1), NEG, F32), jnp.zeros((T, 1), F32), jnp.zeros((T, Dh), F32))
                     for _ in range(FOX_HP))
        carry = lax.fori_loop(0, qi, functools.partial(step, diagonal=False), init)
        carry = step(qi, carry, True)
        for h in range(FOX_HP):
            m, l, acc = carry[h]
            o_ref[h] = (acc / l).astype(o_ref.dtype)
            lse_ref[h] = m + jnp.log(l)

    blk, full, colb, rowf, grid = _fox_specs(H, S, Dh, T)
    return _call_with_side(body, side, name, grid, [blk, full, full, colb, rowf], [blk, colb],
                           [SDS((H, S, Dh), BF16), SDS((H, S, 1), F32)], (q, k, v, f_col, f_row))


def fox_attn_bwd_dq(q, k, v, f_col, f_row, o, lse, do, name, side=None):
    H, S, Dh = q.shape
    T = min(FOX_T, S)
    scale = Dh ** -0.5

    def body(q_ref, k_ref, v_ref, fc_ref, fr_ref, o_ref, lse_ref, do_ref, dq_ref, dl_ref, df_ref):
        qi = pl.program_id(1)
        qs = [q_ref[h] for h in range(FOX_HP)]
        fqs = [fc_ref[h] for h in range(FOX_HP)]
        lses = [lse_ref[h] for h in range(FOX_HP)]
        dos = [do_ref[h] for h in range(FOX_HP)]
        deltas = [jnp.sum(dos[h].astype(F32) * o_ref[h].astype(F32), axis=-1, keepdims=True) for h in range(FOX_HP)]
        for h in range(FOX_HP):
            dl_ref[h] = deltas[h]

        def step(j, carry, diagonal):
            off = pl.multiple_of(j * T, T)
            out = []
            for h in range(FOX_HP):
                dq, dfq = carry[h]
                kv = k_ref[h, pl.ds(off, T), :]
                vv = v_ref[h, pl.ds(off, T), :]
                s = _dot_nt(qs[h], kv) * scale + (fqs[h] - fr_ref[h, j])
                if diagonal:
                    s = jnp.where(_causal_tile(T), s, NEG)
                p = jnp.exp(s - lses[h])
                ds = p * (_dot_nt(dos[h], vv) - deltas[h])
                out.append((dq + _dot(ds.astype(BF16), kv), dfq + jnp.sum(ds, axis=-1, keepdims=True)))
            return tuple(out)

        init = tuple((jnp.zeros((T, Dh), F32), jnp.zeros((T, 1), F32)) for _ in range(FOX_HP))
        carry = lax.fori_loop(0, qi, functools.partial(step, diagonal=False), init)
        carry = step(qi, carry, True)
        for h in range(FOX_HP):
            dq_ref[h] = carry[h][0] * scale
            df_ref[h] = carry[h][1]

    blk, full, colb, rowf, grid = _fox_specs(H, S, Dh, T)
    return _call_with_side(body, side, name, grid, [blk, full, full, colb, rowf, blk, colb, blk], [blk, colb, colb],
                           [SDS((H, S, Dh), F32), SDS((H, S, 1), F32), SDS((H, S, 1), F32)],
                           (q, k, v, f_col, f_row, o, lse, do))


def fox_attn_bwd_dkv(q, k, v, f_col, f_row, lse_row, delta_row, do, name, side=None):
    H, S, Dh = q.shape
    T = min(FOX_T, S)
    nq = S // T
    scale = Dh ** -0.5

    def body(q_ref, k_ref, v_ref, fc_ref, fr_ref, lse_ref, dl_ref, do_ref, dk_ref, dv_ref, df_ref):
        kj = pl.program_id(1)
        ks = [k_ref[h] for h in range(FOX_HP)]
        vs = [v_ref[h] for h in range(FOX_HP)]
        fks = [fc_ref[h] for h in range(FOX_HP)]

        def step(i, carry, diagonal):
            off = pl.multiple_of(i * T, T)
            out = []
            for h in range(FOX_HP):
                dk, dv, dfk = carry[h]
                qv = q_ref[h, pl.ds(off, T), :]
                dov = do_ref[h, pl.ds(off, T), :]
                st = _dot_nt(ks[h], qv) * scale + (fr_ref[h, i] - fks[h])
                if diagonal:
                    st = jnp.where(lax.broadcasted_iota(jnp.int32, (T, T), 0)
                                   <= lax.broadcasted_iota(jnp.int32, (T, T), 1), st, NEG)
                pt = jnp.exp(st - lse_ref[h, i])
                dv = dv + _dot(pt.astype(BF16), dov)
                dst = pt * (_dot_nt(vs[h], dov) - dl_ref[h, i])
                dk = dk + _dot(dst.astype(BF16), qv)
                out.append((dk, dv, dfk + jnp.sum(dst, axis=-1, keepdims=True)))
            return tuple(out)

        init = tuple((jnp.zeros((T, Dh), F32), jnp.zeros((T, Dh), F32), jnp.zeros((T, 1), F32))
                     for _ in range(FOX_HP))
        carry = step(kj, init, True)
        carry = lax.fori_loop(kj + 1, nq, functools.partial(step, diagonal=False), carry)
        for h in range(FOX_HP):
            dk_ref[h] = carry[h][0] * scale
            dv_ref[h] = carry[h][1]
            df_ref[h] = -carry[h][2]

    blk, full, colb, rowf, grid = _fox_specs(H, S, Dh, T)
    return _call_with_side(body, side, name, grid, [full, blk, blk, colb, rowf, rowf, rowf, full], [blk, blk, colb],
                           [SDS((H, S, Dh), F32), SDS((H, S, Dh), F32), SDS((H, S, 1), F32)],
                           (q, k, v, f_col, f_row, lse_row, delta_row, do))


HALO = 8
CONV_TM = 512


def _conv_ext(e, w_ref):
    return w_ref[0:1, :] * pltpu.roll(e, 2, 0) + w_ref[1:2, :] * pltpu.roll(e, 1, 0) + w_ref[2:3, :] * e


def _conv_t_ext(d, w_ref):
    n = d.shape[0]
    return w_ref[2:3, :] * d + w_ref[1:2, :] * pltpu.roll(d, n - 1, 0) + w_ref[0:1, :] * pltpu.roll(d, n - 2, 0)


def _time_specs(S, tm, width, col_block):
    per = tm // HALO
    last = S // HALO - 1
    cur = BS((tm, width), lambda j, i: (i, col_block(j)))
    prev = BS((HALO, width), lambda j, i: (jnp.maximum(i * per - 1, 0), col_block(j)))
    nxt = BS((HALO, width), lambda j, i: (jnp.minimum((i + 1) * per, last), col_block(j)))
    return cur, prev, nxt


def shortconv_fwd(proj, w, name):
    S = proj.shape[0]
    tm = _tile(S, CONV_TM, 8)
    nb = CONV_WIDTH // LANES
    b0 = OFF_B // LANES

    def body(xb_ref, xbp_ref, gb_ref, gc_ref, gcp_ref, w_ref, o_ref):
        keep = (pl.program_id(1) > 0).astype(F32)
        e = jnp.concatenate([gcp_ref[...] * xbp_ref[...] * keep, gc_ref[...] * xb_ref[...]], axis=0)
        o_ref[...] = (gb_ref[...] * _conv_ext(e, w_ref)[HALO:, :]).astype(o_ref.dtype)

    xb, xbp, _ = _time_specs(S, tm, LANES, lambda j: b0 + j)
    gb, _, _ = _time_specs(S, tm, LANES, lambda j: b0 + nb + j)
    gc, gcp, _ = _time_specs(S, tm, LANES, lambda j: b0 + 2 * nb + j)
    return pl.pallas_call(
        body, name=name, grid=(nb, S // tm),
        in_specs=[xb, xbp, gb, gc, gcp, BS((3, LANES), lambda j, i: (0, j))],
        out_specs=BS((tm, LANES), lambda j, i: (i, j)), out_shape=SDS((S, CONV_WIDTH), BF16),
        compiler_params=_params(2))(proj, proj, proj, proj, proj, w)


def shortconv_bwd(proj, w, do_b, name):
    S = proj.shape[0]
    tm = _tile(S, CONV_TM, 8)
    nt = S // tm
    nb = CONV_WIDTH // LANES
    b0 = OFF_B // LANES

    def body(xb_ref, xbp_ref, gb_ref, gbn_ref, gc_ref, gcp_ref, do_ref, don_ref, w_ref,
             dxb_ref, dgb_ref, dgc_ref, dw_ref):
        i = pl.program_id(1)
        keep_prev = (i > 0).astype(F32)
        keep_next = (i < nt - 1).astype(F32)
        xb, gb, gc = xb_ref[...], gb_ref[...], gc_ref[...]
        do = do_ref[...].astype(F32)
        u = gc * xb
        e = jnp.concatenate([gcp_ref[...] * xbp_ref[...] * keep_prev, u], axis=0)
        dgb_ref[...] = (do * _conv_ext(e, w_ref)[HALO:, :]).astype(dgb_ref.dtype)
        dcv = do * gb
        d_ext = jnp.concatenate([dcv, don_ref[...].astype(F32) * gbn_ref[...] * keep_next], axis=0)
        du = _conv_t_ext(d_ext, w_ref)[:tm, :]
        dgc_ref[...] = (du * xb).astype(dgc_ref.dtype)
        dxb_ref[...] = (du * gc).astype(dxb_ref.dtype)

        @pl.when(i == 0)
        def _():
            dw_ref[...] = jnp.zeros_like(dw_ref)

        dw_ref[0:1, :] += jnp.sum(dcv * pltpu.roll(e, 2, 0)[HALO:, :], axis=0, keepdims=True)
        dw_ref[1:2, :] += jnp.sum(dcv * pltpu.roll(e, 1, 0)[HALO:, :], axis=0, keepdims=True)
        dw_ref[2:3, :] += jnp.sum(dcv * u, axis=0, keepdims=True)

    xb, xbp, _ = _time_specs(S, tm, LANES, lambda j: b0 + j)
    gb, _, gbn = _time_specs(S, tm, LANES, lambda j: b0 + nb + j)
    gc, gcp, _ = _time_specs(S, tm, LANES, lambda j: b0 + 2 * nb + j)
    do, _, don = _time_specs(S, tm, LANES, lambda j: j)
    out = BS((tm, LANES), lambda j, i: (i, j))
    wspec = BS((3, LANES), lambda j, i: (0, j))
    return pl.pallas_call(
        body, name=name, grid=(nb, nt), in_specs=[xb, xbp, gb, gbn, gc, gcp, do, don, wspec],
        out_specs=[out, out, out, wspec],
        out_shape=[SDS((S, CONV_WIDTH), BF16)] * 3 + [SDS((3, CONV_WIDTH), F32)],
        compiler_params=_params(2))(proj, proj, proj, proj, proj, proj, do_b, do_b, w)


_GELU_C = 0.7978845608028654


def _gelu(x):
    return 0.5 * x * (1.0 + jnp.tanh(_GELU_C * (x + 0.044715 * x * x * x)))


def _gelu_grad(x):
    t = jnp.tanh(_GELU_C * (x + 0.044715 * x * x * x))
    return 0.5 * (1.0 + t) + 0.5 * x * (1.0 - t * t) * _GELU_C * (1.0 + 3.0 * 0.044715 * x * x)


def _tril_masks():
    row = lax.broadcasted_iota(jnp.int32, (SGU_CHUNK, SGU_CHUNK), 0)
    col = lax.broadcasted_iota(jnp.int32, (SGU_CHUNK, SGU_CHUNK), 1)
    return row >= col, row <= col


def _lane_column(mat, g):
    lane = lax.broadcasted_iota(jnp.int32, mat.shape, 1)
    return jnp.sum(jnp.where(lane == g, mat, 0.0), axis=-1, keepdims=True)


def sgu_fwd(proj, norm_g, w_s, b_t, name):
    S = proj.shape[0]
    T = SGU_CHUNK
    G = SGU_WIDTH // T
    c0 = OFF_C // (2 * SGU_WIDTH)

    def body(c_ref, g_ref, w_ref, b_ref, o_ref):
        u = _gelu(c_ref[:, 0:SGU_WIDTH])
        v = _gelu(c_ref[:, SGU_WIDTH:2 * SGU_WIDTH])
        r = lax.rsqrt(jnp.mean(v * v, axis=-1, keepdims=True) + EPS)
        vn = ((v * r) * g_ref[...]).astype(BF16)
        mask, _ = _tril_masks()
        bias = b_ref[...]
        for g in range(G):
            sl = slice(g * T, (g + 1) * T)
            wt = jnp.where(mask, w_ref[g], 0.0).astype(BF16)
            mixed = _dot(wt, vn[:, sl]) + _lane_column(bias, g)
            o_ref[:, sl] = (u[:, sl] * mixed).astype(o_ref.dtype)

    return pl.pallas_call(
        body, name=name, grid=(S // T,),
        in_specs=[BS((T, 2 * SGU_WIDTH), lambda i: (i, c0)), BS((1, SGU_WIDTH), lambda i: (0, 0)),
                  BS((G, T, T), lambda i: (0, 0, 0)), BS((T, T), lambda i: (0, 0))],
        out_specs=BS((T, SGU_WIDTH), lambda i: (i, 0)), out_shape=SDS((S, SGU_WIDTH), BF16),
        compiler_params=_params(1))(proj, norm_g, w_s, b_t)


def sgu_bwd(proj, norm_g, w_s, w_st, b_t, do_c, name):
    S = proj.shape[0]
    T = SGU_CHUNK
    G = SGU_WIDTH // T
    c0 = OFF_C // (2 * SGU_WIDTH)

    def body(c_ref, g_ref, w_ref, wt_ref, b_ref, do_ref, dc_ref, dg_ref, dw_ref, db_ref):
        cu = c_ref[:, 0:SGU_WIDTH]
        cv = c_ref[:, SGU_WIDTH:2 * SGU_WIDTH]
        u = _gelu(cu)
        v = _gelu(cv)
        r = lax.rsqrt(jnp.mean(v * v, axis=-1, keepdims=True) + EPS)
        xh = v * r
        gv = g_ref[...]
        vn = (xh * gv).astype(BF16)
        do = do_ref[...].astype(F32)
        mask, mask_t = _tril_masks()
        bias = b_ref[...]
        lane = lax.broadcasted_iota(jnp.int32, (T, T), 1)

        @pl.when(pl.program_id(0) == 0)
        def _():
            dg_ref[...] = jnp.zeros_like(dg_ref)
            dw_ref[...] = jnp.zeros_like(dw_ref)
            db_ref[...] = jnp.zeros_like(db_ref)

        dvn_parts = []
        du_parts = []
        dbias = jnp.zeros((T, T), F32)
        for g in range(G):
            sl = slice(g * T, (g + 1) * T)
            wt = jnp.where(mask, w_ref[g], 0.0).astype(BF16)
            mixed = _dot(wt, vn[:, sl]) + _lane_column(bias, g)
            du_parts.append(do[:, sl] * mixed)
            dmix = do[:, sl] * u[:, sl]
            dmix_b = dmix.astype(BF16)
            dw_ref[g] += jnp.where(mask, _dot_nt(dmix_b, vn[:, sl]), 0.0)
            dbias = dbias + jnp.where(lane == g, jnp.sum(dmix, axis=-1, keepdims=True), 0.0)
            wtt = jnp.where(mask_t, wt_ref[g], 0.0).astype(BF16)
            dvn_parts.append(_dot(wtt, dmix_b))
        db_ref[...] += dbias
        dvn = jnp.concatenate(dvn_parts, axis=-1)
        du = jnp.concatenate(du_parts, axis=-1)
        dg_ref[...] += jnp.sum(dvn * xh, axis=0, keepdims=True)
        dxh = dvn * gv
        dv = r * (dxh - xh * jnp.mean(dxh * xh, axis=-1, keepdims=True))
        dc_ref[:, 0:SGU_WIDTH] = (du * _gelu_grad(cu)).astype(dc_ref.dtype)
        dc_ref[:, SGU_WIDTH:2 * SGU_WIDTH] = (dv * _gelu_grad(cv)).astype(dc_ref.dtype)

    wspec = BS((G, T, T), lambda i: (0, 0, 0))
    gspec = BS((1, SGU_WIDTH), lambda i: (0, 0))
    return pl.pallas_call(
        body, name=name, grid=(S // T,),
        in_specs=[BS((T, 2 * SGU_WIDTH), lambda i: (i, c0)), gspec, wspec, wspec, BS((T, T), lambda i: (0, 0)),
                  BS((T, SGU_WIDTH), lambda i: (i, 0))],
        out_specs=[BS((T, 2 * SGU_WIDTH), lambda i: (i, 0)), gspec, wspec, BS((T, T), lambda i: (0, 0))],
        out_shape=[SDS((S, 2 * SGU_WIDTH), BF16), SDS((1, SGU_WIDTH), F32), SDS((G, T, T), F32), SDS((T, T), F32)],
        compiler_params=_params(1))(proj, norm_g, w_s, w_st, b_t, do_c)


def rope_tables(positions):
    half = ROPE_DIM // 2
    inv = ROPE_THETA ** (-jnp.arange(half, dtype=F32) * (2.0 / ROPE_DIM))
    ang = positions.astype(F32)[:, None] * inv
    cos, sin = jnp.cos(ang), jnp.sin(ang)
    S = positions.shape[0]
    ones = jnp.ones((S, HEAD_DIM - ROPE_DIM), F32)
    zeros = jnp.zeros((S, HEAD_DIM - ROPE_DIM), F32)
    zh = jnp.zeros((S, half), F32)
    c = jnp.concatenate([cos, cos, ones], axis=-1)
    sa = jnp.concatenate([zh, sin, zeros], axis=-1)
    sb = jnp.concatenate([-sin, zh, zeros], axis=-1)
    tile2 = lambda t: jnp.concatenate([t, t], axis=-1)
    return tile2(c), tile2(sa), tile2(sb)


def rope_apply(x, col0, ncols, cos, sa, sb, out_dtype, name):
    S = x.shape[0]
    tm = _tile(S, 512, 8)
    half = ROPE_DIM // 2
    b0 = col0 // LANES

    def body(x_ref, c_ref, sa_ref, sb_ref, o_ref):
        xv = x_ref[...].astype(F32)
        o_ref[...] = (xv * c_ref[...] + pltpu.roll(xv, half, 1) * sa_ref[...]
                      + pltpu.roll(xv, LANES - half, 1) * sb_ref[...]).astype(o_ref.dtype)

    tab = BS((tm, LANES), lambda i, j: (i, 0))
    return pl.pallas_call(
        body, name=name, grid=(S // tm, ncols // LANES),
        in_specs=[BS((tm, LANES), lambda i, j: (i, b0 + j)), tab, tab, tab],
        out_specs=BS((tm, LANES), lambda i, j: (i, j)), out_shape=SDS((S, ncols), out_dtype),
        compiler_params=_params(2))(x, cos, sa, sb)


def _dil_masks(first):
    qi = lax.broadcasted_iota(jnp.int32, (DIL_SPAN, DIL_SPAN), 0)
    ki = lax.broadcasted_iota(jnp.int32, (DIL_SPAN, DIL_SPAN), 1)
    return ki >= qi + first.astype(jnp.int32) * DIL_SPAN, ki <= qi


def dil_attn_fwd(q, k, v, nb, name):
    H, S, Dh = q.shape
    T = DIL_SPAN
    nblk = S // T
    scale = Dh ** -0.5

    def body(q_ref, kp_ref, kc_ref, vp_ref, vc_ref, o_ref, lse_ref):
        mp, mc = _dil_masks(pl.program_id(0) % nb == 0)
        for h in range(H):
            qv = q_ref[h]
            sp = jnp.where(mp, _dot_nt(qv, kp_ref[h]) * scale, NEG)
            sc = jnp.where(mc, _dot_nt(qv, kc_ref[h]) * scale, NEG)
            m = jnp.maximum(jnp.max(sp, axis=-1, keepdims=True), jnp.max(sc, axis=-1, keepdims=True))
            ep = jnp.exp(sp - m)
            ec = jnp.exp(sc - m)
            l = jnp.sum(ep, axis=-1, keepdims=True) + jnp.sum(ec, axis=-1, keepdims=True)
            o_ref[h] = (_dot(ep.astype(BF16), vp_ref[h]) + _dot(ec.astype(BF16), vc_ref[h])) / l
            lse_ref[h] = m + jnp.log(l)

    cur = BS((H, T, Dh), lambda b: (0, b, 0))
    prev = BS((H, T, Dh), lambda b: (0, jnp.maximum(b - 1, 0), 0))
    colb = BS((H, T, 1), lambda b: (0, b, 0))
    return pl.pallas_call(
        body, name=name, grid=(nblk,), in_specs=[cur, prev, cur, prev, cur], out_specs=[cur, colb],
        out_shape=[SDS((H, S, Dh), F32), SDS((H, S, 1), F32)], compiler_params=_params(1))(q, k, k, v, v)


def dil_attn_bwd_dq(q, k, v, o, lse, do, dlse, nb, name):
    H, S, Dh = q.shape
    T = DIL_SPAN
    nblk = S // T
    scale = Dh ** -0.5

    def body(q_ref, kp_ref, kc_ref, vp_ref, vc_ref, o_ref, lse_ref, do_ref, dlse_ref, dq_ref, dl_ref):
        mp, mc = _dil_masks(pl.program_id(0) % nb == 0)
        for h in range(H):
            qv = q_ref[h]
            kp, kc = kp_ref[h], kc_ref[h]
            dov = do_ref[h]
            delta = jnp.sum(dov * o_ref[h], axis=-1, keepdims=True) - dlse_ref[h]
            dl_ref[h] = delta
            dob = dov.astype(BF16)
            lse_v = lse_ref[h]
            pp = jnp.exp(jnp.where(mp, _dot_nt(qv, kp) * scale, NEG) - lse_v)
            pc = jnp.exp(jnp.where(mc, _dot_nt(qv, kc) * scale, NEG) - lse_v)
            dsp = pp * (_dot_nt(dob, vp_ref[h]) - delta)
            dsc = pc * (_dot_nt(dob, vc_ref[h]) - delta)
            dq_ref[h] = (_dot(dsp.astype(BF16), kp) + _dot(dsc.astype(BF16), kc)) * scale

    cur = BS((H, T, Dh), lambda b: (0, b, 0))
    prev = BS((H, T, Dh), lambda b: (0, jnp.maximum(b - 1, 0), 0))
    colb = BS((H, T, 1), lambda b: (0, b, 0))
    return pl.pallas_call(
        body, name=name, grid=(nblk,), in_specs=[cur, prev, cur, prev, cur, cur, colb, cur, colb],
        out_specs=[cur, colb], out_shape=[SDS((H, S, Dh), F32), SDS((H, S, 1), F32)],
        compiler_params=_params(1))(q, k, k, v, v, o, lse, do, dlse)


def dil_attn_bwd_dkv(q, k, v, lse_row, delta_row, do, nb, name):
    H, S, Dh = q.shape
    T = DIL_SPAN
    nblk = S // T
    scale = Dh ** -0.5

    def body(k_ref, v_ref, qc_ref, qn_ref, doc_ref, don_ref, lc_ref, ln_ref, dc_ref, dn_ref, dk_ref, dv_ref):
        no_next = ((pl.program_id(0) + 1) % nb == 0).astype(jnp.int32)
        si = lax.broadcasted_iota(jnp.int32, (T, T), 0)
        ti = lax.broadcasted_iota(jnp.int32, (T, T), 1)
        m_cur = si <= ti
        m_next = si >= ti + no_next * T
        for h in range(H):
            kv, vv = k_ref[h], v_ref[h]
            qc, qn = qc_ref[h], qn_ref[h]
            doc, don = doc_ref[h].astype(BF16), don_ref[h].astype(BF16)
            pt = jnp.exp(jnp.where(m_cur, _dot_nt(kv, qc) * scale, NEG) - lc_ref[h])
            ptn = jnp.exp(jnp.where(m_next, _dot_nt(kv, qn) * scale, NEG) - ln_ref[h])
            dv_ref[h] = _dot(pt.astype(BF16), doc) + _dot(ptn.astype(BF16), don)
            dst = pt * (_dot_nt(vv, doc) - dc_ref[h])
            dstn = ptn * (_dot_nt(vv, don) - dn_ref[h])
            dk_ref[h] = (_dot(dst.astype(BF16), qc) + _dot(dstn.astype(BF16), qn)) * scale

    cur = BS((H, T, Dh), lambda b: (0, b, 0))
    nxt = BS((H, T, Dh), lambda b: (0, jnp.minimum(b + 1, nblk - 1), 0))
    rcur = BS((H, None, 1, T), lambda b: (0, b, 0, 0))
    rnxt = BS((H, None, 1, T), lambda b: (0, jnp.minimum(b + 1, nblk - 1), 0, 0))
    return pl.pallas_call(
        body, name=name, grid=(nblk,), in_specs=[cur, cur, cur, nxt, cur, nxt, rcur, rnxt, rcur, rnxt],
        out_specs=[cur, cur], out_shape=[SDS((H, S, Dh), F32), SDS((H, S, Dh), F32)],
        compiler_params=_params(1))(k, v, q, q, do, do, lse_row, lse_row, delta_row, delta_row)


def dil_merge_fwd(outs, lses, name):
    H, S, Dh = outs[0].shape
    tm = _tile(S, 512, 8)

    def body(o0, o1, o2, l0, l1, l2, out_ref):
        ls = [l0[...], l1[...], l2[...]]
        m = jnp.maximum(jnp.maximum(ls[0], ls[1]), ls[2])
        es = [jnp.exp(l - m) for l in ls]
        den = es[0] + es[1] + es[2]
        out_ref[...] = (es[0] * o0[...] + es[1] * o1[...] + es[2] * o2[...]) / den

    blk = BS((None, tm, Dh), lambda h, i: (h, i, 0))
    colb = BS((None, tm, 1), lambda h, i: (h, i, 0))
    return pl.pallas_call(
        body, name=name, grid=(H, S // tm), in_specs=[blk] * 3 + [colb] * 3, out_specs=blk,
        out_shape=SDS((H, S, Dh), F32), compiler_params=_params(2))(*outs, *lses)


def dil_merge_bwd(outs, lses, d_out, name):
    H, S, Dh = outs[0].shape
    tm = _tile(S, 512, 8)

    def body(o0, o1, o2, l0, l1, l2, d_ref, do0, do1, do2, dl0, dl1, dl2):
        ls = [l0[...], l1[...], l2[...]]
        m = jnp.maximum(jnp.maximum(ls[0], ls[1]), ls[2])
        es = [jnp.exp(l - m) for l in ls]
        den = es[0] + es[1] + es[2]
        ws = [e / den for e in es]
        dv = d_ref[...]
        dws = [jnp.sum(dv * o[...], axis=-1, keepdims=True) for o in (o0, o1, o2)]
        mean = ws[0] * dws[0] + ws[1] * dws[1] + ws[2] * dws[2]
        for w, dw, do_ref, dl_ref in zip(ws, dws, (do0, do1, do2), (dl0, dl1, dl2)):
            do_ref[...] = w * dv
            dl_ref[...] = w * (dw - mean)

    blk = BS((None, tm, Dh), lambda h, i: (h, i, 0))
    colb = BS((None, tm, 1), lambda h, i: (h, i, 0))
    return pl.pallas_call(
        body, name=name, grid=(H, S // tm), in_specs=[blk] * 3 + [colb] * 3 + [blk],
        out_specs=[blk] * 3 + [colb] * 3,
        out_shape=[SDS((H, S, Dh), F32)] * 3 + [SDS((H, S, 1), F32)] * 3,
        compiler_params=_params(2))(*outs, *lses, d_out)


MERGE_TN = 512


def merge_fwd(o_stack, w_br, proj, name):
    _, S, K = o_stack.shape
    tm = _tile(S, 512, 8)
    tn = MERGE_TN
    nj = D_MODEL // tn

    def body(o_ref, w_ref, gl_ref, m_ref, acc):
        br = pl.program_id(2)

        @pl.when(br == 0)
        def _():
            acc[...] = jnp.zeros_like(acc)

        acc[...] += _sigmoid(gl_ref[...]) * _dot(o_ref[...], w_ref[...])

        @pl.when(br == N_BRANCH - 1)
        def _():
            m_ref[...] = acc[...].astype(m_ref.dtype)

    return pl.pallas_call(
        body, name=name, grid=(S // tm, nj, N_BRANCH),
        in_specs=[BS((None, tm, K), lambda i, j, b: (b, i, 0)), BS((None, K, tn), lambda i, j, b: (b, 0, j)),
                  BS((tm, tn), lambda i, j, b: (i, b * nj + j))],
        out_specs=BS((tm, tn), lambda i, j, b: (i, j)), out_shape=SDS((S, D_MODEL), BF16),
        scratch_shapes=[pltpu.VMEM((tm, tn), F32)], compiler_params=_params(3))(o_stack, w_br, proj)


def merge_bwd(o_stack, w_br, proj, d_merged, name):
    _, S, K = o_stack.shape
    tm = _tile(S, 512, 8)
    tn = MERGE_TN
    nj = D_MODEL // tn

    def body(o_ref, w_ref, gl_ref, dm_ref, dgl_ref, dy_ref):
        gate = _sigmoid(gl_ref[...])
        y = _dot(o_ref[...], w_ref[...])
        dm = dm_ref[...].astype(F32)
        dgl_ref[...] = (dm * y * gate * (1.0 - gate)).astype(dgl_ref.dtype)
        dy_ref[...] = (dm * gate).astype(dy_ref.dtype)

    return pl.pallas_call(
        body, name=name, grid=(S // tm, nj, N_BRANCH),
        in_specs=[BS((None, tm, K), lambda i, j, b: (b, i, 0)), BS((None, K, tn), lambda i, j, b: (b, 0, j)),
                  BS((tm, tn), lambda i, j, b: (i, b * nj + j)), BS((tm, tn), lambda i, j, b: (i, j))],
        out_specs=[BS((tm, tn), lambda i, j, b: (i, b * nj + j)), BS((None, tm, tn), lambda i, j, b: (b, i, j))],
        out_shape=[SDS((S, N_BRANCH * D_MODEL), BF16), SDS((N_BRANCH, S, D_MODEL), BF16)],
        compiler_params=_params(3))(o_stack, w_br, proj, d_merged)


FFN_CW = 256


def ffn_mid_fwd(pre_g, pre_v, w_g, w_v, name, side=None):
    S = pre_g.shape[0]
    tm = _tile(S, CONV_TM, 8)

    def body(g_ref, gp_ref, v_ref, vp_ref, wg_ref, wv_ref, a_ref):
        keep = (pl.program_id(1) > 0).astype(F32)
        ug = _conv_ext(jnp.concatenate([gp_ref[...] * keep, g_ref[...]], axis=0), wg_ref)[HALO:, :]
        uv = _conv_ext(jnp.concatenate([vp_ref[...] * keep, v_ref[...]], axis=0), wv_ref)[HALO:, :]
        a_ref[...] = (ug * _sigmoid(ug) * uv).astype(a_ref.dtype)

    cur, prev, _ = _time_specs(S, tm, FFN_CW, lambda j: j)
    wspec = BS((3, FFN_CW), lambda j, i: (0, j))
    (a,), side_out = _call_with_side(body, side, name, (D_FF // FFN_CW, S // tm),
                                     [cur, prev, cur, prev, wspec, wspec], [cur], [SDS((S, D_FF), BF16)],
                                     (pre_g, pre_g, pre_v, pre_v, w_g, w_v))
    return a, side_out


def ffn_mid_bwd(pre_g, pre_v, w_g, w_v, d_a, name, side=None):
    S = pre_g.shape[0]
    tm = _tile(S, CONV_TM, 8)
    nt = S // tm

    def body(g_ref, gp_ref, gn_ref, v_ref, vp_ref, vn_ref, wg_ref, wv_ref, da_ref, dan_ref,
             dg_ref, dv_ref, dwg_ref, dwv_ref):
        i = pl.program_id(1)
        keep_prev = (i > 0).astype(F32)
        keep_next = (i < nt - 1).astype(F32)
        eg = jnp.concatenate([gp_ref[...] * keep_prev, g_ref[...], gn_ref[...]], axis=0)
        ev = jnp.concatenate([vp_ref[...] * keep_prev, v_ref[...], vn_ref[...]], axis=0)
        ug = _conv_ext(eg, wg_ref)
        uv = _conv_ext(ev, wv_ref)
        da = jnp.concatenate([jnp.zeros((HALO, FFN_CW), F32), da_ref[...], dan_ref[...] * keep_next], axis=0)
        sg = _sigmoid(ug)
        dug = da * uv * (sg * (1.0 + ug * (1.0 - sg)))
        duv = da * (ug * sg)
        dg_ref[...] = _conv_t_ext(dug, wg_ref)[HALO:HALO + tm, :].astype(dg_ref.dtype)
        dv_ref[...] = _conv_t_ext(duv, wv_ref)[HALO:HALO + tm, :].astype(dv_ref.dtype)

        @pl.when(i == 0)
        def _():
            dwg_ref[...] = jnp.zeros_like(dwg_ref)
            dwv_ref[...] = jnp.zeros_like(dwv_ref)

        for dup_e, e, dw_ref in ((dug, eg, dwg_ref), (duv, ev, dwv_ref)):
            dup = dup_e[HALO:HALO + tm, :]
            dw_ref[0:1, :] += jnp.sum(dup * pltpu.roll(e, 2, 0)[HALO:HALO + tm, :], axis=0, keepdims=True)
            dw_ref[1:2, :] += jnp.sum(dup * pltpu.roll(e, 1, 0)[HALO:HALO + tm, :], axis=0, keepdims=True)
            dw_ref[2:3, :] += jnp.sum(dup * e[HALO:HALO + tm, :], axis=0, keepdims=True)

    cur, prev, nxt = _time_specs(S, tm, FFN_CW, lambda j: j)
    wspec = BS((3, FFN_CW), lambda j, i: (0, j))
    return _call_with_side(
        body, side, name, (D_FF // FFN_CW, nt), [cur, prev, nxt, cur, prev, nxt, wspec, wspec, cur, nxt],
        [cur, cur, wspec, wspec], [SDS((S, D_FF), BF16)] * 2 + [SDS((3, D_FF), F32)] * 2,
        (pre_g, pre_g, pre_g, pre_v, pre_v, pre_v, w_g, w_v, d_a, d_a))


def ple_fwd(x, a, e, name):
    S, Dm = x.shape
    tm = _tile(S, 256, 8)

    def body(x_ref, a_ref, e_ref, o_ref):
        o_ref[...] = x_ref[...] + _sigmoid(a_ref[...]) * e_ref[...]

    row = BS((tm, Dm), lambda i: (i, 0))
    return pl.pallas_call(body, name=name, grid=(S // tm,), in_specs=[row] * 3, out_specs=row,
                          out_shape=SDS((S, Dm), F32), compiler_params=_params(1))(x, a, e)


def ple_bwd(a, e, dx, name):
    S, Dm = a.shape
    tm = _tile(S, 256, 8)

    def body(a_ref, e_ref, dx_ref, da_ref, de_ref):
        s = _sigmoid(a_ref[...])
        d = dx_ref[...]
        da_ref[...] = (d * e_ref[...] * s * (1.0 - s)).astype(da_ref.dtype)
        de_ref[...] = (d * s).astype(de_ref.dtype)

    row = BS((tm, Dm), lambda i: (i, 0))
    return pl.pallas_call(body, name=name, grid=(S // tm,), in_specs=[row] * 3, out_specs=[row, row],
                          out_shape=[SDS((S, Dm), BF16)] * 2, compiler_params=_params(1))(a, e, dx)


def to_heads(x, n_heads, dil):
    S = x.shape[0]
    x = x.reshape(S // dil, dil, n_heads, HEAD_DIM).transpose(2, 1, 0, 3)
    return x.reshape(n_heads, S, HEAD_DIM)


def from_heads(y, dil):
    H, S, C = y.shape
    y = y.reshape(H, dil, S // dil, C).transpose(2, 1, 0, 3)
    return y.reshape(S, H * C)


def _columns(x, lo, hi):
    return lax.optimization_barrier(x[:, lo:hi])


def rows_of(col, t):
    H, S, _ = col.shape
    return col.reshape(H, S // t, 1, t)


IN_SEGMENTS = ((SRC_A, SRC_F, OFF_A), (SRC_F, SRC_B, OFF_F), (SRC_B, SRC_C, OFF_B), (SRC_C, SRC_DQ, OFF_C),
               (SRC_DQ, SRC_G, OFF_D), (SRC_G, D_IN, OFF_G))
IN_SHARD = D_IN // N_CHIPS


def w_in_aligned_from_chips(t):
    pieces = []
    for ref_lo, ref_hi, _ in sorted(IN_SEGMENTS, key=lambda seg: seg[2]):
        for k in range(N_CHIPS):
            lo, hi = max(ref_lo, k * IN_SHARD), min(ref_hi, (k + 1) * IN_SHARD)
            if lo < hi:
                pieces.append(t[k][:, lo - k * IN_SHARD:hi - k * IN_SHARD])
    pieces.append(jnp.zeros((t[0].shape[0], W_AL - D_IN), t[0].dtype))
    return jnp.concatenate(pieces, axis=1)


def w_in_chips_from_aligned(g):
    slots = []
    for k in range(N_CHIPS):
        pieces = []
        for ref_lo, ref_hi, al in IN_SEGMENTS:
            lo, hi = max(ref_lo, k * IN_SHARD), min(ref_hi, (k + 1) * IN_SHARD)
            if lo < hi:
                pieces.append(g[:, al + lo - ref_lo:al + hi - ref_lo])
        slots.append(jnp.concatenate(pieces, axis=1))
    return jnp.stack(slots, axis=0)


def chips_to_full(t, name):
    return jnp.concatenate([t[k] for k in range(N_CHIPS)], axis=0 if name in ROW_SHARDED else 1)


def full_to_chips(g, name):
    if name in ROW_SHARDED:
        return g.reshape(N_CHIPS, g.shape[0] // N_CHIPS, g.shape[1])
    return g.reshape(g.shape[0], N_CHIPS, g.shape[1] // N_CHIPS).transpose(1, 0, 2)


def halves_from_chips(t):
    return jnp.concatenate([t[0], t[1]], axis=1), jnp.concatenate([t[2], t[3]], axis=1)


def chips_from_halves(g, v):
    c = g.shape[1] // 2
    return jnp.stack([g[:, :c], g[:, c:], v[:, :c], v[:, c:]], axis=0)


def layer_fwd(x, p_l, rope, w, tag, sides=None):
    S = x.shape[0]
    sides = sides or {}
    side_out = {}
    sv = {"x0": x}
    h = rmsnorm_fwd(x, w["norm_mix_g"], f"{tag}_norm_mix")
    proj = matmul(h, w["w_in_al"], "nn", F32, f"{tag}_proj", side=sides.get("proj"))
    if "proj" in sides:
        proj, side_out["proj"] = proj
    sv["h"], sv["proj"] = h, proj

    af_t, f_cum = fox_prep_fwd(proj, w["fox_forget_b"].reshape(FOX_HEADS, 1), f"{tag}_fox_prep")
    T = min(FOX_T, S)
    f_col = f_cum.reshape(FOX_HEADS, S, 1)
    f_row = f_cum.reshape(FOX_HEADS, S // T, 1, T)
    qkv = to_heads(_columns(proj, OFF_A, OFF_B).astype(BF16), 3 * FOX_HEADS, 1)
    qa, ka, va = qkv[:FOX_HEADS], qkv[FOX_HEADS:2 * FOX_HEADS], qkv[2 * FOX_HEADS:]
    (oa_h, lse_a), side_out["fox"] = fox_attn_fwd(qa, ka, va, f_col, f_row, f"{tag}_fox_fwd", sides.get("fox"))
    o_a = from_heads(oa_h, 1)
    sv.update(af_t=af_t, f_col=f_col, f_row=f_row, qa=qa, ka=ka, va=va, oa_h=oa_h, lse_a=lse_a)

    o_b = shortconv_fwd(proj, w["shortconv_w"], f"{tag}_sconv_fwd")

    o_c = sgu_fwd(proj, w["sgu_norm_g"].reshape(1, SGU_WIDTH), w["sgu_w"], _sgu_bias(w["sgu_b"]), f"{tag}_sgu_fwd")

    cos, sa, sb = rope
    qk = rope_apply(proj, OFF_D, 2 * DIL_WIDTH, cos, sa, sb, BF16, f"{tag}_rope_fwd")
    vd = _columns(proj, OFF_D + 2 * DIL_WIDTH, OFF_D + 3 * DIL_WIDTH).astype(BF16)
    outs, lses, dil_sv = [], [], []
    for g, (window, dil) in enumerate(DIL_PATTERNS):
        sl = slice(g * DIL_OUT, (g + 1) * DIL_OUT)
        qg = to_heads(qk[:, sl], 4, dil)
        kg = to_heads(qk[:, DIL_WIDTH:][:, sl], 4, dil)
        vg = to_heads(vd[:, sl], 4, dil)
        nb = (S // dil) // DIL_SPAN
        og, lg = dil_attn_fwd(qg, kg, vg, nb, f"{tag}_dil{g}_fwd")
        dil_sv.append((qg, kg, vg, og, lg, nb))
        outs.append(_heads_unperm(og, dil))
        lses.append(_col_unperm(lg, dil))
    od_h = dil_merge_fwd(outs, lses, f"{tag}_dil_merge_fwd")
    o_d = from_heads(od_h, 1)
    sv.update(dil=dil_sv, outs=outs, lses=lses)

    o_d_pad = jnp.concatenate([o_d.astype(BF16), jnp.zeros((S, FOX_WIDTH - DIL_OUT), BF16)], axis=-1)
    o_stack = jnp.stack([o_a, o_b, o_c, o_d_pad], axis=0)
    merged = merge_fwd(o_stack, w["w_br"], proj, f"{tag}_merge_fwd")
    x1 = matmul(merged, w["w_out"], "nn", F32, f"{tag}_out_proj", res=x)
    sv.update(o_stack=o_stack, merged=merged, x1=x1)

    h2 = rmsnorm_fwd(x1, w["norm_ffn_g"], f"{tag}_norm_ffn")
    pre = (matmul(h2, w["w_up_g"], "nn", F32, f"{tag}_up_g"), matmul(h2, w["w_up_v"], "nn", F32, f"{tag}_up_v"))
    a, side_out["ffn"] = ffn_mid_fwd(pre[0], pre[1], w["ffn_conv_g"], w["ffn_conv_v"], f"{tag}_ffn_mid_fwd",
                                     sides.get("ffn"))
    x2 = matmul(a, w["w_down"], "nn", F32, f"{tag}_down", res=x1)
    sv.update(h2=h2, pre=pre, a=a, x2=x2)

    n3 = rmsnorm_fwd(x2, w["norm_ple_g"], f"{tag}_norm_ple")
    pg = matmul(n3, w["w_ple_gate"], "nn", F32, f"{tag}_ple_gate")
    pe = matmul(p_l, w["w_ple_proj"], "nn", F32, f"{tag}_ple_proj")
    x3 = ple_fwd(x2, pg, pe, f"{tag}_ple_fwd")
    sv.update(n3=n3, pg=pg, pe=pe, p_l=p_l)
    return x3, sv, side_out


def _sgu_bias(b):
    return jnp.pad(b.T, ((0, 0), (0, SGU_CHUNK - b.shape[0])))


def _col_unperm(col, dil):
    H, S, _ = col.shape
    return col.reshape(H, dil, S // dil).transpose(0, 2, 1).reshape(H, S, 1)


def _col_perm(col, dil):
    H, S, _ = col.shape
    return col.reshape(H, S // dil, dil).transpose(0, 2, 1).reshape(H, S, 1)


def _heads_perm(y, dil):
    H, S, C = y.shape
    return y.reshape(H, S // dil, dil, C).transpose(0, 2, 1, 3).reshape(H, S, C)


def _heads_unperm(y, dil):
    H, S, C = y.shape
    return y.reshape(H, dil, S // dil, C).transpose(0, 2, 1, 3).reshape(H, S, C)


def layer_bwd(dx3, sv, rope, w, tag, exch):
    side_out = {}
    S = dx3.shape[0]
    gr = {}
    da, de = ple_bwd(sv["pg"], sv["pe"], dx3, f"{tag}_ple_bwd")
    gr["w_ple_proj"] = matmul(sv["p_l"], de, "tn", F32, f"{tag}_dw_ple_proj")
    gr["w_ple_gate"] = matmul(sv["n3"], da, "tn", F32, f"{tag}_dw_ple_gate")
    dn3 = matmul(da, w["w_ple_gate"], "nt", BF16, f"{tag}_dn3")
    dx2, gr["norm_ple_g"] = rmsnorm_bwd(sv["x2"], w["norm_ple_g"], dn3, dx3, f"{tag}_norm_ple_bwd")

    d_a = matmul(dx2, w["w_down"], "nt", F32, f"{tag}_da")
    gr["w_down"] = matmul(sv["a"], dx2, "tn", F32, f"{tag}_dw_down")
    (dpre_g, dpre_v, dwc_g, dwc_v), swapped = ffn_mid_bwd(sv["pre"][0], sv["pre"][1], w["ffn_conv_g"],
                                                          w["ffn_conv_v"], d_a, f"{tag}_ffn_mid_bwd",
                                                          exch.swap_exchange())
    sides = exch.ici_exchanges(swapped)
    gr["ffn_conv_w"] = (dwc_g, dwc_v)
    h2_t = sv["h2"].T
    gr["w_up"] = (matmul(h2_t, dpre_g, "nn", F32, f"{tag}_dw_up_g", tm=2048, tn=512),
                  matmul(h2_t, dpre_v, "nn", F32, f"{tag}_dw_up_v", tm=2048, tn=512))
    dh2_g = matmul(dpre_g, w["w_up_g"], "nt", F32, f"{tag}_dh2_g")
    dh2 = matmul(dpre_v, w["w_up_v"], "nt", BF16, f"{tag}_dh2_v", res=dh2_g)
    dx1, gr["norm_ffn_g"] = rmsnorm_bwd(sv["x1"], w["norm_ffn_g"], dh2, dx2, f"{tag}_norm_ffn_bwd")

    d_merged = matmul(dx1, w["w_out"], "nt", BF16, f"{tag}_dmerged")
    gr["w_out"] = matmul(sv["merged"], dx1, "tn", F32, f"{tag}_dw_out")
    proj = sv["proj"]
    dgl, dy = merge_bwd(sv["o_stack"], w["w_br"], proj, d_merged, f"{tag}_merge_bwd")
    d_o, d_wbr = [], []
    for b in range(N_BRANCH):
        d_o.append(matmul(dy[b], w["w_br"][b], "nt", F32, f"{tag}_do{b}"))
        d_wbr.append(matmul(sv["o_stack"][b], dy[b], "tn", F32, f"{tag}_dw_br{b}"))
    gr["w_br"] = d_wbr

    do_a = to_heads(d_o[0].astype(BF16), FOX_HEADS, 1)
    T = min(FOX_T, S)
    (dqa, delta_a, d_fq), side_out["dq"] = fox_attn_bwd_dq(
        sv["qa"], sv["ka"], sv["va"], sv["f_col"], sv["f_row"], sv["oa_h"], sv["lse_a"], do_a, f"{tag}_fox_dq",
        sides.get("dq"))
    (dka, dva, d_fk), side_out["dkv"] = fox_attn_bwd_dkv(
        sv["qa"], sv["ka"], sv["va"], sv["f_col"], sv["f_row"], rows_of(sv["lse_a"], T), rows_of(delta_a, T), do_a,
        f"{tag}_fox_dkv", sides.get("dkv"))
    daf_t, dfb = fox_prep_bwd(sv["af_t"], w["fox_forget_b"].reshape(FOX_HEADS, 1), d_fq.reshape(FOX_HEADS, S),
                              d_fk.reshape(FOX_HEADS, S), f"{tag}_fox_prep_bwd")
    gr["fox_forget_b"] = dfb.reshape(FOX_HEADS)
    d_proj_a = from_heads(jnp.concatenate([dqa, dka, dva], axis=0), 1).astype(BF16)

    dxb, dgb, dgc, gr["shortconv_w"] = shortconv_bwd(proj, w["shortconv_w"], d_o[1], f"{tag}_sconv_bwd")

    d_c, dsg, dsw, dsb = sgu_bwd(proj, w["sgu_norm_g"].reshape(1, SGU_WIDTH), w["sgu_w"],
                                 jnp.swapaxes(w["sgu_w"], 1, 2), _sgu_bias(w["sgu_b"]), d_o[2], f"{tag}_sgu_bwd")
    gr["sgu_norm_g"] = dsg.reshape(SGU_WIDTH)
    gr["sgu_w"] = dsw
    gr["sgu_b"] = dsb[:, :SGU_WIDTH // SGU_CHUNK].T

    d_od = to_heads(d_o[3][:, :DIL_OUT], 4, 1)
    d_outs_lses = dil_merge_bwd(sv["outs"], sv["lses"], d_od, f"{tag}_dil_merge_bwd")
    d_outs, d_lses = d_outs_lses[:3], d_outs_lses[3:]
    dq_parts, dk_parts, dv_parts = [], [], []
    for g, (window, dil) in enumerate(DIL_PATTERNS):
        qg, kg, vg, og, lg, nb = sv["dil"][g]
        do_g = _heads_perm(d_outs[g], dil)
        dl_g = _col_perm(d_lses[g], dil)
        dqg, delta_g = dil_attn_bwd_dq(qg, kg, vg, og, lg, do_g, dl_g, nb, f"{tag}_dil{g}_dq")
        dkg, dvg = dil_attn_bwd_dkv(qg, kg, vg, rows_of(lg, DIL_SPAN), rows_of(delta_g, DIL_SPAN), do_g, nb,
                                    f"{tag}_dil{g}_dkv")
        dq_parts.append(from_heads(dqg, dil))
        dk_parts.append(from_heads(dkg, dil))
        dv_parts.append(from_heads(dvg, dil))
    cos, sa, sb = rope
    d_qk_rot = jnp.concatenate(dq_parts + dk_parts, axis=-1)
    d_qk = rope_apply(d_qk_rot, 0, 2 * DIL_WIDTH, cos, -sa, -sb, BF16, f"{tag}_rope_bwd")
    d_vd = jnp.concatenate(dv_parts, axis=-1).astype(BF16)

    d_f_cols = jnp.concatenate([daf_t.T.astype(BF16), jnp.zeros((S, W_AL - OFF_F - FOX_HEADS), BF16)], axis=-1)
    d_proj = jnp.concatenate([dgl, d_proj_a, dxb, dgb, dgc, d_c, d_qk, d_vd, d_f_cols], axis=-1)
    gr["w_in_al"] = matmul(sv["h"], d_proj, "tn", F32, f"{tag}_dw_in", tm=2048, tn=512)
    dh = matmul(d_proj, w["w_in_al"], "nt", BF16, f"{tag}_dh", tk=W_AL // 4)
    dx0, gr["norm_mix_g"] = rmsnorm_bwd(sv["x0"], w["norm_mix_g"], dh, dx1, f"{tag}_norm_mix_bwd")
    exch.ici_arrived(side_out)
    return dx0, gr


def local_weights(chips, repl, layer):
    w = {n: repl[n][layer] for n in REPLICATED}
    cast = lambda n, dtype: [chips[n][k].astype(dtype) for k in range(N_CHIPS)]
    full = {n: chips_to_full(cast(n, BF16), n)
            for n in ("w_br_fox", "w_br_conv", "w_br_sgu", "w_br_dil", "w_out", "w_down", "w_ple_gate", "w_ple_proj")}
    w["w_in_al"] = w_in_aligned_from_chips(cast("w_in", BF16))
    w["shortconv_w"] = chips_to_full(cast("shortconv_w", F32), "shortconv_w")
    pad = jnp.zeros((FOX_WIDTH - DIL_OUT, D_MODEL), BF16)
    w["w_br"] = jnp.stack([full["w_br_fox"], full["w_br_conv"], full["w_br_sgu"],
                           jnp.concatenate([full["w_br_dil"], pad], axis=0)], axis=0)
    w["w_up_g"], w["w_up_v"] = halves_from_chips(cast("w_up", BF16))
    w["ffn_conv_g"], w["ffn_conv_v"] = halves_from_chips(cast("ffn_conv_w", F32))
    for n in ("w_out", "w_down", "w_ple_gate", "w_ple_proj"):
        w[n] = full[n]
    return w


def grads_to_chips(gr):
    out = {n: gr[n] for n in ("fox_forget_b", "sgu_norm_g", "sgu_w", "sgu_b")}
    out["norm_mix_g"] = gr["norm_mix_g"].reshape(D_MODEL)
    out["norm_ffn_g"] = gr["norm_ffn_g"].reshape(D_MODEL)
    out["norm_ple_g"] = gr["norm_ple_g"].reshape(D_MODEL)
    out["w_in"] = w_in_chips_from_aligned(gr["w_in_al"])
    out["w_up"] = chips_from_halves(*gr["w_up"])
    out["ffn_conv_w"] = chips_from_halves(*gr["ffn_conv_w"])
    for b, n in enumerate(("w_br_fox", "w_br_conv", "w_br_sgu")):
        out[n] = full_to_chips(gr["w_br"][b], n)
    out["w_br_dil"] = full_to_chips(gr["w_br"][3][:DIL_OUT], "w_br_dil")
    for n in ("shortconv_w", "w_out", "w_down", "w_ple_gate", "w_ple_proj"):
        out[n] = full_to_chips(gr[n], n)
    return out


def local_step(x, p, positions, repl, final_norm_g, loss_target, exch):
    depth = p.shape[0]
    rope = rope_tables(positions)
    saved, ws = [], []
    chips = exch.first_weights()
    for layer in range(depth):
        w = local_weights(chips, repl, layer)
        side = exch.weights_exchange(layer + 1) if layer + 1 < depth else None
        x, sv, side_out = layer_fwd(x, p[layer].astype(BF16), rope, w, f"l{layer}", side)
        if layer + 1 < depth:
            chips = exch.weights_arrived(layer + 1, side_out)
        saved.append(sv)
        ws.append(w)
    loss_part, dx, dgf = final_loss(x, final_norm_g, loss_target, "final_loss")
    for layer in range(depth - 1, -1, -1):
        dx, gr = layer_bwd(dx, saved[layer], rope, ws[layer], f"l{layer}", exch)
        exch.grads_ready(layer, grads_to_chips(gr))
    exch.grads_flush()
    return loss_part[0, 0], dx, dgf.reshape(-1)


def _position():
    return lax.axis_index("x"), lax.axis_index("y"), lax.axis_index("c")


def _other_chips(x, y):
    return [(1 - x, y), (x, 1 - y), (1 - x, 1 - y)]


def _remote(src, dst, send_sem, recv_sem, device):
    return pltpu.make_async_remote_copy(src_ref=src, dst_ref=dst, send_sem=send_sem, recv_sem=recv_sem,
                                        device_id=device, device_id_type=MESH)


def _chip_index():
    return 2 * lax.axis_index("x") + lax.axis_index("y")


def _block_rows(rows, cols, unit):
    return _tile(rows, max(unit, (1 << 20) // cols // unit * unit), unit)


def gather_chip_shards(packs, name):
    return _run_exchange(gather_exchange(packs), name)


def gather_exchange(packs):
    n = len(packs)
    halves = [p.shape[0] // 2 for p in packs]

    def half(outs, t, chip, core):
        return outs[t].at[chip, pl.ds(core * halves[t], halves[t]), :]

    def ici_sends(srcs, outs, send_sems, recv_sems):
        x, y, c = _position()
        me = 2 * x + y
        return [_remote(srcs[t].at[pl.ds(c * halves[t], halves[t]), :], half(outs, t, me, c),
                        send_sems.at[6 * t + j], recv_sems.at[6 * t + j], (px, py, c))
                for t in range(n) for j, (px, py) in enumerate(_other_chips(x, y))]

    def own_to_sibling(srcs, outs, send_sems, recv_sems):
        x, y, c = _position()
        return [_remote(srcs[t], outs[t].at[2 * x + y], send_sems.at[6 * n + t], recv_sems.at[6 * n + t],
                        (x, y, 1 - c)) for t in range(n)]

    def start(srcs, outs, send_sems, recv_sems):
        for cp in ici_sends(srcs, outs, send_sems, recv_sems) + own_to_sibling(srcs, outs, send_sems, recv_sems):
            cp.start()

    def finish(srcs, outs, send_sems, recv_sems):
        x, y, c = _position()
        sibling = (x, y, 1 - c)
        chips = _other_chips(x, y)
        passed = []
        for t in range(n):
            for j, (px, py) in enumerate(chips):
                k = 2 * px + py
                s = 6 * t + j
                landed = half(outs, t, k, c)
                _remote(landed, landed, send_sems.at[s], recv_sems.at[s], (px, py, c)).wait_recv()
                fwd = _remote(landed, landed, send_sems.at[s + 3], recv_sems.at[s + 3], sibling)
                fwd.start()
                passed.append(fwd)
        for t in range(n):
            for j, (px, py) in enumerate(chips):
                s = 6 * t + j + 3
                theirs = half(outs, t, 2 * px + py, 1 - c)
                _remote(theirs, theirs, send_sems.at[s], recv_sems.at[s], sibling).wait_recv()
        for cp in own_to_sibling(srcs, outs, send_sems, recv_sems):
            cp.wait()
        for cp in ici_sends(srcs, outs, send_sems, recv_sems) + passed:
            cp.wait_send()

    return SideExchange(list(packs), [SDS((N_CHIPS,) + p.shape, p.dtype) for p in packs], 7 * n, start, finish)


def _run_exchange(side, name):
    n_in, n_out = len(side.operands), len(side.out_shapes)

    def body(*refs):
        srcs, outs, (send_sems, recv_sems) = refs[:n_in], refs[n_in:n_in + n_out], refs[n_in + n_out:]
        side.start(srcs, outs, send_sems, recv_sems)
        side.finish(srcs, outs, send_sems, recv_sems)

    return pl.pallas_call(
        body, name=name, in_specs=[ANY] * n_in, out_specs=[ANY] * n_out, out_shape=side.out_shapes,
        scratch_shapes=[pltpu.SemaphoreType.DMA((side.n_sems,)), pltpu.SemaphoreType.DMA((side.n_sems,))],
    )(*side.operands)


def swap_halves_with_sibling(gs, name):
    return _run_exchange(swap_exchange(gs), name)


def swap_exchange(gs):
    n = len(gs)
    halves = [g.shape[1] // 2 for g in gs]

    def copies(srcs, lands, send_sems, recv_sems):
        x, y, c = _position()
        return [_remote(srcs[t].at[:, pl.ds((1 - c) * halves[t], halves[t]), :], lands[t], send_sems.at[t],
                        recv_sems.at[t], (x, y, 1 - c)) for t in range(n)]

    def start(srcs, lands, send_sems, recv_sems):
        for cp in copies(srcs, lands, send_sems, recv_sems):
            cp.start()

    def finish(srcs, lands, send_sems, recv_sems):
        for cp in copies(srcs, lands, send_sems, recv_sems):
            cp.wait()

    return SideExchange(list(gs), [SDS((g.shape[0], h, g.shape[2]), g.dtype) for g, h in zip(gs, halves)], n,
                        start, finish)


def add_my_half(g, other, out_dtype, name):
    n, R, C = g.shape
    H = R // 2
    tr = _block_rows(H, C, 16) if H % 16 == 0 else H
    nb = H // tr
    core = lax.axis_index("c").astype(jnp.int32).reshape(1)

    def body(c_ref, g_ref, o_ref, out_ref):
        out_ref[...] = (g_ref[...] + o_ref[...]).astype(out_ref.dtype)

    grid_spec = pltpu.PrefetchScalarGridSpec(
        num_scalar_prefetch=1, grid=(n, nb),
        in_specs=[BS((None, tr, C), lambda s, i, c_ref: (s, c_ref[0] * nb + i, 0)),
                  BS((None, tr, C), lambda s, i, c_ref: (s, i, 0))],
        out_specs=BS((None, tr, C), lambda s, i, c_ref: (s, i, 0)))
    return pl.pallas_call(body, name=name, grid_spec=grid_spec, out_shape=SDS((n, H, C), out_dtype),
                          compiler_params=_params(2))(core, g, other)


def exchange_slots_between_chips(parts, name):
    return _run_exchange(slot_exchange(parts), name)


def slot_exchange(parts):
    n = len(parts)

    def sends(srcs, lands, send_sems, recv_sems):
        x, y, c = _position()
        me = 2 * x + y
        return [_remote(srcs[t].at[2 * px + py], lands[t].at[me], send_sems.at[3 * t + j], recv_sems.at[3 * t + j],
                        (px, py, c)) for t in range(n) for j, (px, py) in enumerate(_other_chips(x, y))]

    def start(srcs, lands, send_sems, recv_sems):
        for cp in sends(srcs, lands, send_sems, recv_sems):
            cp.start()

    def finish(srcs, lands, send_sems, recv_sems):
        x, y, c = _position()
        for t in range(n):
            for j, (px, py) in enumerate(_other_chips(x, y)):
                k = 2 * px + py
                _remote(srcs[t].at[k], lands[t].at[k], send_sems.at[3 * t + j], recv_sems.at[3 * t + j],
                        (px, py, c)).wait_recv()
        for cp in sends(srcs, lands, send_sems, recv_sems):
            cp.wait_send()

    return SideExchange(list(parts), [SDS(p.shape, p.dtype) for p in parts], 3 * n, start, finish)


def sum_slots_into_my_half(landed, mine, name):
    n, H, C = landed.shape
    tr = _block_rows(H, C, 16) if H % 16 == 0 else H
    nb = H // tr
    where = jnp.stack([lax.axis_index("c"), _chip_index()]).astype(jnp.int32)

    def body(w_ref, l_ref, m_ref, o_ref):
        me = w_ref[1]
        o_ref[...] = jnp.zeros_like(o_ref)
        for k in range(n):
            @pl.when(me == k)
            def _():
                o_ref[...] += m_ref[k].astype(F32)

            @pl.when(me != k)
            def _():
                o_ref[...] += l_ref[k].astype(F32)

    slots = BS((n, tr, C), lambda i, w_ref: (0, i, 0))
    grid_spec = pltpu.PrefetchScalarGridSpec(
        num_scalar_prefetch=1, grid=(nb,), in_specs=[slots, slots],
        out_specs=BS((tr, C), lambda i, w_ref: (w_ref[0] * nb + i, 0)))
    return pl.pallas_call(body, name=name, grid_spec=grid_spec, out_shape=SDS((2 * H, C), F32),
                          compiler_params=_params(1))(where, landed, mine)


def sum_slots(parts, name):
    n, H, C = parts.shape
    tr = _tile(H, 256, 16)

    def body(p_ref, o_ref):
        acc = p_ref[0].astype(F32)
        for k in range(1, n):
            acc = acc + p_ref[k].astype(F32)
        o_ref[...] = acc

    return pl.pallas_call(
        body, name=name, grid=(H // tr,), in_specs=[BS((n, tr, C), lambda i: (0, i, 0))],
        out_specs=BS((tr, C), lambda i: (i, 0)), out_shape=SDS((H, C), F32), compiler_params=_params(1))(parts)


def join_halves_with_sibling(arrs, name):
    n = len(arrs)
    halves = [a.shape[0] // 2 for a in arrs]

    def body(*refs):
        outs, (send_sems, recv_sems) = refs[n:2 * n], refs[2 * n:]
        x, y, c = _position()

        def half(t, core):
            return outs[t].at[pl.ds(core * halves[t], halves[t]), :]

        sends = [_remote(half(t, c), half(t, c), send_sems.at[t], recv_sems.at[t], (x, y, 1 - c)) for t in range(n)]
        for cp in sends:
            cp.start()
        for t in range(n):
            _remote(half(t, 1 - c), half(t, 1 - c), send_sems.at[t], recv_sems.at[t], (x, y, 1 - c)).wait_recv()
        for cp in sends:
            cp.wait_send()

    return pl.pallas_call(
        body, name=name, in_specs=[ANY] * n, out_specs=[ANY] * n, out_shape=[SDS(a.shape, a.dtype) for a in arrs],
        input_output_aliases={t: t for t in range(n)},
        scratch_shapes=[pltpu.SemaphoreType.DMA((n,)), pltpu.SemaphoreType.DMA((n,))])(*arrs)


def reduce_scatter_pair_sums(gs, others, tag):
    n = len(gs)
    return [add_my_half(g, o, BF16 if t < n - 1 else F32, f"{tag}_pair_sum{t}")
            for t, (g, o) in enumerate(zip(gs, others))]


def reduce_scatter_finish(parts, landed, tag):
    sums = [sum_slots_into_my_half(l, p, f"{tag}_chip_sum{t}") for t, (l, p) in enumerate(zip(landed, parts))]
    return join_halves_with_sibling(sums, f"{tag}_join")


def gather_all_devices(pack, name):
    R, C = pack.shape

    def body(src, out, send_sems, recv_sems, local_sem):
        x, y, c = _position()
        me = 4 * x + 2 * y + c
        local = pltpu.make_async_copy(src, out.at[me], local_sem)
        local.start()
        peers = []
        for m in range(1, N_DEV):
            px = 1 - x if m & 4 else x
            py = 1 - y if m & 2 else y
            pc = 1 - c if m & 1 else c
            peers.append((px, py, pc))
        sends = [_remote(src, out.at[me], send_sems.at[j], recv_sems.at[j], peer) for j, peer in enumerate(peers)]
        for cp in sends:
            cp.start()
        for j, (px, py, pc) in enumerate(peers):
            k = 4 * px + 2 * py + pc
            _remote(src, out.at[k], send_sems.at[j], recv_sems.at[j], (px, py, pc)).wait_recv()
        for cp in sends:
            cp.wait_send()
        local.wait()

    return pl.pallas_call(
        body, name=name, in_specs=[ANY], out_specs=ANY, out_shape=SDS((N_DEV, R, C), pack.dtype),
        scratch_shapes=[pltpu.SemaphoreType.DMA((N_DEV - 1,)), pltpu.SemaphoreType.DMA((N_DEV - 1,)),
                        pltpu.SemaphoreType.DMA(())])(pack)


def _adamw_update(w, g, m, v):
    c1 = 1.0 / (1.0 - ADAM_B1 ** ADAM_STEP)
    c2 = 1.0 / (1.0 - ADAM_B2 ** ADAM_STEP)
    mn = ADAM_B1 * m + (1.0 - ADAM_B1) * g
    vn = ADAM_B2 * v + (1.0 - ADAM_B2) * (g * g)
    return -ADAM_LR * ((mn * c1) / (jnp.sqrt(vn * c2) + ADAM_EPS) + ADAM_WD * w), mn, vn


def adamw_layers(w, gs, m, v, name):
    L, r, c = w.shape
    tr = r if r * c * 4 <= (1 << 20) else _tile(r, max(8, ((1 << 20) // (c * 4)) // 8 * 8), 8)
    nb = r // tr

    def g_spec(l):
        return BS((tr, c), lambda layer, i: (jnp.where(layer == l, i, jnp.where(layer < l, 0, nb - 1)), 0))

    def body(w_ref, m_ref, v_ref, *rest):
        g_refs, (go_ref, d_ref, mo_ref, vo_ref) = rest[:L], rest[L:]
        layer = pl.program_id(0)
        for l in range(L):
            @pl.when(layer == l)
            def _(g_ref=g_refs[l]):
                gv = g_ref[...]
                go_ref[...] = gv
                d_ref[...], mo_ref[...], vo_ref[...] = _adamw_update(w_ref[...], gv, m_ref[...], v_ref[...])

    blk = BS((None, tr, c), lambda layer, i: (layer, i, 0))
    return pl.pallas_call(
        body, name=name, grid=(L, nb), in_specs=[blk] * 3 + [g_spec(l) for l in range(L)], out_specs=[blk] * 4,
        out_shape=[SDS((L, r, c), F32)] * 4, compiler_params=_params(2))(w, m, v, *gs)


def adamw(w, g, m, v, name):
    shape = w.shape
    cols = shape[-1] if len(shape) > 1 else shape[0]
    rows = w.size // cols
    two = lambda t: t.reshape(rows, cols)
    tr = rows
    if rows * cols * 4 > (1 << 21):
        tr = _tile(rows, max(8, ((1 << 21) // (cols * 4)) // 8 * 8), 8)

    def body(w_ref, g_ref, m_ref, v_ref, d_ref, mo_ref, vo_ref):
        d_ref[...], mo_ref[...], vo_ref[...] = _adamw_update(w_ref[...], g_ref[...], m_ref[...], v_ref[...])

    blk = BS((tr, cols), lambda i: (i, 0))
    d, mo, vo = pl.pallas_call(
        body, name=name, grid=(rows // tr,), in_specs=[blk] * 4, out_specs=[blk] * 3,
        out_shape=[SDS((rows, cols), F32)] * 3, compiler_params=_params(1))(two(w), two(g), two(m), two(v))
    return d.reshape(shape), mo.reshape(shape), vo.reshape(shape)


def _rows_for(n, unit):
    rows = -(-n // PACK_COLS)
    return -(-rows // unit) * unit


ROWS_GROUP = ("w_out", "w_ple_gate", "w_down")
COLS_GROUP = ("w_br_fox", "w_br_conv", "w_br_sgu", "w_br_dil", "w_ple_proj")
SMALL_GROUP = ("shortconv_w", "ffn_conv_w")
SMALL_ROWS = 16


def group_shards(t, dtype):
    lead = t["w_in"].shape[:-2]
    small = jnp.concatenate([t[n].astype(F32).reshape(lead + (-1,)) for n in SMALL_GROUP], axis=-1)
    pad = jnp.zeros(lead + (SMALL_ROWS * PACK_COLS - small.shape[-1],), F32)
    small = jnp.concatenate([small, pad], axis=-1).reshape(lead + (SMALL_ROWS, PACK_COLS))
    return [t["w_in"].astype(dtype), t["w_up"].astype(dtype),
            jnp.concatenate([t[n].astype(dtype) for n in ROWS_GROUP], axis=-2),
            jnp.concatenate([t[n].astype(dtype) for n in COLS_GROUP], axis=-2), small]


def ungroup_shards(arrs, shard_shapes):
    w_in_s, w_up_s, rows, cols, small = arrs
    lead = w_in_s.shape[:-2]
    out = {"w_in": w_in_s, "w_up": w_up_s}
    for group, arr in ((ROWS_GROUP, rows), (COLS_GROUP, cols)):
        off = 0
        for n in group:
            r = shard_shapes[n][0]
            out[n] = arr[..., off:off + r, :]
            off += r
    flat = small.reshape(lead + (-1,))
    off = 0
    for n in SMALL_GROUP:
        size = shard_shapes[n][0] * shard_shapes[n][1]
        out[n] = flat[..., off:off + size].reshape(lead + shard_shapes[n])
        off += size
    return out


class ShardExchange:
    def __init__(self, weights, depth, shard_shapes):
        self.shard_shapes = shard_shapes
        self.packs = [group_shards({n: weights[n][layer] for n in SHARDED}, BF16) for layer in range(depth)]
        self.pending = None
        self.parts = None
        self.shard_grads = [None] * depth
        self.repl_grads = [None] * depth

    def _chips(self, layer, gathered):
        per_chip = [ungroup_shards([g[k] for g in gathered], self.shard_shapes) for k in range(N_CHIPS)]
        return {n: [per_chip[k][n] for k in range(N_CHIPS)] for n in SHARDED}

    def first_weights(self):
        return self._chips(0, gather_chip_shards(self.packs[0], "gather_w0"))

    FWD_HOSTS = {"fox": (0, 3, 4), "proj": (1,), "ffn": (2,)}
    BWD_HOSTS = {"dq": (0, 3, 4), "dkv": (1, 2)}

    @staticmethod
    def _split(hosts, arrs, make):
        return {host: make([arrs[t] for t in idx]) for host, idx in hosts.items()}

    @staticmethod
    def _join(hosts, outs):
        arrs = [None] * sum(len(idx) for idx in hosts.values())
        for host, idx in hosts.items():
            for t, arr in zip(idx, outs[host]):
                arrs[t] = arr
        return arrs

    def weights_exchange(self, layer):
        return self._split(self.FWD_HOSTS, self.packs[layer], gather_exchange)

    def weights_arrived(self, layer, outs):
        return self._chips(layer, self._join(self.FWD_HOSTS, outs))

    def grads_ready(self, layer, gr):
        self.repl_grads[layer] = {n: gr[n] for n in REPLICATED}
        self.pending = (layer, group_shards({n: gr[n] for n in SHARDED}, F32))

    def swap_exchange(self):
        return swap_exchange(self.pending[1]) if self.pending is not None else None

    def ici_exchanges(self, swapped):
        if self.pending is None:
            return {}
        layer, slots = self.pending
        self.parts = reduce_scatter_pair_sums(slots, swapped, f"rs{layer}")
        return self._split(self.BWD_HOSTS, self.parts, slot_exchange)

    def ici_arrived(self, outs):
        if self.pending is not None:
            self._finish(self._join(self.BWD_HOSTS, outs))

    def grads_flush(self):
        layer, slots = self.pending
        self.parts = reduce_scatter_pair_sums(slots, swap_halves_with_sibling(slots, f"rs{layer}_swap"), f"rs{layer}")
        self._finish(exchange_slots_between_chips(self.parts, f"rs{layer}_ici"))

    def _finish(self, landed):
        layer = self.pending[0]
        self.shard_grads[layer] = ungroup_shards(reduce_scatter_finish(self.parts, landed, f"rs{layer}"),
                                                 self.shard_shapes)
        self.pending = None


REPL_SHAPES = {"norm_mix_g": (D_MODEL,), "fox_forget_b": (FOX_HEADS,), "sgu_norm_g": (SGU_WIDTH,),
               "sgu_w": (4, SGU_CHUNK, SGU_CHUNK), "sgu_b": (4, SGU_CHUNK), "norm_ffn_g": (D_MODEL,),
               "norm_ple_g": (D_MODEL,)}


def kernel(x, p, positions, norm_mix_g, w_in, fox_forget_b, shortconv_w, sgu_norm_g, sgu_w, sgu_b, w_br_fox, w_br_conv, w_br_sgu, w_br_dil, w_out, norm_ffn_g, w_up, ffn_conv_w, w_down, norm_ple_g, w_ple_gate, w_ple_proj, final_norm_g, loss_target, m_norm_mix_g, m_w_in, m_fox_forget_b, m_shortconv_w, m_sgu_norm_g, m_sgu_w, m_sgu_b, m_w_br_fox, m_w_br_conv, m_w_br_sgu, m_w_br_dil, m_w_out, m_norm_ffn_g, m_w_up, m_ffn_conv_w, m_w_down, m_norm_ple_g, m_w_ple_gate, m_w_ple_proj, m_final_norm_g, v_norm_mix_g, v_w_in, v_fox_forget_b, v_shortconv_w, v_sgu_norm_g, v_sgu_w, v_sgu_b, v_w_br_fox, v_w_br_conv, v_w_br_sgu, v_w_br_dil, v_w_out, v_norm_ffn_g, v_w_up, v_ffn_conv_w, v_w_down, v_norm_ple_g, v_w_ple_gate, v_w_ple_proj, v_final_norm_g):
    weights = dict(norm_mix_g=norm_mix_g, w_in=w_in, fox_forget_b=fox_forget_b, shortconv_w=shortconv_w,
                   sgu_norm_g=sgu_norm_g, sgu_w=sgu_w, sgu_b=sgu_b, w_br_fox=w_br_fox, w_br_conv=w_br_conv,
                   w_br_sgu=w_br_sgu, w_br_dil=w_br_dil, w_out=w_out, norm_ffn_g=norm_ffn_g, w_up=w_up,
                   ffn_conv_w=ffn_conv_w, w_down=w_down, norm_ple_g=norm_ple_g, w_ple_gate=w_ple_gate,
                   w_ple_proj=w_ple_proj, final_norm_g=final_norm_g)
    mom1 = dict(norm_mix_g=m_norm_mix_g, w_in=m_w_in, fox_forget_b=m_fox_forget_b, shortconv_w=m_shortconv_w,
                sgu_norm_g=m_sgu_norm_g, sgu_w=m_sgu_w, sgu_b=m_sgu_b, w_br_fox=m_w_br_fox, w_br_conv=m_w_br_conv,
                w_br_sgu=m_w_br_sgu, w_br_dil=m_w_br_dil, w_out=m_w_out, norm_ffn_g=m_norm_ffn_g, w_up=m_w_up,
                ffn_conv_w=m_ffn_conv_w, w_down=m_w_down, norm_ple_g=m_norm_ple_g, w_ple_gate=m_w_ple_gate,
                w_ple_proj=m_w_ple_proj, final_norm_g=m_final_norm_g)
    mom2 = dict(norm_mix_g=v_norm_mix_g, w_in=v_w_in, fox_forget_b=v_fox_forget_b, shortconv_w=v_shortconv_w,
                sgu_norm_g=v_sgu_norm_g, sgu_w=v_sgu_w, sgu_b=v_sgu_b, w_br_fox=v_w_br_fox, w_br_conv=v_w_br_conv,
                w_br_sgu=v_w_br_sgu, w_br_dil=v_w_br_dil, w_out=v_w_out, norm_ffn_g=v_norm_ffn_g, w_up=v_w_up,
                ffn_conv_w=v_ffn_conv_w, w_down=v_w_down, norm_ple_g=v_norm_ple_g, w_ple_gate=v_w_ple_gate,
                w_ple_proj=v_w_ple_proj, final_norm_g=v_final_norm_g)
    depth = w_in.shape[0]
    shard_shapes = {n: tuple(weights[n].shape[1:]) for n in SHARDED}

    exch = ShardExchange(weights, depth, shard_shapes)
    repl = {n: weights[n] for n in REPLICATED}
    loss_part, grad_x, d_final = local_step(x[0], p[:, 0], positions[0], repl, final_norm_g, loss_target[0], exch)
    loss = lax.psum(loss_part, ("x", "y", "c"))
    grads = exch.repl_grads
    grad_w, deltas, new_m, new_v = {}, {}, {}, {}
    for n in SHARDED:
        grad_w[n], deltas[n], new_m[n], new_v[n] = adamw_layers(
            weights[n], [exch.shard_grads[layer][n] for layer in range(depth)], mom1[n], mom2[n], f"adamw_{n}")

    flat = jnp.concatenate([grads[layer][n].astype(F32).reshape(-1) for layer in range(depth) for n in REPLICATED]
                           + [d_final])
    Rr = _rows_for(flat.shape[0], 16)
    packed = jnp.concatenate([flat, jnp.zeros((Rr * PACK_COLS - flat.shape[0],), F32)]).reshape(Rr, PACK_COLS)
    total = sum_slots(gather_all_devices(packed, "gather_repl"), "sum_repl").reshape(-1)
    off = 0
    g_rep = {n: [] for n in REPLICATED}
    for layer in range(depth):
        for n in REPLICATED:
            size = 1
            for s in REPL_SHAPES[n]:
                size *= s
            g_rep[n].append(total[off:off + size].reshape(REPL_SHAPES[n]))
            off += size
    for n in REPLICATED:
        grad_w[n] = jnp.stack(g_rep[n], axis=0)
    grad_w["final_norm_g"] = total[off:off + D_MODEL]

    for n in REPLICATED + ("final_norm_g",):
        deltas[n], new_m[n], new_v[n] = adamw(weights[n], grad_w[n], mom1[n], mom2[n], f"adamw_{n}")
    return (loss, grad_x[None], *[grad_w[n] for n in WEIGHTS], *[deltas[n] for n in WEIGHTS],
            *[new_m[n] for n in WEIGHTS], *[new_v[n] for n in WEIGHTS])
```

```python
import functools

import jax
import jax.numpy as jnp
from jax import lax
from jax.experimental import pallas as pl
from jax.experimental.pallas import tpu as pltpu

F32 = jnp.float32
BF16 = jnp.bfloat16
MESH = pl.DeviceIdType.MESH
BS = pl.BlockSpec
SDS = jax.ShapeDtypeStruct
ANY = pl.BlockSpec(memory_space=pl.ANY)

VMEM_LIMIT_BYTES = 52 * 1024 * 1024
LANES = 128

D_MODEL = 2048
HEAD_DIM = 64
EPS = 1e-6
NEG = -1e30
FOX_HEADS = 8
FOX_WIDTH = 512
CONV_WIDTH = 512
SGU_WIDTH = 512
SGU_CHUNK = 128
DIL_PATTERNS = ((128, 1), (512, 4), (2048, 16))
DIL_SPAN = 128
DIL_HEADS = 12
DIL_WIDTH = 768
DIL_OUT = 256
ROPE_THETA = 500000.0
ROPE_DIM = 16
N_BRANCH = 4
D_FF = 5632
PLE_DIM = 256
D_IN = 14600

OFF_G, OFF_A, OFF_B, OFF_C, OFF_D, OFF_F, W_AL = 0, 8192, 9728, 11264, 12288, 14592, 14848
SRC_A, SRC_F, SRC_B, SRC_C, SRC_DQ, SRC_G = 0, 1536, 1544, 3080, 4104, 6408

ADAM_LR, ADAM_B1, ADAM_B2, ADAM_EPS, ADAM_WD, ADAM_STEP = 0.001, 0.9, 0.999, 1e-08, 0.01, 10

PACK_COLS = 1024
N_CHIPS = 4
N_DEV = 8

SHARDED = ("w_in", "shortconv_w", "w_br_fox", "w_br_conv", "w_br_sgu", "w_br_dil", "w_out", "w_up",
           "ffn_conv_w", "w_down", "w_ple_gate", "w_ple_proj")
ROW_SHARDED = ("w_out", "w_down", "w_ple_gate")
REPLICATED = ("norm_mix_g", "fox_forget_b", "sgu_norm_g", "sgu_w", "sgu_b", "norm_ffn_g", "norm_ple_g")
WEIGHTS = ("norm_mix_g", "w_in", "fox_forget_b", "shortconv_w", "sgu_norm_g", "sgu_w", "sgu_b", "w_br_fox",
           "w_br_conv", "w_br_sgu", "w_br_dil", "w_out", "norm_ffn_g", "w_up", "ffn_conv_w", "w_down",
           "norm_ple_g", "w_ple_gate", "w_ple_proj", "final_norm_g")


def _params(n_grid):
    return pltpu.CompilerParams(dimension_semantics=("arbitrary",) * n_grid, vmem_limit_bytes=VMEM_LIMIT_BYTES)


def _tile(n, pref, unit=LANES):
    best = None
    t = unit
    while t <= min(n, pref):
        if n % t == 0:
            best = t
        t += unit
    return best if best is not None else n


def _sigmoid(z):
    return 0.5 * jnp.tanh(0.5 * z) + 0.5


MAX_RESIDENT_K = 2048


def matmul(a, b, mode, out_dtype, name, res=None, tm=1536, tn=1024, tk=1536, side=None):
    if mode == "tn":
        a, mode = a.astype(BF16).T, "nn"
    if mode == "nn":
        (M, K), (K2, N) = a.shape, b.shape
    else:
        (M, K), (N, K2) = a.shape, b.shape
    assert K == K2, (name, a.shape, b.shape)
    if K <= MAX_RESIDENT_K:
        tk = K
    tm, tn, tk = _tile(M, tm), _tile(N, tn), _tile(K, tk)
    nk = K // tk
    if mode == "nn":
        a_spec, b_spec = BS((tm, tk), lambda i, j, k: (i, k)), BS((tk, tn), lambda i, j, k: (k, j))
        dims = (((1,), (0,)), ((), ()))
    else:
        a_spec, b_spec = BS((tm, tk), lambda i, j, k: (i, k)), BS((tn, tk), lambda i, j, k: (j, k))
        dims = (((1,), (1,)), ((), ()))
    has_res = res is not None

    def body(*refs):
        if has_res:
            a_ref, b_ref, r_ref, o_ref, acc = refs
        else:
            a_ref, b_ref, o_ref, acc = refs
        k = pl.program_id(2)

        @pl.when(k == 0)
        def _():
            acc[...] = jnp.zeros_like(acc)

        acc[...] += lax.dot_general(a_ref[...].astype(BF16), b_ref[...].astype(BF16), dims,
                                    preferred_element_type=F32)

        @pl.when(k == nk - 1)
        def _():
            r = acc[...]
            if has_res:
                r = r + r_ref[...]
            o_ref[...] = r.astype(o_ref.dtype)

    in_specs = [a_spec, b_spec]
    args = [a, b]
    if has_res:
        in_specs.append(BS((tm, tn), lambda i, j, k: (i, j)))
        args.append(res)
    (out,), side_out = _call_with_side(
        body, side, name, (M // tm, N // tn, nk), in_specs, [BS((tm, tn), lambda i, j, k: (i, j))],
        [SDS((M, N), out_dtype)], args, scratch=[pltpu.VMEM((tm, tn), F32)])
    return out if side is None else (out, side_out)


def rmsnorm_fwd(x, g, name):
    S, Dm = x.shape
    tm = _tile(S, 256, LANES)

    def body(x_ref, g_ref, y_ref, yt_ref):
        xf = x_ref[...]
        r = lax.rsqrt(jnp.mean(xf * xf, axis=-1, keepdims=True) + EPS)
        y = (xf * r) * g_ref[...]
        y_ref[...] = y.astype(y_ref.dtype)
        yt_ref[...] = y.T.astype(yt_ref.dtype)

    return pl.pallas_call(
        body, name=name, grid=(S // tm,),
        in_specs=[BS((tm, Dm), lambda i: (i, 0)), BS((1, Dm), lambda i: (0, 0))],
        out_specs=[BS((tm, Dm), lambda i: (i, 0)), BS((Dm, tm), lambda i: (0, i))],
        out_shape=[SDS((S, Dm), BF16), SDS((Dm, S), BF16)], compiler_params=_params(1))(x, g.reshape(1, Dm))


def rmsnorm_bwd(x, g, dy, dres, name):
    S, Dm = x.shape
    tm = _tile(S, 256, 8)

    def body(x_ref, g_ref, dy_ref, dres_ref, dx_ref, dxb_ref, dg_ref):
        xf = x_ref[...]
        r = lax.rsqrt(jnp.mean(xf * xf, axis=-1, keepdims=True) + EPS)
        xh = xf * r
        dy = dy_ref[...].astype(F32)
        dxh = dy * g_ref[...]
        dx = r * (dxh - xh * jnp.mean(dxh * xh, axis=-1, keepdims=True)) + dres_ref[...]
        dx_ref[...] = dx
        dxb_ref[...] = dx.astype(dxb_ref.dtype)

        @pl.when(pl.program_id(0) == 0)
        def _():
            dg_ref[...] = jnp.zeros_like(dg_ref)

        dg_ref[...] += jnp.sum(dy * xh, axis=0, keepdims=True)

    row = BS((tm, Dm), lambda i: (i, 0))
    vec = BS((1, Dm), lambda i: (0, 0))
    return pl.pallas_call(
        body, name=name, grid=(S // tm,), in_specs=[row, vec, row, row], out_specs=[row, row, vec],
        out_shape=[SDS((S, Dm), F32), SDS((S, Dm), BF16), SDS((1, Dm), F32)],
        compiler_params=_params(1))(x, g.reshape(1, Dm), dy, dres)


def final_loss(x, g, target, name):
    S, Dm = x.shape
    tm = _tile(S, 256, 8)

    def body(x_ref, g_ref, t_ref, loss_ref, dx_ref, dg_ref):
        xf = x_ref[...]
        r = lax.rsqrt(jnp.mean(xf * xf, axis=-1, keepdims=True) + EPS)
        xh = xf * r
        gv = g_ref[...]
        err = xh * gv - t_ref[...]
        dy = err * (1.0 / Dm)
        dxh = dy * gv
        dx_ref[...] = r * (dxh - xh * jnp.mean(dxh * xh, axis=-1, keepdims=True))

        @pl.when(pl.program_id(0) == 0)
        def _():
            dg_ref[...] = jnp.zeros_like(dg_ref)
            loss_ref[...] = jnp.zeros_like(loss_ref)

        dg_ref[...] += jnp.sum(dy * xh, axis=0, keepdims=True)
        part = 0.5 * jnp.sum(jnp.mean(err * err, axis=-1, keepdims=True), axis=0, keepdims=True)
        loss_ref[...] += jnp.broadcast_to(part, loss_ref.shape)

    row = BS((tm, Dm), lambda i: (i, 0))
    vec = BS((1, Dm), lambda i: (0, 0))
    return pl.pallas_call(
        body, name=name, grid=(S // tm,), in_specs=[row, vec, row],
        out_specs=[BS((1, LANES), lambda i: (0, 0)), row, vec],
        out_shape=[SDS((1, LANES), F32), SDS((S, Dm), F32), SDS((1, Dm), F32)],
        compiler_params=_params(1))(x, g.reshape(1, Dm), target)


def _dot_f32(a, b):
    return jnp.dot(a, b, preferred_element_type=F32, precision=lax.Precision.HIGHEST)


def _dot(a, b):
    return jnp.dot(a, b, preferred_element_type=F32)


def _dot_nt(a, b):
    return lax.dot_general(a, b, (((1,), (1,)), ((), ())), preferred_element_type=F32)


def fox_prep_fwd(proj, bias, name):
    S = proj.shape[0]
    H = FOX_HEADS
    nc = S // LANES

    def body(p_ref, b_ref, a_ref, f_ref):
        row = lax.broadcasted_iota(jnp.int32, (LANES, LANES), 0)
        col = lax.broadcasted_iota(jnp.int32, (LANES, LANES), 1)
        upper = (row <= col).astype(F32)
        carry = jnp.zeros((H, 1), F32)
        for c in range(nc):
            sl = slice(c * LANES, (c + 1) * LANES)
            logits = p_ref[sl, :].T[0:H, :]
            a_ref[:, sl] = logits
            z = logits + b_ref[...]
            chunk = jnp.minimum(z, 0.0) - jnp.log(1.0 + jnp.exp(-jnp.abs(z)))
            f_ref[:, sl] = _dot_f32(chunk, upper) + carry
            carry = carry + jnp.sum(chunk, axis=1, keepdims=True)

    full = BS((H, S), lambda i: (0, 0))
    return pl.pallas_call(
        body, name=name, grid=(1,),
        in_specs=[BS((S, LANES), lambda i: (0, OFF_F // LANES)), BS((H, 1), lambda i: (0, 0))],
        out_specs=[full, full], out_shape=[SDS((H, S), F32)] * 2, compiler_params=_params(1))(proj, bias)


def fox_prep_bwd(af_t, bias, d_fq, d_fk, name):
    H, S = af_t.shape
    nc = S // LANES

    def body(a_ref, b_ref, dfq_ref, dfk_ref, da_ref, db_ref):
        row = lax.broadcasted_iota(jnp.int32, (LANES, LANES), 0)
        col = lax.broadcasted_iota(jnp.int32, (LANES, LANES), 1)
        lower = (row >= col).astype(F32)
        carry = jnp.zeros((H, 1), F32)
        dbias = jnp.zeros((H, 1), F32)
        for c in range(nc - 1, -1, -1):
            sl = slice(c * LANES, (c + 1) * LANES)
            chunk = dfq_ref[:, sl] + dfk_ref[:, sl]
            dlogf = _dot_f32(chunk, lower) + carry
            carry = carry + jnp.sum(chunk, axis=1, keepdims=True)
            z = a_ref[:, sl] + b_ref[...]
            da = dlogf / (1.0 + jnp.exp(z))
            da_ref[:, sl] = da
            dbias = dbias + jnp.sum(da, axis=1, keepdims=True)
        db_ref[...] = dbias

    full = BS((H, S), lambda i: (0, 0))
    vec = BS((H, 1), lambda i: (0, 0))
    return pl.pallas_call(
        body, name=name, grid=(1,), in_specs=[full, vec, full, full], out_specs=[full, vec],
        out_shape=[SDS((H, S), F32), SDS((H, 1), F32)], compiler_params=_params(1))(af_t, bias, d_fq, d_fk)


FOX_T = 256
FOX_HP = 4


def _causal_tile(T):
    return lax.broadcasted_iota(jnp.int32, (T, T), 1) <= lax.broadcasted_iota(jnp.int32, (T, T), 0)


class SideExchange:
    def __init__(self, operands, out_shapes, n_sems, start, finish):
        self.operands, self.out_shapes, self.n_sems, self.start, self.finish = operands, out_shapes, n_sems, start, finish


def _call_with_side(body, side, name, grid, in_specs, out_specs, out_shape, args, scratch=()):
    scratch = list(scratch)
    if side is None:
        return pl.pallas_call(body, name=name, grid=grid, in_specs=in_specs, out_specs=out_specs,
                              out_shape=out_shape, scratch_shapes=scratch,
                              compiler_params=_params(len(grid)))(*args), []
    n_in, n_out = len(in_specs), len(out_specs)
    s_in, s_out = len(side.operands), len(side.out_shapes)

    def wrapped(*refs):
        main_in, side_in = refs[:n_in], refs[n_in:n_in + s_in]
        main_out = refs[n_in + s_in:n_in + s_in + n_out]
        side_out = refs[n_in + s_in + n_out:n_in + s_in + n_out + s_out]
        main_scratch = refs[n_in + s_in + n_out + s_out:-2]
        send_sems, recv_sems = refs[-2:]
        first = pl.program_id(0) == 0
        last = pl.program_id(0) == grid[0] - 1
        for axis in range(1, len(grid)):
            first = jnp.logical_and(first, pl.program_id(axis) == 0)
            last = jnp.logical_and(last, pl.program_id(axis) == grid[axis] - 1)

        @pl.when(first)
        def _():
            side.start(side_in, side_out, send_sems, recv_sems)

        body(*main_in, *main_out, *main_scratch)

        @pl.when(last)
        def _():
            side.finish(side_in, side_out, send_sems, recv_sems)

    outs = pl.pallas_call(
        wrapped, name=name, grid=grid, in_specs=list(in_specs) + [ANY] * s_in,
        out_specs=list(out_specs) + [ANY] * s_out, out_shape=list(out_shape) + list(side.out_shapes),
        scratch_shapes=scratch + [pltpu.SemaphoreType.DMA((side.n_sems,)), pltpu.SemaphoreType.DMA((side.n_sems,))],
        compiler_params=_params(len(grid)))(*args, *side.operands)
    return outs[:n_out], outs[n_out:]


def _fox_specs(H, S, Dh, T):
    nq = S // T
    blk = BS((FOX_HP, T, Dh), lambda h, i: (h, i, 0))
    full = BS((FOX_HP, S, Dh), lambda h, i: (h, 0, 0))
    colb = BS((FOX_HP, T, 1), lambda h, i: (h, i, 0))
    rowf = BS((FOX_HP, nq, 1, T), lambda h, i: (h, 0, 0, 0))
    return blk, full, colb, rowf, (H // FOX_HP, nq)


def fox_attn_fwd(q, k, v, f_col, f_row, name, side=None):
    H, S, Dh = q.shape
    T = min(FOX_T, S)
    scale = Dh ** -0.5

    def body(q_ref, k_ref, v_ref, fc_ref, fr_ref, o_ref, lse_ref):
        qi = pl.program_id(1)
        qs = [q_ref[h] for h in range(FOX_HP)]
        fqs = [fc_ref[h] for h in range(FOX_HP)]

        def step(j, carry, diagonal):
            off = pl.multiple_of(j * T, T)
            out = []
            for h in range(FOX_HP):
                m, l, acc = carry[h]
                kv = k_ref[h, pl.ds(off, T), :]
                vv = v_ref[h, pl.ds(off, T), :]
                s = _dot_nt(qs[h], kv) * scale + (fqs[h] - fr_ref[h, j])
                if diagonal:
                    s = jnp.where(_causal_tile(T), s, NEG)
                m_new = jnp.maximum(m, jnp.max(s, axis=-1, keepdims=True))
                p = jnp.exp(s - m_new)
                alpha = jnp.exp(m - m_new)
                l = alpha * l + jnp.sum(p, axis=-1, keepdims=True)
                acc = alpha * acc + _dot(p.astype(BF16), vv)
                out.append((m_new, l, acc))
            return tuple(out)

        init = tuple((jnp.full((T, 1), NEG, F32), jnp.zeros((T, 1), F32), jnp.zeros((T, Dh), F32))
                     for _ in range(FOX_HP))
        carry = lax.fori_loop(0, qi, functools.partial(step, diagonal=False), init)
        carry = step(qi, carry, True)
        for h in range(FOX_HP):
            m, l, acc = carry[h]
            o_ref[h] = (acc / l).astype(o_ref.dtype)
            lse_ref[h] = m + jnp.log(l)

    blk, full, colb, rowf, grid = _fox_specs(H, S, Dh, T)
    return _call_with_side(body, side, name, grid, [blk, full, full, colb, rowf], [blk, colb],
                           [SDS((H, S, Dh), BF16), SDS((H, S, 1), F32)], (q, k, v, f_col, f_row))


def fox_attn_bwd_dq(q, k, v, f_col, f_row, o, lse, do, name, side=None):
    H, S, Dh = q.shape
    T = min(FOX_T, S)
    scale = Dh ** -0.5

    def body(q_ref, k_ref, v_ref, fc_ref, fr_ref, o_ref, lse_ref, do_ref, dq_ref, dl_ref, df_ref):
        qi = pl.program_id(1)
        qs = [q_ref[h] for h in range(FOX_HP)]
        fqs = [fc_ref[h] for h in range(FOX_HP)]
        lses = [lse_ref[h] for h in range(FOX_HP)]
        dos = [do_ref[h] for h in range(FOX_HP)]
        deltas = [jnp.sum(dos[h].astype(F32) * o_ref[h].astype(F32), axis=-1, keepdims=True) for h in range(FOX_HP)]
        for h in range(FOX_HP):
            dl_ref[h] = deltas[h]

        def step(j, carry, diagonal):
            off = pl.multiple_of(j * T, T)
            out = []
            for h in range(FOX_HP):
                dq, dfq = carry[h]
                kv = k_ref[h, pl.ds(off, T), :]
                vv = v_ref[h, pl.ds(off, T), :]
                s = _dot_nt(qs[h], kv) * scale + (fqs[h] - fr_ref[h, j])
                if diagonal:
                    s = jnp.where(_causal_tile(T), s, NEG)
                p = jnp.exp(s - lses[h])
                ds = p * (_dot_nt(dos[h], vv) - deltas[h])
                out.append((dq + _dot(ds.astype(BF16), kv), dfq + jnp.sum(ds, axis=-1, keepdims=True)))
            return tuple(out)

        init = tuple((jnp.zeros((T, Dh), F32), jnp.zeros((T, 1), F32)) for _ in range(FOX_HP))
        carry = lax.fori_loop(0, qi, functools.partial(step, diagonal=False), init)
        carry = step(qi, carry, True)
        for h in range(FOX_HP):
            dq_ref[h] = carry[h][0] * scale
            df_ref[h] = carry[h][1]

    blk, full, colb, rowf, grid = _fox_specs(H, S, Dh, T)
    return _call_with_side(body, side, name, grid, [blk, full, full, colb, rowf, blk, colb, blk], [blk, colb, colb],
                           [SDS((H, S, Dh), F32), SDS((H, S, 1), F32), SDS((H, S, 1), F32)],
                           (q, k, v, f_col, f_row, o, lse, do))


def fox_attn_bwd_dkv(q, k, v, f_col, f_row, lse_row, delta_row, do, name, side=None):
    H, S, Dh = q.shape
    T = min(FOX_T, S)
    nq = S // T
    scale = Dh ** -0.5

    def body(q_ref, k_ref, v_ref, fc_ref, fr_ref, lse_ref, dl_ref, do_ref, dk_ref, dv_ref, df_ref):
        kj = pl.program_id(1)
        ks = [k_ref[h] for h in range(FOX_HP)]
        vs = [v_ref[h] for h in range(FOX_HP)]
        fks = [fc_ref[h] for h in range(FOX_HP)]

        def step(i, carry, diagonal):
            off = pl.multiple_of(i * T, T)
            out = []
            for h in range(FOX_HP):
                dk, dv, dfk = carry[h]
                qv = q_ref[h, pl.ds(off, T), :]
                dov = do_ref[h, pl.ds(off, T), :]
                st = _dot_nt(ks[h], qv) * scale + (fr_ref[h, i] - fks[h])
                if diagonal:
                    st = jnp.where(lax.broadcasted_iota(jnp.int32, (T, T), 0)
                                   <= lax.broadcasted_iota(jnp.int32, (T, T), 1), st, NEG)
                pt = jnp.exp(st - lse_ref[h, i])
                dv = dv + _dot(pt.astype(BF16), dov)
                dst = pt * (_dot_nt(vs[h], dov) - dl_ref[h, i])
                dk = dk + _dot(dst.astype(BF16), qv)
                out.append((dk, dv, dfk + jnp.sum(dst, axis=-1, keepdims=True)))
            return tuple(out)

        init = tuple((jnp.zeros((T, Dh), F32), jnp.zeros((T, Dh), F32), jnp.zeros((T, 1), F32))
                     for _ in range(FOX_HP))
        carry = step(kj, init, True)
        carry = lax.fori_loop(kj + 1, nq, functools.partial(step, diagonal=False), carry)
        for h in range(FOX_HP):
            dk_ref[h] = carry[h][0] * scale
            dv_ref[h] = carry[h][1]
            df_ref[h] = -carry[h][2]

    blk, full, colb, rowf, grid = _fox_specs(H, S, Dh, T)
    return _call_with_side(body, side, name, grid, [full, blk, blk, colb, rowf, rowf, rowf, full], [blk, blk, colb],
                           [SDS((H, S, Dh), F32), SDS((H, S, Dh), F32), SDS((H, S, 1), F32)],
                           (q, k, v, f_col, f_row, lse_row, delta_row, do))


HALO = 8
CONV_TM = 512


def _conv_ext(e, w_ref):
    return w_ref[0:1, :] * pltpu.roll(e, 2, 0) + w_ref[1:2, :] * pltpu.roll(e, 1, 0) + w_ref[2:3, :] * e


def _conv_t_ext(d, w_ref):
    n = d.shape[0]
    return w_ref[2:3, :] * d + w_ref[1:2, :] * pltpu.roll(d, n - 1, 0) + w_ref[0:1, :] * pltpu.roll(d, n - 2, 0)


def _time_specs(S, tm, width, col_block):
    per = tm // HALO
    last = S // HALO - 1
    cur = BS((tm, width), lambda j, i: (i, col_block(j)))
    prev = BS((HALO, width), lambda j, i: (jnp.maximum(i * per - 1, 0), col_block(j)))
    nxt = BS((HALO, width), lambda j, i: (jnp.minimum((i + 1) * per, last), col_block(j)))
    return cur, prev, nxt


def shortconv_fwd(proj, w, name):
    S = proj.shape[0]
    tm = _tile(S, CONV_TM, 8)
    nb = CONV_WIDTH // LANES
    b0 = OFF_B // LANES

    def body(xb_ref, xbp_ref, gb_ref, gc_ref, gcp_ref, w_ref, o_ref):
        keep = (pl.program_id(1) > 0).astype(F32)
        e = jnp.concatenate([gcp_ref[...] * xbp_ref[...] * keep, gc_ref[...] * xb_ref[...]], axis=0)
        o_ref[...] = (gb_ref[...] * _conv_ext(e, w_ref)[HALO:, :]).astype(o_ref.dtype)

    xb, xbp, _ = _time_specs(S, tm, LANES, lambda j: b0 + j)
    gb, _, _ = _time_specs(S, tm, LANES, lambda j: b0 + nb + j)
    gc, gcp, _ = _time_specs(S, tm, LANES, lambda j: b0 + 2 * nb + j)
    return pl.pallas_call(
        body, name=name, grid=(nb, S // tm),
        in_specs=[xb, xbp, gb, gc, gcp, BS((3, LANES), lambda j, i: (0, j))],
        out_specs=BS((tm, LANES), lambda j, i: (i, j)), out_shape=SDS((S, CONV_WIDTH), BF16),
        compiler_params=_params(2))(proj, proj, proj, proj, proj, w)


def shortconv_bwd(proj, w, do_b, name):
    S = proj.shape[0]
    tm = _tile(S, CONV_TM, 8)
    nt = S // tm
    nb = CONV_WIDTH // LANES
    b0 = OFF_B // LANES

    def body(xb_ref, xbp_ref, gb_ref, gbn_ref, gc_ref, gcp_ref, do_ref, don_ref, w_ref,
             dxb_ref, dgb_ref, dgc_ref, dw_ref):
        i = pl.program_id(1)
        keep_prev = (i > 0).astype(F32)
        keep_next = (i < nt - 1).astype(F32)
        xb, gb, gc = xb_ref[...], gb_ref[...], gc_ref[...]
        do = do_ref[...].astype(F32)
        u = gc * xb
        e = jnp.concatenate([gcp_ref[...] * xbp_ref[...] * keep_prev, u], axis=0)
        dgb_ref[...] = (do * _conv_ext(e, w_ref)[HALO:, :]).astype(dgb_ref.dtype)
        dcv = do * gb
        d_ext = jnp.concatenate([dcv, don_ref[...].astype(F32) * gbn_ref[...] * keep_next], axis=0)
        du = _conv_t_ext(d_ext, w_ref)[:tm, :]
        dgc_ref[...] = (du * xb).astype(dgc_ref.dtype)
        dxb_ref[...] = (du * gc).astype(dxb_ref.dtype)

        @pl.when(i == 0)
        def _():
            dw_ref[...] = jnp.zeros_like(dw_ref)

        dw_ref[0:1, :] += jnp.sum(dcv * pltpu.roll(e, 2, 0)[HALO:, :], axis=0, keepdims=True)
        dw_ref[1:2, :] += jnp.sum(dcv * pltpu.roll(e, 1, 0)[HALO:, :], axis=0, keepdims=True)
        dw_ref[2:3, :] += jnp.sum(dcv * u, axis=0, keepdims=True)

    xb, xbp, _ = _time_specs(S, tm, LANES, lambda j: b0 + j)
    gb, _, gbn = _time_specs(S, tm, LANES, lambda j: b0 + nb + j)
    gc, gcp, _ = _time_specs(S, tm, LANES, lambda j: b0 + 2 * nb + j)
    do, _, don = _time_specs(S, tm, LANES, lambda j: j)
    out = BS((tm, LANES), lambda j, i: (i, j))
    wspec = BS((3, LANES), lambda j, i: (0, j))
    return pl.pallas_call(
        body, name=name, grid=(nb, nt), in_specs=[xb, xbp, gb, gbn, gc, gcp, do, don, wspec],
        out_specs=[out, out, out, wspec],
        out_shape=[SDS((S, CONV_WIDTH), BF16)] * 3 + [SDS((3, CONV_WIDTH), F32)],
        compiler_params=_params(2))(proj, proj, proj, proj, proj, proj, do_b, do_b, w)


_GELU_C = 0.7978845608028654


def _gelu(x):
    return 0.5 * x * (1.0 + jnp.tanh(_GELU_C * (x + 0.044715 * x * x * x)))


def _gelu_grad(x):
    t = jnp.tanh(_GELU_C * (x + 0.044715 * x * x * x))
    return 0.5 * (1.0 + t) + 0.5 * x * (1.0 - t * t) * _GELU_C * (1.0 + 3.0 * 0.044715 * x * x)


def _tril_masks():
    row = lax.broadcasted_iota(jnp.int32, (SGU_CHUNK, SGU_CHUNK), 0)
    col = lax.broadcasted_iota(jnp.int32, (SGU_CHUNK, SGU_CHUNK), 1)
    return row >= col, row <= col


def _lane_column(mat, g):
    lane = lax.broadcasted_iota(jnp.int32, mat.shape, 1)
    return jnp.sum(jnp.where(lane == g, mat, 0.0), axis=-1, keepdims=True)


def sgu_fwd(proj, norm_g, w_s, b_t, name):
    S = proj.shape[0]
    T = SGU_CHUNK
    G = SGU_WIDTH // T
    c0 = OFF_C // (2 * SGU_WIDTH)

    def body(c_ref, g_ref, w_ref, b_ref, o_ref):
        u = _gelu(c_ref[:, 0:SGU_WIDTH])
        v = _gelu(c_ref[:, SGU_WIDTH:2 * SGU_WIDTH])
        r = lax.rsqrt(jnp.mean(v * v, axis=-1, keepdims=True) + EPS)
        vn = ((v * r) * g_ref[...]).astype(BF16)
        mask, _ = _tril_masks()
        bias = b_ref[...]
        for g in range(G):
            sl = slice(g * T, (g + 1) * T)
            wt = jnp.where(mask, w_ref[g], 0.0).astype(BF16)
            mixed = _dot(wt, vn[:, sl]) + _lane_column(bias, g)
            o_ref[:, sl] = (u[:, sl] * mixed).astype(o_ref.dtype)

    return pl.pallas_call(
        body, name=name, grid=(S // T,),
        in_specs=[BS((T, 2 * SGU_WIDTH), lambda i: (i, c0)), BS((1, SGU_WIDTH), lambda i: (0, 0)),
                  BS((G, T, T), lambda i: (0, 0, 0)), BS((T, T), lambda i: (0, 0))],
        out_specs=BS((T, SGU_WIDTH), lambda i: (i, 0)), out_shape=SDS((S, SGU_WIDTH), BF16),
        compiler_params=_params(1))(proj, norm_g, w_s, b_t)


def sgu_bwd(proj, norm_g, w_s, w_st, b_t, do_c, name):
    S = proj.shape[0]
    T = SGU_CHUNK
    G = SGU_WIDTH // T
    c0 = OFF_C // (2 * SGU_WIDTH)

    def body(c_ref, g_ref, w_ref, wt_ref, b_ref, do_ref, dc_ref, dg_ref, dw_ref, db_ref):
        cu = c_ref[:, 0:SGU_WIDTH]
        cv = c_ref[:, SGU_WIDTH:2 * SGU_WIDTH]
        u = _gelu(cu)
        v = _gelu(cv)
        r = lax.rsqrt(jnp.mean(v * v, axis=-1, keepdims=True) + EPS)
        xh = v * r
        gv = g_ref[...]
        vn = (xh * gv).astype(BF16)
        do = do_ref[...].astype(F32)
        mask, mask_t = _tril_masks()
        bias = b_ref[...]
        lane = lax.broadcasted_iota(jnp.int32, (T, T), 1)

        @pl.when(pl.program_id(0) == 0)
        def _():
            dg_ref[...] = jnp.zeros_like(dg_ref)
            dw_ref[...] = jnp.zeros_like(dw_ref)
            db_ref[...] = jnp.zeros_like(db_ref)

        dvn_parts = []
        du_parts = []
        dbias = jnp.zeros((T, T), F32)
        for g in range(G):
            sl = slice(g * T, (g + 1) * T)
            wt = jnp.where(mask, w_ref[g], 0.0).astype(BF16)
            mixed = _dot(wt, vn[:, sl]) + _lane_column(bias, g)
            du_parts.append(do[:, sl] * mixed)
            dmix = do[:, sl] * u[:, sl]
            dmix_b = dmix.astype(BF16)
            dw_ref[g] += jnp.where(mask, _dot_nt(dmix_b, vn[:, sl]), 0.0)
            dbias = dbias + jnp.where(lane == g, jnp.sum(dmix, axis=-1, keepdims=True), 0.0)
            wtt = jnp.where(mask_t, wt_ref[g], 0.0).astype(BF16)
            dvn_parts.append(_dot(wtt, dmix_b))
        db_ref[...] += dbias
        dvn = jnp.concatenate(dvn_parts, axis=-1)
        du = jnp.concatenate(du_parts, axis=-1)
        dg_ref[...] += jnp.sum(dvn * xh, axis=0, keepdims=True)
        dxh = dvn * gv
        dv = r * (dxh - xh * jnp.mean(dxh * xh, axis=-1, keepdims=True))
        dc_ref[:, 0:SGU_WIDTH] = (du * _gelu_grad(cu)).astype(dc_ref.dtype)
        dc_ref[:, SGU_WIDTH:2 * SGU_WIDTH] = (dv * _gelu_grad(cv)).astype(dc_ref.dtype)

    wspec = BS((G, T, T), lambda i: (0, 0, 0))
    gspec = BS((1, SGU_WIDTH), lambda i: (0, 0))
    return pl.pallas_call(
        body, name=name, grid=(S // T,),
        in_specs=[BS((T, 2 * SGU_WIDTH), lambda i: (i, c0)), gspec, wspec, wspec, BS((T, T), lambda i: (0, 0)),
                  BS((T, SGU_WIDTH), lambda i: (i, 0))],
        out_specs=[BS((T, 2 * SGU_WIDTH), lambda i: (i, 0)), gspec, wspec, BS((T, T), lambda i: (0, 0))],
        out_shape=[SDS((S, 2 * SGU_WIDTH), BF16), SDS((1, SGU_WIDTH), F32), SDS((G, T, T), F32), SDS((T, T), F32)],
        compiler_params=_params(1))(proj, norm_g, w_s, w_st, b_t, do_c)


def rope_tables(positions):
    half = ROPE_DIM // 2
    inv = ROPE_THETA ** (-jnp.arange(half, dtype=F32) * (2.0 / ROPE_DIM))
    ang = positions.astype(F32)[:, None] * inv
    cos, sin = jnp.cos(ang), jnp.sin(ang)
    S = positions.shape[0]
    ones = jnp.ones((S, HEAD_DIM - ROPE_DIM), F32)
    zeros = jnp.zeros((S, HEAD_DIM - ROPE_DIM), F32)
    zh = jnp.zeros((S, half), F32)
    c = jnp.concatenate([cos, cos, ones], axis=-1)
    sa = jnp.concatenate([zh, sin, zeros], axis=-1)
    sb = jnp.concatenate([-sin, zh, zeros], axis=-1)
    tile2 = lambda t: jnp.concatenate([t, t], axis=-1)
    return tile2(c), tile2(sa), tile2(sb)


def rope_apply(x, col0, ncols, cos, sa, sb, out_dtype, name):
    S = x.shape[0]
    tm = _tile(S, 512, 8)
    half = ROPE_DIM // 2
    b0 = col0 // LANES

    def body(x_ref, c_ref, sa_ref, sb_ref, o_ref):
        xv = x_ref[...].astype(F32)
        o_ref[...] = (xv * c_ref[...] + pltpu.roll(xv, half, 1) * sa_ref[...]
                      + pltpu.roll(xv, LANES - half, 1) * sb_ref[...]).astype(o_ref.dtype)

    tab = BS((tm, LANES), lambda i, j: (i, 0))
    return pl.pallas_call(
        body, name=name, grid=(S // tm, ncols // LANES),
        in_specs=[BS((tm, LANES), lambda i, j: (i, b0 + j)), tab, tab, tab],
        out_specs=BS((tm, LANES), lambda i, j: (i, j)), out_shape=SDS((S, ncols), out_dtype),
        compiler_params=_params(2))(x, cos, sa, sb)


def _dil_masks(first):
    qi = lax.broadcasted_iota(jnp.int32, (DIL_SPAN, DIL_SPAN), 0)
    ki = lax.broadcasted_iota(jnp.int32, (DIL_SPAN, DIL_SPAN), 1)
    return ki >= qi + first.astype(jnp.int32) * DIL_SPAN, ki <= qi


def dil_attn_fwd(q, k, v, nb, name):
    H, S, Dh = q.shape
    T = DIL_SPAN
    nblk = S // T
    scale = Dh ** -0.5

    def body(q_ref, kp_ref, kc_ref, vp_ref, vc_ref, o_ref, lse_ref):
        mp, mc = _dil_masks(pl.program_id(0) % nb == 0)
        for h in range(H):
            qv = q_ref[h]
            sp = jnp.where(mp, _dot_nt(qv, kp_ref[h]) * scale, NEG)
            sc = jnp.where(mc, _dot_nt(qv, kc_ref[h]) * scale, NEG)
            m = jnp.maximum(jnp.max(sp, axis=-1, keepdims=True), jnp.max(sc, axis=-1, keepdims=True))
            ep = jnp.exp(sp - m)
            ec = jnp.exp(sc - m)
            l = jnp.sum(ep, axis=-1, keepdims=True) + jnp.sum(ec, axis=-1, keepdims=True)
            o_ref[h] = (_dot(ep.astype(BF16), vp_ref[h]) + _dot(ec.astype(BF16), vc_ref[h])) / l
            lse_ref[h] = m + jnp.log(l)

    cur = BS((H, T, Dh), lambda b: (0, b, 0))
    prev = BS((H, T, Dh), lambda b: (0, jnp.maximum(b - 1, 0), 0))
    colb = BS((H, T, 1), lambda b: (0, b, 0))
    return pl.pallas_call(
        body, name=name, grid=(nblk,), in_specs=[cur, prev, cur, prev, cur], out_specs=[cur, colb],
        out_shape=[SDS((H, S, Dh), F32), SDS((H, S, 1), F32)], compiler_params=_params(1))(q, k, k, v, v)


def dil_attn_bwd_dq(q, k, v, o, lse, do, dlse, nb, name):
    H, S, Dh = q.shape
    T = DIL_SPAN
    nblk = S // T
    scale = Dh ** -0.5

    def body(q_ref, kp_ref, kc_ref, vp_ref, vc_ref, o_ref, lse_ref, do_ref, dlse_ref, dq_ref, dl_ref):
        mp, mc = _dil_masks(pl.program_id(0) % nb == 0)
        for h in range(H):
            qv = q_ref[h]
            kp, kc = kp_ref[h], kc_ref[h]
            dov = do_ref[h]
            delta = jnp.sum(dov * o_ref[h], axis=-1, keepdims=True) - dlse_ref[h]
            dl_ref[h] = delta
            dob = dov.astype(BF16)
            lse_v = lse_ref[h]
            pp = jnp.exp(jnp.where(mp, _dot_nt(qv, kp) * scale, NEG) - lse_v)
            pc = jnp.exp(jnp.where(mc, _dot_nt(qv, kc) * scale, NEG) - lse_v)
            dsp = pp * (_dot_nt(dob, vp_ref[h]) - delta)
            dsc = pc * (_dot_nt(dob, vc_ref[h]) - delta)
            dq_ref[h] = (_dot(dsp.astype(BF16), kp) + _dot(dsc.astype(BF16), kc)) * scale

    cur = BS((H, T, Dh), lambda b: (0, b, 0))
    prev = BS((H, T, Dh), lambda b: (0, jnp.maximum(b - 1, 0), 0))
    colb = BS((H, T, 1), lambda b: (0, b, 0))
    return pl.pallas_call(
        body, name=name, grid=(nblk,), in_specs=[cur, prev, cur, prev, cur, cur, colb, cur, colb],
        out_specs=[cur, colb], out_shape=[SDS((H, S, Dh), F32), SDS((H, S, 1), F32)],
        compiler_params=_params(1))(q, k, k, v, v, o, lse, do, dlse)


def dil_attn_bwd_dkv(q, k, v, lse_row, delta_row, do, nb, name):
    H, S, Dh = q.shape
    T = DIL_SPAN
    nblk = S // T
    scale = Dh ** -0.5

    def body(k_ref, v_ref, qc_ref, qn_ref, doc_ref, don_ref, lc_ref, ln_ref, dc_ref, dn_ref, dk_ref, dv_ref):
        no_next = ((pl.program_id(0) + 1) % nb == 0).astype(jnp.int32)
        si = lax.broadcasted_iota(jnp.int32, (T, T), 0)
        ti = lax.broadcasted_iota(jnp.int32, (T, T), 1)
        m_cur = si <= ti
        m_next = si >= ti + no_next * T
        for h in range(H):
            kv, vv = k_ref[h], v_ref[h]
            qc, qn = qc_ref[h], qn_ref[h]
            doc, don = doc_ref[h].astype(BF16), don_ref[h].astype(BF16)
            pt = jnp.exp(jnp.where(m_cur, _dot_nt(kv, qc) * scale, NEG) - lc_ref[h])
            ptn = jnp.exp(jnp.where(m_next, _dot_nt(kv, qn) * scale, NEG) - ln_ref[h])
            dv_ref[h] = _dot(pt.astype(BF16), doc) + _dot(ptn.astype(BF16), don)
            dst = pt * (_dot_nt(vv, doc) - dc_ref[h])
            dstn = ptn * (_dot_nt(vv, don) - dn_ref[h])
            dk_ref[h] = (_dot(dst.astype(BF16), qc) + _dot(dstn.astype(BF16), qn)) * scale

    cur = BS((H, T, Dh), lambda b: (0, b, 0))
    nxt = BS((H, T, Dh), lambda b: (0, jnp.minimum(b + 1, nblk - 1), 0))
    rcur = BS((H, None, 1, T), lambda b: (0, b, 0, 0))
    rnxt = BS((H, None, 1, T), lambda b: (0, jnp.minimum(b + 1, nblk - 1), 0, 0))
    return pl.pallas_call(
        body, name=name, grid=(nblk,), in_specs=[cur, cur, cur, nxt, cur, nxt, rcur, rnxt, rcur, rnxt],
        out_specs=[cur, cur], out_shape=[SDS((H, S, Dh), F32), SDS((H, S, Dh), F32)],
        compiler_params=_params(1))(k, v, q, q, do, do, lse_row, lse_row, delta_row, delta_row)


def dil_merge_fwd(outs, lses, name):
    H, S, Dh = outs[0].shape
    tm = _tile(S, 512, 8)

    def body(o0, o1, o2, l0, l1, l2, out_ref):
        ls = [l0[...], l1[...], l2[...]]
        m = jnp.maximum(jnp.maximum(ls[0], ls[1]), ls[2])
        es = [jnp.exp(l - m) for l in ls]
        den = es[0] + es[1] + es[2]
        out_ref[...] = (es[0] * o0[...] + es[1] * o1[...] + es[2] * o2[...]) / den

    blk = BS((None, tm, Dh), lambda h, i: (h, i, 0))
    colb = BS((None, tm, 1), lambda h, i: (h, i, 0))
    return pl.pallas_call(
        body, name=name, grid=(H, S // tm), in_specs=[blk] * 3 + [colb] * 3, out_specs=blk,
        out_shape=SDS((H, S, Dh), F32), compiler_params=_params(2))(*outs, *lses)


def dil_merge_bwd(outs, lses, d_out, name):
    H, S, Dh = outs[0].shape
    tm = _tile(S, 512, 8)

    def body(o0, o1, o2, l0, l1, l2, d_ref, do0, do1, do2, dl0, dl1, dl2):
        ls = [l0[...], l1[...], l2[...]]
        m = jnp.maximum(jnp.maximum(ls[0], ls[1]), ls[2])
        es = [jnp.exp(l - m) for l in ls]
        den = es[0] + es[1] + es[2]
        ws = [e / den for e in es]
        dv = d_ref[...]
        dws = [jnp.sum(dv * o[...], axis=-1, keepdims=True) for o in (o0, o1, o2)]
        mean = ws[0] * dws[0] + ws[1] * dws[1] + ws[2] * dws[2]
        for w, dw, do_ref, dl_ref in zip(ws, dws, (do0, do1, do2), (dl0, dl1, dl2)):
            do_ref[...] = w * dv
            dl_ref[...] = w * (dw - mean)

    blk = BS((None, tm, Dh), lambda h, i: (h, i, 0))
    colb = BS((None, tm, 1), lambda h, i: (h, i, 0))
    return pl.pallas_call(
        body, name=name, grid=(H, S // tm), in_specs=[blk] * 3 + [colb] * 3 + [blk],
        out_specs=[blk] * 3 + [colb] * 3,
        out_shape=[SDS((H, S, Dh), F32)] * 3 + [SDS((H, S, 1), F32)] * 3,
        compiler_params=_params(2))(*outs, *lses, d_out)


MERGE_TN = 512


def merge_fwd(o_stack, w_br, proj, name):
    _, S, K = o_stack.shape
    tm = _tile(S, 512, 8)
    tn = MERGE_TN
    nj = D_MODEL // tn

    def body(o_ref, w_ref, gl_ref, m_ref, acc):
        br = pl.program_id(2)

        @pl.when(br == 0)
        def _():
            acc[...] = jnp.zeros_like(acc)

        acc[...] += _sigmoid(gl_ref[...]) * _dot(o_ref[...], w_ref[...])

        @pl.when(br == N_BRANCH - 1)
        def _():
            m_ref[...] = acc[...].astype(m_ref.dtype)

    return pl.pallas_call(
        body, name=name, grid=(S // tm, nj, N_BRANCH),
        in_specs=[BS((None, tm, K), lambda i, j, b: (b, i, 0)), BS((None, K, tn), lambda i, j, b: (b, 0, j)),
                  BS((tm, tn), lambda i, j, b: (i, b * nj + j))],
        out_specs=BS((tm, tn), lambda i, j, b: (i, j)), out_shape=SDS((S, D_MODEL), BF16),
        scratch_shapes=[pltpu.VMEM((tm, tn), F32)], compiler_params=_params(3))(o_stack, w_br, proj)


def merge_bwd(o_stack, w_br, proj, d_merged, name):
    _, S, K = o_stack.shape
    tm = _tile(S, 512, 8)
    tn = MERGE_TN
    nj = D_MODEL // tn

    def body(o_ref, w_ref, gl_ref, dm_ref, dgl_ref, dy_ref):
        gate = _sigmoid(gl_ref[...])
        y = _dot(o_ref[...], w_ref[...])
        dm = dm_ref[...].astype(F32)
        dgl_ref[...] = (dm * y * gate * (1.0 - gate)).astype(dgl_ref.dtype)
        dy_ref[...] = (dm * gate).astype(dy_ref.dtype)

    return pl.pallas_call(
        body, name=name, grid=(S // tm, nj, N_BRANCH),
        in_specs=[BS((None, tm, K), lambda i, j, b: (b, i, 0)), BS((None, K, tn), lambda i, j, b: (b, 0, j)),
                  BS((tm, tn), lambda i, j, b: (i, b * nj + j)), BS((tm, tn), lambda i, j, b: (i, j))],
        out_specs=[BS((tm, tn), lambda i, j, b: (i, b * nj + j)), BS((None, tm, tn), lambda i, j, b: (b, i, j))],
        out_shape=[SDS((S, N_BRANCH * D_MODEL), BF16), SDS((N_BRANCH, S, D_MODEL), BF16)],
        compiler_params=_params(3))(o_stack, w_br, proj, d_merged)


FFN_CW = 256


def ffn_mid_fwd(pre_g, pre_v, w_g, w_v, name, side=None):
    S = pre_g.shape[0]
    tm = _tile(S, CONV_TM, 8)

    def body(g_ref, gp_ref, v_ref, vp_ref, wg_ref, wv_ref, a_ref):
        keep = (pl.program_id(1) > 0).astype(F32)
        ug = _conv_ext(jnp.concatenate([gp_ref[...] * keep, g_ref[...]], axis=0), wg_ref)[HALO:, :]
        uv = _conv_ext(jnp.concatenate([vp_ref[...] * keep, v_ref[...]], axis=0), wv_ref)[HALO:, :]
        a_ref[...] = (ug * _sigmoid(ug) * uv).astype(a_ref.dtype)

    cur, prev, _ = _time_specs(S, tm, FFN_CW, lambda j: j)
    wspec = BS((3, FFN_CW), lambda j, i: (0, j))
    (a,), side_out = _call_with_side(body, side, name, (D_FF // FFN_CW, S // tm),
                                     [cur, prev, cur, prev, wspec, wspec], [cur], [SDS((S, D_FF), BF16)],
                                     (pre_g, pre_g, pre_v, pre_v, w_g, w_v))
    return a, side_out


def ffn_mid_bwd(pre_g, pre_v, w_g, w_v, d_a, name, side=None):
    S = pre_g.shape[0]
    tm = _tile(S, CONV_TM, 8)
    nt = S // tm

    def body(g_ref, gp_ref, gn_ref, v_ref, vp_ref, vn_ref, wg_ref, wv_ref, da_ref, dan_ref,
             dg_ref, dv_ref, dwg_ref, dwv_ref):
        i = pl.program_id(1)
        keep_prev = (i > 0).astype(F32)
        keep_next = (i < nt - 1).astype(F32)
        eg = jnp.concatenate([gp_ref[...] * keep_prev, g_ref[...], gn_ref[...]], axis=0)
        ev = jnp.concatenate([vp_ref[...] * keep_prev, v_ref[...], vn_ref[...]], axis=0)
        ug = _conv_ext(eg, wg_ref)
        uv = _conv_ext(ev, wv_ref)
        da = jnp.concatenate([jnp.zeros((HALO, FFN_CW), F32), da_ref[...], dan_ref[...] * keep_next], axis=0)
        sg = _sigmoid(ug)
        dug = da * uv * (sg * (1.0 + ug * (1.0 - sg)))
        duv = da * (ug * sg)
        dg_ref[...] = _conv_t_ext(dug, wg_ref)[HALO:HALO + tm, :].astype(dg_ref.dtype)
        dv_ref[...] = _conv_t_ext(duv, wv_ref)[HALO:HALO + tm, :].astype(dv_ref.dtype)

        @pl.when(i == 0)
        def _():
            dwg_ref[...] = jnp.zeros_like(dwg_ref)
            dwv_ref[...] = jnp.zeros_like(dwv_ref)

        for dup_e, e, dw_ref in ((dug, eg, dwg_ref), (duv, ev, dwv_ref)):
            dup = dup_e[HALO:HALO + tm, :]
            dw_ref[0:1, :] += jnp.sum(dup * pltpu.roll(e, 2, 0)[HALO:HALO + tm, :], axis=0, keepdims=True)
            dw_ref[1:2, :] += jnp.sum(dup * pltpu.roll(e, 1, 0)[HALO:HALO + tm, :], axis=0, keepdims=True)
            dw_ref[2:3, :] += jnp.sum(dup * e[HALO:HALO + tm, :], axis=0, keepdims=True)

    cur, prev, nxt = _time_specs(S, tm, FFN_CW, lambda j: j)
    wspec = BS((3, FFN_CW), lambda j, i: (0, j))
    return _call_with_side(
        body, side, name, (D_FF // FFN_CW, nt), [cur, prev, nxt, cur, prev, nxt, wspec, wspec, cur, nxt],
        [cur, cur, wspec, wspec], [SDS((S, D_FF), BF16)] * 2 + [SDS((3, D_FF), F32)] * 2,
        (pre_g, pre_g, pre_g, pre_v, pre_v, pre_v, w_g, w_v, d_a, d_a))


def ple_fwd(x, a, e, name):
    S, Dm = x.shape
    tm = _tile(S, 256, 8)

    def body(x_ref, a_ref, e_ref, o_ref):
        o_ref[...] = x_ref[...] + _sigmoid(a_ref[...]) * e_ref[...]

    row = BS((tm, Dm), lambda i: (i, 0))
    return pl.pallas_call(body, name=name, grid=(S // tm,), in_specs=[row] * 3, out_specs=row,
                          out_shape=SDS((S, Dm), F32), compiler_params=_params(1))(x, a, e)


def ple_bwd(a, e, dx, name):
    S, Dm = a.shape
    tm = _tile(S, 256, 8)

    def body(a_ref, e_ref, dx_ref, da_ref, de_ref):
        s = _sigmoid(a_ref[...])
        d = dx_ref[...]
        da_ref[...] = (d * e_ref[...] * s * (1.0 - s)).astype(da_ref.dtype)
        de_ref[...] = (d * s).astype(de_ref.dtype)

    row = BS((tm, Dm), lambda i: (i, 0))
    return pl.pallas_call(body, name=name, grid=(S // tm,), in_specs=[row] * 3, out_specs=[row, row],
                          out_shape=[SDS((S, Dm), BF16)] * 2, compiler_params=_params(1))(a, e, dx)


def to_heads(x, n_heads, dil):
    S = x.shape[0]
    x = x.reshape(S // dil, dil, n_heads, HEAD_DIM).transpose(2, 1, 0, 3)
    return x.reshape(n_heads, S, HEAD_DIM)


def from_heads(y, dil):
    H, S, C = y.shape
    y = y.reshape(H, dil, S // dil, C).transpose(2, 1, 0, 3)
    return y.reshape(S, H * C)


def _columns(x, lo, hi):
    return lax.optimization_barrier(x[:, lo:hi])


def rows_of(col, t):
    H, S, _ = col.shape
    return col.reshape(H, S // t, 1, t)


IN_SEGMENTS = ((SRC_A, SRC_F, OFF_A), (SRC_F, SRC_B, OFF_F), (SRC_B, SRC_C, OFF_B), (SRC_C, SRC_DQ, OFF_C),
               (SRC_DQ, SRC_G, OFF_D), (SRC_G, D_IN, OFF_G))
IN_SHARD = D_IN // N_CHIPS


def w_in_aligned_from_chips(t):
    pieces = []
    for ref_lo, ref_hi, _ in sorted(IN_SEGMENTS, key=lambda seg: seg[2]):
        for k in range(N_CHIPS):
            lo, hi = max(ref_lo, k * IN_SHARD), min(ref_hi, (k + 1) * IN_SHARD)
            if lo < hi:
                pieces.append(t[k][:, lo - k * IN_SHARD:hi - k * IN_SHARD])
    pieces.append(jnp.zeros((t[0].shape[0], W_AL - D_IN), t[0].dtype))
    return jnp.concatenate(pieces, axis=1)


def w_in_chips_from_aligned(g):
    slots = []
    for k in range(N_CHIPS):
        pieces = []
        for ref_lo, ref_hi, al in IN_SEGMENTS:
            lo, hi = max(ref_lo, k * IN_SHARD), min(ref_hi, (k + 1) * IN_SHARD)
            if lo < hi:
                pieces.append(g[:, al + lo - ref_lo:al + hi - ref_lo])
        slots.append(jnp.concatenate(pieces, axis=1))
    return jnp.stack(slots, axis=0)


def chips_to_full(t, name):
    return jnp.concatenate([t[k] for k in range(N_CHIPS)], axis=0 if name in ROW_SHARDED else 1)


def full_to_chips(g, name):
    if name in ROW_SHARDED:
        return g.reshape(N_CHIPS, g.shape[0] // N_CHIPS, g.shape[1])
    return g.reshape(g.shape[0], N_CHIPS, g.shape[1] // N_CHIPS).transpose(1, 0, 2)


def halves_from_chips(t):
    return jnp.concatenate([t[0], t[1]], axis=1), jnp.concatenate([t[2], t[3]], axis=1)


def chips_from_halves(g, v):
    c = g.shape[1] // 2
    return jnp.stack([g[:, :c], g[:, c:], v[:, :c], v[:, c:]], axis=0)


def layer_fwd(x, p_l, rope, w, tag, sides=None):
    S = x.shape[0]
    sides = sides or {}
    side_out = {}
    sv = {"x0": x}
    h, sv["h_t"] = rmsnorm_fwd(x, w["norm_mix_g"], f"{tag}_norm_mix")
    proj = matmul(h, w["w_in_al"], "nn", F32, f"{tag}_proj", side=sides.get("proj"))
    if "proj" in sides:
        proj, side_out["proj"] = proj
    sv["proj"] = proj

    af_t, f_cum = fox_prep_fwd(proj, w["fox_forget_b"].reshape(FOX_HEADS, 1), f"{tag}_fox_prep")
    T = min(FOX_T, S)
    f_col = f_cum.reshape(FOX_HEADS, S, 1)
    f_row = f_cum.reshape(FOX_HEADS, S // T, 1, T)
    qkv = to_heads(_columns(proj, OFF_A, OFF_B).astype(BF16), 3 * FOX_HEADS, 1)
    qa, ka, va = qkv[:FOX_HEADS], qkv[FOX_HEADS:2 * FOX_HEADS], qkv[2 * FOX_HEADS:]
    (oa_h, lse_a), side_out["fox"] = fox_attn_fwd(qa, ka, va, f_col, f_row, f"{tag}_fox_fwd", sides.get("fox"))
    o_a = from_heads(oa_h, 1)
    sv.update(af_t=af_t, f_col=f_col, f_row=f_row, qa=qa, ka=ka, va=va, oa_h=oa_h, lse_a=lse_a)

    o_b = shortconv_fwd(proj, w["shortconv_w"], f"{tag}_sconv_fwd")

    o_c = sgu_fwd(proj, w["sgu_norm_g"].reshape(1, SGU_WIDTH), w["sgu_w"], _sgu_bias(w["sgu_b"]), f"{tag}_sgu_fwd")

    cos, sa, sb = rope
    qk = rope_apply(proj, OFF_D, 2 * DIL_WIDTH, cos, sa, sb, BF16, f"{tag}_rope_fwd")
    vd = _columns(proj, OFF_D + 2 * DIL_WIDTH, OFF_D + 3 * DIL_WIDTH).astype(BF16)
    outs, lses, dil_sv = [], [], []
    for g, (window, dil) in enumerate(DIL_PATTERNS):
        sl = slice(g * DIL_OUT, (g + 1) * DIL_OUT)
        qg = to_heads(qk[:, sl], 4, dil)
        kg = to_heads(qk[:, DIL_WIDTH:][:, sl], 4, dil)
        vg = to_heads(vd[:, sl], 4, dil)
        nb = (S // dil) // DIL_SPAN
        og, lg = dil_attn_fwd(qg, kg, vg, nb, f"{tag}_dil{g}_fwd")
        dil_sv.append((qg, kg, vg, og, lg, nb))
        outs.append(_heads_unperm(og, dil))
        lses.append(_col_unperm(lg, dil))
    od_h = dil_merge_fwd(outs, lses, f"{tag}_dil_merge_fwd")
    o_d = from_heads(od_h, 1)
    sv.update(dil=dil_sv, outs=outs, lses=lses)

    o_d_pad = jnp.concatenate([o_d.astype(BF16), jnp.zeros((S, FOX_WIDTH - DIL_OUT), BF16)], axis=-1)
    o_stack = jnp.stack([o_a, o_b, o_c, o_d_pad], axis=0)
    merged = merge_fwd(o_stack, w["w_br"], proj, f"{tag}_merge_fwd")
    x1 = matmul(merged, w["w_out"], "nn", F32, f"{tag}_out_proj", res=x)
    sv.update(o_stack=o_stack, merged=merged, x1=x1)

    h2, sv["h2_t"] = rmsnorm_fwd(x1, w["norm_ffn_g"], f"{tag}_norm_ffn")
    pre = (matmul(h2, w["w_up_g"], "nn", F32, f"{tag}_up_g"), matmul(h2, w["w_up_v"], "nn", F32, f"{tag}_up_v"))
    a, side_out["ffn"] = ffn_mid_fwd(pre[0], pre[1], w["ffn_conv_g"], w["ffn_conv_v"], f"{tag}_ffn_mid_fwd",
                                     sides.get("ffn"))
    x2 = matmul(a, w["w_down"], "nn", F32, f"{tag}_down", res=x1)
    sv.update(pre=pre, a=a, x2=x2)

    n3, sv["n3_t"] = rmsnorm_fwd(x2, w["norm_ple_g"], f"{tag}_norm_ple")
    pg = matmul(n3, w["w_ple_gate"], "nn", F32, f"{tag}_ple_gate")
    pe = matmul(p_l, w["w_ple_proj"], "nn", F32, f"{tag}_ple_proj")
    x3 = ple_fwd(x2, pg, pe, f"{tag}_ple_fwd")
    sv.update(pg=pg, pe=pe, p_l=p_l)
    return x3, sv, side_out


def _sgu_bias(b):
    return jnp.pad(b.T, ((0, 0), (0, SGU_CHUNK - b.shape[0])))


def _col_unperm(col, dil):
    H, S, _ = col.shape
    return col.reshape(H, dil, S // dil).transpose(0, 2, 1).reshape(H, S, 1)


def _col_perm(col, dil):
    H, S, _ = col.shape
    return col.reshape(H, S // dil, dil).transpose(0, 2, 1).reshape(H, S, 1)


def _heads_perm(y, dil):
    H, S, C = y.shape
    return y.reshape(H, S // dil, dil, C).transpose(0, 2, 1, 3).reshape(H, S, C)


def _heads_unperm(y, dil):
    H, S, C = y.shape
    return y.reshape(H, dil, S // dil, C).transpose(0, 2, 1, 3).reshape(H, S, C)


def layer_bwd(dx3, sv, rope, w, tag, exch):
    side_out = {}
    S = dx3.shape[0]
    gr = {}
    da, de = ple_bwd(sv["pg"], sv["pe"], dx3, f"{tag}_ple_bwd")
    gr["w_ple_proj"] = matmul(sv["p_l"], de, "tn", F32, f"{tag}_dw_ple_proj")
    gr["w_ple_gate"] = matmul(sv["n3_t"], da, "nn", F32, f"{tag}_dw_ple_gate")
    dn3 = matmul(da, w["w_ple_gate"], "nt", BF16, f"{tag}_dn3")
    dx2, dx2_b, gr["norm_ple_g"] = rmsnorm_bwd(sv["x2"], w["norm_ple_g"], dn3, dx3, f"{tag}_norm_ple_bwd")

    d_a = matmul(dx2_b, w["w_down"], "nt", F32, f"{tag}_da")
    gr["w_down"] = matmul(sv["a"], dx2_b, "tn", F32, f"{tag}_dw_down")
    (dpre_g, dpre_v, dwc_g, dwc_v), swapped = ffn_mid_bwd(sv["pre"][0], sv["pre"][1], w["ffn_conv_g"],
                                                          w["ffn_conv_v"], d_a, f"{tag}_ffn_mid_bwd",
                                                          exch.swap_exchange())
    sides = exch.ici_exchanges(swapped)
    gr["ffn_conv_w"] = (dwc_g, dwc_v)
    gr["w_up"] = (matmul(sv["h2_t"], dpre_g, "nn", F32, f"{tag}_dw_up_g", tm=2048, tn=512),
                  matmul(sv["h2_t"], dpre_v, "nn", F32, f"{tag}_dw_up_v", tm=2048, tn=512))
    dh2_g = matmul(dpre_g, w["w_up_g"], "nt", F32, f"{tag}_dh2_g")
    dh2 = matmul(dpre_v, w["w_up_v"], "nt", BF16, f"{tag}_dh2_v", res=dh2_g)
    dx1, dx1_b, gr["norm_ffn_g"] = rmsnorm_bwd(sv["x1"], w["norm_ffn_g"], dh2, dx2, f"{tag}_norm_ffn_bwd")

    d_merged = matmul(dx1_b, w["w_out"], "nt", BF16, f"{tag}_dmerged")
    gr["w_out"] = matmul(sv["merged"], dx1_b, "tn", F32, f"{tag}_dw_out")
    proj = sv["proj"]
    dgl, dy = merge_bwd(sv["o_stack"], w["w_br"], proj, d_merged, f"{tag}_merge_bwd")
    d_o, d_wbr = [], []
    for b in range(N_BRANCH):
        d_o.append(matmul(dy[b], w["w_br"][b], "nt", F32, f"{tag}_do{b}"))
        d_wbr.append(matmul(sv["o_stack"][b], dy[b], "tn", F32, f"{tag}_dw_br{b}"))
    gr["w_br"] = d_wbr

    do_a = to_heads(d_o[0].astype(BF16), FOX_HEADS, 1)
    T = min(FOX_T, S)
    (dqa, delta_a, d_fq), side_out["dq"] = fox_attn_bwd_dq(
        sv["qa"], sv["ka"], sv["va"], sv["f_col"], sv["f_row"], sv["oa_h"], sv["lse_a"], do_a, f"{tag}_fox_dq",
        sides.get("dq"))
    (dka, dva, d_fk), side_out["dkv"] = fox_attn_bwd_dkv(
        sv["qa"], sv["ka"], sv["va"], sv["f_col"], sv["f_row"], rows_of(sv["lse_a"], T), rows_of(delta_a, T), do_a,
        f"{tag}_fox_dkv", sides.get("dkv"))
    daf_t, dfb = fox_prep_bwd(sv["af_t"], w["fox_forget_b"].reshape(FOX_HEADS, 1), d_fq.reshape(FOX_HEADS, S),
                              d_fk.reshape(FOX_HEADS, S), f"{tag}_fox_prep_bwd")
    gr["fox_forget_b"] = dfb.reshape(FOX_HEADS)
    d_proj_a = from_heads(jnp.concatenate([dqa, dka, dva], axis=0), 1).astype(BF16)

    dxb, dgb, dgc, gr["shortconv_w"] = shortconv_bwd(proj, w["shortconv_w"], d_o[1], f"{tag}_sconv_bwd")

    d_c, dsg, dsw, dsb = sgu_bwd(proj, w["sgu_norm_g"].reshape(1, SGU_WIDTH), w["sgu_w"],
                                 jnp.swapaxes(w["sgu_w"], 1, 2), _sgu_bias(w["sgu_b"]), d_o[2], f"{tag}_sgu_bwd")
    gr["sgu_norm_g"] = dsg.reshape(SGU_WIDTH)
    gr["sgu_w"] = dsw
    gr["sgu_b"] = dsb[:, :SGU_WIDTH // SGU_CHUNK].T

    d_od = to_heads(d_o[3][:, :DIL_OUT], 4, 1)
    d_outs_lses = dil_merge_bwd(sv["outs"], sv["lses"], d_od, f"{tag}_dil_merge_bwd")
    d_outs, d_lses = d_outs_lses[:3], d_outs_lses[3:]
    dq_parts, dk_parts, dv_parts = [], [], []
    for g, (window, dil) in enumerate(DIL_PATTERNS):
        qg, kg, vg, og, lg, nb = sv["dil"][g]
        do_g = _heads_perm(d_outs[g], dil)
        dl_g = _col_perm(d_lses[g], dil)
        dqg, delta_g = dil_attn_bwd_dq(qg, kg, vg, og, lg, do_g, dl_g, nb, f"{tag}_dil{g}_dq")
        dkg, dvg = dil_attn_bwd_dkv(qg, kg, vg, rows_of(lg, DIL_SPAN), rows_of(delta_g, DIL_SPAN), do_g, nb,
                                    f"{tag}_dil{g}_dkv")
        dq_parts.append(from_heads(dqg, dil))
        dk_parts.append(from_heads(dkg, dil))
        dv_parts.append(from_heads(dvg, dil))
    cos, sa, sb = rope
    d_qk_rot = jnp.concatenate(dq_parts + dk_parts, axis=-1)
    d_qk = rope_apply(d_qk_rot, 0, 2 * DIL_WIDTH, cos, -sa, -sb, BF16, f"{tag}_rope_bwd")
    d_vd = jnp.concatenate(dv_parts, axis=-1).astype(BF16)

    d_f_cols = jnp.concatenate([daf_t.T.astype(BF16), jnp.zeros((S, W_AL - OFF_F - FOX_HEADS), BF16)], axis=-1)
    d_proj = jnp.concatenate([dgl, d_proj_a, dxb, dgb, dgc, d_c, d_qk, d_vd, d_f_cols], axis=-1)
    gr["w_in_al"] = matmul(sv["h_t"], d_proj, "nn", F32, f"{tag}_dw_in", tm=2048, tn=512)
    dh = matmul(d_proj, w["w_in_al"], "nt", BF16, f"{tag}_dh", tk=W_AL // 4)
    dx0, _, gr["norm_mix_g"] = rmsnorm_bwd(sv["x0"], w["norm_mix_g"], dh, dx1, f"{tag}_norm_mix_bwd")
    exch.ici_arrived(side_out)
    return dx0, gr


def local_weights(chips, repl, layer):
    w = {n: repl[n][layer] for n in REPLICATED}
    cast = lambda n, dtype: [chips[n][k].astype(dtype) for k in range(N_CHIPS)]
    full = {n: chips_to_full(cast(n, BF16), n)
            for n in ("w_br_fox", "w_br_conv", "w_br_sgu", "w_br_dil", "w_out", "w_down", "w_ple_gate", "w_ple_proj")}
    w["w_in_al"] = w_in_aligned_from_chips(cast("w_in", BF16))
    w["shortconv_w"] = chips_to_full(cast("shortconv_w", F32), "shortconv_w")
    pad = jnp.zeros((FOX_WIDTH - DIL_OUT, D_MODEL), BF16)
    w["w_br"] = jnp.stack([full["w_br_fox"], full["w_br_conv"], full["w_br_sgu"],
                           jnp.concatenate([full["w_br_dil"], pad], axis=0)], axis=0)
    w["w_up_g"], w["w_up_v"] = halves_from_chips(cast("w_up", BF16))
    w["ffn_conv_g"], w["ffn_conv_v"] = halves_from_chips(cast("ffn_conv_w", F32))
    for n in ("w_out", "w_down", "w_ple_gate", "w_ple_proj"):
        w[n] = full[n]
    return w


def grads_to_chips(gr):
    out = {n: gr[n] for n in ("fox_forget_b", "sgu_norm_g", "sgu_w", "sgu_b")}
    out["norm_mix_g"] = gr["norm_mix_g"].reshape(D_MODEL)
    out["norm_ffn_g"] = gr["norm_ffn_g"].reshape(D_MODEL)
    out["norm_ple_g"] = gr["norm_ple_g"].reshape(D_MODEL)
    out["w_in"] = w_in_chips_from_aligned(gr["w_in_al"])
    out["w_up"] = chips_from_halves(*gr["w_up"])
    out["ffn_conv_w"] = chips_from_halves(*gr["ffn_conv_w"])
    for b, n in enumerate(("w_br_fox", "w_br_conv", "w_br_sgu")):
        out[n] = full_to_chips(gr["w_br"][b], n)
    out["w_br_dil"] = full_to_chips(gr["w_br"][3][:DIL_OUT], "w_br_dil")
    for n in ("shortconv_w", "w_out", "w_down", "w_ple_gate", "w_ple_proj"):
        out[n] = full_to_chips(gr[n], n)
    return out


def local_step(x, p, positions, repl, final_norm_g, loss_target, exch):
    depth = p.shape[0]
    rope = rope_tables(positions)
    saved, ws = [], []
    chips = exch.first_weights()
    for layer in range(depth):
        w = local_weights(chips, repl, layer)
        side = exch.weights_exchange(layer + 1) if layer + 1 < depth else None
        x, sv, side_out = layer_fwd(x, p[layer].astype(BF16), rope, w, f"l{layer}", side)
        if layer + 1 < depth:
            chips = exch.weights_arrived(layer + 1, side_out)
        saved.append(sv)
        ws.append(w)
    loss_part, dx, dgf = final_loss(x, final_norm_g, loss_target, "final_loss")
    for layer in range(depth - 1, -1, -1):
        dx, gr = layer_bwd(dx, saved[layer], rope, ws[layer], f"l{layer}", exch)
        exch.grads_ready(layer, grads_to_chips(gr))
    exch.grads_flush()
    return loss_part[0, 0], dx, dgf.reshape(-1)


def _position():
    return lax.axis_index("x"), lax.axis_index("y"), lax.axis_index("c")


def _other_chips(x, y):
    return [(1 - x, y), (x, 1 - y), (1 - x, 1 - y)]


def _remote(src, dst, send_sem, recv_sem, device):
    return pltpu.make_async_remote_copy(src_ref=src, dst_ref=dst, send_sem=send_sem, recv_sem=recv_sem,
                                        device_id=device, device_id_type=MESH)


def _chip_index():
    return 2 * lax.axis_index("x") + lax.axis_index("y")


def _block_rows(rows, cols, unit):
    return _tile(rows, max(unit, (1 << 20) // cols // unit * unit), unit)


def gather_chip_shards(packs, name):
    return _run_exchange(gather_exchange(packs), name)


def gather_exchange(packs):
    n = len(packs)
    halves = [p.shape[0] // 2 for p in packs]

    def half(outs, t, chip, core):
        return outs[t].at[chip, pl.ds(core * halves[t], halves[t]), :]

    def ici_sends(srcs, outs, send_sems, recv_sems):
        x, y, c = _position()
        me = 2 * x + y
        return [_remote(srcs[t].at[pl.ds(c * halves[t], halves[t]), :], half(outs, t, me, c),
                        send_sems.at[6 * t + j], recv_sems.at[6 * t + j], (px, py, c))
                for t in range(n) for j, (px, py) in enumerate(_other_chips(x, y))]

    def own_to_sibling(srcs, outs, send_sems, recv_sems):
        x, y, c = _position()
        return [_remote(srcs[t], outs[t].at[2 * x + y], send_sems.at[6 * n + t], recv_sems.at[6 * n + t],
                        (x, y, 1 - c)) for t in range(n)]

    def start(srcs, outs, send_sems, recv_sems):
        for cp in ici_sends(srcs, outs, send_sems, recv_sems) + own_to_sibling(srcs, outs, send_sems, recv_sems):
            cp.start()

    def finish(srcs, outs, send_sems, recv_sems):
        x, y, c = _position()
        sibling = (x, y, 1 - c)
        chips = _other_chips(x, y)
        passed = []
        for t in range(n):
            for j, (px, py) in enumerate(chips):
                k = 2 * px + py
                s = 6 * t + j
                landed = half(outs, t, k, c)
                _remote(landed, landed, send_sems.at[s], recv_sems.at[s], (px, py, c)).wait_recv()
                fwd = _remote(landed, landed, send_sems.at[s + 3], recv_sems.at[s + 3], sibling)
                fwd.start()
                passed.append(fwd)
        for t in range(n):
            for j, (px, py) in enumerate(chips):
                s = 6 * t + j + 3
                theirs = half(outs, t, 2 * px + py, 1 - c)
                _remote(theirs, theirs, send_sems.at[s], recv_sems.at[s], sibling).wait_recv()
        for cp in own_to_sibling(srcs, outs, send_sems, recv_sems):
            cp.wait()
        for cp in ici_sends(srcs, outs, send_sems, recv_sems) + passed:
            cp.wait_send()

    return SideExchange(list(packs), [SDS((N_CHIPS,) + p.shape, p.dtype) for p in packs], 7 * n, start, finish)


def _run_exchange(side, name):
    n_in, n_out = len(side.operands), len(side.out_shapes)

    def body(*refs):
        srcs, outs, (send_sems, recv_sems) = refs[:n_in], refs[n_in:n_in + n_out], refs[n_in + n_out:]
        side.start(srcs, outs, send_sems, recv_sems)
        side.finish(srcs, outs, send_sems, recv_sems)

    return pl.pallas_call(
        body, name=name, in_specs=[ANY] * n_in, out_specs=[ANY] * n_out, out_shape=side.out_shapes,
        scratch_shapes=[pltpu.SemaphoreType.DMA((side.n_sems,)), pltpu.SemaphoreType.DMA((side.n_sems,))],
    )(*side.operands)


def swap_halves_with_sibling(gs, name):
    return _run_exchange(swap_exchange(gs), name)


def swap_exchange(gs):
    n = len(gs)
    halves = [g.shape[1] // 2 for g in gs]

    def copies(srcs, lands, send_sems, recv_sems):
        x, y, c = _position()
        return [_remote(srcs[t].at[:, pl.ds((1 - c) * halves[t], halves[t]), :], lands[t], send_sems.at[t],
                        recv_sems.at[t], (x, y, 1 - c)) for t in range(n)]

    def start(srcs, lands, send_sems, recv_sems):
        for cp in copies(srcs, lands, send_sems, recv_sems):
            cp.start()

    def finish(srcs, lands, send_sems, recv_sems):
        for cp in copies(srcs, lands, send_sems, recv_sems):
            cp.wait()

    return SideExchange(list(gs), [SDS((g.shape[0], h, g.shape[2]), g.dtype) for g, h in zip(gs, halves)], n,
                        start, finish)


def add_my_half(g, other, out_dtype, name):
    n, R, C = g.shape
    H = R // 2
    tr = _block_rows(H, C, 16) if H % 16 == 0 else H
    nb = H // tr
    core = lax.axis_index("c").astype(jnp.int32).reshape(1)

    def body(c_ref, g_ref, o_ref, out_ref):
        out_ref[...] = (g_ref[...] + o_ref[...]).astype(out_ref.dtype)

    grid_spec = pltpu.PrefetchScalarGridSpec(
        num_scalar_prefetch=1, grid=(n, nb),
        in_specs=[BS((None, tr, C), lambda s, i, c_ref: (s, c_ref[0] * nb + i, 0)),
                  BS((None, tr, C), lambda s, i, c_ref: (s, i, 0))],
        out_specs=BS((None, tr, C), lambda s, i, c_ref: (s, i, 0)))
    return pl.pallas_call(body, name=name, grid_spec=grid_spec, out_shape=SDS((n, H, C), out_dtype),
                          compiler_params=_params(2))(core, g, other)


def exchange_slots_between_chips(parts, name):
    return _run_exchange(slot_exchange(parts), name)


def slot_exchange(parts):
    n = len(parts)

    def sends(srcs, lands, send_sems, recv_sems):
        x, y, c = _position()
        me = 2 * x + y
        return [_remote(srcs[t].at[2 * px + py], lands[t].at[me], send_sems.at[3 * t + j], recv_sems.at[3 * t + j],
                        (px, py, c)) for t in range(n) for j, (px, py) in enumerate(_other_chips(x, y))]

    def start(srcs, lands, send_sems, recv_sems):
        for cp in sends(srcs, lands, send_sems, recv_sems):
            cp.start()

    def finish(srcs, lands, send_sems, recv_sems):
        x, y, c = _position()
        for t in range(n):
            for j, (px, py) in enumerate(_other_chips(x, y)):
                k = 2 * px + py
                _remote(srcs[t].at[k], lands[t].at[k], send_sems.at[3 * t + j], recv_sems.at[3 * t + j],
                        (px, py, c)).wait_recv()
        for cp in sends(srcs, lands, send_sems, recv_sems):
            cp.wait_send()

    return SideExchange(list(parts), [SDS(p.shape, p.dtype) for p in parts], 3 * n, start, finish)


def sum_slots_into_my_half(landed, mine, name):
    n, H, C = landed.shape
    tr = _block_rows(H, C, 16) if H % 16 == 0 else H
    nb = H // tr
    where = jnp.stack([lax.axis_index("c"), _chip_index()]).astype(jnp.int32)

    def body(w_ref, l_ref, m_ref, o_ref):
        me = w_ref[1]
        o_ref[...] = jnp.zeros_like(o_ref)
        for k in range(n):
            @pl.when(me == k)
            def _():
                o_ref[...] += m_ref[k].astype(F32)

            @pl.when(me != k)
            def _():
                o_ref[...] += l_ref[k].astype(F32)

    slots = BS((n, tr, C), lambda i, w_ref: (0, i, 0))
    grid_spec = pltpu.PrefetchScalarGridSpec(
        num_scalar_prefetch=1, grid=(nb,), in_specs=[slots, slots],
        out_specs=BS((tr, C), lambda i, w_ref: (w_ref[0] * nb + i, 0)))
    return pl.pallas_call(body, name=name, grid_spec=grid_spec, out_shape=SDS((2 * H, C), F32),
                          compiler_params=_params(1))(where, landed, mine)


def sum_slots(parts, name):
    n, H, C = parts.shape
    tr = _tile(H, 256, 16)

    def body(p_ref, o_ref):
        acc = p_ref[0].astype(F32)
        for k in range(1, n):
            acc = acc + p_ref[k].astype(F32)
        o_ref[...] = acc

    return pl.pallas_call(
        body, name=name, grid=(H // tr,), in_specs=[BS((n, tr, C), lambda i: (0, i, 0))],
        out_specs=BS((tr, C), lambda i: (i, 0)), out_shape=SDS((H, C), F32), compiler_params=_params(1))(parts)


def join_halves_with_sibling(arrs, name):
    n = len(arrs)
    halves = [a.shape[0] // 2 for a in arrs]

    def body(*refs):
        outs, (send_sems, recv_sems) = refs[n:2 * n], refs[2 * n:]
        x, y, c = _position()

        def half(t, core):
            return outs[t].at[pl.ds(core * halves[t], halves[t]), :]

        sends = [_remote(half(t, c), half(t, c), send_sems.at[t], recv_sems.at[t], (x, y, 1 - c)) for t in range(n)]
        for cp in sends:
            cp.start()
        for t in range(n):
            _remote(half(t, 1 - c), half(t, 1 - c), send_sems.at[t], recv_sems.at[t], (x, y, 1 - c)).wait_recv()
        for cp in sends:
            cp.wait_send()

    return pl.pallas_call(
        body, name=name, in_specs=[ANY] * n, out_specs=[ANY] * n, out_shape=[SDS(a.shape, a.dtype) for a in arrs],
        input_output_aliases={t: t for t in range(n)},
        scratch_shapes=[pltpu.SemaphoreType.DMA((n,)), pltpu.SemaphoreType.DMA((n,))])(*arrs)


def reduce_scatter_pair_sums(gs, others, tag):
    n = len(gs)
    return [add_my_half(g, o, BF16 if t < n - 1 else F32, f"{tag}_pair_sum{t}")
            for t, (g, o) in enumerate(zip(gs, others))]


def reduce_scatter_finish(parts, landed, tag):
    sums = [sum_slots_into_my_half(l, p, f"{tag}_chip_sum{t}") for t, (l, p) in enumerate(zip(landed, parts))]
    return join_halves_with_sibling(sums, f"{tag}_join")


def gather_all_devices(pack, name):
    R, C = pack.shape

    def body(src, out, send_sems, recv_sems, local_sem):
        x, y, c = _position()
        me = 4 * x + 2 * y + c
        local = pltpu.make_async_copy(src, out.at[me], local_sem)
        local.start()
        peers = []
        for m in range(1, N_DEV):
            px = 1 - x if m & 4 else x
            py = 1 - y if m & 2 else y
            pc = 1 - c if m & 1 else c
            peers.append((px, py, pc))
        sends = [_remote(src, out.at[me], send_sems.at[j], recv_sems.at[j], peer) for j, peer in enumerate(peers)]
        for cp in sends:
            cp.start()
        for j, (px, py, pc) in enumerate(peers):
            k = 4 * px + 2 * py + pc
            _remote(src, out.at[k], send_sems.at[j], recv_sems.at[j], (px, py, pc)).wait_recv()
        for cp in sends:
            cp.wait_send()
        local.wait()

    return pl.pallas_call(
        body, name=name, in_specs=[ANY], out_specs=ANY, out_shape=SDS((N_DEV, R, C), pack.dtype),
        scratch_shapes=[pltpu.SemaphoreType.DMA((N_DEV - 1,)), pltpu.SemaphoreType.DMA((N_DEV - 1,)),
                        pltpu.SemaphoreType.DMA(())])(pack)


def _adamw_update(w, g, m, v):
    c1 = 1.0 / (1.0 - ADAM_B1 ** ADAM_STEP)
    c2 = 1.0 / (1.0 - ADAM_B2 ** ADAM_STEP)
    mn = ADAM_B1 * m + (1.0 - ADAM_B1) * g
    vn = ADAM_B2 * v + (1.0 - ADAM_B2) * (g * g)
    return -ADAM_LR * ((mn * c1) / (jnp.sqrt(vn * c2) + ADAM_EPS) + ADAM_WD * w), mn, vn


def adamw_layers(w, gs, m, v, name):
    L, r, c = w.shape
    tr = r if r * c * 4 <= (1 << 20) else _tile(r, max(8, ((1 << 20) // (c * 4)) // 8 * 8), 8)
    nb = r // tr

    def g_spec(l):
        return BS((tr, c), lambda layer, i: (jnp.where(layer == l, i, jnp.where(layer < l, 0, nb - 1)), 0))

    def body(w_ref, m_ref, v_ref, *rest):
        g_refs, (go_ref, d_ref, mo_ref, vo_ref) = rest[:L], rest[L:]
        layer = pl.program_id(0)
        for l in range(L):
            @pl.when(layer == l)
            def _(g_ref=g_refs[l]):
                gv = g_ref[...]
                go_ref[...] = gv
                d_ref[...], mo_ref[...], vo_ref[...] = _adamw_update(w_ref[...], gv, m_ref[...], v_ref[...])

    blk = BS((None, tr, c), lambda layer, i: (layer, i, 0))
    return pl.pallas_call(
        body, name=name, grid=(L, nb), in_specs=[blk] * 3 + [g_spec(l) for l in range(L)], out_specs=[blk] * 4,
        out_shape=[SDS((L, r, c), F32)] * 4, compiler_params=_params(2))(w, m, v, *gs)


def adamw(w, g, m, v, name):
    shape = w.shape
    cols = shape[-1] if len(shape) > 1 else shape[0]
    rows = w.size // cols
    two = lambda t: t.reshape(rows, cols)
    tr = rows
    if rows * cols * 4 > (1 << 21):
        tr = _tile(rows, max(8, ((1 << 21) // (cols * 4)) // 8 * 8), 8)

    def body(w_ref, g_ref, m_ref, v_ref, d_ref, mo_ref, vo_ref):
        d_ref[...], mo_ref[...], vo_ref[...] = _adamw_update(w_ref[...], g_ref[...], m_ref[...], v_ref[...])

    blk = BS((tr, cols), lambda i: (i, 0))
    d, mo, vo = pl.pallas_call(
        body, name=name, grid=(rows // tr,), in_specs=[blk] * 4, out_specs=[blk] * 3,
        out_shape=[SDS((rows, cols), F32)] * 3, compiler_params=_params(1))(two(w), two(g), two(m), two(v))
    return d.reshape(shape), mo.reshape(shape), vo.reshape(shape)


def _rows_for(n, unit):
    rows = -(-n // PACK_COLS)
    return -(-rows // unit) * unit


ROWS_GROUP = ("w_out", "w_ple_gate", "w_down")
COLS_GROUP = ("w_br_fox", "w_br_conv", "w_br_sgu", "w_br_dil", "w_ple_proj")
SMALL_GROUP = ("shortconv_w", "ffn_conv_w")
SMALL_ROWS = 16


def group_shards(t, dtype):
    lead = t["w_in"].shape[:-2]
    small = jnp.concatenate([t[n].astype(F32).reshape(lead + (-1,)) for n in SMALL_GROUP], axis=-1)
    pad = jnp.zeros(lead + (SMALL_ROWS * PACK_COLS - small.shape[-1],), F32)
    small = jnp.concatenate([small, pad], axis=-1).reshape(lead + (SMALL_ROWS, PACK_COLS))
    return [t["w_in"].astype(dtype), t["w_up"].astype(dtype),
            jnp.concatenate([t[n].astype(dtype) for n in ROWS_GROUP], axis=-2),
            jnp.concatenate([t[n].astype(dtype) for n in COLS_GROUP], axis=-2), small]


def ungroup_shards(arrs, shard_shapes):
    w_in_s, w_up_s, rows, cols, small = arrs
    lead = w_in_s.shape[:-2]
    out = {"w_in": w_in_s, "w_up": w_up_s}
    for group, arr in ((ROWS_GROUP, rows), (COLS_GROUP, cols)):
        off = 0
        for n in group:
            r = shard_shapes[n][0]
            out[n] = arr[..., off:off + r, :]
            off += r
    flat = small.reshape(lead + (-1,))
    off = 0
    for n in SMALL_GROUP:
        size = shard_shapes[n][0] * shard_shapes[n][1]
        out[n] = flat[..., off:off + size].reshape(lead + shard_shapes[n])
        off += size
    return out


class ShardExchange:
    def __init__(self, weights, depth, shard_shapes):
        self.shard_shapes = shard_shapes
        self.packs = [group_shards({n: weights[n][layer] for n in SHARDED}, BF16) for layer in range(depth)]
        self.pending = None
        self.parts = None
        self.shard_grads = [None] * depth
        self.repl_grads = [None] * depth

    def _chips(self, layer, gathered):
        per_chip = [ungroup_shards([g[k] for g in gathered], self.shard_shapes) for k in range(N_CHIPS)]
        return {n: [per_chip[k][n] for k in range(N_CHIPS)] for n in SHARDED}

    def first_weights(self):
        return self._chips(0, gather_chip_shards(self.packs[0], "gather_w0"))

    FWD_HOSTS = {"fox": (0, 3, 4), "proj": (1,), "ffn": (2,)}
    BWD_HOSTS = {"dq": (0, 3, 4), "dkv": (1, 2)}

    @staticmethod
    def _split(hosts, arrs, make):
        return {host: make([arrs[t] for t in idx]) for host, idx in hosts.items()}

    @staticmethod
    def _join(hosts, outs):
        arrs = [None] * sum(len(idx) for idx in hosts.values())
        for host, idx in hosts.items():
            for t, arr in zip(idx, outs[host]):
                arrs[t] = arr
        return arrs

    def weights_exchange(self, layer):
        return self._split(self.FWD_HOSTS, self.packs[layer], gather_exchange)

    def weights_arrived(self, layer, outs):
        return self._chips(layer, self._join(self.FWD_HOSTS, outs))

    def grads_ready(self, layer, gr):
        self.repl_grads[layer] = {n: gr[n] for n in REPLICATED}
        self.pending = (layer, group_shards({n: gr[n] for n in SHARDED}, F32))

    def swap_exchange(self):
        return swap_exchange(self.pending[1]) if self.pending is not None else None

    def ici_exchanges(self, swapped):
        if self.pending is None:
            return {}
        layer, slots = self.pending
        self.parts = reduce_scatter_pair_sums(slots, swapped, f"rs{layer}")
        return self._split(self.BWD_HOSTS, self.parts, slot_exchange)

    def ici_arrived(self, outs):
        if self.pending is not None:
            self._finish(self._join(self.BWD_HOSTS, outs))

    def grads_flush(self):
        layer, slots = self.pending
        self.parts = reduce_scatter_pair_sums(slots, swap_halves_with_sibling(slots, f"rs{layer}_swap"), f"rs{layer}")
        self._finish(exchange_slots_between_chips(self.parts, f"rs{layer}_ici"))

    def _finish(self, landed):
        layer = self.pending[0]
        self.shard_grads[layer] = ungroup_shards(reduce_scatter_finish(self.parts, landed, f"rs{layer}"),
                                                 self.shard_shapes)
        self.pending = None


REPL_SHAPES = {"norm_mix_g": (D_MODEL,), "fox_forget_b": (FOX_HEADS,), "sgu_norm_g": (SGU_WIDTH,),
               "sgu_w": (4, SGU_CHUNK, SGU_CHUNK), "sgu_b": (4, SGU_CHUNK), "norm_ffn_g": (D_MODEL,),
               "norm_ple_g": (D_MODEL,)}


def kernel(x, p, positions, norm_mix_g, w_in, fox_forget_b, shortconv_w, sgu_norm_g, sgu_w, sgu_b, w_br_fox, w_br_conv, w_br_sgu, w_br_dil, w_out, norm_ffn_g, w_up, ffn_conv_w, w_down, norm_ple_g, w_ple_gate, w_ple_proj, final_norm_g, loss_target, m_norm_mix_g, m_w_in, m_fox_forget_b, m_shortconv_w, m_sgu_norm_g, m_sgu_w, m_sgu_b, m_w_br_fox, m_w_br_conv, m_w_br_sgu, m_w_br_dil, m_w_out, m_norm_ffn_g, m_w_up, m_ffn_conv_w, m_w_down, m_norm_ple_g, m_w_ple_gate, m_w_ple_proj, m_final_norm_g, v_norm_mix_g, v_w_in, v_fox_forget_b, v_shortconv_w, v_sgu_norm_g, v_sgu_w, v_sgu_b, v_w_br_fox, v_w_br_conv, v_w_br_sgu, v_w_br_dil, v_w_out, v_norm_ffn_g, v_w_up, v_ffn_conv_w, v_w_down, v_norm_ple_g, v_w_ple_gate, v_w_ple_proj, v_final_norm_g):
    weights = dict(norm_mix_g=norm_mix_g, w_in=w_in, fox_forget_b=fox_forget_b, shortconv_w=shortconv_w,
                   sgu_norm_g=sgu_norm_g, sgu_w=sgu_w, sgu_b=sgu_b, w_br_fox=w_br_fox, w_br_conv=w_br_conv,
                   w_br_sgu=w_br_sgu, w_br_dil=w_br_dil, w_out=w_out, norm_ffn_g=norm_ffn_g, w_up=w_up,
                   ffn_conv_w=ffn_conv_w, w_down=w_down, norm_ple_g=norm_ple_g, w_ple_gate=w_ple_gate,
                   w_ple_proj=w_ple_proj, final_norm_g=final_norm_g)
    mom1 = dict(norm_mix_g=m_norm_mix_g, w_in=m_w_in, fox_forget_b=m_fox_forget_b, shortconv_w=m_shortconv_w,
                sgu_norm_g=m_sgu_norm_g, sgu_w=m_sgu_w, sgu_b=m_sgu_b, w_br_fox=m_w_br_fox, w_br_conv=m_w_br_conv,
                w_br_sgu=m_w_br_sgu, w_br_dil=m_w_br_dil, w_out=m_w_out, norm_ffn_g=m_norm_ffn_g, w_up=m_w_up,
                ffn_conv_w=m_ffn_conv_w, w_down=m_w_down, norm_ple_g=m_norm_ple_g, w_ple_gate=m_w_ple_gate,
                w_ple_proj=m_w_ple_proj, final_norm_g=m_final_norm_g)
    mom2 = dict(norm_mix_g=v_norm_mix_g, w_in=v_w_in, fox_forget_b=v_fox_forget_b, shortconv_w=v_shortconv_w,
                sgu_norm_g=v_sgu_norm_g, sgu_w=v_sgu_w, sgu_b=v_sgu_b, w_br_fox=v_w_br_fox, w_br_conv=v_w_br_conv,
                w_br_sgu=v_w_br_sgu, w_br_dil=v_w_br_dil, w_out=v_w_out, norm_ffn_g=v_norm_ffn_g, w_up=v_w_up,
                ffn_conv_w=v_ffn_conv_w, w_down=v_w_down, norm_ple_g=v_norm_ple_g, w_ple_gate=v_w_ple_gate,
                w_ple_proj=v_w_ple_proj, final_norm_g=v_final_norm_g)
    depth = w_in.shape[0]
    shard_shapes = {n: tuple(weights[n].shape[1:]) for n in SHARDED}

    exch = ShardExchange(weights, depth, shard_shapes)
    repl = {n: weights[n] for n in REPLICATED}
    loss_part, grad_x, d_final = local_step(x[0], p[:, 0], positions[0], repl, final_norm_g, loss_target[0], exch)
    loss = lax.psum(loss_part, ("x", "y", "c"))
    grads = exch.repl_grads
    grad_w, deltas, new_m, new_v = {}, {}, {}, {}
    for n in SHARDED:
        grad_w[n], deltas[n], new_m[n], new_v[n] = adamw_layers(
            weights[n], [exch.shard_grads[layer][n] for layer in range(depth)], mom1[n], mom2[n], f"adamw_{n}")

    flat = jnp.concatenate([grads[layer][n].astype(F32).reshape(-1) for layer in range(depth) for n in REPLICATED]
                           + [d_final])
    Rr = _rows_for(flat.shape[0], 16)
    packed = jnp.concatenate([flat, jnp.zeros((Rr * PACK_COLS - flat.shape[0],), F32)]).reshape(Rr, PACK_COLS)
    total = sum_slots(gather_all_devices(packed, "gather_repl"), "sum_repl").reshape(-1)
    off = 0
    g_rep = {n: [] for n in REPLICATED}
    for layer in range(depth):
        for n in REPLICATED:
            size = 1
            for s in REPL_SHAPES[n]:
                size *= s
            g_rep[n].append(total[off:off + size].reshape(REPL_SHAPES[n]))
            off += size
    for n in REPLICATED:
        grad_w[n] = jnp.stack(g_rep[n], axis=0)
    grad_w["final_norm_g"] = total[off:off + D_MODEL]

    for n in REPLICATED + ("final_norm_g",):
        deltas[n], new_m[n], new_v[n] = adamw(weights[n], grad_w[n], mom1[n], mom2[n], f"adamw_{n}")
    return (loss, grad_x[None], *[grad_w[n] for n in WEIGHTS], *[deltas[n] for n in WEIGHTS],
            *[new_m[n] for n in WEIGHTS], *[new_v[n] for n in WEIGHTS])
```

```python
import functools

import jax
import jax.numpy as jnp
from jax import lax
from jax.experimental import pallas as pl
from jax.experimental.pallas import tpu as pltpu

F32 = jnp.float32
BF16 = jnp.bfloat16
MESH = pl.DeviceIdType.MESH
BS = pl.BlockSpec
SDS = jax.ShapeDtypeStruct
ANY = pl.BlockSpec(memory_space=pl.ANY)

VMEM_LIMIT_BYTES = 52 * 1024 * 1024
LANES = 128

D_MODEL = 2048
HEAD_DIM = 64
EPS = 1e-6
NEG = -1e30
FOX_HEADS = 8
FOX_WIDTH = 512
CONV_WIDTH = 512
SGU_WIDTH = 512
SGU_CHUNK = 128
DIL_PATTERNS = ((128, 1), (512, 4), (2048, 16))
DIL_SPAN = 128
DIL_HEADS = 12
DIL_WIDTH = 768
DIL_OUT = 256
ROPE_THETA = 500000.0
ROPE_DIM = 16
N_BRANCH = 4
D_FF = 5632
PLE_DIM = 256
D_IN = 14600

OFF_G, OFF_A, OFF_B, OFF_C, OFF_D, OFF_F, W_AL = 0, 8192, 9728, 11264, 12288, 14592, 14848
SRC_A, SRC_F, SRC_B, SRC_C, SRC_DQ, SRC_G = 0, 1536, 1544, 3080, 4104, 6408

ADAM_LR, ADAM_B1, ADAM_B2, ADAM_EPS, ADAM_WD, ADAM_STEP = 0.001, 0.9, 0.999, 1e-08, 0.01, 10

PACK_COLS = 1024
N_CHIPS = 4
N_DEV = 8

SHARDED = ("w_in", "shortconv_w", "w_br_fox", "w_br_conv", "w_br_sgu", "w_br_dil", "w_out", "w_up",
           "ffn_conv_w", "w_down", "w_ple_gate", "w_ple_proj")
ROW_SHARDED = ("w_out", "w_down", "w_ple_gate")
REPLICATED = ("norm_mix_g", "fox_forget_b", "sgu_norm_g", "sgu_w", "sgu_b", "norm_ffn_g", "norm_ple_g")
WEIGHTS = ("norm_mix_g", "w_in", "fox_forget_b", "shortconv_w", "sgu_norm_g", "sgu_w", "sgu_b", "w_br_fox",
           "w_br_conv", "w_br_sgu", "w_br_dil", "w_out", "norm_ffn_g", "w_up", "ffn_conv_w", "w_down",
           "norm_ple_g", "w_ple_gate", "w_ple_proj", "final_norm_g")


def _params(n_grid):
    return pltpu.CompilerParams(dimension_semantics=("arbitrary",) * n_grid, vmem_limit_bytes=VMEM_LIMIT_BYTES)


def _tile(n, pref, unit=LANES):
    best = None
    t = unit
    while t <= min(n, pref):
        if n % t == 0:
            best = t
        t += unit
    return best if best is not None else n


def _sigmoid(z):
    return 0.5 * jnp.tanh(0.5 * z) + 0.5


MAX_RESIDENT_K = 2048


def matmul(a, b, mode, out_dtype, name, res=None, tm=1536, tn=1024, tk=1536, side=None):
    if mode == "tn":
        a, mode = a.astype(BF16).T, "nn"
    if mode == "nn":
        (M, K), (K2, N) = a.shape, b.shape
    else:
        (M, K), (N, K2) = a.shape, b.shape
    assert K == K2, (name, a.shape, b.shape)
    if K <= MAX_RESIDENT_K:
        tk = K
    tm, tn, tk = _tile(M, tm), _tile(N, tn), _tile(K, tk)
    nk = K // tk
    if mode == "nn":
        a_spec, b_spec = BS((tm, tk), lambda i, j, k: (i, k)), BS((tk, tn), lambda i, j, k: (k, j))
        dims = (((1,), (0,)), ((), ()))
    else:
        a_spec, b_spec = BS((tm, tk), lambda i, j, k: (i, k)), BS((tn, tk), lambda i, j, k: (j, k))
        dims = (((1,), (1,)), ((), ()))
    has_res = res is not None

    def body(*refs):
        if has_res:
            a_ref, b_ref, r_ref, o_ref, acc = refs
        else:
            a_ref, b_ref, o_ref, acc = refs
        k = pl.program_id(2)

        @pl.when(k == 0)
        def _():
            acc[...] = jnp.zeros_like(acc)

        acc[...] += lax.dot_general(a_ref[...].astype(BF16), b_ref[...].astype(BF16), dims,
                                    preferred_element_type=F32)

        @pl.when(k == nk - 1)
        def _():
            r = acc[...]
            if has_res:
                r = r + r_ref[...]
            o_ref[...] = r.astype(o_ref.dtype)

    in_specs = [a_spec, b_spec]
    args = [a, b]
    if has_res:
        in_specs.append(BS((tm, tn), lambda i, j, k: (i, j)))
        args.append(res)
    (out,), side_out = _call_with_side(
        body, side, name, (M // tm, N // tn, nk), in_specs, [BS((tm, tn), lambda i, j, k: (i, j))],
        [SDS((M, N), out_dtype)], args, scratch=[pltpu.VMEM((tm, tn), F32)])
    return out if side is None else (out, side_out)


def rmsnorm_fwd(x, g, name):
    S, Dm = x.shape
    tm = _tile(S, 256, LANES)

    def body(x_ref, g_ref, y_ref, yt_ref):
        xf = x_ref[...]
        r = lax.rsqrt(jnp.mean(xf * xf, axis=-1, keepdims=True) + EPS)
        y = (xf * r) * g_ref[...]
        y_ref[...] = y.astype(y_ref.dtype)
        yt_ref[...] = y.T.astype(yt_ref.dtype)

    return pl.pallas_call(
        body, name=name, grid=(S // tm,),
        in_specs=[BS((tm, Dm), lambda i: (i, 0)), BS((1, Dm), lambda i: (0, 0))],
        out_specs=[BS((tm, Dm), lambda i: (i, 0)), BS((Dm, tm), lambda i: (0, i))],
        out_shape=[SDS((S, Dm), BF16), SDS((Dm, S), BF16)], compiler_params=_params(1))(x, g.reshape(1, Dm))


def rmsnorm_bwd(x, g, dy, dres, name):
    S, Dm = x.shape
    tm = _tile(S, 256, 8)

    def body(x_ref, g_ref, dy_ref, dres_ref, dx_ref, dxb_ref, dg_ref):
        xf = x_ref[...]
        r = lax.rsqrt(jnp.mean(xf * xf, axis=-1, keepdims=True) + EPS)
        xh = xf * r
        dy = dy_ref[...].astype(F32)
        dxh = dy * g_ref[...]
        dx = r * (dxh - xh * jnp.mean(dxh * xh, axis=-1, keepdims=True)) + dres_ref[...]
        dx_ref[...] = dx
        dxb_ref[...] = dx.astype(dxb_ref.dtype)

        @pl.when(pl.program_id(0) == 0)
        def _():
            dg_ref[...] = jnp.zeros_like(dg_ref)

        dg_ref[...] += jnp.sum(dy * xh, axis=0, keepdims=True)

    row = BS((tm, Dm), lambda i: (i, 0))
    vec = BS((1, Dm), lambda i: (0, 0))
    return pl.pallas_call(
        body, name=name, grid=(S // tm,), in_specs=[row, vec, row, row], out_specs=[row, row, vec],
        out_shape=[SDS((S, Dm), F32), SDS((S, Dm), BF16), SDS((1, Dm), F32)],
        compiler_params=_params(1))(x, g.reshape(1, Dm), dy, dres)


def final_loss(x, g, target, name):
    S, Dm = x.shape
    tm = _tile(S, 256, 8)

    def body(x_ref, g_ref, t_ref, loss_ref, dx_ref, dg_ref):
        xf = x_ref[...]
        r = lax.rsqrt(jnp.mean(xf * xf, axis=-1, keepdims=True) + EPS)
        xh = xf * r
        gv = g_ref[...]
        err = xh * gv - t_ref[...]
        dy = err * (1.0 / Dm)
        dxh = dy * gv
        dx_ref[...] = r * (dxh - xh * jnp.mean(dxh * xh, axis=-1, keepdims=True))

        @pl.when(pl.program_id(0) == 0)
        def _():
            dg_ref[...] = jnp.zeros_like(dg_ref)
            loss_ref[...] = jnp.zeros_like(loss_ref)

        dg_ref[...] += jnp.sum(dy * xh, axis=0, keepdims=True)
        part = 0.5 * jnp.sum(jnp.mean(err * err, axis=-1, keepdims=True), axis=0, keepdims=True)
        loss_ref[...] += jnp.broadcast_to(part, loss_ref.shape)

    row = BS((tm, Dm), lambda i: (i, 0))
    vec = BS((1, Dm), lambda i: (0, 0))
    return pl.pallas_call(
        body, name=name, grid=(S // tm,), in_specs=[row, vec, row],
        out_specs=[BS((1, LANES), lambda i: (0, 0)), row, vec],
        out_shape=[SDS((1, LANES), F32), SDS((S, Dm), F32), SDS((1, Dm), F32)],
        compiler_params=_params(1))(x, g.reshape(1, Dm), target)


def _dot_f32(a, b):
    return jnp.dot(a, b, preferred_element_type=F32, precision=lax.Precision.HIGHEST)


def _dot(a, b):
    return jnp.dot(a, b, preferred_element_type=F32)


def _dot_nt(a, b):
    return lax.dot_general(a, b, (((1,), (1,)), ((), ())), preferred_element_type=F32)


def fox_prep_fwd(proj, bias, name):
    S = proj.shape[0]
    H = FOX_HEADS
    nc = S // LANES

    def body(p_ref, b_ref, a_ref, f_ref):
        row = lax.broadcasted_iota(jnp.int32, (LANES, LANES), 0)
        col = lax.broadcasted_iota(jnp.int32, (LANES, LANES), 1)
        upper = (row <= col).astype(F32)
        carry = jnp.zeros((H, 1), F32)
        for c in range(nc):
            sl = slice(c * LANES, (c + 1) * LANES)
            logits = p_ref[sl, :].T[0:H, :]
            a_ref[:, sl] = logits
            z = logits + b_ref[...]
            chunk = jnp.minimum(z, 0.0) - jnp.log(1.0 + jnp.exp(-jnp.abs(z)))
            f_ref[:, sl] = _dot_f32(chunk, upper) + carry
            carry = carry + jnp.sum(chunk, axis=1, keepdims=True)

    full = BS((H, S), lambda i: (0, 0))
    return pl.pallas_call(
        body, name=name, grid=(1,),
        in_specs=[BS((S, LANES), lambda i: (0, OFF_F // LANES)), BS((H, 1), lambda i: (0, 0))],
        out_specs=[full, full], out_shape=[SDS((H, S), F32)] * 2, compiler_params=_params(1))(proj, bias)


def fox_prep_bwd(af_t, bias, d_fq, d_fk, name):
    H, S = af_t.shape
    nc = S // LANES

    def body(a_ref, b_ref, dfq_ref, dfk_ref, da_ref, db_ref):
        row = lax.broadcasted_iota(jnp.int32, (LANES, LANES), 0)
        col = lax.broadcasted_iota(jnp.int32, (LANES, LANES), 1)
        lower = (row >= col).astype(F32)
        carry = jnp.zeros((H, 1), F32)
        dbias = jnp.zeros((H, 1), F32)
        for c in range(nc - 1, -1, -1):
            sl = slice(c * LANES, (c + 1) * LANES)
            chunk = dfq_ref[:, sl] + dfk_ref[:, sl]
            dlogf = _dot_f32(chunk, lower) + carry
            carry = carry + jnp.sum(chunk, axis=1, keepdims=True)
            z = a_ref[:, sl] + b_ref[...]
            da = dlogf / (1.0 + jnp.exp(z))
            da_ref[:, sl] = da
            dbias = dbias + jnp.sum(da, axis=1, keepdims=True)
        db_ref[...] = dbias

    full = BS((H, S), lambda i: (0, 0))
    vec = BS((H, 1), lambda i: (0, 0))
    return pl.pallas_call(
        body, name=name, grid=(1,), in_specs=[full, vec, full, full], out_specs=[full, vec],
        out_shape=[SDS((H, S), F32), SDS((H, 1), F32)], compiler_params=_params(1))(af_t, bias, d_fq, d_fk)


FOX_T = 256
FOX_HP = 4


def _causal_tile(T):
    return lax.broadcasted_iota(jnp.int32, (T, T), 1) <= lax.broadcasted_iota(jnp.int32, (T, T), 0)


class SideExchange:
    def __init__(self, operands, out_shapes, n_sems, start, finish):
        self.operands, self.out_shapes, self.n_sems, self.start, self.finish = operands, out_shapes, n_sems, start, finish


def _call_with_side(body, side, name, grid, in_specs, out_specs, out_shape, args, scratch=()):
    scratch = list(scratch)
    if side is None:
        return pl.pallas_call(body, name=name, grid=grid, in_specs=in_specs, out_specs=out_specs,
                              out_shape=out_shape, scratch_shapes=scratch,
                              compiler_params=_params(len(grid)))(*args), []
    n_in, n_out = len(in_specs), len(out_specs)
    s_in, s_out = len(side.operands), len(side.out_shapes)

    def wrapped(*refs):
        main_in, side_in = refs[:n_in], refs[n_in:n_in + s_in]
        main_out = refs[n_in + s_in:n_in + s_in + n_out]
        side_out = refs[n_in + s_in + n_out:n_in + s_in + n_out + s_out]
        main_scratch = refs[n_in + s_in + n_out + s_out:-2]
        send_sems, recv_sems = refs[-2:]
        first = pl.program_id(0) == 0
        last = pl.program_id(0) == grid[0] - 1
        for axis in range(1, len(grid)):
            first = jnp.logical_and(first, pl.program_id(axis) == 0)
            last = jnp.logical_and(last, pl.program_id(axis) == grid[axis] - 1)

        @pl.when(first)
        def _():
            side.start(side_in, side_out, send_sems, recv_sems)

        body(*main_in, *main_out, *main_scratch)

        @pl.when(last)
        def _():
            side.finish(side_in, side_out, send_sems, recv_sems)

    outs = pl.pallas_call(
        wrapped, name=name, grid=grid, in_specs=list(in_specs) + [ANY] * s_in,
        out_specs=list(out_specs) + [ANY] * s_out, out_shape=list(out_shape) + list(side.out_shapes),
        scratch_shapes=scratch + [pltpu.SemaphoreType.DMA((side.n_sems,)), pltpu.SemaphoreType.DMA((side.n_sems,))],
        compiler_params=_params(len(grid)))(*args, *side.operands)
    return outs[:n_out], outs[n_out:]


def _fox_specs(H, S, Dh, T):
    nq = S // T
    blk = BS((FOX_HP, T, Dh), lambda h, i: (h, i, 0))
    full = BS((FOX_HP, S, Dh), lambda h, i: (h, 0, 0))
    colb = BS((FOX_HP, T, 1), lambda h, i: (h, i, 0))
    rowf = BS((FOX_HP, nq, 1, T), lambda h, i: (h, 0, 0, 0))
    return blk, full, colb, rowf, (H // FOX_HP, nq)


def fox_attn_fwd(q, k, v, f_col, f_row, name, side=None):
    H, S, Dh = q.shape
    T = min(FOX_T, S)
    scale = Dh ** -0.5

    def body(q_ref, k_ref, v_ref, fc_ref, fr_ref, o_ref, lse_ref):
        qi = pl.program_id(1)
        qs = [q_ref[h] for h in range(FOX_HP)]
        fqs = [fc_ref[h] for h in range(FOX_HP)]

        def step(j, carry, diagonal):
            off = pl.multiple_of(j * T, T)
            out = []
            for h in range(FOX_HP):
                m, l, acc = carry[h]
                kv = k_ref[h, pl.ds(off, T), :]
                vv = v_ref[h, pl.ds(off, T), :]
                s = _dot_nt(qs[h], kv) * scale + (fqs[h] - fr_ref[h, j])
                if diagonal:
                    s = jnp.where(_causal_tile(T), s, NEG)
                m_new = jnp.maximum(m, jnp.max(s, axis=-1, keepdims=True))
                p = jnp.exp(s - m_new)
                alpha = jnp.exp(m - m_new)
                l = alpha * l + jnp.sum(p, axis=-1, keepdims=True)
                acc = alpha * acc + _dot(p.astype(BF16), vv)
                out.append((m_new, l, acc))
            return tuple(out)

        init = tuple((jnp.full((T, 1), NEG, F32), jnp.zeros((T, 1), F32), jnp.zeros((T, Dh), F32))
                     for _ in range(FOX_HP))
        carry = lax.fori_loop(0, qi, functools.partial(step, diagonal=False), init)
        carry = step(qi, carry, True)
        for h in range(FOX_HP):
            m, l, acc = carry[h]
            o_ref[h] = (acc / l).astype(o_ref.dtype)
            lse_ref[h] = m + jnp.log(l)

    blk, full, colb, rowf, grid = _fox_specs(H, S, Dh, T)
    return _call_with_side(body, side, name, grid, [blk, full, full, colb, rowf], [blk, colb],
                           [SDS((H, S, Dh), BF16), SDS((H, S, 1), F32)], (q, k, v, f_col, f_row))


def fox_attn_bwd_dq(q, k, v, f_col, f_row, o, lse, do, name, side=None):
    H, S, Dh = q.shape
    T = min(FOX_T, S)
    scale = Dh ** -0.5

    def body(q_ref, k_ref, v_ref, fc_ref, fr_ref, o_ref, lse_ref, do_ref, dq_ref, dl_ref, df_ref):
        qi = pl.program_id(1)
        qs = [q_ref[h] for h in range(FOX_HP)]
        fqs = [fc_ref[h] for h in range(FOX_HP)]
        lses = [lse_ref[h] for h in range(FOX_HP)]
        dos = [do_ref[h] for h in range(FOX_HP)]
        deltas = [jnp.sum(dos[h].astype(F32) * o_ref[h].astype(F32), axis=-1, keepdims=True) for h in range(FOX_HP)]
        for h in range(FOX_HP):
            dl_ref[h] = deltas[h]

        def step(j, carry, diagonal):
            off = pl.multiple_of(j * T, T)
            out = []
            for h in range(FOX_HP):
                dq, dfq = carry[h]
                kv = k_ref[h, pl.ds(off, T), :]
                vv = v_ref[h, pl.ds(off, T), :]
                s = _dot_nt(qs[h], kv) * scale + (fqs[h] - fr_ref[h, j])
                if diagonal:
                    s = jnp.where(_causal_tile(T), s, NEG)
                p = jnp.exp(s - lses[h])
                ds = p * (_dot_nt(dos[h], vv) - deltas[h])
                out.append((dq + _dot(ds.astype(BF16), kv), dfq + jnp.sum(ds, axis=-1, keepdims=True)))
            return tuple(out)

        init = tuple((jnp.zeros((T, Dh), F32), jnp.zeros((T, 1), F32)) for _ in range(FOX_HP))
        carry = lax.fori_loop(0, qi, functools.partial(step, diagonal=False), init)
        carry = step(qi, carry, True)
        for h in range(FOX_HP):
            dq_ref[h] = carry[h][0] * scale
            df_ref[h] = carry[h][1]

    blk, full, colb, rowf, grid = _fox_specs(H, S, Dh, T)
    return _call_with_side(body, side, name, grid, [blk, full, full, colb, rowf, blk, colb, blk], [blk, colb, colb],
                           [SDS((H, S, Dh), F32), SDS((H, S, 1), F32), SDS((H, S, 1), F32)],
                           (q, k, v, f_col, f_row, o, lse, do))


def fox_attn_bwd_dkv(q, k, v, f_col, f_row, lse_row, delta_row, do, name, side=None):
    H, S, Dh = q.shape
    T = min(FOX_T, S)
    nq = S // T
    scale = Dh ** -0.5

    def body(q_ref, k_ref, v_ref, fc_ref, fr_ref, lse_ref, dl_ref, do_ref, dk_ref, dv_ref, df_ref):
        kj = pl.program_id(1)
        ks = [k_ref[h] for h in range(FOX_HP)]
        vs = [v_ref[h] for h in range(FOX_HP)]
        fks = [fc_ref[h] for h in range(FOX_HP)]

        def step(i, carry, diagonal):
            off = pl.multiple_of(i * T, T)
            out = []
            for h in range(FOX_HP):
                dk, dv, dfk = carry[h]
                qv = q_ref[h, pl.ds(off, T), :]
                dov = do_ref[h, pl.ds(off, T), :]
                st = _dot_nt(ks[h], qv) * scale + (fr_ref[h, i] - fks[h])
                if diagonal:
                    st = jnp.where(lax.broadcasted_iota(jnp.int32, (T, T), 0)
                                   <= lax.broadcasted_iota(jnp.int32, (T, T), 1), st, NEG)
                pt = jnp.exp(st - lse_ref[h, i])
                dv = dv + _dot(pt.astype(BF16), dov)
                dst = pt * (_dot_nt(vs[h], dov) - dl_ref[h, i])
                dk = dk + _dot(dst.astype(BF16), qv)
                out.append((dk, dv, dfk + jnp.sum(dst, axis=-1, keepdims=True)))
            return tuple(out)

        init = tuple((jnp.zeros((T, Dh), F32), jnp.zeros((T, Dh), F32), jnp.zeros((T, 1), F32))
                     for _ in range(FOX_HP))
        carry = step(kj, init, True)
        carry = lax.fori_loop(kj + 1, nq, functools.partial(step, diagonal=False), carry)
        for h in range(FOX_HP):
            dk_ref[h] = carry[h][0] * scale
            dv_ref[h] = carry[h][1]
            df_ref[h] = -carry[h][2]

    blk, full, colb, rowf, grid = _fox_specs(H, S, Dh, T)
    return _call_with_side(body, side, name, grid, [full, blk, blk, colb, rowf, rowf, rowf, full], [blk, blk, colb],
                           [SDS((H, S, Dh), F32), SDS((H, S, Dh), F32), SDS((H, S, 1), F32)],
                           (q, k, v, f_col, f_row, lse_row, delta_row, do))


HALO = 8
CONV_TM = 512


def _conv_ext(e, w_ref):
    return w_ref[0:1, :] * pltpu.roll(e, 2, 0) + w_ref[1:2, :] * pltpu.roll(e, 1, 0) + w_ref[2:3, :] * e


def _conv_t_ext(d, w_ref):
    n = d.shape[0]
    return w_ref[2:3, :] * d + w_ref[1:2, :] * pltpu.roll(d, n - 1, 0) + w_ref[0:1, :] * pltpu.roll(d, n - 2, 0)


def _time_specs(S, tm, width, col_block):
    per = tm // HALO
    last = S // HALO - 1
    cur = BS((tm, width), lambda j, i: (i, col_block(j)))
    prev = BS((HALO, width), lambda j, i: (jnp.maximum(i * per - 1, 0), col_block(j)))
    nxt = BS((HALO, width), lambda j, i: (jnp.minimum((i + 1) * per, last), col_block(j)))
    return cur, prev, nxt


def shortconv_fwd(proj, w, name):
    S = proj.shape[0]
    tm = _tile(S, CONV_TM, 8)
    nb = CONV_WIDTH // LANES
    b0 = OFF_B // LANES

    def body(xb_ref, xbp_ref, gb_ref, gc_ref, gcp_ref, w_ref, o_ref):
        keep = (pl.program_id(1) > 0).astype(F32)
        e = jnp.concatenate([gcp_ref[...] * xbp_ref[...] * keep, gc_ref[...] * xb_ref[...]], axis=0)
        o_ref[...] = (gb_ref[...] * _conv_ext(e, w_ref)[HALO:, :]).astype(o_ref.dtype)

    xb, xbp, _ = _time_specs(S, tm, LANES, lambda j: b0 + j)
    gb, _, _ = _time_specs(S, tm, LANES, lambda j: b0 + nb + j)
    gc, gcp, _ = _time_specs(S, tm, LANES, lambda j: b0 + 2 * nb + j)
    return pl.pallas_call(
        body, name=name, grid=(nb, S // tm),
        in_specs=[xb, xbp, gb, gc, gcp, BS((3, LANES), lambda j, i: (0, j))],
        out_specs=BS((tm, LANES), lambda j, i: (i, j)), out_shape=SDS((S, CONV_WIDTH), BF16),
        compiler_params=_params(2))(proj, proj, proj, proj, proj, w)


def shortconv_bwd(proj, w, do_b, name):
    S = proj.shape[0]
    tm = _tile(S, CONV_TM, 8)
    nt = S // tm
    nb = CONV_WIDTH // LANES
    b0 = OFF_B // LANES

    def body(xb_ref, xbp_ref, gb_ref, gbn_ref, gc_ref, gcp_ref, do_ref, don_ref, w_ref,
             dxb_ref, dgb_ref, dgc_ref, dw_ref):
        i = pl.program_id(1)
        keep_prev = (i > 0).astype(F32)
        keep_next = (i < nt - 1).astype(F32)
        xb, gb, gc = xb_ref[...], gb_ref[...], gc_ref[...]
        do = do_ref[...].astype(F32)
        u = gc * xb
        e = jnp.concatenate([gcp_ref[...] * xbp_ref[...] * keep_prev, u], axis=0)
        dgb_ref[...] = (do * _conv_ext(e, w_ref)[HALO:, :]).astype(dgb_ref.dtype)
        dcv = do * gb
        d_ext = jnp.concatenate([dcv, don_ref[...].astype(F32) * gbn_ref[...] * keep_next], axis=0)
        du = _conv_t_ext(d_ext, w_ref)[:tm, :]
        dgc_ref[...] = (du * xb).astype(dgc_ref.dtype)
        dxb_ref[...] = (du * gc).astype(dxb_ref.dtype)

        @pl.when(i == 0)
        def _():
            dw_ref[...] = jnp.zeros_like(dw_ref)

        dw_ref[0:1, :] += jnp.sum(dcv * pltpu.roll(e, 2, 0)[HALO:, :], axis=0, keepdims=True)
        dw_ref[1:2, :] += jnp.sum(dcv * pltpu.roll(e, 1, 0)[HALO:, :], axis=0, keepdims=True)
        dw_ref[2:3, :] += jnp.sum(dcv * u, axis=0, keepdims=True)

    xb, xbp, _ = _time_specs(S, tm, LANES, lambda j: b0 + j)
    gb, _, gbn = _time_specs(S, tm, LANES, lambda j: b0 + nb + j)
    gc, gcp, _ = _time_specs(S, tm, LANES, lambda j: b0 + 2 * nb + j)
    do, _, don = _time_specs(S, tm, LANES, lambda j: j)
    out = BS((tm, LANES), lambda j, i: (i, j))
    wspec = BS((3, LANES), lambda j, i: (0, j))
    return pl.pallas_call(
        body, name=name, grid=(nb, nt), in_specs=[xb, xbp, gb, gbn, gc, gcp, do, don, wspec],
        out_specs=[out, out, out, wspec],
        out_shape=[SDS((S, CONV_WIDTH), BF16)] * 3 + [SDS((3, CONV_WIDTH), F32)],
        compiler_params=_params(2))(proj, proj, proj, proj, proj, proj, do_b, do_b, w)


_GELU_C = 0.7978845608028654


def _gelu(x):
    return 0.5 * x * (1.0 + jnp.tanh(_GELU_C * (x + 0.044715 * x * x * x)))


def _gelu_grad(x):
    t = jnp.tanh(_GELU_C * (x + 0.044715 * x * x * x))
    return 0.5 * (1.0 + t) + 0.5 * x * (1.0 - t * t) * _GELU_C * (1.0 + 3.0 * 0.044715 * x * x)


def _tril_masks():
    row = lax.broadcasted_iota(jnp.int32, (SGU_CHUNK, SGU_CHUNK), 0)
    col = lax.broadcasted_iota(jnp.int32, (SGU_CHUNK, SGU_CHUNK), 1)
    return row >= col, row <= col


def _lane_column(mat, g):
    lane = lax.broadcasted_iota(jnp.int32, mat.shape, 1)
    return jnp.sum(jnp.where(lane == g, mat, 0.0), axis=-1, keepdims=True)


def sgu_fwd(proj, norm_g, w_s, b_t, name):
    S = proj.shape[0]
    T = SGU_CHUNK
    G = SGU_WIDTH // T
    c0 = OFF_C // (2 * SGU_WIDTH)

    def body(c_ref, g_ref, w_ref, b_ref, o_ref):
        u = _gelu(c_ref[:, 0:SGU_WIDTH])
        v = _gelu(c_ref[:, SGU_WIDTH:2 * SGU_WIDTH])
        r = lax.rsqrt(jnp.mean(v * v, axis=-1, keepdims=True) + EPS)
        vn = ((v * r) * g_ref[...]).astype(BF16)
        mask, _ = _tril_masks()
        bias = b_ref[...]
        for g in range(G):
            sl = slice(g * T, (g + 1) * T)
            wt = jnp.where(mask, w_ref[g], 0.0).astype(BF16)
            mixed = _dot(wt, vn[:, sl]) + _lane_column(bias, g)
            o_ref[:, sl] = (u[:, sl] * mixed).astype(o_ref.dtype)

    return pl.pallas_call(
        body, name=name, grid=(S // T,),
        in_specs=[BS((T, 2 * SGU_WIDTH), lambda i: (i, c0)), BS((1, SGU_WIDTH), lambda i: (0, 0)),
                  BS((G, T, T), lambda i: (0, 0, 0)), BS((T, T), lambda i: (0, 0))],
        out_specs=BS((T, SGU_WIDTH), lambda i: (i, 0)), out_shape=SDS((S, SGU_WIDTH), BF16),
        compiler_params=_params(1))(proj, norm_g, w_s, b_t)


def sgu_bwd(proj, norm_g, w_s, w_st, b_t, do_c, name):
    S = proj.shape[0]
    T = SGU_CHUNK
    G = SGU_WIDTH // T
    c0 = OFF_C // (2 * SGU_WIDTH)

    def body(c_ref, g_ref, w_ref, wt_ref, b_ref, do_ref, dc_ref, dg_ref, dw_ref, db_ref):
        cu = c_ref[:, 0:SGU_WIDTH]
        cv = c_ref[:, SGU_WIDTH:2 * SGU_WIDTH]
        u = _gelu(cu)
        v = _gelu(cv)
        r = lax.rsqrt(jnp.mean(v * v, axis=-1, keepdims=True) + EPS)
        xh = v * r
        gv = g_ref[...]
        vn = (xh * gv).astype(BF16)
        do = do_ref[...].astype(F32)
        mask, mask_t = _tril_masks()
        bias = b_ref[...]
        lane = lax.broadcasted_iota(jnp.int32, (T, T), 1)

        @pl.when(pl.program_id(0) == 0)
        def _():
            dg_ref[...] = jnp.zeros_like(dg_ref)
            dw_ref[...] = jnp.zeros_like(dw_ref)
            db_ref[...] = jnp.zeros_like(db_ref)

        dvn_parts = []
        du_parts = []
        dbias = jnp.zeros((T, T), F32)
        for g in range(G):
            sl = slice(g * T, (g + 1) * T)
            wt = jnp.where(mask, w_ref[g], 0.0).astype(BF16)
            mixed = _dot(wt, vn[:, sl]) + _lane_column(bias, g)
            du_parts.append(do[:, sl] * mixed)
            dmix = do[:, sl] * u[:, sl]
            dmix_b = dmix.astype(BF16)
            dw_ref[g] += jnp.where(mask, _dot_nt(dmix_b, vn[:, sl]), 0.0)
            dbias = dbias + jnp.where(lane == g, jnp.sum(dmix, axis=-1, keepdims=True), 0.0)
            wtt = jnp.where(mask_t, wt_ref[g], 0.0).astype(BF16)
            dvn_parts.append(_dot(wtt, dmix_b))
        db_ref[...] += dbias
        dvn = jnp.concatenate(dvn_parts, axis=-1)
        du = jnp.concatenate(du_parts, axis=-1)
        dg_ref[...] += jnp.sum(dvn * xh, axis=0, keepdims=True)
        dxh = dvn * gv
        dv = r * (dxh - xh * jnp.mean(dxh * xh, axis=-1, keepdims=True))
        dc_ref[:, 0:SGU_WIDTH] = (du * _gelu_grad(cu)).astype(dc_ref.dtype)
        dc_ref[:, SGU_WIDTH:2 * SGU_WIDTH] = (dv * _gelu_grad(cv)).astype(dc_ref.dtype)

    wspec = BS((G, T, T), lambda i: (0, 0, 0))
    gspec = BS((1, SGU_WIDTH), lambda i: (0, 0))
    return pl.pallas_call(
        body, name=name, grid=(S // T,),
        in_specs=[BS((T, 2 * SGU_WIDTH), lambda i: (i, c0)), gspec, wspec, wspec, BS((T, T), lambda i: (0, 0)),
                  BS((T, SGU_WIDTH), lambda i: (i, 0))],
        out_specs=[BS((T, 2 * SGU_WIDTH), lambda i: (i, 0)), gspec, wspec, BS((T, T), lambda i: (0, 0))],
        out_shape=[SDS((S, 2 * SGU_WIDTH), BF16), SDS((1, SGU_WIDTH), F32), SDS((G, T, T), F32), SDS((T, T), F32)],
        compiler_params=_params(1))(proj, norm_g, w_s, w_st, b_t, do_c)


def rope_tables(positions):
    half = ROPE_DIM // 2
    inv = ROPE_THETA ** (-jnp.arange(half, dtype=F32) * (2.0 / ROPE_DIM))
    ang = positions.astype(F32)[:, None] * inv
    cos, sin = jnp.cos(ang), jnp.sin(ang)
    S = positions.shape[0]
    ones = jnp.ones((S, HEAD_DIM - ROPE_DIM), F32)
    zeros = jnp.zeros((S, HEAD_DIM - ROPE_DIM), F32)
    zh = jnp.zeros((S, half), F32)
    c = jnp.concatenate([cos, cos, ones], axis=-1)
    sa = jnp.concatenate([zh, sin, zeros], axis=-1)
    sb = jnp.concatenate([-sin, zh, zeros], axis=-1)
    tile2 = lambda t: jnp.concatenate([t, t], axis=-1)
    return tile2(c), tile2(sa), tile2(sb)


def rope_apply(x, col0, ncols, cos, sa, sb, out_dtype, name):
    S = x.shape[0]
    tm = _tile(S, 512, 8)
    half = ROPE_DIM // 2
    b0 = col0 // LANES

    def body(x_ref, c_ref, sa_ref, sb_ref, o_ref):
        xv = x_ref[...].astype(F32)
        o_ref[...] = (xv * c_ref[...] + pltpu.roll(xv, half, 1) * sa_ref[...]
                      + pltpu.roll(xv, LANES - half, 1) * sb_ref[...]).astype(o_ref.dtype)

    tab = BS((tm, LANES), lambda i, j: (i, 0))
    return pl.pallas_call(
        body, name=name, grid=(S // tm, ncols // LANES),
        in_specs=[BS((tm, LANES), lambda i, j: (i, b0 + j)), tab, tab, tab],
        out_specs=BS((tm, LANES), lambda i, j: (i, j)), out_shape=SDS((S, ncols), out_dtype),
        compiler_params=_params(2))(x, cos, sa, sb)


def _dil_masks(first):
    qi = lax.broadcasted_iota(jnp.int32, (DIL_SPAN, DIL_SPAN), 0)
    ki = lax.broadcasted_iota(jnp.int32, (DIL_SPAN, DIL_SPAN), 1)
    return ki >= qi + first.astype(jnp.int32) * DIL_SPAN, ki <= qi


def dil_attn_fwd(q, k, v, nb, name):
    H, S, Dh = q.shape
    T = DIL_SPAN
    nblk = S // T
    scale = Dh ** -0.5

    def body(q_ref, kp_ref, kc_ref, vp_ref, vc_ref, o_ref, lse_ref):
        mp, mc = _dil_masks(pl.program_id(0) % nb == 0)
        for h in range(H):
            qv = q_ref[h]
            sp = jnp.where(mp, _dot_nt(qv, kp_ref[h]) * scale, NEG)
            sc = jnp.where(mc, _dot_nt(qv, kc_ref[h]) * scale, NEG)
            m = jnp.maximum(jnp.max(sp, axis=-1, keepdims=True), jnp.max(sc, axis=-1, keepdims=True))
            ep = jnp.exp(sp - m)
            ec = jnp.exp(sc - m)
            l = jnp.sum(ep, axis=-1, keepdims=True) + jnp.sum(ec, axis=-1, keepdims=True)
            o_ref[h] = (_dot(ep.astype(BF16), vp_ref[h]) + _dot(ec.astype(BF16), vc_ref[h])) / l
            lse_ref[h] = m + jnp.log(l)

    cur = BS((H, T, Dh), lambda b: (0, b, 0))
    prev = BS((H, T, Dh), lambda b: (0, jnp.maximum(b - 1, 0), 0))
    colb = BS((H, T, 1), lambda b: (0, b, 0))
    return pl.pallas_call(
        body, name=name, grid=(nblk,), in_specs=[cur, prev, cur, prev, cur], out_specs=[cur, colb],
        out_shape=[SDS((H, S, Dh), F32), SDS((H, S, 1), F32)], compiler_params=_params(1))(q, k, k, v, v)


def dil_attn_bwd_dq(q, k, v, o, lse, do, dlse, nb, name):
    H, S, Dh = q.shape
    T = DIL_SPAN
    nblk = S // T
    scale = Dh ** -0.5

    def body(q_ref, kp_ref, kc_ref, vp_ref, vc_ref, o_ref, lse_ref, do_ref, dlse_ref, dq_ref, dl_ref):
        mp, mc = _dil_masks(pl.program_id(0) % nb == 0)
        for h in range(H):
            qv = q_ref[h]
            kp, kc = kp_ref[h], kc_ref[h]
            dov = do_ref[h]
            delta = jnp.sum(dov * o_ref[h], axis=-1, keepdims=True) - dlse_ref[h]
            dl_ref[h] = delta
            dob = dov.astype(BF16)
            lse_v = lse_ref[h]
            pp = jnp.exp(jnp.where(mp, _dot_nt(qv, kp) * scale, NEG) - lse_v)
            pc = jnp.exp(jnp.where(mc, _dot_nt(qv, kc) * scale, NEG) - lse_v)
            dsp = pp * (_dot_nt(dob, vp_ref[h]) - delta)
            dsc = pc * (_dot_nt(dob, vc_ref[h]) - delta)
            dq_ref[h] = (_dot(dsp.astype(BF16), kp) + _dot(dsc.astype(BF16), kc)) * scale

    cur = BS((H, T, Dh), lambda b: (0, b, 0))
    prev = BS((H, T, Dh), lambda b: (0, jnp.maximum(b - 1, 0), 0))
    colb = BS((H, T, 1), lambda b: (0, b, 0))
    return pl.pallas_call(
        body, name=name, grid=(nblk,), in_specs=[cur, prev, cur, prev, cur, cur, colb, cur, colb],
        out_specs=[cur, colb], out_shape=[SDS((H, S, Dh), F32), SDS((H, S, 1), F32)],
        compiler_params=_params(1))(q, k, k, v, v, o, lse, do, dlse)


def dil_attn_bwd_dkv(q, k, v, lse_row, delta_row, do, nb, name):
    H, S, Dh = q.shape
    T = DIL_SPAN
    nblk = S // T
    scale = Dh ** -0.5

    def body(k_ref, v_ref, qc_ref, qn_ref, doc_ref, don_ref, lc_ref, ln_ref, dc_ref, dn_ref, dk_ref, dv_ref):
        no_next = ((pl.program_id(0) + 1) % nb == 0).astype(jnp.int32)
        si = lax.broadcasted_iota(jnp.int32, (T, T), 0)
        ti = lax.broadcasted_iota(jnp.int32, (T, T), 1)
        m_cur = si <= ti
        m_next = si >= ti + no_next * T
        for h in range(H):
            kv, vv = k_ref[h], v_ref[h]
            qc, qn = qc_ref[h], qn_ref[h]
            doc, don = doc_ref[h].astype(BF16), don_ref[h].astype(BF16)
            pt = jnp.exp(jnp.where(m_cur, _dot_nt(kv, qc) * scale, NEG) - lc_ref[h])
            ptn = jnp.exp(jnp.where(m_next, _dot_nt(kv, qn) * scale, NEG) - ln_ref[h])
            dv_ref[h] = _dot(pt.astype(BF16), doc) + _dot(ptn.astype(BF16), don)
            dst = pt * (_dot_nt(vv, doc) - dc_ref[h])
            dstn = ptn * (_dot_nt(vv, don) - dn_ref[h])
            dk_ref[h] = (_dot(dst.astype(BF16), qc) + _dot(dstn.astype(BF16), qn)) * scale

    cur = BS((H, T, Dh), lambda b: (0, b, 0))
    nxt = BS((H, T, Dh), lambda b: (0, jnp.minimum(b + 1, nblk - 1), 0))
    rcur = BS((H, None, 1, T), lambda b: (0, b, 0, 0))
    rnxt = BS((H, None, 1, T), lambda b: (0, jnp.minimum(b + 1, nblk - 1), 0, 0))
    return pl.pallas_call(
        body, name=name, grid=(nblk,), in_specs=[cur, cur, cur, nxt, cur, nxt, rcur, rnxt, rcur, rnxt],
        out_specs=[cur, cur], out_shape=[SDS((H, S, Dh), F32), SDS((H, S, Dh), F32)],
        compiler_params=_params(1))(k, v, q, q, do, do, lse_row, lse_row, delta_row, delta_row)


def dil_merge_fwd(outs, lses, name):
    H, S, Dh = outs[0].shape
    tm = _tile(S, 512, 8)

    def body(o0, o1, o2, l0, l1, l2, out_ref):
        ls = [l0[...], l1[...], l2[...]]
        m = jnp.maximum(jnp.maximum(ls[0], ls[1]), ls[2])
        es = [jnp.exp(l - m) for l in ls]
        den = es[0] + es[1] + es[2]
        out_ref[...] = (es[0] * o0[...] + es[1] * o1[...] + es[2] * o2[...]) / den

    blk = BS((None, tm, Dh), lambda h, i: (h, i, 0))
    colb = BS((None, tm, 1), lambda h, i: (h, i, 0))
    return pl.pallas_call(
        body, name=name, grid=(H, S // tm), in_specs=[blk] * 3 + [colb] * 3, out_specs=blk,
        out_shape=SDS((H, S, Dh), F32), compiler_params=_params(2))(*outs, *lses)


def dil_merge_bwd(outs, lses, d_out, name):
    H, S, Dh = outs[0].shape
    tm = _tile(S, 512, 8)

    def body(o0, o1, o2, l0, l1, l2, d_ref, do0, do1, do2, dl0, dl1, dl2):
        ls = [l0[...], l1[...], l2[...]]
        m = jnp.maximum(jnp.maximum(ls[0], ls[1]), ls[2])
        es = [jnp.exp(l - m) for l in ls]
        den = es[0] + es[1] + es[2]
        ws = [e / den for e in es]
        dv = d_ref[...]
        dws = [jnp.sum(dv * o[...], axis=-1, keepdims=True) for o in (o0, o1, o2)]
        mean = ws[0] * dws[0] + ws[1] * dws[1] + ws[2] * dws[2]
        for w, dw, do_ref, dl_ref in zip(ws, dws, (do0, do1, do2), (dl0, dl1, dl2)):
            do_ref[...] = w * dv
            dl_ref[...] = w * (dw - mean)

    blk = BS((None, tm, Dh), lambda h, i: (h, i, 0))
    colb = BS((None, tm, 1), lambda h, i: (h, i, 0))
    return pl.pallas_call(
        body, name=name, grid=(H, S // tm), in_specs=[blk] * 3 + [colb] * 3 + [blk],
        out_specs=[blk] * 3 + [colb] * 3,
        out_shape=[SDS((H, S, Dh), F32)] * 3 + [SDS((H, S, 1), F32)] * 3,
        compiler_params=_params(2))(*outs, *lses, d_out)


MERGE_TM = 1024
MERGE_TN = 512


def merge_fwd(o_stack, w_br, proj, name):
    _, S, K = o_stack.shape
    tm = _tile(S, MERGE_TM, 8)
    tn = MERGE_TN
    nj = D_MODEL // tn

    def body(o_ref, w_ref, gl_ref, m_ref, acc):
        br = pl.program_id(2)

        @pl.when(br == 0)
        def _():
            acc[...] = jnp.zeros_like(acc)

        acc[...] += _sigmoid(gl_ref[...]) * _dot(o_ref[...], w_ref[...])

        @pl.when(br == N_BRANCH - 1)
        def _():
            m_ref[...] = acc[...].astype(m_ref.dtype)

    return pl.pallas_call(
        body, name=name, grid=(S // tm, nj, N_BRANCH),
        in_specs=[BS((None, tm, K), lambda i, j, b: (b, i, 0)), BS((None, K, tn), lambda i, j, b: (b, 0, j)),
                  BS((tm, tn), lambda i, j, b: (i, b * nj + j))],
        out_specs=BS((tm, tn), lambda i, j, b: (i, j)), out_shape=SDS((S, D_MODEL), BF16),
        scratch_shapes=[pltpu.VMEM((tm, tn), F32)], compiler_params=_params(3))(o_stack, w_br, proj)


def merge_bwd(o_stack, w_br, proj, d_merged, name):
    _, S, K = o_stack.shape
    tm = _tile(S, MERGE_TM, 8)
    tn = MERGE_TN
    nj = D_MODEL // tn

    def body(o_ref, w_ref, gl_ref, dm_ref, dgl_ref, dy_ref):
        gate = _sigmoid(gl_ref[...])
        y = _dot(o_ref[...], w_ref[...])
        dm = dm_ref[...].astype(F32)
        dgl_ref[...] = (dm * y * gate * (1.0 - gate)).astype(dgl_ref.dtype)
        dy_ref[...] = (dm * gate).astype(dy_ref.dtype)

    return pl.pallas_call(
        body, name=name, grid=(S // tm, nj, N_BRANCH),
        in_specs=[BS((None, tm, K), lambda i, j, b: (b, i, 0)), BS((None, K, tn), lambda i, j, b: (b, 0, j)),
                  BS((tm, tn), lambda i, j, b: (i, b * nj + j)), BS((tm, tn), lambda i, j, b: (i, j))],
        out_specs=[BS((tm, tn), lambda i, j, b: (i, b * nj + j)), BS((None, tm, tn), lambda i, j, b: (b, i, j))],
        out_shape=[SDS((S, N_BRANCH * D_MODEL), BF16), SDS((N_BRANCH, S, D_MODEL), BF16)],
        compiler_params=_params(3))(o_stack, w_br, proj, d_merged)


FFN_CW = 256


def ffn_mid_fwd(pre_g, pre_v, w_g, w_v, name, side=None):
    S = pre_g.shape[0]
    tm = _tile(S, CONV_TM, 8)

    def body(g_ref, gp_ref, v_ref, vp_ref, wg_ref, wv_ref, a_ref, at_ref):
        keep = (pl.program_id(1) > 0).astype(F32)
        ug = _conv_ext(jnp.concatenate([gp_ref[...] * keep, g_ref[...]], axis=0), wg_ref)[HALO:, :]
        uv = _conv_ext(jnp.concatenate([vp_ref[...] * keep, v_ref[...]], axis=0), wv_ref)[HALO:, :]
        a = ug * _sigmoid(ug) * uv
        a_ref[...] = a.astype(a_ref.dtype)
        at_ref[...] = a.T.astype(at_ref.dtype)

    cur, prev, _ = _time_specs(S, tm, FFN_CW, lambda j: j)
    wspec = BS((3, FFN_CW), lambda j, i: (0, j))
    (a, a_t), side_out = _call_with_side(
        body, side, name, (D_FF // FFN_CW, S // tm), [cur, prev, cur, prev, wspec, wspec],
        [cur, BS((FFN_CW, tm), lambda j, i: (j, i))], [SDS((S, D_FF), BF16), SDS((D_FF, S), BF16)],
        (pre_g, pre_g, pre_v, pre_v, w_g, w_v))
    return a, a_t, side_out


def ffn_mid_bwd(pre_g, pre_v, w_g, w_v, d_a, name, side=None):
    S = pre_g.shape[0]
    tm = _tile(S, CONV_TM, 8)
    nt = S // tm

    def body(g_ref, gp_ref, gn_ref, v_ref, vp_ref, vn_ref, wg_ref, wv_ref, da_ref, dan_ref,
             dg_ref, dv_ref, dwg_ref, dwv_ref):
        i = pl.program_id(1)
        keep_prev = (i > 0).astype(F32)
        keep_next = (i < nt - 1).astype(F32)
        eg = jnp.concatenate([gp_ref[...] * keep_prev, g_ref[...], gn_ref[...]], axis=0)
        ev = jnp.concatenate([vp_ref[...] * keep_prev, v_ref[...], vn_ref[...]], axis=0)
        ug = _conv_ext(eg, wg_ref)
        uv = _conv_ext(ev, wv_ref)
        da = jnp.concatenate([jnp.zeros((HALO, FFN_CW), F32), da_ref[...], dan_ref[...] * keep_next], axis=0)
        sg = _sigmoid(ug)
        dug = da * uv * (sg * (1.0 + ug * (1.0 - sg)))
        duv = da * (ug * sg)
        dg_ref[...] = _conv_t_ext(dug, wg_ref)[HALO:HALO + tm, :].astype(dg_ref.dtype)
        dv_ref[...] = _conv_t_ext(duv, wv_ref)[HALO:HALO + tm, :].astype(dv_ref.dtype)

        @pl.when(i == 0)
        def _():
            dwg_ref[...] = jnp.zeros_like(dwg_ref)
            dwv_ref[...] = jnp.zeros_like(dwv_ref)

        for dup_e, e, dw_ref in ((dug, eg, dwg_ref), (duv, ev, dwv_ref)):
            dup = dup_e[HALO:HALO + tm, :]
            dw_ref[0:1, :] += jnp.sum(dup * pltpu.roll(e, 2, 0)[HALO:HALO + tm, :], axis=0, keepdims=True)
            dw_ref[1:2, :] += jnp.sum(dup * pltpu.roll(e, 1, 0)[HALO:HALO + tm, :], axis=0, keepdims=True)
            dw_ref[2:3, :] += jnp.sum(dup * e[HALO:HALO + tm, :], axis=0, keepdims=True)

    cur, prev, nxt = _time_specs(S, tm, FFN_CW, lambda j: j)
    wspec = BS((3, FFN_CW), lambda j, i: (0, j))
    return _call_with_side(
        body, side, name, (D_FF // FFN_CW, nt), [cur, prev, nxt, cur, prev, nxt, wspec, wspec, cur, nxt],
        [cur, cur, wspec, wspec], [SDS((S, D_FF), BF16)] * 2 + [SDS((3, D_FF), F32)] * 2,
        (pre_g, pre_g, pre_g, pre_v, pre_v, pre_v, w_g, w_v, d_a, d_a))


def ple_fwd(x, a, e, name):
    S, Dm = x.shape
    tm = _tile(S, 256, 8)

    def body(x_ref, a_ref, e_ref, o_ref):
        o_ref[...] = x_ref[...] + _sigmoid(a_ref[...]) * e_ref[...]

    row = BS((tm, Dm), lambda i: (i, 0))
    return pl.pallas_call(body, name=name, grid=(S // tm,), in_specs=[row] * 3, out_specs=row,
                          out_shape=SDS((S, Dm), F32), compiler_params=_params(1))(x, a, e)


def ple_bwd(a, e, dx, name):
    S, Dm = a.shape
    tm = _tile(S, 256, 8)

    def body(a_ref, e_ref, dx_ref, da_ref, de_ref):
        s = _sigmoid(a_ref[...])
        d = dx_ref[...]
        da_ref[...] = (d * e_ref[...] * s * (1.0 - s)).astype(da_ref.dtype)
        de_ref[...] = (d * s).astype(de_ref.dtype)

    row = BS((tm, Dm), lambda i: (i, 0))
    return pl.pallas_call(body, name=name, grid=(S // tm,), in_specs=[row] * 3, out_specs=[row, row],
                          out_shape=[SDS((S, Dm), BF16)] * 2, compiler_params=_params(1))(a, e, dx)


def to_heads(x, n_heads, dil):
    S = x.shape[0]
    x = x.reshape(S // dil, dil, n_heads, HEAD_DIM).transpose(2, 1, 0, 3)
    return x.reshape(n_heads, S, HEAD_DIM)


def from_heads(y, dil):
    H, S, C = y.shape
    y = y.reshape(H, dil, S // dil, C).transpose(2, 1, 0, 3)
    return y.reshape(S, H * C)


def _columns(x, lo, hi):
    return lax.optimization_barrier(x[:, lo:hi])


def rows_of(col, t):
    H, S, _ = col.shape
    return col.reshape(H, S // t, 1, t)


IN_SEGMENTS = ((SRC_A, SRC_F, OFF_A), (SRC_F, SRC_B, OFF_F), (SRC_B, SRC_C, OFF_B), (SRC_C, SRC_DQ, OFF_C),
               (SRC_DQ, SRC_G, OFF_D), (SRC_G, D_IN, OFF_G))
IN_SHARD = D_IN // N_CHIPS


def w_in_aligned_from_chips(t):
    pieces = []
    for ref_lo, ref_hi, _ in sorted(IN_SEGMENTS, key=lambda seg: seg[2]):
        for k in range(N_CHIPS):
            lo, hi = max(ref_lo, k * IN_SHARD), min(ref_hi, (k + 1) * IN_SHARD)
            if lo < hi:
                pieces.append(t[k][:, lo - k * IN_SHARD:hi - k * IN_SHARD])
    pieces.append(jnp.zeros((t[0].shape[0], W_AL - D_IN), t[0].dtype))
    return jnp.concatenate(pieces, axis=1)


def w_in_chips_from_aligned(g):
    slots = []
    for k in range(N_CHIPS):
        pieces = []
        for ref_lo, ref_hi, al in IN_SEGMENTS:
            lo, hi = max(ref_lo, k * IN_SHARD), min(ref_hi, (k + 1) * IN_SHARD)
            if lo < hi:
                pieces.append(g[:, al + lo - ref_lo:al + hi - ref_lo])
        slots.append(jnp.concatenate(pieces, axis=1))
    return jnp.stack(slots, axis=0)


def chips_to_full(t, name):
    return jnp.concatenate([t[k] for k in range(N_CHIPS)], axis=0 if name in ROW_SHARDED else 1)


def full_to_chips(g, name):
    if name in ROW_SHARDED:
        return g.reshape(N_CHIPS, g.shape[0] // N_CHIPS, g.shape[1])
    return g.reshape(g.shape[0], N_CHIPS, g.shape[1] // N_CHIPS).transpose(1, 0, 2)


def halves_from_chips(t):
    return jnp.concatenate([t[0], t[1]], axis=1), jnp.concatenate([t[2], t[3]], axis=1)


def chips_from_halves(g, v):
    c = g.shape[1] // 2
    return jnp.stack([g[:, :c], g[:, c:], v[:, :c], v[:, c:]], axis=0)


def layer_fwd(x, p_l, rope, w, tag, sides=None):
    S = x.shape[0]
    sides = sides or {}
    side_out = {}
    sv = {"x0": x}
    h, sv["h_t"] = rmsnorm_fwd(x, w["norm_mix_g"], f"{tag}_norm_mix")
    proj = matmul(h, w["w_in_al"], "nn", F32, f"{tag}_proj", side=sides.get("proj"))
    if "proj" in sides:
        proj, side_out["proj"] = proj
    sv["proj"] = proj

    af_t, f_cum = fox_prep_fwd(proj, w["fox_forget_b"].reshape(FOX_HEADS, 1), f"{tag}_fox_prep")
    T = min(FOX_T, S)
    f_col = f_cum.reshape(FOX_HEADS, S, 1)
    f_row = f_cum.reshape(FOX_HEADS, S // T, 1, T)
    qkv = to_heads(_columns(proj, OFF_A, OFF_B).astype(BF16), 3 * FOX_HEADS, 1)
    qa, ka, va = qkv[:FOX_HEADS], qkv[FOX_HEADS:2 * FOX_HEADS], qkv[2 * FOX_HEADS:]
    (oa_h, lse_a), side_out["fox"] = fox_attn_fwd(qa, ka, va, f_col, f_row, f"{tag}_fox_fwd", sides.get("fox"))
    o_a = from_heads(oa_h, 1)
    sv.update(af_t=af_t, f_col=f_col, f_row=f_row, qa=qa, ka=ka, va=va, oa_h=oa_h, lse_a=lse_a)

    o_b = shortconv_fwd(proj, w["shortconv_w"], f"{tag}_sconv_fwd")

    o_c = sgu_fwd(proj, w["sgu_norm_g"].reshape(1, SGU_WIDTH), w["sgu_w"], _sgu_bias(w["sgu_b"]), f"{tag}_sgu_fwd")

    cos, sa, sb = rope
    qk = rope_apply(proj, OFF_D, 2 * DIL_WIDTH, cos, sa, sb, BF16, f"{tag}_rope_fwd")
    vd = _columns(proj, OFF_D + 2 * DIL_WIDTH, OFF_D + 3 * DIL_WIDTH).astype(BF16)
    outs, lses, dil_sv = [], [], []
    for g, (window, dil) in enumerate(DIL_PATTERNS):
        sl = slice(g * DIL_OUT, (g + 1) * DIL_OUT)
        qg = to_heads(qk[:, sl], 4, dil)
        kg = to_heads(qk[:, DIL_WIDTH:][:, sl], 4, dil)
        vg = to_heads(vd[:, sl], 4, dil)
        nb = (S // dil) // DIL_SPAN
        og, lg = dil_attn_fwd(qg, kg, vg, nb, f"{tag}_dil{g}_fwd")
        dil_sv.append((qg, kg, vg, og, lg, nb))
        outs.append(_heads_unperm(og, dil))
        lses.append(_col_unperm(lg, dil))
    od_h = dil_merge_fwd(outs, lses, f"{tag}_dil_merge_fwd")
    o_d = from_heads(od_h, 1)
    sv.update(dil=dil_sv, outs=outs, lses=lses)

    o_d_pad = jnp.concatenate([o_d.astype(BF16), jnp.zeros((S, FOX_WIDTH - DIL_OUT), BF16)], axis=-1)
    o_stack = jnp.stack([o_a, o_b, o_c, o_d_pad], axis=0)
    merged = merge_fwd(o_stack, w["w_br"], proj, f"{tag}_merge_fwd")
    x1 = matmul(merged, w["w_out"], "nn", F32, f"{tag}_out_proj", res=x)
    sv.update(o_stack=o_stack, merged=merged, x1=x1)

    h2, sv["h2_t"] = rmsnorm_fwd(x1, w["norm_ffn_g"], f"{tag}_norm_ffn")
    pre = (matmul(h2, w["w_up_g"], "nn", F32, f"{tag}_up_g"), matmul(h2, w["w_up_v"], "nn", F32, f"{tag}_up_v"))
    a, sv["a_t"], side_out["ffn"] = ffn_mid_fwd(pre[0], pre[1], w["ffn_conv_g"], w["ffn_conv_v"],
                                                f"{tag}_ffn_mid_fwd", sides.get("ffn"))
    x2 = matmul(a, w["w_down"], "nn", F32, f"{tag}_down", res=x1)
    sv.update(pre=pre, a=a, x2=x2)

    n3, sv["n3_t"] = rmsnorm_fwd(x2, w["norm_ple_g"], f"{tag}_norm_ple")
    pg = matmul(n3, w["w_ple_gate"], "nn", F32, f"{tag}_ple_gate")
    pe = matmul(p_l, w["w_ple_proj"], "nn", F32, f"{tag}_ple_proj")
    x3 = ple_fwd(x2, pg, pe, f"{tag}_ple_fwd")
    sv.update(pg=pg, pe=pe, p_l=p_l)
    return x3, sv, side_out


def _sgu_bias(b):
    return jnp.pad(b.T, ((0, 0), (0, SGU_CHUNK - b.shape[0])))


def _col_unperm(col, dil):
    H, S, _ = col.shape
    return col.reshape(H, dil, S // dil).transpose(0, 2, 1).reshape(H, S, 1)


def _col_perm(col, dil):
    H, S, _ = col.shape
    return col.reshape(H, S // dil, dil).transpose(0, 2, 1).reshape(H, S, 1)


def _heads_perm(y, dil):
    H, S, C = y.shape
    return y.reshape(H, S // dil, dil, C).transpose(0, 2, 1, 3).reshape(H, S, C)


def _heads_unperm(y, dil):
    H, S, C = y.shape
    return y.reshape(H, dil, S // dil, C).transpose(0, 2, 1, 3).reshape(H, S, C)


def layer_bwd(dx3, sv, rope, w, tag, exch):
    side_out = {}
    S = dx3.shape[0]
    gr = {}
    da, de = ple_bwd(sv["pg"], sv["pe"], dx3, f"{tag}_ple_bwd")
    gr["w_ple_proj"] = matmul(sv["p_l"], de, "tn", F32, f"{tag}_dw_ple_proj")
    gr["w_ple_gate"] = matmul(sv["n3_t"], da, "nn", F32, f"{tag}_dw_ple_gate")
    dn3 = matmul(da, w["w_ple_gate"], "nt", BF16, f"{tag}_dn3")
    dx2, dx2_b, gr["norm_ple_g"] = rmsnorm_bwd(sv["x2"], w["norm_ple_g"], dn3, dx3, f"{tag}_norm_ple_bwd")

    d_a = matmul(dx2_b, w["w_down"], "nt", F32, f"{tag}_da")
    gr["w_down"] = matmul(sv["a_t"], dx2_b, "nn", F32, f"{tag}_dw_down")
    (dpre_g, dpre_v, dwc_g, dwc_v), swapped = ffn_mid_bwd(sv["pre"][0], sv["pre"][1], w["ffn_conv_g"],
                                                          w["ffn_conv_v"], d_a, f"{tag}_ffn_mid_bwd",
                                                          exch.swap_exchange())
    sides = exch.ici_exchanges(swapped)
    gr["ffn_conv_w"] = (dwc_g, dwc_v)
    gr["w_up"] = (matmul(sv["h2_t"], dpre_g, "nn", F32, f"{tag}_dw_up_g", tm=2048, tn=512),
                  matmul(sv["h2_t"], dpre_v, "nn", F32, f"{tag}_dw_up_v", tm=2048, tn=512))
    dh2_g = matmul(dpre_g, w["w_up_g"], "nt", F32, f"{tag}_dh2_g")
    dh2 = matmul(dpre_v, w["w_up_v"], "nt", BF16, f"{tag}_dh2_v", res=dh2_g)
    dx1, dx1_b, gr["norm_ffn_g"] = rmsnorm_bwd(sv["x1"], w["norm_ffn_g"], dh2, dx2, f"{tag}_norm_ffn_bwd")

    d_merged = matmul(dx1_b, w["w_out"], "nt", BF16, f"{tag}_dmerged")
    gr["w_out"] = matmul(sv["merged"], dx1_b, "tn", F32, f"{tag}_dw_out")
    proj = sv["proj"]
    dgl, dy = merge_bwd(sv["o_stack"], w["w_br"], proj, d_merged, f"{tag}_merge_bwd")
    d_o, d_wbr = [], []
    for b in range(N_BRANCH):
        d_o.append(matmul(dy[b], w["w_br"][b], "nt", F32, f"{tag}_do{b}"))
        d_wbr.append(matmul(sv["o_stack"][b], dy[b], "tn", F32, f"{tag}_dw_br{b}"))
    gr["w_br"] = d_wbr

    do_a = to_heads(d_o[0].astype(BF16), FOX_HEADS, 1)
    T = min(FOX_T, S)
    (dqa, delta_a, d_fq), side_out["dq"] = fox_attn_bwd_dq(
        sv["qa"], sv["ka"], sv["va"], sv["f_col"], sv["f_row"], sv["oa_h"], sv["lse_a"], do_a, f"{tag}_fox_dq",
        sides.get("dq"))
    (dka, dva, d_fk), side_out["dkv"] = fox_attn_bwd_dkv(
        sv["qa"], sv["ka"], sv["va"], sv["f_col"], sv["f_row"], rows_of(sv["lse_a"], T), rows_of(delta_a, T), do_a,
        f"{tag}_fox_dkv", sides.get("dkv"))
    daf_t, dfb = fox_prep_bwd(sv["af_t"], w["fox_forget_b"].reshape(FOX_HEADS, 1), d_fq.reshape(FOX_HEADS, S),
                              d_fk.reshape(FOX_HEADS, S), f"{tag}_fox_prep_bwd")
    gr["fox_forget_b"] = dfb.reshape(FOX_HEADS)
    d_proj_a = from_heads(jnp.concatenate([dqa, dka, dva], axis=0), 1).astype(BF16)

    dxb, dgb, dgc, gr["shortconv_w"] = shortconv_bwd(proj, w["shortconv_w"], d_o[1], f"{tag}_sconv_bwd")

    d_c, dsg, dsw, dsb = sgu_bwd(proj, w["sgu_norm_g"].reshape(1, SGU_WIDTH), w["sgu_w"],
                                 jnp.swapaxes(w["sgu_w"], 1, 2), _sgu_bias(w["sgu_b"]), d_o[2], f"{tag}_sgu_bwd")
    gr["sgu_norm_g"] = dsg.reshape(SGU_WIDTH)
    gr["sgu_w"] = dsw
    gr["sgu_b"] = dsb[:, :SGU_WIDTH // SGU_CHUNK].T

    d_od = to_heads(d_o[3][:, :DIL_OUT], 4, 1)
    d_outs_lses = dil_merge_bwd(sv["outs"], sv["lses"], d_od, f"{tag}_dil_merge_bwd")
    d_outs, d_lses = d_outs_lses[:3], d_outs_lses[3:]
    dq_parts, dk_parts, dv_parts = [], [], []
    for g, (window, dil) in enumerate(DIL_PATTERNS):
        qg, kg, vg, og, lg, nb = sv["dil"][g]
        do_g = _heads_perm(d_outs[g], dil)
        dl_g = _col_perm(d_lses[g], dil)
        dqg, delta_g = dil_attn_bwd_dq(qg, kg, vg, og, lg, do_g, dl_g, nb, f"{tag}_dil{g}_dq")
        dkg, dvg = dil_attn_bwd_dkv(qg, kg, vg, rows_of(lg, DIL_SPAN), rows_of(delta_g, DIL_SPAN), do_g, nb,
                                    f"{tag}_dil{g}_dkv")
        dq_parts.append(from_heads(dqg, dil))
        dk_parts.append(from_heads(dkg, dil))
        dv_parts.append(from_heads(dvg, dil))
    cos, sa, sb = rope
    d_qk_rot = jnp.concatenate(dq_parts + dk_parts, axis=-1)
    d_qk = rope_apply(d_qk_rot, 0, 2 * DIL_WIDTH, cos, -sa, -sb, BF16, f"{tag}_rope_bwd")
    d_vd = jnp.concatenate(dv_parts, axis=-1).astype(BF16)

    d_f_cols = jnp.concatenate([daf_t.T.astype(BF16), jnp.zeros((S, W_AL - OFF_F - FOX_HEADS), BF16)], axis=-1)
    d_proj = jnp.concatenate([dgl, d_proj_a, dxb, dgb, dgc, d_c, d_qk, d_vd, d_f_cols], axis=-1)
    gr["w_in_al"] = matmul(sv["h_t"], d_proj, "nn", F32, f"{tag}_dw_in", tm=2048, tn=512)
    dh = matmul(d_proj, w["w_in_al"], "nt", BF16, f"{tag}_dh", tk=W_AL // 4)
    dx0, _, gr["norm_mix_g"] = rmsnorm_bwd(sv["x0"], w["norm_mix_g"], dh, dx1, f"{tag}_norm_mix_bwd")
    exch.ici_arrived(side_out)
    return dx0, gr


def local_weights(chips, repl, layer):
    w = {n: repl[n][layer] for n in REPLICATED}
    cast = lambda n, dtype: [chips[n][k].astype(dtype) for k in range(N_CHIPS)]
    full = {n: chips_to_full(cast(n, BF16), n)
            for n in ("w_br_fox", "w_br_conv", "w_br_sgu", "w_br_dil", "w_out", "w_down", "w_ple_gate", "w_ple_proj")}
    w["w_in_al"] = w_in_aligned_from_chips(cast("w_in", BF16))
    w["shortconv_w"] = chips_to_full(cast("shortconv_w", F32), "shortconv_w")
    pad = jnp.zeros((FOX_WIDTH - DIL_OUT, D_MODEL), BF16)
    w["w_br"] = jnp.stack([full["w_br_fox"], full["w_br_conv"], full["w_br_sgu"],
                           jnp.concatenate([full["w_br_dil"], pad], axis=0)], axis=0)
    w["w_up_g"], w["w_up_v"] = halves_from_chips(cast("w_up", BF16))
    w["ffn_conv_g"], w["ffn_conv_v"] = halves_from_chips(cast("ffn_conv_w", F32))
    for n in ("w_out", "w_down", "w_ple_gate", "w_ple_proj"):
        w[n] = full[n]
    return w


def grads_to_chips(gr):
    out = {n: gr[n] for n in ("fox_forget_b", "sgu_norm_g", "sgu_w", "sgu_b")}
    out["norm_mix_g"] = gr["norm_mix_g"].reshape(D_MODEL)
    out["norm_ffn_g"] = gr["norm_ffn_g"].reshape(D_MODEL)
    out["norm_ple_g"] = gr["norm_ple_g"].reshape(D_MODEL)
    out["w_in"] = w_in_chips_from_aligned(gr["w_in_al"])
    out["w_up"] = chips_from_halves(*gr["w_up"])
    out["ffn_conv_w"] = chips_from_halves(*gr["ffn_conv_w"])
    for b, n in enumerate(("w_br_fox", "w_br_conv", "w_br_sgu")):
        out[n] = full_to_chips(gr["w_br"][b], n)
    out["w_br_dil"] = full_to_chips(gr["w_br"][3][:DIL_OUT], "w_br_dil")
    for n in ("shortconv_w", "w_out", "w_down", "w_ple_gate", "w_ple_proj"):
        out[n] = full_to_chips(gr[n], n)
    return out


def local_step(x, p, positions, repl, final_norm_g, loss_target, exch):
    depth = p.shape[0]
    rope = rope_tables(positions)
    saved, ws = [], []
    chips = exch.first_weights()
    for layer in range(depth):
        w = local_weights(chips, repl, layer)
        side = exch.weights_exchange(layer + 1) if layer + 1 < depth else None
        x, sv, side_out = layer_fwd(x, p[layer].astype(BF16), rope, w, f"l{layer}", side)
        if layer + 1 < depth:
            chips = exch.weights_arrived(layer + 1, side_out)
        saved.append(sv)
        ws.append(w)
    loss_part, dx, dgf = final_loss(x, final_norm_g, loss_target, "final_loss")
    for layer in range(depth - 1, -1, -1):
        dx, gr = layer_bwd(dx, saved[layer], rope, ws[layer], f"l{layer}", exch)
        exch.grads_ready(layer, grads_to_chips(gr))
    exch.grads_flush()
    return loss_part[0, 0], dx, dgf.reshape(-1)


def _position():
    return lax.axis_index("x"), lax.axis_index("y"), lax.axis_index("c")


def _other_chips(x, y):
    return [(1 - x, y), (x, 1 - y), (1 - x, 1 - y)]


def _remote(src, dst, send_sem, recv_sem, device):
    return pltpu.make_async_remote_copy(src_ref=src, dst_ref=dst, send_sem=send_sem, recv_sem=recv_sem,
                                        device_id=device, device_id_type=MESH)


def _chip_index():
    return 2 * lax.axis_index("x") + lax.axis_index("y")


def _block_rows(rows, cols, unit):
    return _tile(rows, max(unit, (1 << 20) // cols // unit * unit), unit)


def gather_chip_shards(packs, name):
    return _run_exchange(gather_exchange(packs), name)


def gather_exchange(packs):
    n = len(packs)
    halves = [p.shape[0] // 2 for p in packs]

    def half(outs, t, chip, core):
        return outs[t].at[chip, pl.ds(core * halves[t], halves[t]), :]

    def ici_sends(srcs, outs, send_sems, recv_sems):
        x, y, c = _position()
        me = 2 * x + y
        return [_remote(srcs[t].at[pl.ds(c * halves[t], halves[t]), :], half(outs, t, me, c),
                        send_sems.at[6 * t + j], recv_sems.at[6 * t + j], (px, py, c))
                for t in range(n) for j, (px, py) in enumerate(_other_chips(x, y))]

    def own_to_sibling(srcs, outs, send_sems, recv_sems):
        x, y, c = _position()
        return [_remote(srcs[t], outs[t].at[2 * x + y], send_sems.at[6 * n + t], recv_sems.at[6 * n + t],
                        (x, y, 1 - c)) for t in range(n)]

    def start(srcs, outs, send_sems, recv_sems):
        for cp in ici_sends(srcs, outs, send_sems, recv_sems) + own_to_sibling(srcs, outs, send_sems, recv_sems):
            cp.start()

    def finish(srcs, outs, send_sems, recv_sems):
        x, y, c = _position()
        sibling = (x, y, 1 - c)
        chips = _other_chips(x, y)
        passed = []
        for t in range(n):
            for j, (px, py) in enumerate(chips):
                k = 2 * px + py
                s = 6 * t + j
                landed = half(outs, t, k, c)
                _remote(landed, landed, send_sems.at[s], recv_sems.at[s], (px, py, c)).wait_recv()
                fwd = _remote(landed, landed, send_sems.at[s + 3], recv_sems.at[s + 3], sibling)
                fwd.start()
                passed.append(fwd)
        for t in range(n):
            for j, (px, py) in enumerate(chips):
                s = 6 * t + j + 3
                theirs = half(outs, t, 2 * px + py, 1 - c)
                _remote(theirs, theirs, send_sems.at[s], recv_sems.at[s], sibling).wait_recv()
        for cp in own_to_sibling(srcs, outs, send_sems, recv_sems):
            cp.wait()
        for cp in ici_sends(srcs, outs, send_sems, recv_sems) + passed:
            cp.wait_send()

    return SideExchange(list(packs), [SDS((N_CHIPS,) + p.shape, p.dtype) for p in packs], 7 * n, start, finish)


def _run_exchange(side, name):
    n_in, n_out = len(side.operands), len(side.out_shapes)

    def body(*refs):
        srcs, outs, (send_sems, recv_sems) = refs[:n_in], refs[n_in:n_in + n_out], refs[n_in + n_out:]
        side.start(srcs, outs, send_sems, recv_sems)
        side.finish(srcs, outs, send_sems, recv_sems)

    return pl.pallas_call(
        body, name=name, in_specs=[ANY] * n_in, out_specs=[ANY] * n_out, out_shape=side.out_shapes,
        scratch_shapes=[pltpu.SemaphoreType.DMA((side.n_sems,)), pltpu.SemaphoreType.DMA((side.n_sems,))],
    )(*side.operands)


def swap_halves_with_sibling(gs, name):
    return _run_exchange(swap_exchange(gs), name)


def swap_exchange(gs):
    n = len(gs)
    halves = [g.shape[1] // 2 for g in gs]

    def copies(srcs, lands, send_sems, recv_sems):
        x, y, c = _position()
        return [_remote(srcs[t].at[:, pl.ds((1 - c) * halves[t], halves[t]), :], lands[t], send_sems.at[t],
                        recv_sems.at[t], (x, y, 1 - c)) for t in range(n)]

    def start(srcs, lands, send_sems, recv_sems):
        for cp in copies(srcs, lands, send_sems, recv_sems):
            cp.start()

    def finish(srcs, lands, send_sems, recv_sems):
        for cp in copies(srcs, lands, send_sems, recv_sems):
            cp.wait()

    return SideExchange(list(gs), [SDS((g.shape[0], h, g.shape[2]), g.dtype) for g, h in zip(gs, halves)], n,
                        start, finish)


def add_my_half(g, other, out_dtype, name):
    n, R, C = g.shape
    H = R // 2
    tr = _block_rows(H, C, 16) if H % 16 == 0 else H
    nb = H // tr
    core = lax.axis_index("c").astype(jnp.int32).reshape(1)

    def body(c_ref, g_ref, o_ref, out_ref):
        out_ref[...] = (g_ref[...] + o_ref[...]).astype(out_ref.dtype)

    grid_spec = pltpu.PrefetchScalarGridSpec(
        num_scalar_prefetch=1, grid=(n, nb),
        in_specs=[BS((None, tr, C), lambda s, i, c_ref: (s, c_ref[0] * nb + i, 0)),
                  BS((None, tr, C), lambda s, i, c_ref: (s, i, 0))],
        out_specs=BS((None, tr, C), lambda s, i, c_ref: (s, i, 0)))
    return pl.pallas_call(body, name=name, grid_spec=grid_spec, out_shape=SDS((n, H, C), out_dtype),
                          compiler_params=_params(2))(core, g, other)


def exchange_slots_between_chips(parts, name):
    return _run_exchange(slot_exchange(parts), name)


def slot_exchange(parts):
    n = len(parts)

    def sends(srcs, lands, send_sems, recv_sems):
        x, y, c = _position()
        me = 2 * x + y
        return [_remote(srcs[t].at[2 * px + py], lands[t].at[me], send_sems.at[3 * t + j], recv_sems.at[3 * t + j],
                        (px, py, c)) for t in range(n) for j, (px, py) in enumerate(_other_chips(x, y))]

    def start(srcs, lands, send_sems, recv_sems):
        for cp in sends(srcs, lands, send_sems, recv_sems):
            cp.start()

    def finish(srcs, lands, send_sems, recv_sems):
        x, y, c = _position()
        for t in range(n):
            for j, (px, py) in enumerate(_other_chips(x, y)):
                k = 2 * px + py
                _remote(srcs[t].at[k], lands[t].at[k], send_sems.at[3 * t + j], recv_sems.at[3 * t + j],
                        (px, py, c)).wait_recv()
        for cp in sends(srcs, lands, send_sems, recv_sems):
            cp.wait_send()

    return SideExchange(list(parts), [SDS(p.shape, p.dtype) for p in parts], 3 * n, start, finish)


def sum_slots_into_my_half(landed, mine, name):
    n, H, C = landed.shape
    tr = _block_rows(H, C, 16) if H % 16 == 0 else H
    nb = H // tr
    where = jnp.stack([lax.axis_index("c"), _chip_index()]).astype(jnp.int32)

    def body(w_ref, l_ref, m_ref, o_ref):
        me = w_ref[1]
        o_ref[...] = jnp.zeros_like(o_ref)
        for k in range(n):
            @pl.when(me == k)
            def _():
                o_ref[...] += m_ref[k].astype(F32)

            @pl.when(me != k)
            def _():
                o_ref[...] += l_ref[k].astype(F32)

    slots = BS((n, tr, C), lambda i, w_ref: (0, i, 0))
    grid_spec = pltpu.PrefetchScalarGridSpec(
        num_scalar_prefetch=1, grid=(nb,), in_specs=[slots, slots],
        out_specs=BS((tr, C), lambda i, w_ref: (w_ref[0] * nb + i, 0)))
    return pl.pallas_call(body, name=name, grid_spec=grid_spec, out_shape=SDS((2 * H, C), F32),
                          compiler_params=_params(1))(where, landed, mine)


def sum_slots(parts, name):
    n, H, C = parts.shape
    tr = _tile(H, 256, 16)

    def body(p_ref, o_ref):
        acc = p_ref[0].astype(F32)
        for k in range(1, n):
            acc = acc + p_ref[k].astype(F32)
        o_ref[...] = acc

    return pl.pallas_call(
        body, name=name, grid=(H // tr,), in_specs=[BS((n, tr, C), lambda i: (0, i, 0))],
        out_specs=BS((tr, C), lambda i: (i, 0)), out_shape=SDS((H, C), F32), compiler_params=_params(1))(parts)


def join_halves_with_sibling(arrs, name):
    n = len(arrs)
    halves = [a.shape[0] // 2 for a in arrs]

    def body(*refs):
        outs, (send_sems, recv_sems) = refs[n:2 * n], refs[2 * n:]
        x, y, c = _position()

        def half(t, core):
            return outs[t].at[pl.ds(core * halves[t], halves[t]), :]

        sends = [_remote(half(t, c), half(t, c), send_sems.at[t], recv_sems.at[t], (x, y, 1 - c)) for t in range(n)]
        for cp in sends:
            cp.start()
        for t in range(n):
            _remote(half(t, 1 - c), half(t, 1 - c), send_sems.at[t], recv_sems.at[t], (x, y, 1 - c)).wait_recv()
        for cp in sends:
            cp.wait_send()

    return pl.pallas_call(
        body, name=name, in_specs=[ANY] * n, out_specs=[ANY] * n, out_shape=[SDS(a.shape, a.dtype) for a in arrs],
        input_output_aliases={t: t for t in range(n)},
        scratch_shapes=[pltpu.SemaphoreType.DMA((n,)), pltpu.SemaphoreType.DMA((n,))])(*arrs)


def reduce_scatter_pair_sums(gs, others, tag):
    n = len(gs)
    return [add_my_half(g, o, BF16 if t < n - 1 else F32, f"{tag}_pair_sum{t}")
            for t, (g, o) in enumerate(zip(gs, others))]


def reduce_scatter_finish(parts, landed, tag):
    sums = [sum_slots_into_my_half(l, p, f"{tag}_chip_sum{t}") for t, (l, p) in enumerate(zip(landed, parts))]
    return join_halves_with_sibling(sums, f"{tag}_join")


def gather_all_devices(pack, name):
    R, C = pack.shape

    def body(src, out, send_sems, recv_sems, local_sem):
        x, y, c = _position()
        me = 4 * x + 2 * y + c
        local = pltpu.make_async_copy(src, out.at[me], local_sem)
        local.start()
        peers = []
        for m in range(1, N_DEV):
            px = 1 - x if m & 4 else x
            py = 1 - y if m & 2 else y
            pc = 1 - c if m & 1 else c
            peers.append((px, py, pc))
        sends = [_remote(src, out.at[me], send_sems.at[j], recv_sems.at[j], peer) for j, peer in enumerate(peers)]
        for cp in sends:
            cp.start()
        for j, (px, py, pc) in enumerate(peers):
            k = 4 * px + 2 * py + pc
            _remote(src, out.at[k], send_sems.at[j], recv_sems.at[j], (px, py, pc)).wait_recv()
        for cp in sends:
            cp.wait_send()
        local.wait()

    return pl.pallas_call(
        body, name=name, in_specs=[ANY], out_specs=ANY, out_shape=SDS((N_DEV, R, C), pack.dtype),
        scratch_shapes=[pltpu.SemaphoreType.DMA((N_DEV - 1,)), pltpu.SemaphoreType.DMA((N_DEV - 1,)),
                        pltpu.SemaphoreType.DMA(())])(pack)


def _adamw_update(w, g, m, v):
    c1 = 1.0 / (1.0 - ADAM_B1 ** ADAM_STEP)
    c2 = 1.0 / (1.0 - ADAM_B2 ** ADAM_STEP)
    mn = ADAM_B1 * m + (1.0 - ADAM_B1) * g
    vn = ADAM_B2 * v + (1.0 - ADAM_B2) * (g * g)
    return -ADAM_LR * ((mn * c1) / (jnp.sqrt(vn * c2) + ADAM_EPS) + ADAM_WD * w), mn, vn


def adamw_layers(w, gs, m, v, name):
    L, r, c = w.shape
    tr = r if r * c * 4 <= (1 << 20) else _tile(r, max(8, ((1 << 20) // (c * 4)) // 8 * 8), 8)
    nb = r // tr

    def g_spec(l):
        return BS((tr, c), lambda layer, i: (jnp.where(layer == l, i, jnp.where(layer < l, 0, nb - 1)), 0))

    def body(w_ref, m_ref, v_ref, *rest):
        g_refs, (go_ref, d_ref, mo_ref, vo_ref) = rest[:L], rest[L:]
        layer = pl.program_id(0)
        for l in range(L):
            @pl.when(layer == l)
            def _(g_ref=g_refs[l]):
                gv = g_ref[...]
                go_ref[...] = gv
                d_ref[...], mo_ref[...], vo_ref[...] = _adamw_update(w_ref[...], gv, m_ref[...], v_ref[...])

    blk = BS((None, tr, c), lambda layer, i: (layer, i, 0))
    return pl.pallas_call(
        body, name=name, grid=(L, nb), in_specs=[blk] * 3 + [g_spec(l) for l in range(L)], out_specs=[blk] * 4,
        out_shape=[SDS((L, r, c), F32)] * 4, compiler_params=_params(2))(w, m, v, *gs)


def adamw(w, g, m, v, name):
    shape = w.shape
    cols = shape[-1] if len(shape) > 1 else shape[0]
    rows = w.size // cols
    two = lambda t: t.reshape(rows, cols)
    tr = rows
    if rows * cols * 4 > (1 << 21):
        tr = _tile(rows, max(8, ((1 << 21) // (cols * 4)) // 8 * 8), 8)

    def body(w_ref, g_ref, m_ref, v_ref, d_ref, mo_ref, vo_ref):
        d_ref[...], mo_ref[...], vo_ref[...] = _adamw_update(w_ref[...], g_ref[...], m_ref[...], v_ref[...])

    blk = BS((tr, cols), lambda i: (i, 0))
    d, mo, vo = pl.pallas_call(
        body, name=name, grid=(rows // tr,), in_specs=[blk] * 4, out_specs=[blk] * 3,
        out_shape=[SDS((rows, cols), F32)] * 3, compiler_params=_params(1))(two(w), two(g), two(m), two(v))
    return d.reshape(shape), mo.reshape(shape), vo.reshape(shape)


def _rows_for(n, unit):
    rows = -(-n // PACK_COLS)
    return -(-rows // unit) * unit


ROWS_GROUP = ("w_out", "w_ple_gate", "w_down")
COLS_GROUP = ("w_br_fox", "w_br_conv", "w_br_sgu", "w_br_dil", "w_ple_proj")
SMALL_GROUP = ("shortconv_w", "ffn_conv_w")
SMALL_ROWS = 16


def group_shards(t, dtype):
    lead = t["w_in"].shape[:-2]
    small = jnp.concatenate([t[n].astype(F32).reshape(lead + (-1,)) for n in SMALL_GROUP], axis=-1)
    pad = jnp.zeros(lead + (SMALL_ROWS * PACK_COLS - small.shape[-1],), F32)
    small = jnp.concatenate([small, pad], axis=-1).reshape(lead + (SMALL_ROWS, PACK_COLS))
    return [t["w_in"].astype(dtype), t["w_up"].astype(dtype),
            jnp.concatenate([t[n].astype(dtype) for n in ROWS_GROUP], axis=-2),
            jnp.concatenate([t[n].astype(dtype) for n in COLS_GROUP], axis=-2), small]


def ungroup_shards(arrs, shard_shapes):
    w_in_s, w_up_s, rows, cols, small = arrs
    lead = w_in_s.shape[:-2]
    out = {"w_in": w_in_s, "w_up": w_up_s}
    for group, arr in ((ROWS_GROUP, rows), (COLS_GROUP, cols)):
        off = 0
        for n in group:
            r = shard_shapes[n][0]
            out[n] = arr[..., off:off + r, :]
            off += r
    flat = small.reshape(lead + (-1,))
    off = 0
    for n in SMALL_GROUP:
        size = shard_shapes[n][0] * shard_shapes[n][1]
        out[n] = flat[..., off:off + size].reshape(lead + shard_shapes[n])
        off += size
    return out


class ShardExchange:
    def __init__(self, weights, depth, shard_shapes):
        self.shard_shapes = shard_shapes
        self.packs = [group_shards({n: weights[n][layer] for n in SHARDED}, BF16) for layer in range(depth)]
        self.pending = None
        self.parts = None
        self.shard_grads = [None] * depth
        self.repl_grads = [None] * depth

    def _chips(self, layer, gathered):
        per_chip = [ungroup_shards([g[k] for g in gathered], self.shard_shapes) for k in range(N_CHIPS)]
        return {n: [per_chip[k][n] for k in range(N_CHIPS)] for n in SHARDED}

    def first_weights(self):
        return self._chips(0, gather_chip_shards(self.packs[0], "gather_w0"))

    FWD_HOSTS = {"fox": (0, 3, 4), "proj": (1,), "ffn": (2,)}
    BWD_HOSTS = {"dq": (0, 3, 4), "dkv": (1, 2)}

    @staticmethod
    def _split(hosts, arrs, make):
        return {host: make([arrs[t] for t in idx]) for host, idx in hosts.items()}

    @staticmethod
    def _join(hosts, outs):
        arrs = [None] * sum(len(idx) for idx in hosts.values())
        for host, idx in hosts.items():
            for t, arr in zip(idx, outs[host]):
                arrs[t] = arr
        return arrs

    def weights_exchange(self, layer):
        return self._split(self.FWD_HOSTS, self.packs[layer], gather_exchange)

    def weights_arrived(self, layer, outs):
        return self._chips(layer, self._join(self.FWD_HOSTS, outs))

    def grads_ready(self, layer, gr):
        self.repl_grads[layer] = {n: gr[n] for n in REPLICATED}
        self.pending = (layer, group_shards({n: gr[n] for n in SHARDED}, F32))

    def swap_exchange(self):
        return swap_exchange(self.pending[1]) if self.pending is not None else None

    def ici_exchanges(self, swapped):
        if self.pending is None:
            return {}
        layer, slots = self.pending
        self.parts = reduce_scatter_pair_sums(slots, swapped, f"rs{layer}")
        return self._split(self.BWD_HOSTS, self.parts, slot_exchange)

    def ici_arrived(self, outs):
        if self.pending is not None:
            self._finish(self._join(self.BWD_HOSTS, outs))

    def grads_flush(self):
        layer, slots = self.pending
        self.parts = reduce_scatter_pair_sums(slots, swap_halves_with_sibling(slots, f"rs{layer}_swap"), f"rs{layer}")
        self._finish(exchange_slots_between_chips(self.parts, f"rs{layer}_ici"))

    def _finish(self, landed):
        layer = self.pending[0]
        self.shard_grads[layer] = ungroup_shards(reduce_scatter_finish(self.parts, landed, f"rs{layer}"),
                                                 self.shard_shapes)
        self.pending = None


REPL_SHAPES = {"norm_mix_g": (D_MODEL,), "fox_forget_b": (FOX_HEADS,), "sgu_norm_g": (SGU_WIDTH,),
               "sgu_w": (4, SGU_CHUNK, SGU_CHUNK), "sgu_b": (4, SGU_CHUNK), "norm_ffn_g": (D_MODEL,),
               "norm_ple_g": (D_MODEL,)}


def kernel(x, p, positions, norm_mix_g, w_in, fox_forget_b, shortconv_w, sgu_norm_g, sgu_w, sgu_b, w_br_fox, w_br_conv, w_br_sgu, w_br_dil, w_out, norm_ffn_g, w_up, ffn_conv_w, w_down, norm_ple_g, w_ple_gate, w_ple_proj, final_norm_g, loss_target, m_norm_mix_g, m_w_in, m_fox_forget_b, m_shortconv_w, m_sgu_norm_g, m_sgu_w, m_sgu_b, m_w_br_fox, m_w_br_conv, m_w_br_sgu, m_w_br_dil, m_w_out, m_norm_ffn_g, m_w_up, m_ffn_conv_w, m_w_down, m_norm_ple_g, m_w_ple_gate, m_w_ple_proj, m_final_norm_g, v_norm_mix_g, v_w_in, v_fox_forget_b, v_shortconv_w, v_sgu_norm_g, v_sgu_w, v_sgu_b, v_w_br_fox, v_w_br_conv, v_w_br_sgu, v_w_br_dil, v_w_out, v_norm_ffn_g, v_w_up, v_ffn_conv_w, v_w_down, v_norm_ple_g, v_w_ple_gate, v_w_ple_proj, v_final_norm_g):
    weights = dict(norm_mix_g=norm_mix_g, w_in=w_in, fox_forget_b=fox_forget_b, shortconv_w=shortconv_w,
                   sgu_norm_g=sgu_norm_g, sgu_w=sgu_w, sgu_b=sgu_b, w_br_fox=w_br_fox, w_br_conv=w_br_conv,
                   w_br_sgu=w_br_sgu, w_br_dil=w_br_dil, w_out=w_out, norm_ffn_g=norm_ffn_g, w_up=w_up,
                   ffn_conv_w=ffn_conv_w, w_down=w_down, norm_ple_g=norm_ple_g, w_ple_gate=w_ple_gate,
                   w_ple_proj=w_ple_proj, final_norm_g=final_norm_g)
    mom1 = dict(norm_mix_g=m_norm_mix_g, w_in=m_w_in, fox_forget_b=m_fox_forget_b, shortconv_w=m_shortconv_w,
                sgu_norm_g=m_sgu_norm_g, sgu_w=m_sgu_w, sgu_b=m_sgu_b, w_br_fox=m_w_br_fox, w_br_conv=m_w_br_conv,
                w_br_sgu=m_w_br_sgu, w_br_dil=m_w_br_dil, w_out=m_w_out, norm_ffn_g=m_norm_ffn_g, w_up=m_w_up,
                ffn_conv_w=m_ffn_conv_w, w_down=m_w_down, norm_ple_g=m_norm_ple_g, w_ple_gate=m_w_ple_gate,
                w_ple_proj=m_w_ple_proj, final_norm_g=m_final_norm_g)
    mom2 = dict(norm_mix_g=v_norm_mix_g, w_in=v_w_in, fox_forget_b=v_fox_forget_b, shortconv_w=v_shortconv_w,
                sgu_norm_g=v_sgu_norm_g, sgu_w=v_sgu_w, sgu_b=v_sgu_b, w_br_fox=v_w_br_fox, w_br_conv=v_w_br_conv,
                w_br_sgu=v_w_br_sgu, w_br_dil=v_w_br_dil, w_out=v_w_out, norm_ffn_g=v_norm_ffn_g, w_up=v_w_up,
                ffn_conv_w=v_ffn_conv_w, w_down=v_w_down, norm_ple_g=v_norm_ple_g, w_ple_gate=v_w_ple_gate,
                w_ple_proj=v_w_ple_proj, final_norm_g=v_final_norm_g)
    depth = w_in.shape[0]
    shard_shapes = {n: tuple(weights[n].shape[1:]) for n in SHARDED}

    exch = ShardExchange(weights, depth, shard_shapes)
    repl = {n: weights[n] for n in REPLICATED}
    loss_part, grad_x, d_final = local_step(x[0], p[:, 0], positions[0], repl, final_norm_g, loss_target[0], exch)
    loss = lax.psum(loss_part, ("x", "y", "c"))
    grads = exch.repl_grads
    grad_w, deltas, new_m, new_v = {}, {}, {}, {}
    for n in SHARDED:
        grad_w[n], deltas[n], new_m[n], new_v[n] = adamw_layers(
            weights[n], [exch.shard_grads[layer][n] for layer in range(depth)], mom1[n], mom2[n], f"adamw_{n}")

    flat = jnp.concatenate([grads[layer][n].astype(F32).reshape(-1) for layer in range(depth) for n in REPLICATED]
                           + [d_final])
    Rr = _rows_for(flat.shape[0], 16)
    packed = jnp.concatenate([flat, jnp.zeros((Rr * PACK_COLS - flat.shape[0],), F32)]).reshape(Rr, PACK_COLS)
    total = sum_slots(gather_all_devices(packed, "gather_repl"), "sum_repl").reshape(-1)
    off = 0
    g_rep = {n: [] for n in REPLICATED}
    for layer in range(depth):
        for n in REPLICATED:
            size = 1
            for s in REPL_SHAPES[n]:
                size *= s
            g_rep[n].append(total[off:off + size].reshape(REPL_SHAPES[n]))
            off += size
    for n in REPLICATED:
        grad_w[n] = jnp.stack(g_rep[n], axis=0)
    grad_w["final_norm_g"] = total[off:off + D_MODEL]

    for n in REPLICATED + ("final_norm_g",):
        deltas[n], new_m[n], new_v[n] = adamw(weights[n], grad_w[n], mom1[n], mom2[n], f"adamw_{n}")
    return (loss, grad_x[None], *[grad_w[n] for n in WEIGHTS], *[deltas[n] for n in WEIGHTS],
            *[new_m[n] for n in WEIGHTS], *[new_v[n] for n in WEIGHTS])
```
